```python
import math
import jax, jax.numpy as jnp
from jax import lax
import numpy as np

D_MODEL = 1024
BATCH = 8
SEQ = 4096
DEPTH = 2
DEC_BATCH = 32
DEC_SEQ = 16
PAST_LEN = 4096

CHUNK = 64
N_META = 16
Q_BLOCK = 128
EPS = 1e-6
N_AB = (DEPTH + 1) // 2
N_CD = DEPTH // 2
MIX_WIDTH = D_MODEL
S5_WIDTH = D_MODEL // 2
S5_GROUP = 16
G_A = S5_WIDTH // S5_GROUP
P_A = 64
DH_B = 64
H_B = D_MODEL // 256
DV_B = 2 * DH_B
DIFF_QK = H_B * 2 * DH_B
DIFF_V = H_B * DV_B
DH_C = 64
H_C = D_MODEL // 128
FOX_W = H_C * DH_C
FOX_BIAS = 3.0
H_D = D_MODEL // 128
NOPE_D = 64
ROPE_D = 32
V_D = 64
Q_LORA = 3 * D_MODEL // 8
KV_LORA = D_MODEL // 4
ROPE_THETA = 10000.0
D_FF = 256 * ((8 * D_MODEL // 3 + 255) // 256)
AB_IN = S5_WIDTH + 2 * DIFF_QK + DIFF_V
AB_SPLITS = (S5_WIDTH, S5_WIDTH + DIFF_QK, S5_WIDTH + 2 * DIFF_QK)
CD_IN = 3 * FOX_W + H_C + Q_LORA + KV_LORA + ROPE_D
CD_SPLITS = (FOX_W, 2 * FOX_W, 3 * FOX_W, 3 * FOX_W + H_C, 3 * FOX_W + H_C + Q_LORA, 3 * FOX_W + H_C + Q_LORA + KV_LORA)

kernel_name = 'hybrid_streaming_encoder_step'


def _rms(x, g):
    xf = x.astype(jnp.float32)
    y = xf * lax.rsqrt(jnp.mean(xf * xf, axis=-1, keepdims=True) + EPS)
    return (y * g.astype(jnp.float32)).astype(x.dtype)


def _swiglu(h, w_in, w_out):
    gate, up = jnp.split(h @ w_in, 2, axis=-1)
    return (jax.nn.silu(gate) * up) @ w_out


def _rope(x, pos):
    half = x.shape[-1] // 2
    inv = ROPE_THETA ** (-jnp.arange(half, dtype=jnp.float32) / half)
    ang = pos.astype(jnp.float32)[:, None] * inv[None, :]
    cos = jnp.cos(ang)[None, :, None, :]
    sin = jnp.sin(ang)[None, :, None, :]
    xf = x.astype(jnp.float32)
    x1, x2 = xf[..., :half], xf[..., half:]
    return jnp.concatenate([x1 * cos - x2 * sin, x2 * cos + x1 * sin], axis=-1).astype(x.dtype)


def _prompt_chunk(pos):
    return jnp.where(pos < N_META, 0, 1 + (pos - N_META) // CHUNK)


def _frame_chunk(pos):
    return 1 + pos // CHUNK


def _sweep_prompt(fn, *qs):
    n = qs[0].shape[1]
    pad = (-n) % Q_BLOCK
    nb = (n + pad) // Q_BLOCK

    def split(a):
        a = jnp.pad(a, [(0, 0), (0, pad)] + [(0, 0)] * (a.ndim - 2))
        return jnp.moveaxis(a.reshape((a.shape[0], nb, Q_BLOCK) + a.shape[2:]), 1, 0)

    qpos = jnp.arange(nb * Q_BLOCK, dtype=jnp.int32).reshape(nb, Q_BLOCK)
    out = lax.map(lambda args: fn(args[0], *args[1:]), (qpos, *[split(a) for a in qs]))
    out = jnp.moveaxis(out, 0, 1)
    return out.reshape((out.shape[0], nb * Q_BLOCK) + out.shape[3:])[:, :n]


def _masked_softmax_av(s, mask, v):
    p = jax.nn.softmax(jnp.where(mask, s, -jnp.inf), axis=-1)
    return jnp.einsum('bhqk,bkhd->bqhd', p, v.astype(jnp.float32))


def _lin_rec(left, right):
    a1, b1 = left
    a2, b2 = right
    return a1 * a2, a2 * b1 + b2


def _s5(u, a_re, a_im, log_step, b_re, b_im, c_re, c_im, d, glu_w, glu_b, h0):
    bsz, t, _ = u.shape
    f32 = jnp.float32
    uf = u.astype(f32).reshape(bsz, t, G_A, S5_GROUP)
    lam = lax.complex(a_re.astype(f32), a_im.astype(f32))
    lam_bar = jnp.exp(lam * jnp.exp(log_step.astype(f32))[:, None])
    b_bar = ((lam_bar - 1.0) / lam)[..., None] * lax.complex(b_re.astype(f32), b_im.astype(f32))
    bu = jnp.einsum('gpc,btgc->btgp', b_bar, uf.astype(b_bar.dtype))
    if h0 is not None:
        bu = bu.at[:, 0].add(lam_bar * h0)
    _, hs = lax.associative_scan(_lin_rec, (jnp.broadcast_to(lam_bar, bu.shape), bu), axis=1)
    c = lax.complex(c_re.astype(f32), c_im.astype(f32))
    y = jnp.einsum('gcp,btgp->btgc', c, hs).real + d.astype(f32).reshape(G_A, S5_GROUP) * uf
    g = jax.nn.gelu(y.reshape(bsz, t, S5_WIDTH))
    out = g * jax.nn.sigmoid(g @ glu_w.astype(f32) + glu_b.astype(f32))
    return out.astype(u.dtype), hs[:, -1].real, hs[:, -1].imag


def _diff_core(q, k, v, lam, slopes, qpos, qchk, kpos, kchk):
    s = jnp.einsum('bqhmd,bkhmd->bhmqk', q, k, preferred_element_type=jnp.float32) * (DH_B ** -0.5)
    dist = jnp.abs(qpos[:, None] - kpos[None, :]).astype(jnp.float32)
    s = s - slopes[:, None, None, None] * dist
    mask = kchk[None, :] <= qchk[:, None]
    p = jax.nn.softmax(jnp.where(mask, s, -jnp.inf), axis=-1)
    w = p[:, :, 0] - lam * p[:, :, 1]
    return jnp.einsum('bhqk,bkhd->bqhd', w, v.astype(jnp.float32))


def _fox_core(q, k, v, fq, fk, qpos, kpos):
    s = jnp.einsum('bqhd,bkhd->bhqk', q, k, preferred_element_type=jnp.float32) * (DH_C ** -0.5)
    s = s + jnp.swapaxes(fq, 1, 2)[..., :, None] - jnp.swapaxes(fk, 1, 2)[..., None, :]
    return _masked_softmax_av(s, kpos[None, :] <= qpos[:, None], v)


def _mla_core(q, k, v, qchk, kchk):
    s = jnp.einsum('bqhd,bkhd->bhqk', q, k, preferred_element_type=jnp.float32) * ((NOPE_D + ROPE_D) ** -0.5)
    return _masked_softmax_av(s, kchk[None, :] <= qchk[:, None], v)


def _ab_mixer(h, w_in, w_out, a_re, a_im, log_step, b_re, b_im, c_re, c_im, d, glu_w, glu_b,
              q_norm, k_norm, lam_vecs, sub_norm, layer, past):
    bsz, t, _ = h.shape
    f32 = jnp.float32
    u, q, k, v = jnp.split(h @ w_in, AB_SPLITS, axis=-1)
    h0 = None if past is None else lax.complex(past[0].astype(f32), past[1].astype(f32))
    y_s5, s_re, s_im = _s5(u, a_re, a_im, log_step, b_re, b_im, c_re, c_im, d, glu_w, glu_b, h0)
    q = _rms(q.reshape(bsz, t, H_B, 2, DH_B), q_norm)
    k = _rms(k.reshape(bsz, t, H_B, 2, DH_B), k_norm)
    v = v.reshape(bsz, t, H_B, DV_B)
    lv = lam_vecs.astype(f32)
    lam_init = 0.8 - 0.6 * math.exp(-0.3 * layer)
    lam = jnp.exp(jnp.sum(lv[0] * lv[1])) - jnp.exp(jnp.sum(lv[2] * lv[3])) + lam_init
    slopes = jnp.exp2(-8.0 * jnp.arange(1, H_B + 1, dtype=f32) / H_B)
    if past is None:
        kpos = jnp.arange(t, dtype=jnp.int32)
        kchk = _prompt_chunk(kpos)
        o = _sweep_prompt(lambda qp, qb: _diff_core(qb, k, v, lam, slopes, qp, _prompt_chunk(qp), kpos, kchk), q)
    else:
        p_len = past[2].shape[1]
        k_all = jnp.concatenate([past[2].reshape(bsz, p_len, H_B, 2, DH_B).astype(k.dtype), k], axis=1)
        v_all = jnp.concatenate([past[3].astype(v.dtype), v], axis=1)
        kpos = jnp.arange(p_len + t, dtype=jnp.int32)
        qpos = p_len + jnp.arange(t, dtype=jnp.int32)
        o = _diff_core(q, k_all, v_all, lam, slopes, qpos, _frame_chunk(qpos), kpos, _frame_chunk(kpos))
    o = (_rms(o, sub_norm) * (1.0 - lam_init)).astype(h.dtype).reshape(bsz, t, DIFF_V)
    out = jnp.concatenate([y_s5, o], axis=-1) @ w_out
    return out, (s_re, s_im, k.reshape(bsz, t, H_B, 2 * DH_B), v)


def _cd_mixer(h, w_in, w_out, fq_norm, fk_norm, f_bias, q_a_norm, q_b, kv_a_norm, kv_b, mq_norm, mk_norm, past):
    bsz, t, _ = h.shape
    f32 = jnp.float32
    fq, fk, fv, fg, qa, kva, kpe_raw = jnp.split(h @ w_in, CD_SPLITS, axis=-1)
    fq = _rms(fq.reshape(bsz, t, H_C, DH_C), fq_norm)
    fk = _rms(fk.reshape(bsz, t, H_C, DH_C), fk_norm)
    fv = fv.reshape(bsz, t, H_C, DH_C)
    logf = jax.nn.log_sigmoid(fg.astype(f32) + f_bias.astype(f32))
    qh = (_rms(qa, q_a_norm) @ q_b).reshape(bsz, t, H_D, NOPE_D + ROPE_D)
    ckv = _rms(kva, kv_a_norm)
    if past is None:
        qpos = jnp.arange(t, dtype=jnp.int32)
        kpos = qpos
        kchk = _prompt_chunk(kpos)
    else:
        p_len = past[0].shape[1]
        qpos = p_len + jnp.arange(t, dtype=jnp.int32)
        kpos = jnp.arange(p_len + t, dtype=jnp.int32)
        kchk = _frame_chunk(kpos)
    kpe = _rope(kpe_raw[:, :, None, :], qpos)[:, :, 0]
    if past is None:
        fk_all, fv_all, logf_all, ckv_all, kpe_all = fk, fv, logf, ckv, kpe
    else:
        fk_all = jnp.concatenate([past[0].astype(fk.dtype), fk], axis=1)
        fv_all = jnp.concatenate([past[1].astype(fv.dtype), fv], axis=1)
        logf_all = jnp.concatenate([past[2].astype(f32), logf], axis=1)
        ckv_all = jnp.concatenate([past[3].astype(ckv.dtype), ckv], axis=1)
        kpe_all = jnp.concatenate([past[4].astype(kpe.dtype), kpe], axis=1)
    tk = ckv_all.shape[1]
    q_pe = _rope(qh[..., NOPE_D:], qpos)
    qm = _rms(jnp.concatenate([qh[..., :NOPE_D], q_pe], axis=-1), mq_norm)
    kv = (ckv_all @ kv_b).reshape(bsz, tk, H_D, NOPE_D + V_D)
    km = _rms(jnp.concatenate([kv[..., :NOPE_D], jnp.broadcast_to(kpe_all[:, :, None, :], (bsz, tk, H_D, ROPE_D))], axis=-1), mk_norm)
    vm = kv[..., NOPE_D:]
    f_cum = jnp.cumsum(logf_all, axis=1)
    f_q = f_cum[:, tk - t:]
    if past is None:
        o_c = _sweep_prompt(lambda qp, qb, fb: _fox_core(qb, fk_all, fv_all, fb, f_cum, qp, kpos), fq, f_q)
        o_d = _sweep_prompt(lambda qp, qb: _mla_core(qb, km, vm, _prompt_chunk(qp), kchk), qm)
    else:
        o_c = _fox_core(fq, fk_all, fv_all, f_q, f_cum, qpos, kpos)
        o_d = _mla_core(qm, km, vm, _frame_chunk(qpos), kchk)
    o = jnp.concatenate([o_c.reshape(bsz, t, FOX_W), o_d.reshape(bsz, t, H_D * V_D)], axis=-1).astype(h.dtype)
    return o @ w_out, (fk, fv, logf, ckv, kpe)


def _layer(x, ffn_norm_l, ffn_w_in_l, ffn_w_out_l, mix_norm_l, mixer, past):
    x = x + 0.5 * _swiglu(_rms(x, ffn_norm_l[0]), ffn_w_in_l[0], ffn_w_out_l[0])
    m, new_state = mixer(_rms(x, mix_norm_l), past)
    x = x + m
    x = x + 0.5 * _swiglu(_rms(x, ffn_norm_l[1]), ffn_w_in_l[1], ffn_w_out_l[1])
    return x, new_state


def _stack_field(rows, j):
    return jnp.stack([r[j] for r in rows])


def setup_inputs(seed: int = 0) -> dict:
    key = jax.random.key(seed)
    ks = iter(jax.random.split(key, 64))
    f32 = jnp.float32

    def nrm(shape, scale):
        return scale * jax.random.normal(next(ks), shape, f32)

    def gain(shape):
        return 1.0 + nrm(shape, 0.01)

    return {
        'x_prompt': nrm((BATCH, SEQ, D_MODEL), 1.0),
        'x_sample': nrm((DEC_BATCH, DEC_SEQ, D_MODEL), 1.0),
        'state_s5_re': nrm((N_AB, DEC_BATCH, G_A, P_A), 0.1),
        'state_s5_im': nrm((N_AB, DEC_BATCH, G_A, P_A), 0.1),
        'cache_diff_k': nrm((N_AB, DEC_BATCH, PAST_LEN, H_B, 2 * DH_B), 1.0),
        'cache_diff_v': nrm((N_AB, DEC_BATCH, PAST_LEN, H_B, DV_B), 1.0),
        'cache_fox_k': nrm((N_CD, DEC_BATCH, PAST_LEN, H_C, DH_C), 1.0),
        'cache_fox_v': nrm((N_CD, DEC_BATCH, PAST_LEN, H_C, DH_C), 1.0),
        'cache_fox_logf': jax.nn.log_sigmoid(FOX_BIAS + nrm((N_CD, DEC_BATCH, PAST_LEN, H_C), 1.0)),
        'cache_mla_ckv': nrm((N_CD, DEC_BATCH, PAST_LEN, KV_LORA), 1.0),
        'cache_mla_kpe': nrm((N_CD, DEC_BATCH, PAST_LEN, ROPE_D), 1.0),
        'meta_tokens': nrm((N_META, D_MODEL), 1.0),
        'ffn_norm': gain((DEPTH, 2, D_MODEL)),
        'ffn_w_in': nrm((DEPTH, 2, D_MODEL, 2 * D_FF), D_MODEL ** -0.5),
        'ffn_w_out': nrm((DEPTH, 2, D_FF, D_MODEL), D_FF ** -0.5),
        'mix_norm': gain((DEPTH, D_MODEL)),
        'ab_w_in': nrm((N_AB, D_MODEL, AB_IN), D_MODEL ** -0.5),
        'ab_w_out': nrm((N_AB, MIX_WIDTH, D_MODEL), MIX_WIDTH ** -0.5),
        's5_a_re': -0.5 + nrm((N_AB, G_A, P_A), 0.01),
        's5_a_im': math.pi * jnp.arange(P_A, dtype=f32) + nrm((N_AB, G_A, P_A), 0.01),
        's5_log_step': jax.random.uniform(next(ks), (N_AB, G_A), f32, math.log(1e-3), math.log(1e-1)),
        's5_b_re': nrm((N_AB, G_A, P_A, S5_GROUP), (2 * S5_GROUP) ** -0.5),
        's5_b_im': nrm((N_AB, G_A, P_A, S5_GROUP), (2 * S5_GROUP) ** -0.5),
        's5_c_re': nrm((N_AB, G_A, S5_GROUP, P_A), P_A ** -0.5),
        's5_c_im': nrm((N_AB, G_A, S5_GROUP, P_A), P_A ** -0.5),
        's5_d': nrm((N_AB, S5_WIDTH), 1.0),
        's5_glu_w': nrm((N_AB, S5_WIDTH, S5_WIDTH), S5_WIDTH ** -0.5),
        's5_glu_b': nrm((N_AB, S5_WIDTH), 0.02),
        'diff_q_norm': gain((N_AB, DH_B)),
        'diff_k_norm': gain((N_AB, DH_B)),
        'diff_lam': nrm((N_AB, 4, DH_B), 0.1),
        'diff_sub_norm': gain((N_AB, DV_B)),
        'cd_w_in': nrm((N_CD, D_MODEL, CD_IN), D_MODEL ** -0.5),
        'cd_w_out': nrm((N_CD, MIX_WIDTH, D_MODEL), MIX_WIDTH ** -0.5),
        'fox_q_norm': gain((N_CD, DH_C)),
        'fox_k_norm': gain((N_CD, DH_C)),
        'fox_f_bias': FOX_BIAS + nrm((N_CD, H_C), 0.5),
        'mla_q_a_norm': gain((N_CD, Q_LORA)),
        'mla_q_b': nrm((N_CD, Q_LORA, H_D * (NOPE_D + ROPE_D)), Q_LORA ** -0.5),
        'mla_kv_a_norm': gain((N_CD, KV_LORA)),
        'mla_kv_b': nrm((N_CD, KV_LORA, H_D * (NOPE_D + V_D)), KV_LORA ** -0.5),
        'mla_q_norm': gain((N_CD, NOPE_D + ROPE_D)),
        'mla_k_norm': gain((N_CD, NOPE_D + ROPE_D)),
    }


def reference(x_prompt, x_sample, state_s5_re, state_s5_im, cache_diff_k, cache_diff_v, cache_fox_k, cache_fox_v,
              cache_fox_logf, cache_mla_ckv, cache_mla_kpe, meta_tokens, ffn_norm, ffn_w_in, ffn_w_out, mix_norm,
              ab_w_in, ab_w_out, s5_a_re, s5_a_im, s5_log_step, s5_b_re, s5_b_im, s5_c_re, s5_c_im, s5_d,
              s5_glu_w, s5_glu_b, diff_q_norm, diff_k_norm, diff_lam, diff_sub_norm, cd_w_in, cd_w_out,
              fox_q_norm, fox_k_norm, fox_f_bias, mla_q_a_norm, mla_q_b, mla_kv_a_norm, mla_kv_b,
              mla_q_norm, mla_k_norm):
    bsz, _, dm = x_prompt.shape
    meta = jnp.broadcast_to(meta_tokens[None].astype(x_prompt.dtype), (bsz, N_META, dm))
    xp = jnp.concatenate([meta, x_prompt], axis=1)
    xs = x_sample
    ab_new_p, ab_new_s, cd_new_p, cd_new_s = [], [], [], []
    for l in range(DEPTH):
        i = l // 2
        if l % 2 == 0:
            def mixer(h, past, i=i, l=l):
                return _ab_mixer(h, ab_w_in[i], ab_w_out[i], s5_a_re[i], s5_a_im[i], s5_log_step[i],
                                 s5_b_re[i], s5_b_im[i], s5_c_re[i], s5_c_im[i], s5_d[i], s5_glu_w[i], s5_glu_b[i],
                                 diff_q_norm[i], diff_k_norm[i], diff_lam[i], diff_sub_norm[i], l, past)
            past_s = (state_s5_re[i], state_s5_im[i], cache_diff_k[i], cache_diff_v[i])
            rows_p, rows_s = ab_new_p, ab_new_s
        else:
            def mixer(h, past, i=i):
                return _cd_mixer(h, cd_w_in[i], cd_w_out[i], fox_q_norm[i], fox_k_norm[i], fox_f_bias[i],
                                 mla_q_a_norm[i], mla_q_b[i], mla_kv_a_norm[i], mla_kv_b[i],
                                 mla_q_norm[i], mla_k_norm[i], past)
            past_s = (cache_fox_k[i], cache_fox_v[i], cache_fox_logf[i], cache_mla_ckv[i], cache_mla_kpe[i])
            rows_p, rows_s = cd_new_p, cd_new_s
        xp, st_p = _layer(xp, ffn_norm[l], ffn_w_in[l], ffn_w_out[l], mix_norm[l], mixer, None)
        xs, st_s = _layer(xs, ffn_norm[l], ffn_w_in[l], ffn_w_out[l], mix_norm[l], mixer, past_s)
        rows_p.append(st_p)
        rows_s.append(st_s)
    y_prompt = xp[:, N_META:]
    y_sample = xs
    s5_re_p = _stack_field(ab_new_p, 0)
    s5_im_p = _stack_field(ab_new_p, 1)
    diff_k_p = _stack_field(ab_new_p, 2)
    diff_v_p = _stack_field(ab_new_p, 3)
    fox_k_p = _stack_field(cd_new_p, 0)
    fox_v_p = _stack_field(cd_new_p, 1)
    fox_logf_p = _stack_field(cd_new_p, 2)
    mla_ckv_p = _stack_field(cd_new_p, 3)
    mla_kpe_p = _stack_field(cd_new_p, 4)
    s5_re_s = _stack_field(ab_new_s, 0)
    s5_im_s = _stack_field(ab_new_s, 1)
    diff_k_s = _stack_field(ab_new_s, 2)
    diff_v_s = _stack_field(ab_new_s, 3)
    fox_k_s = _stack_field(cd_new_s, 0)
    fox_v_s = _stack_field(cd_new_s, 1)
    fox_logf_s = _stack_field(cd_new_s, 2)
    mla_ckv_s = _stack_field(cd_new_s, 3)
    mla_kpe_s = _stack_field(cd_new_s, 4)
    return (y_prompt, y_sample,
            s5_re_p, s5_im_p, diff_k_p, diff_v_p, fox_k_p, fox_v_p, fox_logf_p, mla_ckv_p, mla_kpe_p,
            s5_re_s, s5_im_s, diff_k_s, diff_v_s, fox_k_s, fox_v_s, fox_logf_s, mla_ckv_s, mla_kpe_s)
```

```python
import functools
import math

import jax
import jax.numpy as jnp
from jax import lax
from jax.experimental import pallas as pl
from jax.experimental.pallas import tpu as pltpu

F32 = jnp.float32
BF16 = jnp.bfloat16

EPS = 1e-6
CHUNK = 64
ROW_ALIGN = 256
S5_GROUP = 16
S5_STEP = 16
P_A = 64
DH_B = 64
DH_C = 64
NOPE_D = 64
ROPE_D = 32
V_D = 64
HEAD_PAD = 128
ROPE_THETA = 10000.0
NEG = -1e30
VMEM_LIMIT = 56 * 1024 * 1024
ATT_BLOCK = 256
DEC_KB = 1024


def _dot(a, b):
    return jnp.dot(a, b, preferred_element_type=F32)


def _dot_nt(a, b):
    return lax.dot_general(a, b, (((1,), (1,)), ((), ())), preferred_element_type=F32)


def _rms_rows(x, g):
    ms = jnp.mean(x * x, axis=-1, keepdims=True)
    return x * lax.rsqrt(ms + EPS) * g


def _group_sumsq(x, ones_bd):
    w = x.shape[-1]
    parts = [_dot((x[:, c:c + 256] * x[:, c:c + 256]).astype(BF16), ones_bd) for c in range(0, w, 256)]
    return parts[0] if len(parts) == 1 else jnp.concatenate(parts, axis=1)


def _block_diag_ones(group, n=256):
    r = jnp.arange(n) // group
    return (r[:, None] == r[None, :]).astype(BF16)


def _const_spec(shape):
    nd = len(shape)
    return pl.BlockSpec(shape, lambda *_: (0,) * nd, pipeline_mode=pl.Buffered(1))


def _row_tile(rows, cap=512):
    t = cap
    while rows % t:
        t //= 2
    return t


def _row_call(body, row_ins, consts, out_widths, out_dtypes, tm):
    rows = row_ins[0].shape[0]
    in_specs = [pl.BlockSpec((tm, a.shape[1]), lambda i: (i, 0)) for a in row_ins]
    in_specs += [_const_spec(c.shape) for c in consts]
    out_specs = [pl.BlockSpec((tm, w), lambda i: (i, 0)) for w in out_widths]
    out_shape = [jax.ShapeDtypeStruct((rows, w), d) for w, d in zip(out_widths, out_dtypes)]
    return pl.pallas_call(
        body,
        grid=(rows // tm,),
        in_specs=in_specs,
        out_specs=out_specs,
        out_shape=out_shape,
        compiler_params=pltpu.CompilerParams(
            dimension_semantics=("parallel",), vmem_limit_bytes=VMEM_LIMIT),
    )(*row_ins, *consts)


def _ffn_body(x_ref, g_ref, win_ref, wout_ref, o_ref, *, d_ff, tf):
    x = x_ref[...]
    xn = _rms_rows(x, g_ref[...]).astype(BF16)
    acc = jnp.zeros(x.shape, F32)
    for c in range(0, d_ff, tf):
        gate = _dot(xn, win_ref[:, c:c + tf])
        up = _dot(xn, win_ref[:, d_ff + c:d_ff + c + tf])
        a = (gate * jax.nn.sigmoid(gate) * up).astype(BF16)
        acc = acc + _dot(a, wout_ref[c:c + tf, :])
    o_ref[...] = x + 0.5 * acc


def _ffn(x, g, w_in, w_out, tm):
    d_ff = w_out.shape[0]
    body = functools.partial(_ffn_body, d_ff=d_ff, tf=256)
    return _row_call(body, [x], [g.reshape(1, -1), w_in.astype(BF16), w_out.astype(BF16)],
                     [x.shape[1]], [F32], tm)[0]


def _ab_in_body(x_ref, g_ref, w_ref, gq_ref, gk_ref, ones_ref,
                u_ref, q_ref, k32_ref, k16_ref, v32_ref, v16_ref, *, widths):
    s5w, qkw = widths
    xn = _rms_rows(x_ref[...], g_ref[...]).astype(BF16)
    h = _dot(xn, w_ref[...])
    u_ref[...] = h[:, :s5w].astype(BF16)
    q = h[:, s5w:s5w + qkw]
    k = h[:, s5w + qkw:s5w + 2 * qkw]
    v = h[:, s5w + 2 * qkw:]
    ones_bd = ones_ref[...]
    qn = q * lax.rsqrt(_group_sumsq(q, ones_bd) * (1.0 / DH_B) + EPS) * gq_ref[...]
    kn = k * lax.rsqrt(_group_sumsq(k, ones_bd) * (1.0 / DH_B) + EPS) * gk_ref[...]
    q_ref[...] = qn.astype(BF16)
    k32_ref[...] = kn
    k16_ref[...] = kn.astype(BF16)
    v32_ref[...] = v
    v16_ref[...] = v.astype(BF16)


def _ab_out_body(x_ref, y_ref, o_ref, gluw_ref, glub_ref, subg_ref, wout_ref, ones_ref, out_ref, *, s5w):
    y = y_ref[...]
    g = 0.5 * y * (1.0 + jnp.tanh(math.sqrt(2.0 / math.pi) * (y + 0.044715 * (y * y * y))))
    z = _dot(g.astype(BF16), gluw_ref[...]) + glub_ref[...]
    s5o = g * jax.nn.sigmoid(z)
    o = o_ref[...]
    dv = 2 * DH_B
    on = o * lax.rsqrt(_group_sumsq(o, ones_ref[...]) * (1.0 / dv) + EPS) * subg_ref[...]
    m = _dot(s5o.astype(BF16), wout_ref[:s5w, :]) + _dot(on.astype(BF16), wout_ref[s5w:, :])
    out_ref[...] = x_ref[...] + m


def _cd_in_body(x_ref, cos_ref, sin_ref, g_ref, w_ref, gfq_ref, gfk_ref, fb_ref, gqa_ref, wq2_ref,
                gkva_ref, wk_ref, wv_ref, gmq_ref, ones64_ref, ones128_ref,
                fq_ref, fk32_ref, fk16_ref, fv32_ref, fv16_ref, logf_ref, qm_ref, ckv_ref, kpe_ref,
                km_ref, vm_ref, *, fox_w, q_lora, kv_lora, n_heads):
    xn = _rms_rows(x_ref[...], g_ref[...]).astype(BF16)
    h = _dot(xn, w_ref[...])
    ones64 = ones64_ref[...]
    ones128 = ones128_ref[...]
    fq = h[:, :fox_w]
    fk = h[:, fox_w:2 * fox_w]
    fv = h[:, 2 * fox_w:3 * fox_w]
    c0 = 3 * fox_w
    qa = h[:, c0:c0 + q_lora]
    kva = h[:, c0 + q_lora:c0 + q_lora + kv_lora]
    c1 = c0 + q_lora + kv_lora
    pe_a = h[:, c1:c1 + HEAD_PAD]
    pe_b = h[:, c1 + HEAD_PAD:c1 + 2 * HEAD_PAD]
    fg = h[:, c1 + 2 * HEAD_PAD:c1 + 3 * HEAD_PAD]

    fqn = fq * lax.rsqrt(_group_sumsq(fq, ones64) * (1.0 / DH_C) + EPS) * gfq_ref[...]
    fkn = fk * lax.rsqrt(_group_sumsq(fk, ones64) * (1.0 / DH_C) + EPS) * gfk_ref[...]
    fq_ref[...] = fqn.astype(BF16)
    fk32_ref[...] = fkn
    fk16_ref[...] = fkn.astype(BF16)
    fv32_ref[...] = fv
    fv16_ref[...] = fv.astype(BF16)

    z = fg + fb_ref[...]
    logf = jnp.minimum(z, 0.0) - jnp.log1p(jnp.exp(-jnp.abs(z)))
    logf_ref[...] = logf[:, :logf_ref.shape[1]]

    cos = cos_ref[...]
    sin = sin_ref[...]
    qan = _rms_rows(qa, gqa_ref[...]).astype(BF16)
    q2 = _dot(qan, wq2_ref[...])
    hw = n_heads * HEAD_PAD
    cos_t = jnp.concatenate([cos] * n_heads, axis=1)
    sin_t = jnp.concatenate([sin] * n_heads, axis=1)
    qr = q2[:, :hw] * cos_t + q2[:, hw:] * sin_t
    d_qk = NOPE_D + ROPE_D
    qm = qr * lax.rsqrt(_group_sumsq(qr, ones128) * (1.0 / d_qk) + EPS) * gmq_ref[...]
    qm_ref[...] = qm.astype(BF16)

    ckv = _rms_rows(kva, gkva_ref[...])
    ckv_ref[...] = ckv
    pe = pe_a * cos + pe_b * sin
    kpe_ref[...] = pe[:, NOPE_D:NOPE_D + ROPE_D]
    ckv16 = ckv.astype(BF16)
    kraw = _dot(ckv16, wk_ref[...]) + jnp.concatenate([pe] * n_heads, axis=1)
    km = kraw * lax.rsqrt(_group_sumsq(kraw, ones128) * (1.0 / d_qk) + EPS)
    km_ref[...] = km.astype(BF16)
    vm_ref[...] = _dot(ckv16, wv_ref[...]).astype(BF16)


def _cd_out_body(x_ref, oc_ref, od_ref, wout_ref, out_ref, *, fox_w):
    m = _dot(oc_ref[...], wout_ref[:fox_w, :]) + _dot(od_ref[...], wout_ref[fox_w:, :])
    out_ref[...] = x_ref[...] + m


def _s5_body(u_ref, h0_ref, m_ref, bm_ref, cm_ref, coef_ref, y_ref, st_ref, s2_ref, hp_ref,
             *, n_chunks, bsz, j_last):
    u = u_ref[0]
    s2_ref[...] = _dot(u, bm_ref[0])
    c1 = coef_ref[0, 0:1, :]
    c2 = coef_ref[0, 1:2, :]
    c3 = coef_ref[0, 2:3, :]
    half = 2 * P_A

    def step(j, carry):
        ha, hb = carry
        r = pl.multiple_of(j * bsz, bsz)
        hp_ref[pl.ds(r, bsz), :] = ha
        s = s2_ref[pl.ds(r, bsz), :]
        ha2 = ha * c1 + hb * c2 + s[:, :half]
        hb2 = hb * c1 + ha * c3 + s[:, half:]

        @pl.when(j == j_last)
        def _():
            st_ref[0] = ha2

        return ha2, hb2

    h0 = h0_ref[0]
    lax.fori_loop(0, n_chunks, step, (h0[:, :half], h0[:, half:]))
    y_ref[0] = _dot(u, m_ref[0]) + _dot(hp_ref[...].astype(BF16), cm_ref[0])


def _s5_scan(u_t, h0, mats, n_chunks, bsz, j_last):
    m_mat, bm, cm, coef = mats
    g = u_t.shape[0]
    rows = n_chunks * bsz
    w = S5_STEP * S5_GROUP
    body = functools.partial(_s5_body, n_chunks=n_chunks, bsz=bsz, j_last=j_last)
    per_g = lambda a: pl.BlockSpec((1,) + a.shape[1:], lambda i: (i, 0, 0))
    return pl.pallas_call(
        body,
        grid=(g,),
        in_specs=[per_g(u_t), per_g(h0), per_g(m_mat), per_g(bm), per_g(cm), per_g(coef)],
        out_specs=[pl.BlockSpec((1, rows, w), lambda i: (i, 0, 0)),
                   pl.BlockSpec((1, bsz, 2 * P_A), lambda i: (i, 0, 0))],
        out_shape=[jax.ShapeDtypeStruct((g, rows, w), F32),
                   jax.ShapeDtypeStruct((g, bsz, 2 * P_A), F32)],
        scratch_shapes=[pltpu.VMEM((rows, 4 * P_A), F32), pltpu.VMEM((rows, 2 * P_A), F32)],
        compiler_params=pltpu.CompilerParams(
            dimension_semantics=("parallel",), vmem_limit_bytes=VMEM_LIMIT),
    )(u_t, h0, m_mat, bm, cm, coef)


def _s5_matrices(a_re, a_im, log_step, b_re, b_im, c_re, c_im, d):
    g = a_re.shape[0]
    t = S5_STEP
    lam = lax.complex(a_re, a_im)
    dl = lam * jnp.exp(log_step)[:, None]
    lam_bar = jnp.exp(dl)
    b_bar = ((lam_bar - 1.0) / lam)[..., None] * lax.complex(b_re, b_im)
    c = lax.complex(c_re, c_im)
    pw = jnp.exp(dl[:, None, :] * jnp.arange(t + 1, dtype=F32)[None, :, None])
    bmc = pw[:, t - 1::-1][:, :, :, None] * b_bar[:, None]
    bmc = jnp.swapaxes(bmc, 2, 3).reshape(g, t * S5_GROUP, P_A)
    bm = jnp.concatenate([bmc.real, bmc.imag, bmc.imag, bmc.real], axis=-1)
    kk = jnp.einsum('gcp,gkp,gpd->gkcd', c, pw[:, :t], b_bar).real
    kk = kk.at[:, 0].add(d.reshape(g, S5_GROUP)[:, :, None] * jnp.eye(S5_GROUP, dtype=F32))
    lag = jnp.arange(t)[None, :] - jnp.arange(t)[:, None]
    toep = jnp.where((lag >= 0)[None, :, :, None, None], kk[:, jnp.clip(lag, 0, t - 1)], 0.0)
    m_mat = jnp.transpose(toep, (0, 1, 4, 2, 3)).reshape(g, t * S5_GROUP, t * S5_GROUP)
    cp = c[:, None] * pw[:, 1:, None, :]
    cpm = jnp.transpose(cp, (0, 3, 1, 2)).reshape(g, P_A, t * S5_GROUP)
    cm = jnp.concatenate([cpm.real, -cpm.imag], axis=1)
    a_t = pw[:, t]
    ar, ai = a_t.real, a_t.imag
    zeros = jnp.zeros_like(ar)
    coef = jnp.stack([jnp.concatenate([ar, ar], -1), jnp.concatenate([-ai, ai], -1),
                      jnp.concatenate([ai, -ai], -1), jnp.concatenate([zeros, zeros], -1)], axis=1)
    return m_mat.astype(BF16), bm.astype(BF16), cm.astype(BF16), coef.astype(F32)


def _online(s, v, e, m_ref, l_ref, acc_ref):
    m_prev = m_ref[e]
    m_new = jnp.maximum(m_prev, jnp.max(s, axis=-1, keepdims=True))
    alpha = jnp.exp(m_prev - m_new)
    p = jnp.exp(s - m_new)
    l_ref[e] = alpha * l_ref[e] + jnp.sum(p, axis=-1, keepdims=True)
    acc_ref[e] = alpha * acc_ref[e] + _dot(p.astype(BF16), v)
    m_ref[e] = m_new


def _prompt_attn_body(*refs, kind, front, tq):
    if kind == "diff":
        par_ref, q_ref, k_ref, v_ref, o_ref, m_ref, l_ref, acc_ref = refs
    elif kind == "fox":
        q_ref, k_ref, v_ref, f_ref, o_ref, m_ref, l_ref, acc_ref = refs
    else:
        q_ref, k_ref, v_ref, o_ref, m_ref, l_ref, acc_ref = refs
    tk = tq
    hg = pl.program_id(1)
    i = pl.program_id(2)
    m_ref[...] = jnp.full(m_ref.shape, NEG, F32)
    l_ref[...] = jnp.zeros(l_ref.shape, F32)
    acc_ref[...] = jnp.zeros(acc_ref.shape, F32)

    q = q_ref[0]
    lane = lax.broadcasted_iota(jnp.int32, (1, 128), 1)
    if kind == "mla":
        qs = [q[:, :HEAD_PAD], q[:, HEAD_PAD:]]
    else:
        qs = [jnp.where(lane < 64, q, jnp.zeros_like(q)), jnp.where(lane >= 64, q, jnp.zeros_like(q))]
    qstart = i * tq
    qpos = qstart + lax.broadcasted_iota(jnp.int32, (tq, 1), 0)
    if kind == "diff":
        slope = par_ref[1 + hg]
    if kind == "fox":
        fref = [f_ref[0, 0, e:e + 1, pl.ds(pl.multiple_of(qstart, 128), 128)][:, 0:1] for e in range(2)]

    def block(j, masked):
        k0 = pl.multiple_of(j * tk, tk)
        k = k_ref[0, pl.ds(k0, tk), :]
        v = v_ref[0, pl.ds(k0, tk), :]
        kpos = k0 + lax.broadcasted_iota(jnp.int32, (1, tk), 1)
        if masked:
            valid = kpos >= front
            if kind == "fox":
                mask = jnp.logical_and(kpos <= qpos, valid)
            else:
                mask = jnp.logical_and((kpos >> 6) <= (qpos >> 6), valid)
        for e in range(2):
            ke = k[:, e * HEAD_PAD:(e + 1) * HEAD_PAD] if kind == "mla" else k
            s = _dot_nt(qs[e], ke)
            if kind == "diff":
                if masked:
                    s = s + slope * ((qpos - qstart) - jnp.abs(qpos - kpos)).astype(F32)
                else:
                    s = s + slope * (kpos - qstart).astype(F32)
            elif kind == "fox":
                s = s + (fref[e] - f_ref[0, 0, e:e + 1, pl.ds(k0, tk)])
            if masked:
                s = jnp.where(mask, s, NEG)
            _online(s, v, e, m_ref, l_ref, acc_ref)

    block(0, True)

    def loop_body(j, c):
        block(j, False)
        return c

    lax.fori_loop(1, i, loop_body, 0)

    @pl.when(i > 0)
    def _():
        block(i, True)

    o0 = acc_ref[0] / l_ref[0]
    o1 = acc_ref[1] / l_ref[1]
    if kind == "diff":
        o_ref[0] = o0 - par_ref[0] * o1
    else:
        o_ref[0] = jnp.where(lane < 64, o0, o1).astype(o_ref.dtype)


def _prompt_attn(kind, q, k, v, front, extra=None):
    bsz, lp, _ = q.shape
    tq = ATT_BLOCK
    wq = 2 * HEAD_PAD if kind == "mla" else 128
    n_hg = v.shape[2] // 128
    nq = lp // tq
    in_specs = [pl.BlockSpec((1, tq, wq), lambda b, h, i: (b, i, h)),
                pl.BlockSpec((1, lp, wq), lambda b, h, i: (b, 0, h)),
                pl.BlockSpec((1, lp, 128), lambda b, h, i: (b, 0, h))]
    args = [q, k, v]
    if kind == "diff":
        in_specs = [pl.BlockSpec(memory_space=pltpu.SMEM)] + in_specs
        args = [extra] + args
    elif kind == "fox":
        in_specs.append(pl.BlockSpec((1, 1, 2, lp), lambda b, h, i: (b, h, 0, 0)))
        args.append(extra.reshape(bsz, n_hg, 2, lp))
    out_dtype = F32 if kind == "diff" else BF16
    body = functools.partial(_prompt_attn_body, kind=kind, front=front, tq=tq)
    return pl.pallas_call(
        body,
        grid=(bsz, n_hg, nq),
        in_specs=in_specs,
        out_specs=pl.BlockSpec((1, tq, 128), lambda b, h, i: (b, i, h)),
        out_shape=jax.ShapeDtypeStruct((bsz, lp, n_hg * 128), out_dtype),
        scratch_shapes=[pltpu.VMEM((2, tq, 1), F32), pltpu.VMEM((2, tq, 1), F32),
                        pltpu.VMEM((2, tq, 128), F32)],
        compiler_params=pltpu.CompilerParams(
            dimension_semantics=("parallel", "parallel", "arbitrary"), vmem_limit_bytes=VMEM_LIMIT),
    )(*args)


def _cumsum_body(x_ref, tri_ref, o_ref, carry_ref):
    @pl.when(pl.program_id(0) == 0)
    def _():
        carry_ref[...] = jnp.zeros(carry_ref.shape, F32)

    y = jnp.dot(x_ref[...], tri_ref[...], preferred_element_type=F32,
                precision=lax.Precision.HIGHEST) + carry_ref[...]
    o_ref[...] = y
    carry_ref[...] = y[:, -1:]


def _cumsum_lanes(x, blk=256):
    rows, n = x.shape
    tri = (jnp.arange(blk)[:, None] <= jnp.arange(blk)[None, :]).astype(F32)
    return pl.pallas_call(
        _cumsum_body,
        grid=(n // blk,),
        in_specs=[pl.BlockSpec((rows, blk), lambda j: (0, j)), _const_spec((blk, blk))],
        out_specs=pl.BlockSpec((rows, blk), lambda j: (0, j)),
        out_shape=jax.ShapeDtypeStruct((rows, n), F32),
        scratch_shapes=[pltpu.VMEM((rows, 1), F32)],
        compiler_params=pltpu.CompilerParams(dimension_semantics=("arbitrary",)),
    )(x, tri)


def _expand_rows(x, rep):
    h, n = x.shape
    return jnp.broadcast_to(x[:, None, :], (h, rep, n)).reshape(h * rep, n)


def _dec_online(s, v16, m_ref, l_ref, acc_ref):
    m_prev = m_ref[...]
    m_new = jnp.maximum(m_prev, jnp.max(s, axis=-1, keepdims=True))
    alpha = jnp.exp(m_prev - m_new)
    p = jnp.exp(s - m_new)
    l_ref[...] = alpha * l_ref[...] + jnp.sum(p, axis=-1, keepdims=True)
    acc_ref[...] = alpha * acc_ref[...] + _dot(p.astype(BF16), v16)
    m_ref[...] = m_new


def _dec_init(m_ref, l_ref, acc_ref):
    m_ref[...] = jnp.full(m_ref.shape, NEG, F32)
    l_ref[...] = jnp.zeros(l_ref.shape, F32)
    acc_ref[...] = jnp.zeros(acc_ref.shape, F32)


def _diag_blocks(o, n_heads, ds, width):
    return jnp.concatenate([o[h * ds:(h + 1) * ds, h * width:(h + 1) * width] for h in range(n_heads)], axis=1)


def _diff_dec_body(par_ref, q_ref, kc_ref, vc_ref, kn_ref, vn_ref, o_ref, m_ref, l_ref, acc_ref,
                   *, past, kb, ds, n_heads):
    jb = pl.program_id(1)
    rows = 2 * n_heads * ds
    r = lax.broadcasted_iota(jnp.int32, (rows, 1), 0)
    head = (r // ds) % n_heads
    slope = jnp.exp2(-8.0 * (head + 1).astype(F32) / n_heads)
    qpos = past + (r % ds)
    q = q_ref[0]

    @pl.when(jb == 0)
    def _():
        _dec_init(m_ref, l_ref, acc_ref)
        kpos = past + lax.broadcasted_iota(jnp.int32, (1, ds), 1)
        s = _dot_nt(q, kn_ref[0]) - slope * jnp.abs(qpos - kpos).astype(F32)
        _dec_online(s, vn_ref[0], m_ref, l_ref, acc_ref)

    kpos = jb * kb + lax.broadcasted_iota(jnp.int32, (1, kb), 1)
    s = _dot_nt(q, kc_ref[0, 0].astype(BF16)) - slope * (qpos - kpos).astype(F32)
    _dec_online(s, vc_ref[0, 0].astype(BF16), m_ref, l_ref, acc_ref)

    @pl.when(jb == pl.num_programs(1) - 1)
    def _():
        o = acc_ref[...] / l_ref[...]
        half = n_heads * ds
        w = o[:half] - par_ref[0] * o[half:]
        o_ref[0] = _diag_blocks(w, n_heads, ds, 2 * DH_B)


def _fox_dec_body(q_ref, kc_ref, vc_ref, kn_ref, vn_ref, fc_ref, fn_ref, o_ref, m_ref, l_ref, acc_ref,
                  *, ds, n_heads):
    jb = pl.program_id(1)
    rows = n_heads * ds
    q = q_ref[0]
    fnew = fn_ref[0][:, :ds]
    fref = _expand_rows(fn_ref[0][:, 0:1], ds)

    @pl.when(jb == 0)
    def _():
        _dec_init(m_ref, l_ref, acc_ref)
        r = lax.broadcasted_iota(jnp.int32, (rows, 1), 0)
        kidx = lax.broadcasted_iota(jnp.int32, (1, ds), 1)
        s = _dot_nt(q, kn_ref[0]) + (fref - _expand_rows(fnew, ds))
        s = jnp.where(kidx <= (r % ds), s, NEG)
        _dec_online(s, vn_ref[0], m_ref, l_ref, acc_ref)

    s = _dot_nt(q, kc_ref[0, 0].astype(BF16)) + (fref - _expand_rows(fc_ref[0], ds))
    _dec_online(s, vc_ref[0, 0].astype(BF16), m_ref, l_ref, acc_ref)

    @pl.when(jb == pl.num_programs(1) - 1)
    def _():
        o = acc_ref[...] / l_ref[...]
        o_ref[0] = _diag_blocks(o, n_heads, ds, DH_C).astype(o_ref.dtype)


def _mla_dec_body(qn_ref, qp_ref, cc_ref, pc_ref, cn_ref, pn_ref, wk_ref, wv_ref, ones_ref,
                  o_ref, m_ref, l_ref, acc_ref, *, ds, n_heads):
    jb = pl.program_id(1)
    qn = qn_ref[0]
    qp = qp_ref[0]
    ones_h = ones_ref[...]

    def key_block(ckv, kpe):
        c16 = ckv.astype(BF16)
        kn = _dot(c16, wk_ref[...])
        v = _dot(c16, wv_ref[...])
        n = kpe.shape[0]
        ss = _dot_nt(ones_h, (kn * kn).astype(BF16)) + _dot_nt(jnp.ones((n_heads, ROPE_D), BF16),
                                                               (kpe * kpe).astype(BF16))
        rinv = lax.rsqrt(ss * (1.0 / (NOPE_D + ROPE_D)) + EPS)
        s = _dot_nt(qn, kn.astype(BF16)) + _dot_nt(qp, kpe.astype(BF16))
        s = s * _expand_rows(rinv, ds)
        _dec_online(s, v.astype(BF16), m_ref, l_ref, acc_ref)

    @pl.when(jb == 0)
    def _():
        _dec_init(m_ref, l_ref, acc_ref)
        key_block(cn_ref[0], pn_ref[0])

    key_block(cc_ref[0, 0], pc_ref[0, 0])

    @pl.when(jb == pl.num_programs(1) - 1)
    def _():
        o = acc_ref[...] / l_ref[...]
        o_ref[0] = _diag_blocks(o, n_heads, ds, V_D).astype(o_ref.dtype)


def _per_seq(a):
    return (a, pl.BlockSpec((1,) + a.shape[1:], lambda b, j: (b, 0, 0)))


def _dec_const(a):
    return (a, _const_spec(a.shape))


def _dec_call(body, ins, out_w, out_dtype, rows, acc_w, nb, n_kb, smem=None):
    in_specs = [spec for _, spec in ins]
    args = [a for a, _ in ins]
    if smem is not None:
        in_specs = [pl.BlockSpec(memory_space=pltpu.SMEM)] + in_specs
        args = [smem] + args
    return pl.pallas_call(
        body,
        grid=(nb, n_kb),
        in_specs=in_specs,
        out_specs=pl.BlockSpec((1, out_w[0], out_w[1]), lambda b, j: (b, 0, 0)),
        out_shape=jax.ShapeDtypeStruct((nb, out_w[0], out_w[1]), out_dtype),
        scratch_shapes=[pltpu.VMEM((rows, 1), F32), pltpu.VMEM((rows, 1), F32), pltpu.VMEM((rows, acc_w), F32)],
        compiler_params=pltpu.CompilerParams(
            dimension_semantics=("parallel", "arbitrary"), vmem_limit_bytes=VMEM_LIMIT),
    )(*args)


def kernel(x_prompt, x_sample, state_s5_re, state_s5_im, cache_diff_k, cache_diff_v, cache_fox_k, cache_fox_v, cache_fox_logf, cache_mla_ckv, cache_mla_kpe, meta_tokens, ffn_norm, ffn_w_in, ffn_w_out, mix_norm, ab_w_in, ab_w_out, s5_a_re, s5_a_im, s5_log_step, s5_b_re, s5_b_im, s5_c_re, s5_c_im, s5_d, s5_glu_w, s5_glu_b, diff_q_norm, diff_k_norm, diff_lam, diff_sub_norm, cd_w_in, cd_w_out, fox_q_norm, fox_k_norm, fox_f_bias, mla_q_a_norm, mla_q_b, mla_kv_a_norm, mla_kv_b, mla_q_norm, mla_k_norm):
    bsz, seq, dm = x_prompt.shape
    nb, ds, _ = x_sample.shape
    n_meta = meta_tokens.shape[0]
    past = cache_diff_k.shape[2]
    front = ROW_ALIGN - n_meta
    lp = front + n_meta + seq
    ltot = n_meta + seq
    assert n_meta + front == ROW_ALIGN and lp % ATT_BLOCK == 0 and front % CHUNK == CHUNK - n_meta
    assert ds == S5_STEP and past % CHUNK == 0 and ds <= CHUNK
    kb = min(DEC_KB, past)
    assert past % kb == 0
    n_kb = past // kb
    assert ffn_norm.shape[0] == 2 and ab_w_in.shape[0] == 1 and cd_w_in.shape[0] == 1

    h_b = cache_diff_k.shape[3]
    h_c = cache_fox_k.shape[3]
    h_d = mla_q_b.shape[2] // (NOPE_D + ROPE_D)
    s5w = s5_glu_w.shape[1]
    n_grp = s5w // S5_GROUP
    qkw = h_b * 2 * DH_B
    fox_w = h_c * DH_C
    q_lora = mla_q_a_norm.shape[1]
    kv_lora = mla_kv_a_norm.shape[1]
    d_qk = NOPE_D + ROPE_D

    meta = jnp.broadcast_to(meta_tokens[None].astype(F32), (bsz, n_meta, dm))
    xp = jnp.concatenate([jnp.zeros((bsz, front, dm), F32), meta, x_prompt], axis=1)
    n_p = bsz * lp
    x = jnp.concatenate([xp.reshape(n_p, dm), x_sample.reshape(nb * ds, dm)], axis=0)
    rows = x.shape[0]
    tm = _row_tile(rows)

    ones64 = _block_diag_ones(64)
    ones128 = _block_diag_ones(128)

    x = _ffn(x, ffn_norm[0, 0], ffn_w_in[0, 0], ffn_w_out[0, 0], tm)

    gq = (jnp.tile(diff_q_norm[0], 2 * h_b) * (DH_B ** -0.5)).reshape(1, qkw)
    gk = jnp.tile(diff_k_norm[0], 2 * h_b).reshape(1, qkw)
    u16, q16, k32, k16, v32, v16 = _row_call(
        functools.partial(_ab_in_body, widths=(s5w, qkw)),
        [x], [mix_norm[0].reshape(1, dm), ab_w_in[0].astype(BF16), gq, gk, ones64],
        [s5w, qkw, qkw, qkw, h_b * 2 * DH_B, h_b * 2 * DH_B], [BF16, BF16, F32, BF16, F32, BF16], tm)

    mats = _s5_matrices(s5_a_re[0], s5_a_im[0], s5_log_step[0], s5_b_re[0], s5_b_im[0],
                        s5_c_re[0], s5_c_im[0], s5_d[0])
    n_ch = lp // S5_STEP
    u_p = u16[:n_p].reshape(bsz, n_ch, S5_STEP, n_grp, S5_GROUP)
    u_p = jnp.transpose(u_p, (3, 1, 0, 2, 4)).reshape(n_grp, n_ch * bsz, S5_STEP * S5_GROUP)
    y_p, st_p = _s5_scan(u_p, jnp.zeros((n_grp, bsz, 4 * P_A), F32), mats, n_ch, bsz,
                         (front + ltot) // S5_STEP - 1)
    y_p = jnp.transpose(y_p.reshape(n_grp, n_ch, bsz, S5_STEP, S5_GROUP), (2, 1, 3, 0, 4)).reshape(n_p, s5w)
    u_s = u16[n_p:].reshape(nb, 1, S5_STEP, n_grp, S5_GROUP)
    u_s = jnp.transpose(u_s, (3, 1, 0, 2, 4)).reshape(n_grp, nb, S5_STEP * S5_GROUP)
    h_re = jnp.transpose(state_s5_re[0].astype(F32), (1, 0, 2))
    h_im = jnp.transpose(state_s5_im[0].astype(F32), (1, 0, 2))
    y_s, st_s = _s5_scan(u_s, jnp.concatenate([h_re, h_im, h_im, h_re], axis=-1), mats, 1, nb, 0)
    y_s = jnp.transpose(y_s.reshape(n_grp, 1, nb, S5_STEP, S5_GROUP), (2, 1, 3, 0, 4)).reshape(nb * ds, s5w)
    y_all = jnp.concatenate([y_p, y_s], axis=0)

    lv = diff_lam[0].astype(F32)
    lam_init = 0.8 - 0.6 * math.exp(-0.3 * 0)
    lam = jnp.exp(jnp.sum(lv[0] * lv[1])) - jnp.exp(jnp.sum(lv[2] * lv[3])) + lam_init
    slopes = jnp.exp2(-8.0 * jnp.arange(1, h_b + 1, dtype=F32) / h_b)
    par = jnp.concatenate([lam[None], slopes]).astype(F32)
    o_p = _prompt_attn("diff", q16[:n_p].reshape(bsz, lp, qkw), k16[:n_p].reshape(bsz, lp, qkw),
                       v16[:n_p].reshape(bsz, lp, qkw), front, par)
    qs = q16[n_p:].reshape(nb, ds, h_b, 2, DH_B)
    eye_h = jnp.eye(h_b, dtype=BF16)
    eye_2 = jnp.eye(2, dtype=BF16)
    qbd = jnp.einsum('bqhmd,hH,mM->bmhqHMd', qs, eye_h, eye_2).reshape(nb, 2 * h_b * ds, qkw)
    kc = cache_diff_k.reshape(cache_diff_k.shape[0], nb, past, qkw)
    vc = cache_diff_v.reshape(cache_diff_v.shape[0], nb, past, qkw)
    cache_spec = lambda w: pl.BlockSpec((1, 1, kb, w), lambda b, j: (0, b, j, 0))
    o_s = _dec_call(
        functools.partial(_diff_dec_body, past=past, kb=kb, ds=ds, n_heads=h_b),
        [_per_seq(qbd), (kc, cache_spec(qkw)), (vc, cache_spec(qkw)),
         _per_seq(k16[n_p:].reshape(nb, ds, qkw)), _per_seq(v16[n_p:].reshape(nb, ds, qkw))],
        (ds, qkw), F32, 2 * h_b * ds, qkw, nb, n_kb, smem=par)
    o_all = jnp.concatenate([o_p.reshape(n_p, qkw), o_s.reshape(nb * ds, qkw)], axis=0)

    subg = (jnp.tile(diff_sub_norm[0], h_b) * (1.0 - lam_init)).reshape(1, qkw)
    x = _row_call(
        functools.partial(_ab_out_body, s5w=s5w),
        [x, y_all, o_all],
        [s5_glu_w[0].astype(BF16), s5_glu_b[0].reshape(1, s5w), subg, ab_w_out[0].astype(BF16), ones128],
        [dm], [F32], tm)[0]

    x = _ffn(x, ffn_norm[0, 1], ffn_w_in[0, 1], ffn_w_out[0, 1], tm)

    x = _ffn(x, ffn_norm[1, 0], ffn_w_in[1, 0], ffn_w_out[1, 0], tm)

    half = ROPE_D // 2
    inv = ROPE_THETA ** (-jnp.arange(half, dtype=F32) / half)
    pos = jnp.concatenate([jnp.tile(jnp.arange(lp, dtype=jnp.int32) - front, bsz),
                           jnp.tile(past + jnp.arange(ds, dtype=jnp.int32), nb)]).astype(F32)
    ang = pos[:, None] * inv[None, :]
    pad_r = HEAD_PAD - NOPE_D - ROPE_D
    cos_t = jnp.concatenate([jnp.ones((rows, NOPE_D), F32), jnp.cos(ang), jnp.cos(ang),
                             jnp.zeros((rows, pad_r), F32)], axis=1)
    sin_t = jnp.concatenate([jnp.zeros((rows, NOPE_D), F32), jnp.sin(ang), jnp.sin(ang),
                             jnp.zeros((rows, pad_r), F32)], axis=1)

    wcd = cd_w_in[0]
    c_fg = 3 * fox_w
    c_qa = c_fg + h_c
    c_kva = c_qa + q_lora
    c_pe = c_kva + kv_lora
    w_pe = wcd[:, c_pe:c_pe + ROPE_D]
    zc = lambda n: jnp.zeros((dm, n), F32)
    w_cd = jnp.concatenate([
        wcd[:, :3 * fox_w], wcd[:, c_qa:c_qa + q_lora], wcd[:, c_kva:c_kva + kv_lora],
        zc(NOPE_D), w_pe, zc(pad_r),
        zc(NOPE_D), -w_pe[:, half:], w_pe[:, :half], zc(pad_r),
        wcd[:, c_fg:c_fg + h_c], zc(HEAD_PAD - h_c)], axis=1).astype(BF16)
    qb = mla_q_b[0].reshape(q_lora, h_d, d_qk)
    zq = lambda n: jnp.zeros((q_lora, h_d, n), F32)
    qb_pad = jnp.concatenate([qb, zq(pad_r)], axis=-1).reshape(q_lora, h_d * HEAD_PAD)
    qb_rot = jnp.concatenate([zq(NOPE_D), -qb[..., NOPE_D + half:], qb[..., NOPE_D:NOPE_D + half], zq(pad_r)],
                             axis=-1).reshape(q_lora, h_d * HEAD_PAD)
    wq2 = jnp.concatenate([qb_pad, qb_rot], axis=1).astype(BF16)
    kvb = mla_kv_b[0].reshape(kv_lora, h_d, NOPE_D + V_D)
    wk_pad = jnp.concatenate([kvb[..., :NOPE_D], jnp.zeros((kv_lora, h_d, HEAD_PAD - NOPE_D), F32)],
                             axis=-1).reshape(kv_lora, h_d * HEAD_PAD).astype(BF16)
    wk_cmp = kvb[..., :NOPE_D].reshape(kv_lora, h_d * NOPE_D).astype(BF16)
    wv_cmp = kvb[..., NOPE_D:].reshape(kv_lora, h_d * V_D).astype(BF16)
    gfq = (jnp.tile(fox_q_norm[0], h_c) * (DH_C ** -0.5)).reshape(1, fox_w)
    gfk = jnp.tile(fox_k_norm[0], h_c).reshape(1, fox_w)
    fbias = jnp.concatenate([fox_f_bias[0], jnp.zeros((HEAD_PAD - h_c,), F32)]).reshape(1, HEAD_PAD)
    gmq = jnp.tile(jnp.concatenate([mla_q_norm[0] * mla_k_norm[0] * (d_qk ** -0.5), jnp.zeros((pad_r,), F32)]),
                   h_d).reshape(1, h_d * HEAD_PAD)

    (fq16, fk32, fk16, fv32, fv16, logf, qm16, ckv32, kpe32, km16, vm16) = _row_call(
        functools.partial(_cd_in_body, fox_w=fox_w, q_lora=q_lora, kv_lora=kv_lora, n_heads=h_d),
        [x, cos_t, sin_t],
        [mix_norm[1].reshape(1, dm), w_cd, gfq, gfk, fbias, mla_q_a_norm[0].reshape(1, q_lora), wq2,
         mla_kv_a_norm[0].reshape(1, kv_lora), wk_pad, wv_cmp, gmq, ones64, ones128],
        [fox_w, fox_w, fox_w, fox_w, fox_w, h_c, h_d * HEAD_PAD, kv_lora, ROPE_D, h_d * HEAD_PAD, h_d * V_D],
        [BF16, F32, BF16, F32, BF16, F32, BF16, F32, F32, BF16, BF16], tm)

    logf_p = jnp.transpose(logf[:n_p].reshape(bsz, lp, h_c), (0, 2, 1)).reshape(bsz * h_c, lp)
    f_p = _cumsum_lanes(logf_p).reshape(bsz, h_c, lp)
    logf_s = jnp.concatenate([
        jnp.transpose(cache_fox_logf[0].astype(F32), (0, 2, 1)),
        jnp.transpose(logf[n_p:].reshape(nb, ds, h_c), (0, 2, 1)),
        jnp.zeros((nb, h_c, 256 - ds), F32)], axis=2).reshape(nb * h_c, past + 256)
    f_s = _cumsum_lanes(logf_s).reshape(nb, h_c, past + 256)

    oc_p = _prompt_attn("fox", fq16[:n_p].reshape(bsz, lp, fox_w), fk16[:n_p].reshape(bsz, lp, fox_w),
                        fv16[:n_p].reshape(bsz, lp, fox_w), front, f_p)
    od_p = _prompt_attn("mla", qm16[:n_p].reshape(bsz, lp, h_d * HEAD_PAD),
                        km16[:n_p].reshape(bsz, lp, h_d * HEAD_PAD),
                        vm16[:n_p].reshape(bsz, lp, h_d * V_D), front)

    eye_c = jnp.eye(h_c, dtype=BF16)
    fqs = fq16[n_p:].reshape(nb, ds, h_c, DH_C)
    fq_bd = jnp.einsum('bqhd,hH->bhqHd', fqs, eye_c).reshape(nb, h_c * ds, fox_w)
    fkc = cache_fox_k.reshape(cache_fox_k.shape[0], nb, past, fox_w)
    fvc = cache_fox_v.reshape(cache_fox_v.shape[0], nb, past, fox_w)
    oc_s = _dec_call(
        functools.partial(_fox_dec_body, ds=ds, n_heads=h_c),
        [_per_seq(fq_bd), (fkc, cache_spec(fox_w)), (fvc, cache_spec(fox_w)),
         _per_seq(fk16[n_p:].reshape(nb, ds, fox_w)), _per_seq(fv16[n_p:].reshape(nb, ds, fox_w)),
         (f_s, pl.BlockSpec((1, h_c, kb), lambda b, j: (b, 0, j))),
         (f_s, pl.BlockSpec((1, h_c, 128), lambda b, j: (b, 0, past // 128)))],
        (ds, fox_w), BF16, h_c * ds, fox_w, nb, n_kb)

    eye_d = jnp.eye(h_d, dtype=BF16)
    qms = qm16[n_p:].reshape(nb, ds, h_d, HEAD_PAD)
    qn_bd = jnp.einsum('bqhd,hH->bhqHd', qms[..., :NOPE_D], eye_d).reshape(nb, h_d * ds, h_d * NOPE_D)
    qp_s = jnp.transpose(qms[..., NOPE_D:NOPE_D + ROPE_D], (0, 2, 1, 3)).reshape(nb, h_d * ds, ROPE_D)
    ones_h = jnp.repeat(jnp.eye(h_d, dtype=BF16), NOPE_D, axis=1)
    od_s = _dec_call(
        functools.partial(_mla_dec_body, ds=ds, n_heads=h_d),
        [_per_seq(qn_bd), _per_seq(qp_s),
         (cache_mla_ckv, pl.BlockSpec((1, 1, kb, kv_lora), lambda b, j: (0, b, j, 0))),
         (cache_mla_kpe, pl.BlockSpec((1, 1, kb, ROPE_D), lambda b, j: (0, b, j, 0))),
         _per_seq(ckv32[n_p:].reshape(nb, ds, kv_lora)), _per_seq(kpe32[n_p:].reshape(nb, ds, ROPE_D)),
         _dec_const(wk_cmp), _dec_const(wv_cmp), _dec_const(ones_h)],
        (ds, h_d * V_D), BF16, h_d * ds, h_d * V_D, nb, n_kb)

    oc_all = jnp.concatenate([oc_p.reshape(n_p, fox_w), oc_s.reshape(nb * ds, fox_w)], axis=0)
    od_all = jnp.concatenate([od_p.reshape(n_p, h_d * V_D), od_s.reshape(nb * ds, h_d * V_D)], axis=0)
    x = _row_call(functools.partial(_cd_out_body, fox_w=fox_w), [x, oc_all, od_all],
                  [cd_w_out[0].astype(BF16)], [dm], [F32], tm)[0]

    x = _ffn(x, ffn_norm[1, 1], ffn_w_in[1, 1], ffn_w_out[1, 1], tm)

    def p_rows(a, shape):
        w = a.shape[1]
        return a[:n_p].reshape(bsz, lp, w)[:, front:front + ltot].reshape((1, bsz, ltot) + shape)

    def s_rows(a, shape):
        return a[n_p:].reshape((1, nb, ds) + shape)

    def s5_state(st):
        st = jnp.transpose(st, (1, 0, 2))
        return st[None, :, :, :P_A], st[None, :, :, P_A:]

    y_prompt = x[:n_p].reshape(bsz, lp, dm)[:, front + n_meta:]
    y_sample = x[n_p:].reshape(nb, ds, dm)
    s5_re_p, s5_im_p = s5_state(st_p)
    s5_re_s, s5_im_s = s5_state(st_s)
    return (y_prompt, y_sample,
            s5_re_p, s5_im_p, p_rows(k32, (h_b, 2 * DH_B)), p_rows(v32, (h_b, 2 * DH_B)),
            p_rows(fk32, (h_c, DH_C)), p_rows(fv32, (h_c, DH_C)), p_rows(logf, (h_c,)),
            p_rows(ckv32, (kv_lora,)), p_rows(kpe32, (ROPE_D,)),
            s5_re_s, s5_im_s, s_rows(k32, (h_b, 2 * DH_B)), s_rows(v32, (h_b, 2 * DH_B)),
            s_rows(fk32, (h_c, DH_C)), s_rows(fv32, (h_c, DH_C)), s_rows(logf, (h_c,)),
            s_rows(ckv32, (kv_lora,)), s_rows(kpe32, (ROPE_D,)))
```

```python
import functools
import math

import jax
import jax.numpy as jnp
from jax import lax
from jax.experimental import pallas as pl
from jax.experimental.pallas import tpu as pltpu

F32 = jnp.float32
BF16 = jnp.bfloat16

EPS = 1e-6
CHUNK = 64
ROW_ALIGN = 256
S5_GROUP = 16
S5_STEP = 16
P_A = 64
DH_B = 64
DH_C = 64
NOPE_D = 64
ROPE_D = 32
V_D = 64
HEAD_PAD = 128
ROPE_THETA = 10000.0
NEG = -1e30
LOG2E = math.log2(math.e)
VMEM_LIMIT = 56 * 1024 * 1024
ATT_BLOCK = 256
DEC_KB = 1024


def _dot(a, b):
    return jnp.dot(a, b, preferred_element_type=F32)


def _dot_nt(a, b):
    return lax.dot_general(a, b, (((1,), (1,)), ((), ())), preferred_element_type=F32)


def _rms_rows(x, g):
    ms = jnp.mean(x * x, axis=-1, keepdims=True)
    return x * lax.rsqrt(ms + EPS) * g


def _group_sumsq(x, ones_bd):
    w = x.shape[-1]
    parts = [_dot((x[:, c:c + 256] * x[:, c:c + 256]).astype(BF16), ones_bd) for c in range(0, w, 256)]
    return parts[0] if len(parts) == 1 else jnp.concatenate(parts, axis=1)


def _block_diag_ones(group, n=256):
    r = jnp.arange(n) // group
    return (r[:, None] == r[None, :]).astype(BF16)


def _const_spec(shape):
    nd = len(shape)
    return pl.BlockSpec(shape, lambda *_: (0,) * nd, pipeline_mode=pl.Buffered(1))


def _row_tile(rows, cap=512):
    t = cap
    while rows % t:
        t //= 2
    return t


def _row_call(body, row_ins, consts, out_widths, out_dtypes, tm):
    rows = row_ins[0].shape[0]
    in_specs = [pl.BlockSpec((tm, a.shape[1]), lambda i: (i, 0)) for a in row_ins]
    in_specs += [_const_spec(c.shape) for c in consts]
    out_specs = [pl.BlockSpec((tm, w), lambda i: (i, 0)) for w in out_widths]
    out_shape = [jax.ShapeDtypeStruct((rows, w), d) for w, d in zip(out_widths, out_dtypes)]
    return pl.pallas_call(
        body,
        grid=(rows // tm,),
        in_specs=in_specs,
        out_specs=out_specs,
        out_shape=out_shape,
        compiler_params=pltpu.CompilerParams(
            dimension_semantics=("parallel",), vmem_limit_bytes=VMEM_LIMIT),
    )(*row_ins, *consts)


def _ffn_body(x_ref, g_ref, win_ref, wout_ref, o_ref, *, d_ff, tf):
    x = x_ref[...]
    xn = _rms_rows(x, g_ref[...]).astype(BF16)
    acc = jnp.zeros(x.shape, F32)
    for c in range(0, d_ff, tf):
        gate = _dot(xn, win_ref[:, c:c + tf])
        up = _dot(xn, win_ref[:, d_ff + c:d_ff + c + tf])
        a = (gate * jax.nn.sigmoid(gate) * up).astype(BF16)
        acc = acc + _dot(a, wout_ref[c:c + tf, :])
    o_ref[...] = x + 0.5 * acc


def _ffn(x, g, w_in, w_out, tm):
    d_ff = w_out.shape[0]
    body = functools.partial(_ffn_body, d_ff=d_ff, tf=256)
    return _row_call(body, [x], [g.reshape(1, -1), w_in.astype(BF16), w_out.astype(BF16)],
                     [x.shape[1]], [F32], tm)[0]


def _ab_in_body(x_ref, g_ref, w_ref, gq_ref, gk_ref, ones_ref,
                u_ref, q_ref, k32_ref, k16_ref, v32_ref, v16_ref, *, widths):
    s5w, qkw = widths
    xn = _rms_rows(x_ref[...], g_ref[...]).astype(BF16)
    h = _dot(xn, w_ref[...])
    u_ref[...] = h[:, :s5w].astype(BF16)
    q = h[:, s5w:s5w + qkw]
    k = h[:, s5w + qkw:s5w + 2 * qkw]
    v = h[:, s5w + 2 * qkw:]
    ones_bd = ones_ref[...]
    qn = q * lax.rsqrt(_group_sumsq(q, ones_bd) * (1.0 / DH_B) + EPS) * gq_ref[...]
    kn = k * lax.rsqrt(_group_sumsq(k, ones_bd) * (1.0 / DH_B) + EPS) * gk_ref[...]
    q_ref[...] = qn.astype(BF16)
    k32_ref[...] = kn
    k16_ref[...] = kn.astype(BF16)
    v32_ref[...] = v
    v16_ref[...] = v.astype(BF16)


def _ab_out_body(x_ref, y_ref, o_ref, gluw_ref, glub_ref, wout_ref, out_ref, *, s5w):
    y = y_ref[...]
    g = 0.5 * y * (1.0 + jnp.tanh(math.sqrt(2.0 / math.pi) * (y + 0.044715 * (y * y * y))))
    z = _dot(g.astype(BF16), gluw_ref[...]) + glub_ref[...]
    s5o = g * jax.nn.sigmoid(z)
    m = _dot(s5o.astype(BF16), wout_ref[:s5w, :]) + _dot(o_ref[...], wout_ref[s5w:, :])
    out_ref[...] = x_ref[...] + m


def _heads_with_ones(v):
    lane = lax.broadcasted_iota(jnp.int32, (1, 128), 1)
    outs = []
    for c in range(0, v.shape[1], 128):
        col = v[:, c:c + 128]
        outs.append(jnp.where(lane < 64, col, 1.0))
        outs.append(jnp.where(lane < 64, pltpu.roll(col, 64, axis=1), 1.0))
    return jnp.concatenate(outs, axis=1)


def _cd_in_body(x_ref, cos_ref, sin_ref, g_ref, w_ref, gfq_ref, gfk_ref, fb_ref, gqa_ref, wq2_ref,
                gkva_ref, wk_ref, wv_ref, gmq_ref, ones64_ref, ones128_ref,
                fq_ref, fk32_ref, fk16_ref, fv32_ref, fv16_ref, logf_ref, qm_ref, ckv_ref, kpe_ref,
                km_ref, vm_ref, *, fox_w, q_lora, kv_lora, n_heads):
    xn = _rms_rows(x_ref[...], g_ref[...]).astype(BF16)
    h = _dot(xn, w_ref[...])
    ones64 = ones64_ref[...]
    ones128 = ones128_ref[...]
    fq = h[:, :fox_w]
    fk = h[:, fox_w:2 * fox_w]
    fv = h[:, 2 * fox_w:3 * fox_w]
    c0 = 3 * fox_w
    qa = h[:, c0:c0 + q_lora]
    kva = h[:, c0 + q_lora:c0 + q_lora + kv_lora]
    c1 = c0 + q_lora + kv_lora
    pe_a = h[:, c1:c1 + HEAD_PAD]
    pe_b = h[:, c1 + HEAD_PAD:c1 + 2 * HEAD_PAD]
    fg = h[:, c1 + 2 * HEAD_PAD:c1 + 3 * HEAD_PAD]

    fqn = fq * lax.rsqrt(_group_sumsq(fq, ones64) * (1.0 / DH_C) + EPS) * gfq_ref[...]
    fkn = fk * lax.rsqrt(_group_sumsq(fk, ones64) * (1.0 / DH_C) + EPS) * gfk_ref[...]
    fq_ref[...] = fqn.astype(BF16)
    fk32_ref[...] = fkn
    fk16_ref[...] = fkn.astype(BF16)
    fv32_ref[...] = fv
    fv16_ref[...] = _heads_with_ones(fv).astype(BF16)

    z = fg + fb_ref[...]
    logf = jnp.minimum(z, 0.0) - jnp.log1p(jnp.exp(-jnp.abs(z)))
    logf_ref[...] = logf[:, :logf_ref.shape[1]]

    cos = cos_ref[...]
    sin = sin_ref[...]
    qan = _rms_rows(qa, gqa_ref[...]).astype(BF16)
    q2 = _dot(qan, wq2_ref[...])
    hw = n_heads * HEAD_PAD
    cos_t = jnp.concatenate([cos] * n_heads, axis=1)
    sin_t = jnp.concatenate([sin] * n_heads, axis=1)
    qr = q2[:, :hw] * cos_t + q2[:, hw:] * sin_t
    d_qk = NOPE_D + ROPE_D
    qm = qr * lax.rsqrt(_group_sumsq(qr, ones128) * (1.0 / d_qk) + EPS) * gmq_ref[...]
    qm_ref[...] = qm.astype(BF16)

    ckv = _rms_rows(kva, gkva_ref[...])
    ckv_ref[...] = ckv
    pe = pe_a * cos + pe_b * sin
    kpe_ref[...] = pe[:, NOPE_D:NOPE_D + ROPE_D]
    ckv16 = ckv.astype(BF16)
    kraw = _dot(ckv16, wk_ref[...]) + jnp.concatenate([pe] * n_heads, axis=1)
    km = kraw * lax.rsqrt(_group_sumsq(kraw, ones128) * (1.0 / d_qk) + EPS)
    km_ref[...] = km.astype(BF16)
    vm_ref[...] = _heads_with_ones(_dot(ckv16, wv_ref[...])).astype(BF16)


def _cd_out_body(x_ref, oc_ref, od_ref, wout_ref, out_ref, *, fox_w):
    m = _dot(oc_ref[...], wout_ref[:fox_w, :]) + _dot(od_ref[...], wout_ref[fox_w:, :])
    out_ref[...] = x_ref[...] + m


def _s5_body(u_ref, h0_ref, m_ref, bm_ref, cm_ref, coef_ref, y_ref, st_ref, s2_ref, hp_ref,
             *, n_chunks, bsz, j_last):
    u = u_ref[0]
    s2_ref[...] = _dot(u, bm_ref[0])
    c1 = coef_ref[0, 0:1, :]
    c2 = coef_ref[0, 1:2, :]
    c3 = coef_ref[0, 2:3, :]
    half = 2 * P_A

    def step(j, carry):
        ha, hb = carry
        r = pl.multiple_of(j * bsz, bsz)
        hp_ref[pl.ds(r, bsz), :] = ha
        s = s2_ref[pl.ds(r, bsz), :]
        ha2 = ha * c1 + hb * c2 + s[:, :half]
        hb2 = hb * c1 + ha * c3 + s[:, half:]

        @pl.when(j == j_last)
        def _():
            st_ref[0] = ha2

        return ha2, hb2

    h0 = h0_ref[0]
    lax.fori_loop(0, n_chunks, step, (h0[:, :half], h0[:, half:]))
    y_ref[0] = _dot(u, m_ref[0]) + _dot(hp_ref[...].astype(BF16), cm_ref[0])


def _s5_scan(u_t, h0, mats, n_chunks, bsz, j_last):
    m_mat, bm, cm, coef = mats
    g = u_t.shape[0]
    rows = n_chunks * bsz
    w = S5_STEP * S5_GROUP
    body = functools.partial(_s5_body, n_chunks=n_chunks, bsz=bsz, j_last=j_last)
    per_g = lambda a: pl.BlockSpec((1,) + a.shape[1:], lambda i: (i, 0, 0))
    return pl.pallas_call(
        body,
        grid=(g,),
        in_specs=[per_g(u_t), per_g(h0), per_g(m_mat), per_g(bm), per_g(cm), per_g(coef)],
        out_specs=[pl.BlockSpec((1, rows, w), lambda i: (i, 0, 0)),
                   pl.BlockSpec((1, bsz, 2 * P_A), lambda i: (i, 0, 0))],
        out_shape=[jax.ShapeDtypeStruct((g, rows, w), F32),
                   jax.ShapeDtypeStruct((g, bsz, 2 * P_A), F32)],
        scratch_shapes=[pltpu.VMEM((rows, 4 * P_A), F32), pltpu.VMEM((rows, 2 * P_A), F32)],
        compiler_params=pltpu.CompilerParams(
            dimension_semantics=("parallel",), vmem_limit_bytes=VMEM_LIMIT),
    )(u_t, h0, m_mat, bm, cm, coef)


def _s5_matrices(a_re, a_im, log_step, b_re, b_im, c_re, c_im, d):
    g = a_re.shape[0]
    t = S5_STEP
    lam = lax.complex(a_re, a_im)
    dl = lam * jnp.exp(log_step)[:, None]
    lam_bar = jnp.exp(dl)
    b_bar = ((lam_bar - 1.0) / lam)[..., None] * lax.complex(b_re, b_im)
    c = lax.complex(c_re, c_im)
    pw = jnp.exp(dl[:, None, :] * jnp.arange(t + 1, dtype=F32)[None, :, None])
    bmc = pw[:, t - 1::-1][:, :, :, None] * b_bar[:, None]
    bmc = jnp.swapaxes(bmc, 2, 3).reshape(g, t * S5_GROUP, P_A)
    bm = jnp.concatenate([bmc.real, bmc.imag, bmc.imag, bmc.real], axis=-1)
    kk = jnp.einsum('gcp,gkp,gpd->gkcd', c, pw[:, :t], b_bar).real
    kk = kk.at[:, 0].add(d.reshape(g, S5_GROUP)[:, :, None] * jnp.eye(S5_GROUP, dtype=F32))
    lag = jnp.arange(t)[None, :] - jnp.arange(t)[:, None]
    toep = jnp.where((lag >= 0)[None, :, :, None, None], kk[:, jnp.clip(lag, 0, t - 1)], 0.0)
    m_mat = jnp.transpose(toep, (0, 1, 4, 2, 3)).reshape(g, t * S5_GROUP, t * S5_GROUP)
    cp = c[:, None] * pw[:, 1:, None, :]
    cpm = jnp.transpose(cp, (0, 3, 1, 2)).reshape(g, P_A, t * S5_GROUP)
    cm = jnp.concatenate([cpm.real, -cpm.imag], axis=1)
    a_t = pw[:, t]
    ar, ai = a_t.real, a_t.imag
    zeros = jnp.zeros_like(ar)
    coef = jnp.stack([jnp.concatenate([ar, ar], -1), jnp.concatenate([-ai, ai], -1),
                      jnp.concatenate([ai, -ai], -1), jnp.concatenate([zeros, zeros], -1)], axis=1)
    return m_mat.astype(BF16), bm.astype(BF16), cm.astype(BF16), coef.astype(F32)


def _online(s, v, e, m_ref, l_ref, acc_ref, sum_in_v):
    m_prev = m_ref[e]
    m_new = jnp.maximum(m_prev, jnp.max(s, axis=-1, keepdims=True))
    alpha = jnp.exp2(m_prev - m_new)
    p = jnp.exp2(s - jnp.concatenate([m_new] * (s.shape[1] // 128), axis=1))
    if not sum_in_v:
        l_ref[e] = alpha * l_ref[e] + jnp.sum(p, axis=-1, keepdims=True)
    acc_ref[e] = alpha * acc_ref[e] + _dot(p.astype(BF16), v)
    m_ref[e] = m_new


def _prompt_attn_body(*refs, kind, front, tq):
    if kind == "diff":
        par_ref, q_ref, k_ref, v_ref, kb_ref, g_ref, o_ref, m_ref, l_ref, acc_ref, s_ref = refs
    else:
        q_ref, k_ref, v_ref, kb_ref, o_ref, m_ref, l_ref, acc_ref, s_ref = refs
    sum_in_v = kind != "diff"
    hg = pl.program_id(1)
    i = pl.program_id(2)
    m_ref[...] = jnp.full(m_ref.shape, NEG, F32)
    l_ref[...] = jnp.zeros(l_ref.shape, F32)
    acc_ref[...] = jnp.zeros(acc_ref.shape, F32)

    q = q_ref[0]
    lane = lax.broadcasted_iota(jnp.int32, (1, 128), 1)
    if kind == "mla":
        qs = [q[:, :HEAD_PAD], q[:, HEAD_PAD:]]
    else:
        qs = [jnp.where(lane < 64, q, jnp.zeros_like(q)), jnp.where(lane >= 64, q, jnp.zeros_like(q))]
    qstart = pl.multiple_of(i * tq, tq)
    qend = qstart + (tq - 1)
    qpos = qstart + lax.broadcasted_iota(jnp.int32, (tq, 1), 0)
    ref = [-kb_ref[0, 0, e:e + 1, pl.ds(qstart + (tq - 128), 128)][:, 127:128] for e in range(2)]
    if kind == "diff":
        slope = LOG2E * par_ref[1 + hg]

    tk = tq

    def scores(j, slot):
        k = k_ref[0, pl.ds(pl.multiple_of(j * tk, tk), tk), :]
        for e in range(2):
            ke = k[:, e * HEAD_PAD:(e + 1) * HEAD_PAD] if kind == "mla" else k
            s_ref[slot, e] = _dot_nt(qs[e], ke)

    def softmax_pv(j, slot, diag):
        k0 = pl.multiple_of(j * tk, tk)
        v = v_ref[0, pl.ds(k0, tk), :]
        if diag:
            kpos = k0 + lax.broadcasted_iota(jnp.int32, (1, tk), 1)
            valid = kpos >= front
            if kind == "fox":
                mask = jnp.logical_and(kpos <= qpos, valid)
            else:
                mask = jnp.logical_and((kpos >> 6) <= (qpos >> 6), valid)
        for e in range(2):
            ve = v[:, e * 128:(e + 1) * 128] if sum_in_v else v
            s = s_ref[slot, e]
            if diag and kind == "diff":
                s = s + slope * ((qpos - qend) - jnp.abs(qpos - kpos)).astype(F32)
            elif not (diag and kind == "mla"):
                s = s + (kb_ref[0, 0, e:e + 1, pl.ds(k0, tk)] + ref[e])
            if diag:
                s = jnp.where(mask, s, NEG)
            _online(s, ve, e, m_ref, l_ref, acc_ref, sum_in_v)

    scores(0, 0)

    def pair_body(jj, c):
        j = 2 * jj
        scores(j + 1, 1)
        softmax_pv(j, 0, False)
        scores(j + 2, 0)
        softmax_pv(j + 1, 1, False)
        return c

    lax.fori_loop(0, i // 2, pair_body, 0)

    @pl.when(i % 2 == 0)
    def _():
        softmax_pv(i, 0, True)

    @pl.when(i % 2 == 1)
    def _():
        scores(i, 1)
        softmax_pv(i - 1, 0, False)
        softmax_pv(i, 1, True)

    if kind == "diff":
        o = acc_ref[0] / l_ref[0] - par_ref[0] * (acc_ref[1] / l_ref[1])
        ms = jnp.mean(o * o, axis=-1, keepdims=True)
        o_ref[0] = (o * lax.rsqrt(ms + EPS) * g_ref[...]).astype(o_ref.dtype)
    else:
        r0 = acc_ref[0] / pltpu.roll(acc_ref[0], 64, axis=1)
        r1 = acc_ref[1] / pltpu.roll(acc_ref[1], 64, axis=1)
        o_ref[0] = jnp.where(lane < 64, r0, pltpu.roll(r1, 64, axis=1)).astype(o_ref.dtype)


def _prompt_attn(kind, q, k, v, kb, front, par=None, gain=None):
    bsz, lp, _ = q.shape
    tq = ATT_BLOCK
    wq = 2 * HEAD_PAD if kind == "mla" else 128
    wv = 128 if kind == "diff" else 256
    n_hg = v.shape[2] // wv
    nq = lp // tq
    kb_b, kb_h = kb.shape[0] > 1, kb.shape[1] > 1
    in_specs = [pl.BlockSpec((1, tq, wq), lambda b, h, i: (b, i, h)),
                pl.BlockSpec((1, lp, wq), lambda b, h, i: (b, 0, h)),
                pl.BlockSpec((1, lp, wv), lambda b, h, i: (b, 0, h)),
                pl.BlockSpec((1, 1, 2, lp), lambda b, h, i: (b if kb_b else 0, h if kb_h else 0, 0, 0))]
    args = [q, k, v, kb]
    if kind == "diff":
        in_specs = [pl.BlockSpec(memory_space=pltpu.SMEM)] + in_specs + [pl.BlockSpec((1, 128), lambda b, h, i: (0, h))]
        args = [par] + args + [gain]
    body = functools.partial(_prompt_attn_body, kind=kind, front=front, tq=tq)
    return pl.pallas_call(
        body,
        grid=(bsz, n_hg, nq),
        in_specs=in_specs,
        out_specs=pl.BlockSpec((1, tq, 128), lambda b, h, i: (b, i, h)),
        out_shape=jax.ShapeDtypeStruct((bsz, lp, n_hg * 128), BF16),
        scratch_shapes=[pltpu.VMEM((2, tq, 128), F32), pltpu.VMEM((2, tq, 128), F32),
                        pltpu.VMEM((2, tq, 128), F32), pltpu.VMEM((2, 2, tq, tq), F32)],
        compiler_params=pltpu.CompilerParams(
            dimension_semantics=("parallel", "parallel", "arbitrary"), vmem_limit_bytes=VMEM_LIMIT),
    )(*args)


def _cumsum_body(x_ref, tri_ref, o_ref, carry_ref):
    @pl.when(pl.program_id(0) == 0)
    def _():
        carry_ref[...] = jnp.zeros(carry_ref.shape, F32)

    y = jnp.dot(x_ref[...], tri_ref[...], preferred_element_type=F32,
                precision=lax.Precision.HIGHEST) + carry_ref[...]
    o_ref[...] = y
    carry_ref[...] = y[:, -1:]


def _cumsum_lanes(x, blk=256):
    rows, n = x.shape
    tri = (jnp.arange(blk)[:, None] <= jnp.arange(blk)[None, :]).astype(F32)
    return pl.pallas_call(
        _cumsum_body,
        grid=(n // blk,),
        in_specs=[pl.BlockSpec((rows, blk), lambda j: (0, j)), _const_spec((blk, blk))],
        out_specs=pl.BlockSpec((rows, blk), lambda j: (0, j)),
        out_shape=jax.ShapeDtypeStruct((rows, n), F32),
        scratch_shapes=[pltpu.VMEM((rows, 1), F32)],
        compiler_params=pltpu.CompilerParams(dimension_semantics=("arbitrary",)),
    )(x, tri)


def _expand_rows(x, rep):
    h, n = x.shape
    return jnp.broadcast_to(x[:, None, :], (h, rep, n)).reshape(h * rep, n)


def _dec_online(s, v16, m_ref, l_ref, acc_ref):
    m_prev = m_ref[...]
    m_new = jnp.maximum(m_prev, jnp.max(s, axis=-1, keepdims=True))
    alpha = jnp.exp2(m_prev - m_new)
    p = jnp.exp2(s - m_new)
    l_ref[...] = alpha * l_ref[...] + jnp.sum(p, axis=-1, keepdims=True)
    acc_ref[...] = alpha * acc_ref[...] + _dot(p.astype(BF16), v16)
    m_ref[...] = m_new


def _dec_init(m_ref, l_ref, acc_ref):
    m_ref[...] = jnp.full(m_ref.shape, NEG, F32)
    l_ref[...] = jnp.zeros(l_ref.shape, F32)
    acc_ref[...] = jnp.zeros(acc_ref.shape, F32)


def _diag_blocks(o, n_heads, ds, width):
    return jnp.concatenate([o[h * ds:(h + 1) * ds, h * width:(h + 1) * width] for h in range(n_heads)], axis=1)


def _diff_dec_body(par_ref, q_ref, kc_ref, vc_ref, kn_ref, vn_ref, g_ref, o_ref, m_ref, l_ref, acc_ref,
                   *, past, kb, ds, n_heads):
    jb = pl.program_id(1)
    rows = 2 * n_heads * ds
    r = lax.broadcasted_iota(jnp.int32, (rows, 1), 0)
    head = (r // ds) % n_heads
    slope = LOG2E * jnp.exp2(-8.0 * (head + 1).astype(F32) / n_heads)
    qpos = past + (r % ds)
    q = q_ref[0]

    @pl.when(jb == 0)
    def _():
        _dec_init(m_ref, l_ref, acc_ref)
        kpos = past + lax.broadcasted_iota(jnp.int32, (1, ds), 1)
        s = _dot_nt(q, kn_ref[0]) - slope * jnp.abs(qpos - kpos).astype(F32)
        _dec_online(s, vn_ref[0], m_ref, l_ref, acc_ref)

    kpos = jb * kb + lax.broadcasted_iota(jnp.int32, (1, kb), 1)
    s = _dot_nt(q, kc_ref[0, 0].astype(BF16)) - slope * (qpos - kpos).astype(F32)
    _dec_online(s, vc_ref[0, 0].astype(BF16), m_ref, l_ref, acc_ref)

    @pl.when(jb == pl.num_programs(1) - 1)
    def _():
        o = acc_ref[...] / l_ref[...]
        half = n_heads * ds
        w = o[:half] - par_ref[0] * o[half:]
        dv = 2 * DH_B
        outs = []
        for h in range(n_heads):
            oh = w[h * ds:(h + 1) * ds, h * dv:(h + 1) * dv]
            ms = jnp.mean(oh * oh, axis=-1, keepdims=True)
            outs.append(oh * lax.rsqrt(ms + EPS) * g_ref[:, h * dv:(h + 1) * dv])
        o_ref[0] = jnp.concatenate(outs, axis=1).astype(o_ref.dtype)


def _fox_dec_body(q_ref, kc_ref, vc_ref, kn_ref, vn_ref, fc_ref, fn_ref, o_ref, m_ref, l_ref, acc_ref,
                  *, ds, n_heads):
    jb = pl.program_id(1)
    rows = n_heads * ds
    q = q_ref[0]
    fnew = fn_ref[0][:, :ds]
    fref = _expand_rows(fn_ref[0][:, 0:1], ds)

    @pl.when(jb == 0)
    def _():
        _dec_init(m_ref, l_ref, acc_ref)
        r = lax.broadcasted_iota(jnp.int32, (rows, 1), 0)
        kidx = lax.broadcasted_iota(jnp.int32, (1, ds), 1)
        s = _dot_nt(q, kn_ref[0]) + LOG2E * (fref - _expand_rows(fnew, ds))
        s = jnp.where(kidx <= (r % ds), s, NEG)
        _dec_online(s, vn_ref[0].astype(BF16), m_ref, l_ref, acc_ref)

    s = _dot_nt(q, kc_ref[0, 0].astype(BF16)) + LOG2E * (fref - _expand_rows(fc_ref[0], ds))
    _dec_online(s, vc_ref[0, 0].astype(BF16), m_ref, l_ref, acc_ref)

    @pl.when(jb == pl.num_programs(1) - 1)
    def _():
        o = acc_ref[...] / l_ref[...]
        o_ref[0] = _diag_blocks(o, n_heads, ds, DH_C).astype(o_ref.dtype)


def _mla_dec_body(qn_ref, qp_ref, cc_ref, pc_ref, cn_ref, pn_ref, wk_ref, wv_ref, ones_ref,
                  o_ref, m_ref, l_ref, acc_ref, *, ds, n_heads):
    jb = pl.program_id(1)
    qn = qn_ref[0]
    qp = qp_ref[0]
    ones_h = ones_ref[...]

    def key_block(ckv, kpe):
        c16 = ckv.astype(BF16)
        kn = _dot(c16, wk_ref[...])
        v = _dot(c16, wv_ref[...])
        n = kpe.shape[0]
        ss = _dot_nt(ones_h, (kn * kn).astype(BF16)) + _dot_nt(jnp.ones((n_heads, ROPE_D), BF16),
                                                               (kpe * kpe).astype(BF16))
        rinv = lax.rsqrt(ss * (1.0 / (NOPE_D + ROPE_D)) + EPS)
        s = _dot_nt(qn, kn.astype(BF16)) + _dot_nt(qp, kpe.astype(BF16))
        s = s * _expand_rows(rinv, ds)
        _dec_online(s, v.astype(BF16), m_ref, l_ref, acc_ref)

    @pl.when(jb == 0)
    def _():
        _dec_init(m_ref, l_ref, acc_ref)
        key_block(cn_ref[0], pn_ref[0])

    key_block(cc_ref[0, 0], pc_ref[0, 0])

    @pl.when(jb == pl.num_programs(1) - 1)
    def _():
        o = acc_ref[...] / l_ref[...]
        o_ref[0] = _diag_blocks(o, n_heads, ds, V_D).astype(o_ref.dtype)


def _per_seq(a):
    return (a, pl.BlockSpec((1,) + a.shape[1:], lambda b, j: (b, 0, 0)))


def _dec_const(a):
    return (a, _const_spec(a.shape))


def _dec_call(body, ins, out_w, out_dtype, rows, acc_w, nb, n_kb, smem=None):
    in_specs = [spec for _, spec in ins]
    args = [a for a, _ in ins]
    if smem is not None:
        in_specs = [pl.BlockSpec(memory_space=pltpu.SMEM)] + in_specs
        args = [smem] + args
    return pl.pallas_call(
        body,
        grid=(nb, n_kb),
        in_specs=in_specs,
        out_specs=pl.BlockSpec((1, out_w[0], out_w[1]), lambda b, j: (b, 0, 0)),
        out_shape=jax.ShapeDtypeStruct((nb, out_w[0], out_w[1]), out_dtype),
        scratch_shapes=[pltpu.VMEM((rows, 1), F32), pltpu.VMEM((rows, 1), F32), pltpu.VMEM((rows, acc_w), F32)],
        compiler_params=pltpu.CompilerParams(
            dimension_semantics=("parallel", "arbitrary"), vmem_limit_bytes=VMEM_LIMIT),
    )(*args)


def kernel(x_prompt, x_sample, state_s5_re, state_s5_im, cache_diff_k, cache_diff_v, cache_fox_k, cache_fox_v, cache_fox_logf, cache_mla_ckv, cache_mla_kpe, meta_tokens, ffn_norm, ffn_w_in, ffn_w_out, mix_norm, ab_w_in, ab_w_out, s5_a_re, s5_a_im, s5_log_step, s5_b_re, s5_b_im, s5_c_re, s5_c_im, s5_d, s5_glu_w, s5_glu_b, diff_q_norm, diff_k_norm, diff_lam, diff_sub_norm, cd_w_in, cd_w_out, fox_q_norm, fox_k_norm, fox_f_bias, mla_q_a_norm, mla_q_b, mla_kv_a_norm, mla_kv_b, mla_q_norm, mla_k_norm):
    bsz, seq, dm = x_prompt.shape
    nb, ds, _ = x_sample.shape
    n_meta = meta_tokens.shape[0]
    past = cache_diff_k.shape[2]
    front = ROW_ALIGN - n_meta
    lp = front + n_meta + seq
    ltot = n_meta + seq
    assert n_meta + front == ROW_ALIGN and lp % ATT_BLOCK == 0 and front % CHUNK == CHUNK - n_meta
    assert ds == S5_STEP and past % CHUNK == 0 and ds <= CHUNK
    kb = min(DEC_KB, past)
    assert past % kb == 0
    n_kb = past // kb
    assert ffn_norm.shape[0] == 2 and ab_w_in.shape[0] == 1 and cd_w_in.shape[0] == 1

    h_b = cache_diff_k.shape[3]
    h_c = cache_fox_k.shape[3]
    h_d = mla_q_b.shape[2] // (NOPE_D + ROPE_D)
    s5w = s5_glu_w.shape[1]
    n_grp = s5w // S5_GROUP
    qkw = h_b * 2 * DH_B
    fox_w = h_c * DH_C
    q_lora = mla_q_a_norm.shape[1]
    kv_lora = mla_kv_a_norm.shape[1]
    d_qk = NOPE_D + ROPE_D

    meta = jnp.broadcast_to(meta_tokens[None].astype(F32), (bsz, n_meta, dm))
    xp = jnp.concatenate([jnp.zeros((bsz, front, dm), F32), meta, x_prompt], axis=1)
    n_p = bsz * lp
    x = jnp.concatenate([xp.reshape(n_p, dm), x_sample.reshape(nb * ds, dm)], axis=0)
    rows = x.shape[0]
    tm = _row_tile(rows)

    ones64 = _block_diag_ones(64)
    ones128 = _block_diag_ones(128)

    x = _ffn(x, ffn_norm[0, 0], ffn_w_in[0, 0], ffn_w_out[0, 0], tm)

    gq = (jnp.tile(diff_q_norm[0], 2 * h_b) * (DH_B ** -0.5 * LOG2E)).reshape(1, qkw)
    gk = jnp.tile(diff_k_norm[0], 2 * h_b).reshape(1, qkw)
    u16, q16, k32, k16, v32, v16 = _row_call(
        functools.partial(_ab_in_body, widths=(s5w, qkw)),
        [x], [mix_norm[0].reshape(1, dm), ab_w_in[0].astype(BF16), gq, gk, ones64],
        [s5w, qkw, qkw, qkw, h_b * 2 * DH_B, h_b * 2 * DH_B], [BF16, BF16, F32, BF16, F32, BF16], tm)

    mats = _s5_matrices(s5_a_re[0], s5_a_im[0], s5_log_step[0], s5_b_re[0], s5_b_im[0],
                        s5_c_re[0], s5_c_im[0], s5_d[0])
    n_ch = lp // S5_STEP
    u_p = u16[:n_p].reshape(bsz, n_ch, S5_STEP, n_grp, S5_GROUP)
    u_p = jnp.transpose(u_p, (3, 1, 0, 2, 4)).reshape(n_grp, n_ch * bsz, S5_STEP * S5_GROUP)
    y_p, st_p = _s5_scan(u_p, jnp.zeros((n_grp, bsz, 4 * P_A), F32), mats, n_ch, bsz,
                         (front + ltot) // S5_STEP - 1)
    y_p = jnp.transpose(y_p.reshape(n_grp, n_ch, bsz, S5_STEP, S5_GROUP), (2, 1, 3, 0, 4)).reshape(n_p, s5w)
    u_s = u16[n_p:].reshape(nb, 1, S5_STEP, n_grp, S5_GROUP)
    u_s = jnp.transpose(u_s, (3, 1, 0, 2, 4)).reshape(n_grp, nb, S5_STEP * S5_GROUP)
    h_re = jnp.transpose(state_s5_re[0].astype(F32), (1, 0, 2))
    h_im = jnp.transpose(state_s5_im[0].astype(F32), (1, 0, 2))
    y_s, st_s = _s5_scan(u_s, jnp.concatenate([h_re, h_im, h_im, h_re], axis=-1), mats, 1, nb, 0)
    y_s = jnp.transpose(y_s.reshape(n_grp, 1, nb, S5_STEP, S5_GROUP), (2, 1, 3, 0, 4)).reshape(nb * ds, s5w)
    y_all = jnp.concatenate([y_p, y_s], axis=0)

    lv = diff_lam[0].astype(F32)
    lam_init = 0.8 - 0.6 * math.exp(-0.3 * 0)
    lam = jnp.exp(jnp.sum(lv[0] * lv[1])) - jnp.exp(jnp.sum(lv[2] * lv[3])) + lam_init
    slopes = jnp.exp2(-8.0 * jnp.arange(1, h_b + 1, dtype=F32) / h_b)
    par = jnp.concatenate([lam[None], slopes]).astype(F32)
    subg = (jnp.tile(diff_sub_norm[0], h_b) * (1.0 - lam_init)).reshape(1, qkw)
    kpad = jnp.arange(lp) < front
    kb_diff = jnp.where(kpad[None, :], NEG, LOG2E * slopes[:, None] * jnp.arange(lp, dtype=F32)[None, :])
    kb_diff = jnp.broadcast_to(kb_diff[None, :, None, :], (1, h_b, 2, lp))
    o_p = _prompt_attn("diff", q16[:n_p].reshape(bsz, lp, qkw), k16[:n_p].reshape(bsz, lp, qkw),
                       v16[:n_p].reshape(bsz, lp, qkw), kb_diff, front, par=par, gain=subg)
    qs = q16[n_p:].reshape(nb, ds, h_b, 2, DH_B)
    eye_h = jnp.eye(h_b, dtype=BF16)
    eye_2 = jnp.eye(2, dtype=BF16)
    qbd = jnp.einsum('bqhmd,hH,mM->bmhqHMd', qs, eye_h, eye_2).reshape(nb, 2 * h_b * ds, qkw)
    kc = cache_diff_k.reshape(cache_diff_k.shape[0], nb, past, qkw)
    vc = cache_diff_v.reshape(cache_diff_v.shape[0], nb, past, qkw)
    cache_spec = lambda w: pl.BlockSpec((1, 1, kb, w), lambda b, j: (0, b, j, 0))
    o_s = _dec_call(
        functools.partial(_diff_dec_body, past=past, kb=kb, ds=ds, n_heads=h_b),
        [_per_seq(qbd), (kc, cache_spec(qkw)), (vc, cache_spec(qkw)),
         _per_seq(k16[n_p:].reshape(nb, ds, qkw)), _per_seq(v16[n_p:].reshape(nb, ds, qkw)),
         _dec_const(subg)],
        (ds, qkw), BF16, 2 * h_b * ds, qkw, nb, n_kb, smem=par)
    o_all = jnp.concatenate([o_p.reshape(n_p, qkw), o_s.reshape(nb * ds, qkw)], axis=0)

    x = _row_call(
        functools.partial(_ab_out_body, s5w=s5w),
        [x, y_all, o_all],
        [s5_glu_w[0].astype(BF16), s5_glu_b[0].reshape(1, s5w), ab_w_out[0].astype(BF16)],
        [dm], [F32], tm)[0]

    x = _ffn(x, ffn_norm[0, 1], ffn_w_in[0, 1], ffn_w_out[0, 1], tm)

    x = _ffn(x, ffn_norm[1, 0], ffn_w_in[1, 0], ffn_w_out[1, 0], tm)

    half = ROPE_D // 2
    inv = ROPE_THETA ** (-jnp.arange(half, dtype=F32) / half)
    pos = jnp.concatenate([jnp.tile(jnp.arange(lp, dtype=jnp.int32) - front, bsz),
                           jnp.tile(past + jnp.arange(ds, dtype=jnp.int32), nb)]).astype(F32)
    ang = pos[:, None] * inv[None, :]
    pad_r = HEAD_PAD - NOPE_D - ROPE_D
    cos_t = jnp.concatenate([jnp.ones((rows, NOPE_D), F32), jnp.cos(ang), jnp.cos(ang),
                             jnp.zeros((rows, pad_r), F32)], axis=1)
    sin_t = jnp.concatenate([jnp.zeros((rows, NOPE_D), F32), jnp.sin(ang), jnp.sin(ang),
                             jnp.zeros((rows, pad_r), F32)], axis=1)

    wcd = cd_w_in[0]
    c_fg = 3 * fox_w
    c_qa = c_fg + h_c
    c_kva = c_qa + q_lora
    c_pe = c_kva + kv_lora
    w_pe = wcd[:, c_pe:c_pe + ROPE_D]
    zc = lambda n: jnp.zeros((dm, n), F32)
    w_cd = jnp.concatenate([
        wcd[:, :3 * fox_w], wcd[:, c_qa:c_qa + q_lora], wcd[:, c_kva:c_kva + kv_lora],
        zc(NOPE_D), w_pe, zc(pad_r),
        zc(NOPE_D), -w_pe[:, half:], w_pe[:, :half], zc(pad_r),
        wcd[:, c_fg:c_fg + h_c], zc(HEAD_PAD - h_c)], axis=1).astype(BF16)
    qb = mla_q_b[0].reshape(q_lora, h_d, d_qk)
    zq = lambda n: jnp.zeros((q_lora, h_d, n), F32)
    qb_pad = jnp.concatenate([qb, zq(pad_r)], axis=-1).reshape(q_lora, h_d * HEAD_PAD)
    qb_rot = jnp.concatenate([zq(NOPE_D), -qb[..., NOPE_D + half:], qb[..., NOPE_D:NOPE_D + half], zq(pad_r)],
                             axis=-1).reshape(q_lora, h_d * HEAD_PAD)
    wq2 = jnp.concatenate([qb_pad, qb_rot], axis=1).astype(BF16)
    kvb = mla_kv_b[0].reshape(kv_lora, h_d, NOPE_D + V_D)
    wk_pad = jnp.concatenate([kvb[..., :NOPE_D], jnp.zeros((kv_lora, h_d, HEAD_PAD - NOPE_D), F32)],
                             axis=-1).reshape(kv_lora, h_d * HEAD_PAD).astype(BF16)
    wk_cmp = kvb[..., :NOPE_D].reshape(kv_lora, h_d * NOPE_D).astype(BF16)
    wv_cmp = kvb[..., NOPE_D:].reshape(kv_lora, h_d * V_D).astype(BF16)
    gfq = (jnp.tile(fox_q_norm[0], h_c) * (DH_C ** -0.5 * LOG2E)).reshape(1, fox_w)
    gfk = jnp.tile(fox_k_norm[0], h_c).reshape(1, fox_w)
    fbias = jnp.concatenate([fox_f_bias[0], jnp.zeros((HEAD_PAD - h_c,), F32)]).reshape(1, HEAD_PAD)
    gmq = jnp.tile(jnp.concatenate([mla_q_norm[0] * mla_k_norm[0] * (d_qk ** -0.5 * LOG2E), jnp.zeros((pad_r,), F32)]),
                   h_d).reshape(1, h_d * HEAD_PAD)

    (fq16, fk32, fk16, fv32, fv16, logf, qm16, ckv32, kpe32, km16, vm16) = _row_call(
        functools.partial(_cd_in_body, fox_w=fox_w, q_lora=q_lora, kv_lora=kv_lora, n_heads=h_d),
        [x, cos_t, sin_t],
        [mix_norm[1].reshape(1, dm), w_cd, gfq, gfk, fbias, mla_q_a_norm[0].reshape(1, q_lora), wq2,
         mla_kv_a_norm[0].reshape(1, kv_lora), wk_pad, wv_cmp, gmq, ones64, ones128],
        [fox_w, fox_w, fox_w, fox_w, 2 * fox_w, h_c, h_d * HEAD_PAD, kv_lora, ROPE_D, h_d * HEAD_PAD, 2 * h_d * V_D],
        [BF16, F32, BF16, F32, BF16, F32, BF16, F32, F32, BF16, BF16], tm)

    logf_p = jnp.transpose(logf[:n_p].reshape(bsz, lp, h_c), (0, 2, 1)).reshape(bsz * h_c, lp)
    f_p = _cumsum_lanes(logf_p).reshape(bsz, h_c, lp)
    logf_s = jnp.concatenate([
        jnp.transpose(cache_fox_logf[0].astype(F32), (0, 2, 1)),
        jnp.transpose(logf[n_p:].reshape(nb, ds, h_c), (0, 2, 1)),
        jnp.zeros((nb, h_c, 256 - ds), F32)], axis=2).reshape(nb * h_c, past + 256)
    f_s = _cumsum_lanes(logf_s).reshape(nb, h_c, past + 256)

    kb_fox = jnp.where(kpad[None, None, :], NEG, -LOG2E * f_p).reshape(bsz, h_c // 2, 2, lp)
    oc_p = _prompt_attn("fox", fq16[:n_p].reshape(bsz, lp, fox_w), fk16[:n_p].reshape(bsz, lp, fox_w),
                        fv16[:n_p].reshape(bsz, lp, 2 * fox_w), kb_fox, front)
    kb_mla = jnp.broadcast_to(jnp.where(kpad, NEG, 0.0).astype(F32)[None, None, None, :], (1, 1, 2, lp))
    od_p = _prompt_attn("mla", qm16[:n_p].reshape(bsz, lp, h_d * HEAD_PAD),
                        km16[:n_p].reshape(bsz, lp, h_d * HEAD_PAD),
                        vm16[:n_p].reshape(bsz, lp, 2 * h_d * V_D), kb_mla, front)

    eye_c = jnp.eye(h_c, dtype=BF16)
    fqs = fq16[n_p:].reshape(nb, ds, h_c, DH_C)
    fq_bd = jnp.einsum('bqhd,hH->bhqHd', fqs, eye_c).reshape(nb, h_c * ds, fox_w)
    fkc = cache_fox_k.reshape(cache_fox_k.shape[0], nb, past, fox_w)
    fvc = cache_fox_v.reshape(cache_fox_v.shape[0], nb, past, fox_w)
    oc_s = _dec_call(
        functools.partial(_fox_dec_body, ds=ds, n_heads=h_c),
        [_per_seq(fq_bd), (fkc, cache_spec(fox_w)), (fvc, cache_spec(fox_w)),
         _per_seq(fk16[n_p:].reshape(nb, ds, fox_w)), _per_seq(fv32[n_p:].reshape(nb, ds, fox_w)),
         (f_s, pl.BlockSpec((1, h_c, kb), lambda b, j: (b, 0, j))),
         (f_s, pl.BlockSpec((1, h_c, 128), lambda b, j: (b, 0, past // 128)))],
        (ds, fox_w), BF16, h_c * ds, fox_w, nb, n_kb)

    eye_d = jnp.eye(h_d, dtype=BF16)
    qms = qm16[n_p:].reshape(nb, ds, h_d, HEAD_PAD)
    qn_bd = jnp.einsum('bqhd,hH->bhqHd', qms[..., :NOPE_D], eye_d).reshape(nb, h_d * ds, h_d * NOPE_D)
    qp_s = jnp.transpose(qms[..., NOPE_D:NOPE_D + ROPE_D], (0, 2, 1, 3)).reshape(nb, h_d * ds, ROPE_D)
    ones_h = jnp.repeat(jnp.eye(h_d, dtype=BF16), NOPE_D, axis=1)
    od_s = _dec_call(
        functools.partial(_mla_dec_body, ds=ds, n_heads=h_d),
        [_per_seq(qn_bd), _per_seq(qp_s),
         (cache_mla_ckv, pl.BlockSpec((1, 1, kb, kv_lora), lambda b, j: (0, b, j, 0))),
         (cache_mla_kpe, pl.BlockSpec((1, 1, kb, ROPE_D), lambda b, j: (0, b, j, 0))),
         _per_seq(ckv32[n_p:].reshape(nb, ds, kv_lora)), _per_seq(kpe32[n_p:].reshape(nb, ds, ROPE_D)),
         _dec_const(wk_cmp), _dec_const(wv_cmp), _dec_const(ones_h)],
        (ds, h_d * V_D), BF16, h_d * ds, h_d * V_D, nb, n_kb)

    oc_all = jnp.concatenate([oc_p.reshape(n_p, fox_w), oc_s.reshape(nb * ds, fox_w)], axis=0)
    od_all = jnp.concatenate([od_p.reshape(n_p, h_d * V_D), od_s.reshape(nb * ds, h_d * V_D)], axis=0)
    x = _row_call(functools.partial(_cd_out_body, fox_w=fox_w), [x, oc_all, od_all],
                  [cd_w_out[0].astype(BF16)], [dm], [F32], tm)[0]

    x = _ffn(x, ffn_norm[1, 1], ffn_w_in[1, 1], ffn_w_out[1, 1], tm)

    def p_rows(a, shape):
        w = a.shape[1]
        return a[:n_p].reshape(bsz, lp, w)[:, front:front + ltot].reshape((1, bsz, ltot) + shape)

    def s_rows(a, shape):
        return a[n_p:].reshape((1, nb, ds) + shape)

    def s5_state(st):
        st = jnp.transpose(st, (1, 0, 2))
        return st[None, :, :, :P_A], st[None, :, :, P_A:]

    y_prompt = x[:n_p].reshape(bsz, lp, dm)[:, front + n_meta:]
    y_sample = x[n_p:].reshape(nb, ds, dm)
    s5_re_p, s5_im_p = s5_state(st_p)
    s5_re_s, s5_im_s = s5_state(st_s)
    return (y_prompt, y_sample,
            s5_re_p, s5_im_p, p_rows(k32, (h_b, 2 * DH_B)), p_rows(v32, (h_b, 2 * DH_B)),
            p_rows(fk32, (h_c, DH_C)), p_rows(fv32, (h_c, DH_C)), p_rows(logf, (h_c,)),
            p_rows(ckv32, (kv_lora,)), p_rows(kpe32, (ROPE_D,)),
            s5_re_s, s5_im_s, s_rows(k32, (h_b, 2 * DH_B)), s_rows(v32, (h_b, 2 * DH_B)),
            s_rows(fk32, (h_c, DH_C)), s_rows(fv32, (h_c, DH_C)), s_rows(logf, (h_c,)),
            s_rows(ckv32, (kv_lora,)), s_rows(kpe32, (ROPE_D,)))
```

```python
import functools
import math

import jax
import jax.numpy as jnp
from jax import lax
from jax.experimental import pallas as pl
from jax.experimental.pallas import tpu as pltpu

F32 = jnp.float32
BF16 = jnp.bfloat16

EPS = 1e-6
CHUNK = 64
ROW_ALIGN = 256
S5_GROUP = 16
S5_STEP = 16
P_A = 64
DH_B = 64
DH_C = 64
NOPE_D = 64
ROPE_D = 32
V_D = 64
HEAD_PAD = 128
ROPE_THETA = 10000.0
NEG = -1e30
LOG2E = math.log2(math.e)
VMEM_LIMIT = 56 * 1024 * 1024
ATT_BLOCK = 256
ATT_STREAMS = 4
ATT_ROWS = 64
DEC_KB = 1024


def _dot(a, b):
    return jnp.dot(a, b, preferred_element_type=F32)


def _dot_nt(a, b):
    return lax.dot_general(a, b, (((1,), (1,)), ((), ())), preferred_element_type=F32)


def _rms_rows(x, g):
    ms = jnp.mean(x * x, axis=-1, keepdims=True)
    return x * lax.rsqrt(ms + EPS) * g


def _group_sumsq(x, ones_bd):
    w = x.shape[-1]
    parts = [_dot((x[:, c:c + 256] * x[:, c:c + 256]).astype(BF16), ones_bd) for c in range(0, w, 256)]
    return parts[0] if len(parts) == 1 else jnp.concatenate(parts, axis=1)


def _block_diag_ones(group, n=256):
    r = jnp.arange(n) // group
    return (r[:, None] == r[None, :]).astype(BF16)


def _const_spec(shape):
    nd = len(shape)
    return pl.BlockSpec(shape, lambda *_: (0,) * nd, pipeline_mode=pl.Buffered(1))


def _row_tile(rows, cap=512):
    t = cap
    while rows % t:
        t //= 2
    return t


def _group_spec(a, tm):
    return pl.BlockSpec((a.shape[0], tm // S5_STEP, a.shape[2]), lambda i: (0, i, 0))


def _row_call(body, row_ins, consts, out_widths, out_dtypes, tm, group_ins=(), group_outs=(), scratch=()):
    rows = row_ins[0].shape[0]
    in_specs = [pl.BlockSpec((tm, a.shape[1]), lambda i: (i, 0)) for a in row_ins]
    in_specs += [_group_spec(a, tm) for a in group_ins]
    in_specs += [_const_spec(c.shape) for c in consts]
    out_specs = [pl.BlockSpec((tm, w), lambda i: (i, 0)) for w in out_widths]
    out_specs += [_group_spec(a, tm) for a in group_outs]
    out_shape = [jax.ShapeDtypeStruct((rows, w), d) for w, d in zip(out_widths, out_dtypes)]
    out_shape += list(group_outs)
    return pl.pallas_call(
        body,
        grid=(rows // tm,),
        in_specs=in_specs,
        out_specs=out_specs,
        out_shape=out_shape,
        scratch_shapes=list(scratch),
        compiler_params=pltpu.CompilerParams(
            dimension_semantics=("parallel",), vmem_limit_bytes=VMEM_LIMIT),
    )(*row_ins, *group_ins, *consts)


def _ffn_body(x_ref, g_ref, win_ref, wout_ref, o_ref, *, d_ff, tf):
    x = x_ref[...]
    xn = _rms_rows(x, g_ref[...]).astype(BF16)
    acc = jnp.zeros(x.shape, F32)
    for c in range(0, d_ff, tf):
        gate = _dot(xn, win_ref[:, c:c + tf])
        up = _dot(xn, win_ref[:, d_ff + c:d_ff + c + tf])
        a = (gate * jax.nn.sigmoid(gate) * up).astype(BF16)
        acc = acc + _dot(a, wout_ref[c:c + tf, :])
    o_ref[...] = x + 0.5 * acc


def _ffn(x, g, w_in, w_out, tm):
    d_ff = w_out.shape[0]
    body = functools.partial(_ffn_body, d_ff=d_ff, tf=256)
    return _row_call(body, [x], [g.reshape(1, -1), w_in.astype(BF16), w_out.astype(BF16)],
                     [x.shape[1]], [F32], tm)[0]


def _ab_in_body(x_ref, g_ref, w_ref, gq_ref, gk_ref, ones_ref,
                q_ref, k32_ref, k16_ref, v32_ref, v16_ref, ut_ref, us_ref, *, widths):
    s5w, qkw = widths
    xn = _rms_rows(x_ref[...], g_ref[...]).astype(BF16)
    h = _dot(xn, w_ref[...])
    n_chunk = us_ref.shape[1] // S5_STEP
    per_col = 128 // S5_GROUP
    for v in range(s5w // 128):
        us_ref[v] = h[:, v * 128:(v + 1) * 128]
        steps = [us_ref[v, pl.ds(t, n_chunk, stride=S5_STEP), :] for t in range(S5_STEP)]
        for gl in range(per_col):
            ut_ref[v * per_col + gl] = jnp.concatenate(
                [x[:, gl * S5_GROUP:(gl + 1) * S5_GROUP] for x in steps], axis=1).astype(BF16)
    q = h[:, s5w:s5w + qkw]
    k = h[:, s5w + qkw:s5w + 2 * qkw]
    v = h[:, s5w + 2 * qkw:]
    ones_bd = ones_ref[...]
    qn = q * lax.rsqrt(_group_sumsq(q, ones_bd) * (1.0 / DH_B) + EPS) * gq_ref[...]
    kn = k * lax.rsqrt(_group_sumsq(k, ones_bd) * (1.0 / DH_B) + EPS) * gk_ref[...]
    q_ref[...] = qn.astype(BF16)
    k32_ref[...] = kn
    k16_ref[...] = kn.astype(BF16)
    v32_ref[...] = v
    v16_ref[...] = v.astype(BF16)


def _ab_out_body(x_ref, o_ref, yt_ref, gluw_ref, glub_ref, wout_ref, out_ref, ys_ref, *, s5w):
    n_chunk = ys_ref.shape[1] // S5_STEP
    per_col = 128 // S5_GROUP
    for v in range(s5w // 128):
        for t in range(S5_STEP):
            ys_ref[v, pl.ds(t, n_chunk, stride=S5_STEP), :] = jnp.concatenate(
                [yt_ref[v * per_col + gl][:, t * S5_GROUP:(t + 1) * S5_GROUP] for gl in range(per_col)], axis=1)
    y = jnp.concatenate([ys_ref[v] for v in range(s5w // 128)], axis=1)
    g = 0.5 * y * (1.0 + jnp.tanh(math.sqrt(2.0 / math.pi) * (y + 0.044715 * (y * y * y))))
    z = _dot(g.astype(BF16), gluw_ref[...]) + glub_ref[...]
    s5o = g * jax.nn.sigmoid(z)
    m = _dot(s5o.astype(BF16), wout_ref[:s5w, :]) + _dot(o_ref[...], wout_ref[s5w:, :])
    out_ref[...] = x_ref[...] + m


def _heads_with_ones(v):
    lane = lax.broadcasted_iota(jnp.int32, (1, 128), 1)
    outs = []
    for c in range(0, v.shape[1], 128):
        col = v[:, c:c + 128]
        outs.append(jnp.where(lane < 64, col, 1.0))
        outs.append(jnp.where(lane < 64, pltpu.roll(col, 64, axis=1), 1.0))
    return jnp.concatenate(outs, axis=1)


def _cd_in_body(x_ref, cos_ref, sin_ref, g_ref, w_ref, gfq_ref, gfk_ref, fb_ref, gqa_ref, wq2_ref,
                gkva_ref, wk_ref, wv_ref, gmq_ref, ones64_ref, ones128_ref,
                fq_ref, fk32_ref, fk16_ref, fv32_ref, fv16_ref, logf_ref, qm_ref, ckv_ref, kpe_ref,
                km_ref, vm_ref, *, fox_w, q_lora, kv_lora, n_heads):
    xn = _rms_rows(x_ref[...], g_ref[...]).astype(BF16)
    h = _dot(xn, w_ref[...])
    ones64 = ones64_ref[...]
    ones128 = ones128_ref[...]
    fq = h[:, :fox_w]
    fk = h[:, fox_w:2 * fox_w]
    fv = h[:, 2 * fox_w:3 * fox_w]
    c0 = 3 * fox_w
    qa = h[:, c0:c0 + q_lora]
    kva = h[:, c0 + q_lora:c0 + q_lora + kv_lora]
    c1 = c0 + q_lora + kv_lora
    pe_a = h[:, c1:c1 + HEAD_PAD]
    pe_b = h[:, c1 + HEAD_PAD:c1 + 2 * HEAD_PAD]
    fg = h[:, c1 + 2 * HEAD_PAD:c1 + 3 * HEAD_PAD]

    fqn = fq * lax.rsqrt(_group_sumsq(fq, ones64) * (1.0 / DH_C) + EPS) * gfq_ref[...]
    fkn = fk * lax.rsqrt(_group_sumsq(fk, ones64) * (1.0 / DH_C) + EPS) * gfk_ref[...]
    fq_ref[...] = fqn.astype(BF16)
    fk32_ref[...] = fkn
    fk16_ref[...] = fkn.astype(BF16)
    fv32_ref[...] = fv
    fv16_ref[...] = _heads_with_ones(fv).astype(BF16)

    z = fg + fb_ref[...]
    logf = jnp.minimum(z, 0.0) - jnp.log1p(jnp.exp(-jnp.abs(z)))
    logf_ref[...] = logf[:, :logf_ref.shape[1]]

    cos = cos_ref[...]
    sin = sin_ref[...]
    qan = _rms_rows(qa, gqa_ref[...]).astype(BF16)
    q2 = _dot(qan, wq2_ref[...])
    hw = n_heads * HEAD_PAD
    cos_t = jnp.concatenate([cos] * n_heads, axis=1)
    sin_t = jnp.concatenate([sin] * n_heads, axis=1)
    qr = q2[:, :hw] * cos_t + q2[:, hw:] * sin_t
    d_qk = NOPE_D + ROPE_D
    qm = qr * lax.rsqrt(_group_sumsq(qr, ones128) * (1.0 / d_qk) + EPS) * gmq_ref[...]
    qm_ref[...] = qm.astype(BF16)

    ckv = _rms_rows(kva, gkva_ref[...])
    ckv_ref[...] = ckv
    pe = pe_a * cos + pe_b * sin
    kpe_ref[...] = pe[:, NOPE_D:NOPE_D + ROPE_D]
    ckv16 = ckv.astype(BF16)
    kraw = _dot(ckv16, wk_ref[...]) + jnp.concatenate([pe] * n_heads, axis=1)
    km = kraw * lax.rsqrt(_group_sumsq(kraw, ones128) * (1.0 / d_qk) + EPS)
    km_ref[...] = km.astype(BF16)
    vm_ref[...] = _heads_with_ones(_dot(ckv16, wv_ref[...])).astype(BF16)


def _cd_out_body(x_ref, oc_ref, od_ref, wout_ref, out_ref, *, fox_w):
    m = _dot(oc_ref[...], wout_ref[:fox_w, :]) + _dot(od_ref[...], wout_ref[fox_w:, :])
    out_ref[...] = x_ref[...] + m


def _s5_body(*refs, n_chunks, bsz, aliased):
    if aliased:
        u_ref, h0_ref, m_ref, bm_ref, cm_ref, coef_ref, _, y_ref, st_ref, s2_ref, hp_ref = refs
    else:
        u_ref, h0_ref, m_ref, bm_ref, cm_ref, coef_ref, y_ref, st_ref, s2_ref, hp_ref = refs
    u = u_ref[0]
    half = 2 * P_A
    s2 = _dot(u, bm_ref[0])
    s2_ref[0] = s2[:, :half]
    s2_ref[1] = s2[:, half:]
    c1 = coef_ref[0, 0:1, :]
    c2 = coef_ref[0, 1:2, :]
    c3 = coef_ref[0, 2:3, :]

    def step(j, carry):
        ha, hb = carry
        hp_ref[pl.ds(j, bsz, stride=n_chunks), :] = ha
        sa = s2_ref[0, pl.ds(j, bsz, stride=n_chunks), :]
        sb = s2_ref[1, pl.ds(j, bsz, stride=n_chunks), :]
        return ha * c1 + hb * c2 + sa, hb * c1 + ha * c3 + sb

    h0 = h0_ref[0]
    ha, _ = lax.fori_loop(0, n_chunks, step, (h0[:, :half], h0[:, half:]))
    st_ref[0] = ha
    y_ref[0] = _dot(u, m_ref[0]) + _dot(hp_ref[...].astype(BF16), cm_ref[0])


def _s5_scan(u_t, h0, mats, n_chunks, bsz, row0, y_prev=None):
    m_mat, bm, cm, coef = mats
    g, rows_all, w = u_t.shape
    rows = n_chunks * bsz
    assert row0 % rows == 0
    blk = row0 // rows
    aliased = y_prev is not None
    body = functools.partial(_s5_body, n_chunks=n_chunks, bsz=bsz, aliased=aliased)
    per_g = lambda a: pl.BlockSpec((1,) + a.shape[1:], lambda i: (i, 0, 0))
    in_specs = [pl.BlockSpec((1, rows, w), lambda i: (i, blk, 0)),
                per_g(h0), per_g(m_mat), per_g(bm), per_g(cm), per_g(coef)]
    args = [u_t, h0, m_mat, bm, cm, coef]
    if aliased:
        in_specs.append(pl.BlockSpec(memory_space=pl.ANY))
        args.append(y_prev)
    return pl.pallas_call(
        body,
        grid=(g,),
        in_specs=in_specs,
        out_specs=[pl.BlockSpec((1, rows, w), lambda i: (i, blk, 0)),
                   pl.BlockSpec((1, bsz, 2 * P_A), lambda i: (i, 0, 0))],
        out_shape=[jax.ShapeDtypeStruct((g, rows_all, w), F32),
                   jax.ShapeDtypeStruct((g, bsz, 2 * P_A), F32)],
        input_output_aliases={6: 0} if aliased else {},
        scratch_shapes=[pltpu.VMEM((2, rows, 2 * P_A), F32), pltpu.VMEM((rows, 2 * P_A), F32)],
        compiler_params=pltpu.CompilerParams(
            dimension_semantics=("parallel",), vmem_limit_bytes=VMEM_LIMIT),
    )(*args)


def _s5_matrices(a_re, a_im, log_step, b_re, b_im, c_re, c_im, d):
    g = a_re.shape[0]
    t = S5_STEP
    lam = lax.complex(a_re, a_im)
    dl = lam * jnp.exp(log_step)[:, None]
    lam_bar = jnp.exp(dl)
    b_bar = ((lam_bar - 1.0) / lam)[..., None] * lax.complex(b_re, b_im)
    c = lax.complex(c_re, c_im)
    pw = jnp.exp(dl[:, None, :] * jnp.arange(t + 1, dtype=F32)[None, :, None])
    bmc = pw[:, t - 1::-1][:, :, :, None] * b_bar[:, None]
    bmc = jnp.swapaxes(bmc, 2, 3).reshape(g, t * S5_GROUP, P_A)
    bm = jnp.concatenate([bmc.real, bmc.imag, bmc.imag, bmc.real], axis=-1)
    kk = jnp.einsum('gcp,gkp,gpd->gkcd', c, pw[:, :t], b_bar).real
    kk = kk.at[:, 0].add(d.reshape(g, S5_GROUP)[:, :, None] * jnp.eye(S5_GROUP, dtype=F32))
    lag = jnp.arange(t)[None, :] - jnp.arange(t)[:, None]
    toep = jnp.where((lag >= 0)[None, :, :, None, None], kk[:, jnp.clip(lag, 0, t - 1)], 0.0)
    m_mat = jnp.transpose(toep, (0, 1, 4, 2, 3)).reshape(g, t * S5_GROUP, t * S5_GROUP)
    cp = c[:, None] * pw[:, 1:, None, :]
    cpm = jnp.transpose(cp, (0, 3, 1, 2)).reshape(g, P_A, t * S5_GROUP)
    cm = jnp.concatenate([cpm.real, -cpm.imag], axis=1)
    a_t = pw[:, t]
    ar, ai = a_t.real, a_t.imag
    zeros = jnp.zeros_like(ar)
    coef = jnp.stack([jnp.concatenate([ar, ar], -1), jnp.concatenate([-ai, ai], -1),
                      jnp.concatenate([ai, -ai], -1), jnp.concatenate([zeros, zeros], -1)], axis=1)
    return m_mat.astype(BF16), bm.astype(BF16), cm.astype(BF16), coef.astype(F32)


def _online(logits, v, e, m_ref, l_ref, acc_ref, p_ref, sum_in_v):
    tq, tk = p_ref.shape[1], p_ref.shape[2]
    for r0 in range(0, tq, ATT_ROWS):
        rows = slice(r0, r0 + ATT_ROWS)
        m_prev = m_ref[e, rows]
        m_new = jnp.maximum(m_prev, jnp.max(logits(r0, ATT_ROWS), axis=-1, keepdims=True))
        alpha = jnp.exp2(m_prev - m_new)
        if not sum_in_v:
            l_ref[e, rows] = alpha * l_ref[e, rows]
        acc_ref[e, rows] = alpha * acc_ref[e, rows]
        m_ref[e, rows] = m_new
    for r0 in range(0, tq, ATT_ROWS):
        rows = slice(r0, r0 + ATT_ROWS)
        p = jnp.exp2(logits(r0, ATT_ROWS) - jnp.concatenate([m_ref[e, rows]] * (tk // 128), axis=1))
        if not sum_in_v:
            l_ref[e, rows] += jnp.sum(p, axis=-1, keepdims=True)
        p_ref[e, rows] = p.astype(BF16)
    acc_ref[e] += _dot(p_ref[e], v)


def _prompt_attn_body(*refs, kind, front, tq, ns):
    if kind == "diff":
        par_ref, q_ref, k_ref, v_ref, kb_ref, g_ref, o_ref, m_ref, l_ref, acc_ref, s_ref, p_ref = refs
    else:
        q_ref, k_ref, v_ref, kb_ref, o_ref, m_ref, l_ref, acc_ref, s_ref, p_ref = refs
    sum_in_v = kind != "diff"
    hg = pl.program_id(1)
    i = pl.program_id(2)
    m_ref[...] = jnp.full(m_ref.shape, NEG, F32)
    l_ref[...] = jnp.zeros(l_ref.shape, F32)
    acc_ref[...] = jnp.zeros(acc_ref.shape, F32)

    q = q_ref[0]
    lane = lax.broadcasted_iota(jnp.int32, (1, 128), 1)
    qs = []
    for e in range(ns):
        if kind == "mla":
            qs.append(q[:, e * HEAD_PAD:(e + 1) * HEAD_PAD])
        else:
            qp = q[:, (e // 2) * 128:(e // 2 + 1) * 128]
            qs.append(jnp.where((lane < 64) if e % 2 == 0 else (lane >= 64), qp, jnp.zeros_like(qp)))
    qstart = pl.multiple_of(i * tq, tq)
    qend = qstart + (tq - 1)
    qpos = qstart + lax.broadcasted_iota(jnp.int32, (tq, 1), 0)
    ref = [-kb_ref[0, 0, e:e + 1, pl.ds(qstart + (tq - 128), 128)][:, 127:128] for e in range(ns)]
    if kind == "diff":
        slope = [LOG2E * par_ref[1 + hg * (ns // 2) + p] for p in range(ns // 2)]

    tk = tq

    def k_slot(k, e):
        if kind == "mla":
            return k[:, e * HEAD_PAD:(e + 1) * HEAD_PAD]
        return k[:, (e // 2) * 128:(e // 2 + 1) * 128]

    def v_slot(v, e):
        if sum_in_v:
            return v[:, e * 128:(e + 1) * 128]
        return v[:, (e // 2) * 128:(e // 2 + 1) * 128]

    def scores(j, slot):
        k0 = pl.multiple_of(j * tk, tk)
        k = k_ref[0, pl.ds(k0, tk), :]
        for e in range(ns):
            s_ref[slot, e] = _dot_nt(qs[e], k_slot(k, e)) + (kb_ref[0, 0, e:e + 1, pl.ds(k0, tk)] + ref[e])

    def softmax_pv(j, slot, diag):
        k0 = pl.multiple_of(j * tk, tk)
        v = v_ref[0, pl.ds(k0, tk), :]
        if diag:
            kpos = k0 + lax.broadcasted_iota(jnp.int32, (1, tk), 1)
            valid = kpos >= front
        for e in range(ns):
            def logits(r0, n, e=e):
                s = s_ref[slot, e, r0:r0 + n, :]
                if diag:
                    qp = qpos[r0:r0 + n]
                if diag and kind == "diff":
                    s = s - (2.0 * slope[e // 2]) * jnp.maximum(kpos - qp, 0).astype(F32)
                if diag:
                    if kind == "fox":
                        mask = jnp.logical_and(kpos <= qp, valid)
                    else:
                        mask = jnp.logical_and((kpos >> 6) <= (qp >> 6), valid)
                    s = jnp.where(mask, s, NEG)
                return s

            _online(logits, v_slot(v, e), e, m_ref, l_ref, acc_ref, p_ref, sum_in_v)

    scores(0, 0)

    def pair_body(jj, c):
        j = 2 * jj
        scores(j + 1, 1)
        softmax_pv(j, 0, False)
        scores(j + 2, 0)
        softmax_pv(j + 1, 1, False)
        return c

    lax.fori_loop(0, i // 2, pair_body, 0)

    @pl.when(i % 2 == 0)
    def _():
        softmax_pv(i, 0, True)

    @pl.when(i % 2 == 1)
    def _():
        scores(i, 1)
        softmax_pv(i - 1, 0, False)
        softmax_pv(i, 1, True)

    outs = []
    for p in range(ns // 2):
        e0, e1 = 2 * p, 2 * p + 1
        if kind == "diff":
            o = acc_ref[e0] / l_ref[e0] - par_ref[0] * (acc_ref[e1] / l_ref[e1])
            ms = jnp.mean(o * o, axis=-1, keepdims=True)
            outs.append(o * lax.rsqrt(ms + EPS) * g_ref[:, p * 128:(p + 1) * 128])
        else:
            r0 = acc_ref[e0] / pltpu.roll(acc_ref[e0], 64, axis=1)
            r1 = acc_ref[e1] / pltpu.roll(acc_ref[e1], 64, axis=1)
            outs.append(jnp.where(lane < 64, r0, pltpu.roll(r1, 64, axis=1)))
    o_ref[0] = (outs[0] if len(outs) == 1 else jnp.concatenate(outs, axis=1)).astype(o_ref.dtype)


def _prompt_attn(kind, q, k, v, kb, bsz, lp, front, par=None, gain=None):
    rows = q.shape[0]
    tq = ATT_BLOCK
    ns = ATT_STREAMS
    wq = (ns // 2) * (2 * HEAD_PAD if kind == "mla" else 128)
    wv = (ns // 2) * (128 if kind == "diff" else 256)
    wo = (ns // 2) * 128
    n_hg = v.shape[1] // wv
    nq = lp // tq
    kb_b, kb_h = kb.shape[0] > 1, kb.shape[1] > 1
    in_specs = [pl.BlockSpec((1, tq, wq), lambda b, h, i: (0, b * nq + i, h)),
                pl.BlockSpec((1, lp, wq), lambda b, h, i: (0, b, h)),
                pl.BlockSpec((1, lp, wv), lambda b, h, i: (0, b, h)),
                pl.BlockSpec((1, 1, ns, lp), lambda b, h, i: (b if kb_b else 0, h if kb_h else 0, 0, 0))]
    args = [q[None], k[None], v[None], kb]
    if kind == "diff":
        in_specs = [pl.BlockSpec(memory_space=pltpu.SMEM)] + in_specs + [pl.BlockSpec((1, wo), lambda b, h, i: (0, h))]
        args = [par] + args + [gain]
    body = functools.partial(_prompt_attn_body, kind=kind, front=front, tq=tq, ns=ns)
    return pl.pallas_call(
        body,
        grid=(bsz, n_hg, nq),
        in_specs=in_specs,
        out_specs=pl.BlockSpec((1, tq, wo), lambda b, h, i: (0, b * nq + i, h)),
        out_shape=jax.ShapeDtypeStruct((1, rows, n_hg * wo), BF16),
        scratch_shapes=[pltpu.VMEM((ns, tq, 128), F32), pltpu.VMEM((ns, tq, 128), F32),
                        pltpu.VMEM((ns, tq, 128), F32), pltpu.VMEM((2, ns, tq, tq), F32),
                        pltpu.VMEM((ns, tq, tq), BF16)],
        compiler_params=pltpu.CompilerParams(
            dimension_semantics=("parallel", "parallel", "arbitrary"), vmem_limit_bytes=VMEM_LIMIT),
    )(*args)[0]


def _cumsum_body(x_ref, tri_ref, o_ref, carry_ref):
    @pl.when(pl.program_id(0) == 0)
    def _():
        carry_ref[...] = jnp.zeros(carry_ref.shape, F32)

    y = jnp.dot(x_ref[...], tri_ref[...], preferred_element_type=F32,
                precision=lax.Precision.HIGHEST) + carry_ref[...]
    o_ref[...] = y
    carry_ref[...] = y[:, -1:]


def _cumsum_lanes(x, blk=256):
    rows, n = x.shape
    tri = (jnp.arange(blk)[:, None] <= jnp.arange(blk)[None, :]).astype(F32)
    return pl.pallas_call(
        _cumsum_body,
        grid=(n // blk,),
        in_specs=[pl.BlockSpec((rows, blk), lambda j: (0, j)), _const_spec((blk, blk))],
        out_specs=pl.BlockSpec((rows, blk), lambda j: (0, j)),
        out_shape=jax.ShapeDtypeStruct((rows, n), F32),
        scratch_shapes=[pltpu.VMEM((rows, 1), F32)],
        compiler_params=pltpu.CompilerParams(dimension_semantics=("arbitrary",)),
    )(x, tri)


def _expand_rows(x, rep):
    h, n = x.shape
    return jnp.broadcast_to(x[:, None, :], (h, rep, n)).reshape(h * rep, n)


def _dec_online(s, v16, m_ref, l_ref, acc_ref):
    m_prev = m_ref[...]
    m_new = jnp.maximum(m_prev, jnp.max(s, axis=-1, keepdims=True))
    alpha = jnp.exp2(m_prev - m_new)
    p = jnp.exp2(s - m_new)
    l_ref[...] = alpha * l_ref[...] + jnp.sum(p, axis=-1, keepdims=True)
    acc_ref[...] = alpha * acc_ref[...] + _dot(p.astype(BF16), v16)
    m_ref[...] = m_new


def _dec_init(m_ref, l_ref, acc_ref):
    m_ref[...] = jnp.full(m_ref.shape, NEG, F32)
    l_ref[...] = jnp.zeros(l_ref.shape, F32)
    acc_ref[...] = jnp.zeros(acc_ref.shape, F32)


def _diag_blocks(o, n_heads, ds, width):
    return jnp.concatenate([o[h * ds:(h + 1) * ds, h * width:(h + 1) * width] for h in range(n_heads)], axis=1)


def _diff_dec_body(par_ref, q_ref, kc_ref, vc_ref, kn_ref, vn_ref, g_ref, _, o_ref, m_ref, l_ref, acc_ref,
                   *, past, kb, ds, n_heads):
    jb = pl.program_id(1)
    hr = 2 * ds
    rows = n_heads * hr
    dv = 2 * DH_B
    r = lax.broadcasted_iota(jnp.int32, (rows, 1), 0)
    head = r // hr
    slope = LOG2E * jnp.exp2(-8.0 * (head + 1).astype(F32) / n_heads)
    qpos = past + (r % ds)
    q = q_ref[0]

    def key_block(k16, v16, key0):
        col = lax.broadcasted_iota(jnp.int32, (1, k16.shape[0]), 1)
        kpos = key0 + col // n_heads
        s = _dot_nt(q, k16) - slope * jnp.abs(qpos - kpos).astype(F32)
        s = jnp.where(col % n_heads == head, s, NEG)
        _dec_online(s, v16, m_ref, l_ref, acc_ref)

    @pl.when(jb == 0)
    def _():
        _dec_init(m_ref, l_ref, acc_ref)
        key_block(kn_ref[0], vn_ref[0], past)

    key_block(kc_ref[0, 0].astype(BF16), vc_ref[0, 0].astype(BF16), jb * kb)

    @pl.when(jb == pl.num_programs(1) - 1)
    def _():
        o = acc_ref[...] / l_ref[...]
        outs = []
        for h in range(n_heads):
            oh = o[h * hr:h * hr + ds] - par_ref[0] * o[h * hr + ds:(h + 1) * hr]
            ms = jnp.mean(oh * oh, axis=-1, keepdims=True)
            outs.append(oh * lax.rsqrt(ms + EPS) * g_ref[:, h * dv:(h + 1) * dv])
        o_ref[0] = jnp.concatenate(outs, axis=1).astype(o_ref.dtype)


def _fox_dec_body(q_ref, kc_ref, vc_ref, kn_ref, vn_ref, fc_ref, fn_ref, _, o_ref, m_ref, l_ref, acc_ref,
                  *, ds, n_heads):
    jb = pl.program_id(1)
    rows = n_heads * ds
    q = q_ref[0]
    fnew = fn_ref[0][:, :ds]
    fref = _expand_rows(fn_ref[0][:, 0:1], ds)

    @pl.when(jb == 0)
    def _():
        _dec_init(m_ref, l_ref, acc_ref)
        r = lax.broadcasted_iota(jnp.int32, (rows, 1), 0)
        kidx = lax.broadcasted_iota(jnp.int32, (1, ds), 1)
        s = _dot_nt(q, kn_ref[0]) + LOG2E * (fref - _expand_rows(fnew, ds))
        s = jnp.where(kidx <= (r % ds), s, NEG)
        _dec_online(s, vn_ref[0].astype(BF16), m_ref, l_ref, acc_ref)

    s = _dot_nt(q, kc_ref[0, 0].astype(BF16)) + LOG2E * (fref - _expand_rows(fc_ref[0], ds))
    _dec_online(s, vc_ref[0, 0].astype(BF16), m_ref, l_ref, acc_ref)

    @pl.when(jb == pl.num_programs(1) - 1)
    def _():
        o = acc_ref[...] / l_ref[...]
        o_ref[0] = _diag_blocks(o, n_heads, ds, DH_C).astype(o_ref.dtype)


def _mla_dec_body(qn_ref, qp_ref, cc_ref, pc_ref, cn_ref, pn_ref, wk_ref, wv_ref, ones_ref, _,
                  o_ref, m_ref, l_ref, acc_ref, *, ds, n_heads):
    jb = pl.program_id(1)
    qn = qn_ref[0]
    qp = qp_ref[0]
    ones_h = ones_ref[...]

    def key_block(ckv, kpe):
        c16 = ckv.astype(BF16)
        kn = _dot(c16, wk_ref[...])
        v = _dot(c16, wv_ref[...])
        n = kpe.shape[0]
        ss = _dot_nt(ones_h, (kn * kn).astype(BF16)) + _dot_nt(jnp.ones((n_heads, ROPE_D), BF16),
                                                               (kpe * kpe).astype(BF16))
        rinv = lax.rsqrt(ss * (1.0 / (NOPE_D + ROPE_D)) + EPS)
        s = _dot_nt(qn, kn.astype(BF16)) + _dot_nt(qp, kpe.astype(BF16))
        s = s * _expand_rows(rinv, ds)
        _dec_online(s, v.astype(BF16), m_ref, l_ref, acc_ref)

    @pl.when(jb == 0)
    def _():
        _dec_init(m_ref, l_ref, acc_ref)
        key_block(cn_ref[0], pn_ref[0])

    key_block(cc_ref[0, 0], pc_ref[0, 0])

    @pl.when(jb == pl.num_programs(1) - 1)
    def _():
        o = acc_ref[...] / l_ref[...]
        o_ref[0] = _diag_blocks(o, n_heads, ds, V_D).astype(o_ref.dtype)


def _per_seq(a):
    return (a, pl.BlockSpec((1,) + a.shape[1:], lambda b, j: (b, 0, 0)))


def _dec_const(a):
    return (a, _const_spec(a.shape))


def _dec_call(body, ins, prev, row0, ds, rows, acc_w, nb, n_kb, smem=None):
    in_specs = [spec for _, spec in ins] + [pl.BlockSpec(memory_space=pl.ANY)]
    args = [a for a, _ in ins] + [prev[None]]
    if smem is not None:
        in_specs = [pl.BlockSpec(memory_space=pltpu.SMEM)] + in_specs
        args = [smem] + args
    blk0 = row0 // ds
    return pl.pallas_call(
        body,
        grid=(nb, n_kb),
        in_specs=in_specs,
        out_specs=pl.BlockSpec((1, ds, prev.shape[1]), lambda b, j: (0, blk0 + b, 0)),
        out_shape=jax.ShapeDtypeStruct((1,) + prev.shape, prev.dtype),
        input_output_aliases={len(args) - 1: 0},
        scratch_shapes=[pltpu.VMEM((rows, 1), F32), pltpu.VMEM((rows, 1), F32), pltpu.VMEM((rows, acc_w), F32)],
        compiler_params=pltpu.CompilerParams(
            dimension_semantics=("parallel", "arbitrary"), vmem_limit_bytes=VMEM_LIMIT),
    )(*args)[0]


def kernel(x_prompt, x_sample, state_s5_re, state_s5_im, cache_diff_k, cache_diff_v, cache_fox_k, cache_fox_v, cache_fox_logf, cache_mla_ckv, cache_mla_kpe, meta_tokens, ffn_norm, ffn_w_in, ffn_w_out, mix_norm, ab_w_in, ab_w_out, s5_a_re, s5_a_im, s5_log_step, s5_b_re, s5_b_im, s5_c_re, s5_c_im, s5_d, s5_glu_w, s5_glu_b, diff_q_norm, diff_k_norm, diff_lam, diff_sub_norm, cd_w_in, cd_w_out, fox_q_norm, fox_k_norm, fox_f_bias, mla_q_a_norm, mla_q_b, mla_kv_a_norm, mla_kv_b, mla_q_norm, mla_k_norm):
    bsz, seq, dm = x_prompt.shape
    nb, ds, _ = x_sample.shape
    n_meta = meta_tokens.shape[0]
    past = cache_diff_k.shape[2]
    front = ROW_ALIGN - n_meta
    lp = front + n_meta + seq
    ltot = n_meta + seq
    assert n_meta + front == ROW_ALIGN and lp % ATT_BLOCK == 0 and front % CHUNK == CHUNK - n_meta
    assert ds == S5_STEP and past % CHUNK == 0 and ds <= CHUNK
    kb = min(DEC_KB, past)
    assert past % kb == 0
    n_kb = past // kb
    assert ffn_norm.shape[0] == 2 and ab_w_in.shape[0] == 1 and cd_w_in.shape[0] == 1

    h_b = cache_diff_k.shape[3]
    h_c = cache_fox_k.shape[3]
    h_d = mla_q_b.shape[2] // (NOPE_D + ROPE_D)
    s5w = s5_glu_w.shape[1]
    n_grp = s5w // S5_GROUP
    qkw = h_b * 2 * DH_B
    fox_w = h_c * DH_C
    q_lora = mla_q_a_norm.shape[1]
    kv_lora = mla_kv_a_norm.shape[1]
    d_qk = NOPE_D + ROPE_D

    meta = jnp.broadcast_to(meta_tokens[None].astype(F32), (bsz, n_meta, dm))
    xp = jnp.concatenate([jnp.zeros((bsz, front, dm), F32), meta, x_prompt], axis=1)
    n_p = bsz * lp
    x = jnp.concatenate([xp.reshape(n_p, dm), x_sample.reshape(nb * ds, dm)], axis=0)
    rows = x.shape[0]
    tm = _row_tile(rows)

    ones64 = _block_diag_ones(64)
    ones128 = _block_diag_ones(128)

    x = _ffn(x, ffn_norm[0, 0], ffn_w_in[0, 0], ffn_w_out[0, 0], tm)

    gq = (jnp.tile(diff_q_norm[0], 2 * h_b) * (DH_B ** -0.5 * LOG2E)).reshape(1, qkw)
    gk = jnp.tile(diff_k_norm[0], 2 * h_b).reshape(1, qkw)
    s5_cols = S5_STEP * S5_GROUP
    q16, k32, k16, v32, v16, u_t = _row_call(
        functools.partial(_ab_in_body, widths=(s5w, qkw)),
        [x], [mix_norm[0].reshape(1, dm), ab_w_in[0].astype(BF16), gq, gk, ones64],
        [qkw, qkw, qkw, h_b * 2 * DH_B, h_b * 2 * DH_B], [BF16, F32, BF16, F32, BF16], tm,
        group_outs=[jax.ShapeDtypeStruct((n_grp, rows // S5_STEP, s5_cols), BF16)],
        scratch=[pltpu.VMEM((s5w // 128, tm, 128), F32)])

    mats = _s5_matrices(s5_a_re[0], s5_a_im[0], s5_log_step[0], s5_b_re[0], s5_b_im[0],
                        s5_c_re[0], s5_c_im[0], s5_d[0])
    n_ch = lp // S5_STEP
    y_t, st_p = _s5_scan(u_t, jnp.zeros((n_grp, bsz, 4 * P_A), F32), mats, n_ch, bsz, 0)
    h_re = jnp.transpose(state_s5_re[0].astype(F32), (1, 0, 2))
    h_im = jnp.transpose(state_s5_im[0].astype(F32), (1, 0, 2))
    y_t, st_s = _s5_scan(u_t, jnp.concatenate([h_re, h_im, h_im, h_re], axis=-1), mats, 1, nb,
                         n_p // S5_STEP, y_prev=y_t)

    lv = diff_lam[0].astype(F32)
    lam_init = 0.8 - 0.6 * math.exp(-0.3 * 0)
    lam = jnp.exp(jnp.sum(lv[0] * lv[1])) - jnp.exp(jnp.sum(lv[2] * lv[3])) + lam_init
    slopes = jnp.exp2(-8.0 * jnp.arange(1, h_b + 1, dtype=F32) / h_b)
    par = jnp.concatenate([lam[None], slopes]).astype(F32)
    subg = (jnp.tile(diff_sub_norm[0], h_b) * (1.0 - lam_init)).reshape(1, qkw)
    kpad = jnp.arange(lp) < front
    kb_diff = jnp.where(kpad[None, :], NEG, LOG2E * slopes[:, None] * jnp.arange(lp, dtype=F32)[None, :])
    kb_diff = jnp.broadcast_to(kb_diff[None, :, None, :], (1, h_b, 2, lp)).reshape(
        1, 2 * h_b // ATT_STREAMS, ATT_STREAMS, lp)
    o_all = _prompt_attn("diff", q16, k16, v16, kb_diff, bsz, lp, front, par=par, gain=subg)
    qs = q16[n_p:].reshape(nb, ds, h_b, 2, DH_B)
    eye_2 = jnp.eye(2, dtype=BF16)
    qbd = jnp.einsum('bqhmd,mM->bhmqMd', qs, eye_2).reshape(nb, h_b * 2 * ds, 2 * DH_B)
    cache_spec = lambda w: pl.BlockSpec((1, 1, kb, w), lambda b, j: (0, b, j, 0))
    head_rows = lambda a: a.reshape(a.shape[0], nb, past * h_b, 2 * DH_B)
    head_cache_spec = pl.BlockSpec((1, 1, kb * h_b, 2 * DH_B), lambda b, j: (0, b, j, 0))
    o_all = _dec_call(
        functools.partial(_diff_dec_body, past=past, kb=kb, ds=ds, n_heads=h_b),
        [_per_seq(qbd), (head_rows(cache_diff_k), head_cache_spec), (head_rows(cache_diff_v), head_cache_spec),
         _per_seq(k16[n_p:].reshape(nb, ds * h_b, 2 * DH_B)), _per_seq(v16[n_p:].reshape(nb, ds * h_b, 2 * DH_B)),
         _dec_const(subg)],
        o_all, n_p, ds, 2 * h_b * ds, 2 * DH_B, nb, n_kb, smem=par)

    x = _row_call(
        functools.partial(_ab_out_body, s5w=s5w),
        [x, o_all],
        [s5_glu_w[0].astype(BF16), s5_glu_b[0].reshape(1, s5w), ab_w_out[0].astype(BF16)],
        [dm], [F32], tm, group_ins=[y_t], scratch=[pltpu.VMEM((s5w // 128, tm, 128), F32)])[0]

    x = _ffn(x, ffn_norm[0, 1], ffn_w_in[0, 1], ffn_w_out[0, 1], tm)

    x = _ffn(x, ffn_norm[1, 0], ffn_w_in[1, 0], ffn_w_out[1, 0], tm)

    half = ROPE_D // 2
    inv = ROPE_THETA ** (-jnp.arange(half, dtype=F32) / half)
    pos = jnp.concatenate([jnp.tile(jnp.arange(lp, dtype=jnp.int32) - front, bsz),
                           jnp.tile(past + jnp.arange(ds, dtype=jnp.int32), nb)]).astype(F32)
    ang = pos[:, None] * inv[None, :]
    pad_r = HEAD_PAD - NOPE_D - ROPE_D
    cos_t = jnp.concatenate([jnp.ones((rows, NOPE_D), F32), jnp.cos(ang), jnp.cos(ang),
                             jnp.zeros((rows, pad_r), F32)], axis=1)
    sin_t = jnp.concatenate([jnp.zeros((rows, NOPE_D), F32), jnp.sin(ang), jnp.sin(ang),
                             jnp.zeros((rows, pad_r), F32)], axis=1)

    wcd = cd_w_in[0]
    c_fg = 3 * fox_w
    c_qa = c_fg + h_c
    c_kva = c_qa + q_lora
    c_pe = c_kva + kv_lora
    w_pe = wcd[:, c_pe:c_pe + ROPE_D]
    zc = lambda n: jnp.zeros((dm, n), F32)
    w_cd = jnp.concatenate([
        wcd[:, :3 * fox_w], wcd[:, c_qa:c_qa + q_lora], wcd[:, c_kva:c_kva + kv_lora],
        zc(NOPE_D), w_pe, zc(pad_r),
        zc(NOPE_D), -w_pe[:, half:], w_pe[:, :half], zc(pad_r),
        wcd[:, c_fg:c_fg + h_c], zc(HEAD_PAD - h_c)], axis=1).astype(BF16)
    qb = mla_q_b[0].reshape(q_lora, h_d, d_qk)
    zq = lambda n: jnp.zeros((q_lora, h_d, n), F32)
    qb_pad = jnp.concatenate([qb, zq(pad_r)], axis=-1).reshape(q_lora, h_d * HEAD_PAD)
    qb_rot = jnp.concatenate([zq(NOPE_D), -qb[..., NOPE_D + half:], qb[..., NOPE_D:NOPE_D + half], zq(pad_r)],
                             axis=-1).reshape(q_lora, h_d * HEAD_PAD)
    wq2 = jnp.concatenate([qb_pad, qb_rot], axis=1).astype(BF16)
    kvb = mla_kv_b[0].reshape(kv_lora, h_d, NOPE_D + V_D)
    wk_pad = jnp.concatenate([kvb[..., :NOPE_D], jnp.zeros((kv_lora, h_d, HEAD_PAD - NOPE_D), F32)],
                             axis=-1).reshape(kv_lora, h_d * HEAD_PAD).astype(BF16)
    wk_cmp = kvb[..., :NOPE_D].reshape(kv_lora, h_d * NOPE_D).astype(BF16)
    wv_cmp = kvb[..., NOPE_D:].reshape(kv_lora, h_d * V_D).astype(BF16)
    gfq = (jnp.tile(fox_q_norm[0], h_c) * (DH_C ** -0.5 * LOG2E)).reshape(1, fox_w)
    gfk = jnp.tile(fox_k_norm[0], h_c).reshape(1, fox_w)
    fbias = jnp.concatenate([fox_f_bias[0], jnp.zeros((HEAD_PAD - h_c,), F32)]).reshape(1, HEAD_PAD)
    gmq = jnp.tile(jnp.concatenate([mla_q_norm[0] * mla_k_norm[0] * (d_qk ** -0.5 * LOG2E), jnp.zeros((pad_r,), F32)]),
                   h_d).reshape(1, h_d * HEAD_PAD)

    (fq16, fk32, fk16, fv32, fv16, logf, qm16, ckv32, kpe32, km16, vm16) = _row_call(
        functools.partial(_cd_in_body, fox_w=fox_w, q_lora=q_lora, kv_lora=kv_lora, n_heads=h_d),
        [x, cos_t, sin_t],
        [mix_norm[1].reshape(1, dm), w_cd, gfq, gfk, fbias, mla_q_a_norm[0].reshape(1, q_lora), wq2,
         mla_kv_a_norm[0].reshape(1, kv_lora), wk_pad, wv_cmp, gmq, ones64, ones128],
        [fox_w, fox_w, fox_w, fox_w, 2 * fox_w, h_c, h_d * HEAD_PAD, kv_lora, ROPE_D, h_d * HEAD_PAD, 2 * h_d * V_D],
        [BF16, F32, BF16, F32, BF16, F32, BF16, F32, F32, BF16, BF16], tm)

    logf_p = jnp.transpose(logf[:n_p].reshape(bsz, lp, h_c), (0, 2, 1)).reshape(bsz * h_c, lp)
    f_p = _cumsum_lanes(logf_p).reshape(bsz, h_c, lp)
    logf_s = jnp.concatenate([
        jnp.transpose(cache_fox_logf[0].astype(F32), (0, 2, 1)),
        jnp.transpose(logf[n_p:].reshape(nb, ds, h_c), (0, 2, 1)),
        jnp.zeros((nb, h_c, 256 - ds), F32)], axis=2).reshape(nb * h_c, past + 256)
    f_s = _cumsum_lanes(logf_s).reshape(nb, h_c, past + 256)

    kb_fox = jnp.where(kpad[None, None, :], NEG, -LOG2E * f_p).reshape(bsz, h_c // ATT_STREAMS, ATT_STREAMS, lp)
    oc_all = _prompt_attn("fox", fq16, fk16, fv16, kb_fox, bsz, lp, front)
    kb_mla = jnp.broadcast_to(jnp.where(kpad, NEG, 0.0).astype(F32)[None, None, None, :], (1, 1, ATT_STREAMS, lp))
    od_all = _prompt_attn("mla", qm16, km16, vm16, kb_mla, bsz, lp, front)

    eye_c = jnp.eye(h_c, dtype=BF16)
    fqs = fq16[n_p:].reshape(nb, ds, h_c, DH_C)
    fq_bd = jnp.einsum('bqhd,hH->bhqHd', fqs, eye_c).reshape(nb, h_c * ds, fox_w)
    fkc = cache_fox_k.reshape(cache_fox_k.shape[0], nb, past, fox_w)
    fvc = cache_fox_v.reshape(cache_fox_v.shape[0], nb, past, fox_w)
    oc_all = _dec_call(
        functools.partial(_fox_dec_body, ds=ds, n_heads=h_c),
        [_per_seq(fq_bd), (fkc, cache_spec(fox_w)), (fvc, cache_spec(fox_w)),
         _per_seq(fk16[n_p:].reshape(nb, ds, fox_w)), _per_seq(fv32[n_p:].reshape(nb, ds, fox_w)),
         (f_s, pl.BlockSpec((1, h_c, kb), lambda b, j: (b, 0, j))),
         (f_s, pl.BlockSpec((1, h_c, 128), lambda b, j: (b, 0, past // 128)))],
        oc_all, n_p, ds, h_c * ds, fox_w, nb, n_kb)

    eye_d = jnp.eye(h_d, dtype=BF16)
    qms = qm16[n_p:].reshape(nb, ds, h_d, HEAD_PAD)
    qn_bd = jnp.einsum('bqhd,hH->bhqHd', qms[..., :NOPE_D], eye_d).reshape(nb, h_d * ds, h_d * NOPE_D)
    qp_s = jnp.transpose(qms[..., NOPE_D:NOPE_D + ROPE_D], (0, 2, 1, 3)).reshape(nb, h_d * ds, ROPE_D)
    ones_h = jnp.repeat(jnp.eye(h_d, dtype=BF16), NOPE_D, axis=1)
    od_all = _dec_call(
        functools.partial(_mla_dec_body, ds=ds, n_heads=h_d),
        [_per_seq(qn_bd), _per_seq(qp_s),
         (cache_mla_ckv, pl.BlockSpec((1, 1, kb, kv_lora), lambda b, j: (0, b, j, 0))),
         (cache_mla_kpe, pl.BlockSpec((1, 1, kb, ROPE_D), lambda b, j: (0, b, j, 0))),
         _per_seq(ckv32[n_p:].reshape(nb, ds, kv_lora)), _per_seq(kpe32[n_p:].reshape(nb, ds, ROPE_D)),
         _dec_const(wk_cmp), _dec_const(wv_cmp), _dec_const(ones_h)],
        od_all, n_p, ds, h_d * ds, h_d * V_D, nb, n_kb)

    x = _row_call(functools.partial(_cd_out_body, fox_w=fox_w), [x, oc_all, od_all],
                  [cd_w_out[0].astype(BF16)], [dm], [F32], tm)[0]

    x = _ffn(x, ffn_norm[1, 1], ffn_w_in[1, 1], ffn_w_out[1, 1], tm)

    def p_rows(a, shape):
        w = a.shape[1]
        return a[:n_p].reshape(bsz, lp, w)[:, front:front + ltot].reshape((1, bsz, ltot) + shape)

    def s_rows(a, shape):
        return a[n_p:].reshape((1, nb, ds) + shape)

    def s5_state(st):
        st = jnp.transpose(st, (1, 0, 2))
        return st[None, :, :, :P_A], st[None, :, :, P_A:]

    y_prompt = x[:n_p].reshape(bsz, lp, dm)[:, front + n_meta:]
    y_sample = x[n_p:].reshape(nb, ds, dm)
    s5_re_p, s5_im_p = s5_state(st_p)
    s5_re_s, s5_im_s = s5_state(st_s)
    return (y_prompt, y_sample,
            s5_re_p, s5_im_p, p_rows(k32, (h_b, 2 * DH_B)), p_rows(v32, (h_b, 2 * DH_B)),
            p_rows(fk32, (h_c, DH_C)), p_rows(fv32, (h_c, DH_C)), p_rows(logf, (h_c,)),
            p_rows(ckv32, (kv_lora,)), p_rows(kpe32, (ROPE_D,)),
            s5_re_s, s5_im_s, s_rows(k32, (h_b, 2 * DH_B)), s_rows(v32, (h_b, 2 * DH_B)),
            s_rows(fk32, (h_c, DH_C)), s_rows(fv32, (h_c, DH_C)), s_rows(logf, (h_c,)),
            s_rows(ckv32, (kv_lora,)), s_rows(kpe32, (ROPE_D,)))
```

```python
import functools
import math

import jax
import jax.numpy as jnp
from jax import lax
from jax.experimental import pallas as pl
from jax.experimental.pallas import tpu as pltpu

F32 = jnp.float32
BF16 = jnp.bfloat16

EPS = 1e-6
CHUNK = 64
ROW_ALIGN = 256
S5_GROUP = 16
S5_STEP = 16
P_A = 64
DH_B = 64
DH_C = 64
NOPE_D = 64
ROPE_D = 32
V_D = 64
HEAD_PAD = 128
ROPE_THETA = 10000.0
NEG = -1e30
LOG2E = math.log2(math.e)
VMEM_LIMIT = 56 * 1024 * 1024
ATT_BLOCK = 256
ATT_STREAMS = 4
ATT_ROWS = 64
DEC_KB = 1024


def _dot(a, b):
    return jnp.dot(a, b, preferred_element_type=F32)


def _dot_nt(a, b):
    return lax.dot_general(a, b, (((1,), (1,)), ((), ())), preferred_element_type=F32)


def _rms_rows(x, g):
    ms = jnp.mean(x * x, axis=-1, keepdims=True)
    return x * lax.rsqrt(ms + EPS) * g


def _group_sumsq(x, ones_bd):
    w = x.shape[-1]
    parts = [_dot((x[:, c:c + 256] * x[:, c:c + 256]).astype(BF16), ones_bd) for c in range(0, w, 256)]
    return parts[0] if len(parts) == 1 else jnp.concatenate(parts, axis=1)


def _block_diag_ones(group, n=256):
    r = jnp.arange(n) // group
    return (r[:, None] == r[None, :]).astype(BF16)


def _const_spec(shape):
    nd = len(shape)
    return pl.BlockSpec(shape, lambda *_: (0,) * nd, pipeline_mode=pl.Buffered(1))


def _row_tile(rows, cap=512):
    t = cap
    while rows % t:
        t //= 2
    return t


def _group_spec(a, tm):
    return pl.BlockSpec((a.shape[0], tm // S5_STEP, a.shape[2]), lambda i: (0, i, 0))


def _row_call(body, row_ins, consts, out_widths, out_dtypes, tm, group_ins=(), group_outs=(), scratch=()):
    rows = row_ins[0].shape[0]
    in_specs = [pl.BlockSpec((tm, a.shape[1]), lambda i: (i, 0)) for a in row_ins]
    in_specs += [_group_spec(a, tm) for a in group_ins]
    in_specs += [_const_spec(c.shape) for c in consts]
    out_specs = [pl.BlockSpec((tm, w), lambda i: (i, 0)) for w in out_widths]
    out_specs += [_group_spec(a, tm) for a in group_outs]
    out_shape = [jax.ShapeDtypeStruct((rows, w), d) for w, d in zip(out_widths, out_dtypes)]
    out_shape += list(group_outs)
    return pl.pallas_call(
        body,
        grid=(rows // tm,),
        in_specs=in_specs,
        out_specs=out_specs,
        out_shape=out_shape,
        scratch_shapes=list(scratch),
        compiler_params=pltpu.CompilerParams(
            dimension_semantics=("parallel",), vmem_limit_bytes=VMEM_LIMIT),
    )(*row_ins, *group_ins, *consts)


def _ffn_body(x_ref, g_ref, win_ref, wout_ref, o_ref, *, d_ff, tf):
    x = x_ref[...]
    xn = _rms_rows(x, g_ref[...]).astype(BF16)
    acc = jnp.zeros(x.shape, F32)
    for c in range(0, d_ff, tf):
        gate = _dot(xn, win_ref[:, c:c + tf])
        up = _dot(xn, win_ref[:, d_ff + c:d_ff + c + tf])
        a = (gate * jax.nn.sigmoid(gate) * up).astype(BF16)
        acc = acc + _dot(a, wout_ref[c:c + tf, :])
    o_ref[...] = x + 0.5 * acc


def _ffn(x, g, w_in, w_out, tm):
    d_ff = w_out.shape[0]
    body = functools.partial(_ffn_body, d_ff=d_ff, tf=256)
    return _row_call(body, [x], [g.reshape(1, -1), w_in.astype(BF16), w_out.astype(BF16)],
                     [x.shape[1]], [F32], tm)[0]


def _ab_in_body(x_ref, g_ref, w_ref, gq_ref, gk_ref, ones_ref,
                q_ref, k32_ref, k16_ref, v32_ref, v16_ref, ut_ref, us_ref, *, widths):
    s5w, qkw = widths
    xn = _rms_rows(x_ref[...], g_ref[...]).astype(BF16)
    h = _dot(xn, w_ref[...])
    n_chunk = us_ref.shape[1] // S5_STEP
    per_col = 128 // S5_GROUP
    for v in range(s5w // 128):
        us_ref[v] = h[:, v * 128:(v + 1) * 128]
        steps = [us_ref[v, pl.ds(t, n_chunk, stride=S5_STEP), :] for t in range(S5_STEP)]
        for gl in range(per_col):
            ut_ref[v * per_col + gl] = jnp.concatenate(
                [x[:, gl * S5_GROUP:(gl + 1) * S5_GROUP] for x in steps], axis=1).astype(BF16)
    q = h[:, s5w:s5w + qkw]
    k = h[:, s5w + qkw:s5w + 2 * qkw]
    v = h[:, s5w + 2 * qkw:]
    ones_bd = ones_ref[...]
    qn = q * lax.rsqrt(_group_sumsq(q, ones_bd) * (1.0 / DH_B) + EPS) * gq_ref[...]
    kn = k * lax.rsqrt(_group_sumsq(k, ones_bd) * (1.0 / DH_B) + EPS) * gk_ref[...]
    q_ref[...] = qn.astype(BF16)
    k32_ref[...] = kn
    k16_ref[...] = kn.astype(BF16)
    v32_ref[...] = v
    v16_ref[...] = v.astype(BF16)


def _ab_out_body(x_ref, o_ref, yt_ref, gluw_ref, glub_ref, wout_ref, out_ref, ys_ref, *, s5w):
    n_chunk = ys_ref.shape[1] // S5_STEP
    per_col = 128 // S5_GROUP
    for v in range(s5w // 128):
        for t in range(S5_STEP):
            ys_ref[v, pl.ds(t, n_chunk, stride=S5_STEP), :] = jnp.concatenate(
                [yt_ref[v * per_col + gl][:, t * S5_GROUP:(t + 1) * S5_GROUP] for gl in range(per_col)], axis=1)
    y = jnp.concatenate([ys_ref[v] for v in range(s5w // 128)], axis=1)
    g = 0.5 * y * (1.0 + jnp.tanh(math.sqrt(2.0 / math.pi) * (y + 0.044715 * (y * y * y))))
    z = _dot(g.astype(BF16), gluw_ref[...]) + glub_ref[...]
    s5o = g * jax.nn.sigmoid(z)
    m = _dot(s5o.astype(BF16), wout_ref[:s5w, :]) + _dot(o_ref[...], wout_ref[s5w:, :])
    out_ref[...] = x_ref[...] + m


def _heads_with_ones(v):
    lane = lax.broadcasted_iota(jnp.int32, (1, 128), 1)
    outs = []
    for c in range(0, v.shape[1], 128):
        col = v[:, c:c + 128]
        outs.append(jnp.where(lane < 64, col, 1.0))
        outs.append(jnp.where(lane < 64, pltpu.roll(col, 64, axis=1), 1.0))
    return jnp.concatenate(outs, axis=1)


def _cd_in_body(x_ref, cos_ref, sin_ref, g_ref, w_ref, gfq_ref, gfk_ref, fb_ref, gqa_ref, wq2_ref,
                gkva_ref, wk_ref, wv_ref, gmq_ref, ones64_ref, ones128_ref,
                fq_ref, fk32_ref, fk16_ref, fv32_ref, fv16_ref, logf_ref, qm_ref, ckv_ref, kpe_ref,
                km_ref, vm_ref, *, fox_w, q_lora, kv_lora, n_heads):
    xn = _rms_rows(x_ref[...], g_ref[...]).astype(BF16)
    h = _dot(xn, w_ref[...])
    ones64 = ones64_ref[...]
    ones128 = ones128_ref[...]
    fq = h[:, :fox_w]
    fk = h[:, fox_w:2 * fox_w]
    fv = h[:, 2 * fox_w:3 * fox_w]
    c0 = 3 * fox_w
    qa = h[:, c0:c0 + q_lora]
    kva = h[:, c0 + q_lora:c0 + q_lora + kv_lora]
    c1 = c0 + q_lora + kv_lora
    pe_a = h[:, c1:c1 + HEAD_PAD]
    pe_b = h[:, c1 + HEAD_PAD:c1 + 2 * HEAD_PAD]
    fg = h[:, c1 + 2 * HEAD_PAD:c1 + 3 * HEAD_PAD]

    fqn = fq * lax.rsqrt(_group_sumsq(fq, ones64) * (1.0 / DH_C) + EPS) * gfq_ref[...]
    fkn = fk * lax.rsqrt(_group_sumsq(fk, ones64) * (1.0 / DH_C) + EPS) * gfk_ref[...]
    fq_ref[...] = fqn.astype(BF16)
    fk32_ref[...] = fkn
    fk16_ref[...] = fkn.astype(BF16)
    fv32_ref[...] = fv
    fv16_ref[...] = _heads_with_ones(fv).astype(BF16)

    z = fg + fb_ref[...]
    logf = jnp.minimum(z, 0.0) - jnp.log1p(jnp.exp(-jnp.abs(z)))
    logf_ref[...] = logf[:, :logf_ref.shape[1]]

    cos = cos_ref[...]
    sin = sin_ref[...]
    qan = _rms_rows(qa, gqa_ref[...]).astype(BF16)
    q2 = _dot(qan, wq2_ref[...])
    hw = n_heads * HEAD_PAD
    cos_t = jnp.concatenate([cos] * n_heads, axis=1)
    sin_t = jnp.concatenate([sin] * n_heads, axis=1)
    qr = q2[:, :hw] * cos_t + q2[:, hw:] * sin_t
    d_qk = NOPE_D + ROPE_D
    qm = qr * lax.rsqrt(_group_sumsq(qr, ones128) * (1.0 / d_qk) + EPS) * gmq_ref[...]
    qm_ref[...] = qm.astype(BF16)

    ckv = _rms_rows(kva, gkva_ref[...])
    ckv_ref[...] = ckv
    pe = pe_a * cos + pe_b * sin
    kpe_ref[...] = pe[:, NOPE_D:NOPE_D + ROPE_D]
    ckv16 = ckv.astype(BF16)
    kraw = _dot(ckv16, wk_ref[...]) + jnp.concatenate([pe] * n_heads, axis=1)
    km = kraw * lax.rsqrt(_group_sumsq(kraw, ones128) * (1.0 / d_qk) + EPS)
    km_ref[...] = km.astype(BF16)
    vm_ref[...] = _heads_with_ones(_dot(ckv16, wv_ref[...])).astype(BF16)


def _cd_out_body(x_ref, oc_ref, od_ref, wout_ref, out_ref, *, fox_w):
    m = _dot(oc_ref[...], wout_ref[:fox_w, :]) + _dot(od_ref[...], wout_ref[fox_w:, :])
    out_ref[...] = x_ref[...] + m


def _s5_body(*refs, n_chunks, bsz, aliased):
    if aliased:
        u_ref, h0_ref, m_ref, bm_ref, cm_ref, coef_ref, _, y_ref, st_ref, s2_ref, hp_ref = refs
    else:
        u_ref, h0_ref, m_ref, bm_ref, cm_ref, coef_ref, y_ref, st_ref, s2_ref, hp_ref = refs
    u = u_ref[0]
    half = 2 * P_A
    s2 = _dot(u, bm_ref[0])
    s2_ref[0] = s2[:, :half]
    s2_ref[1] = s2[:, half:]
    c1 = coef_ref[0, 0:1, :]
    c2 = coef_ref[0, 1:2, :]
    c3 = coef_ref[0, 2:3, :]

    def step(j, carry):
        ha, hb = carry
        hp_ref[pl.ds(j, bsz, stride=n_chunks), :] = ha
        sa = s2_ref[0, pl.ds(j, bsz, stride=n_chunks), :]
        sb = s2_ref[1, pl.ds(j, bsz, stride=n_chunks), :]
        return ha * c1 + hb * c2 + sa, hb * c1 + ha * c3 + sb

    h0 = h0_ref[0]
    ha, _ = lax.fori_loop(0, n_chunks, step, (h0[:, :half], h0[:, half:]))
    st_ref[0] = ha
    y_ref[0] = _dot(u, m_ref[0]) + _dot(hp_ref[...].astype(BF16), cm_ref[0])


def _s5_scan(u_t, h0, mats, n_chunks, bsz, row0, y_prev=None):
    m_mat, bm, cm, coef = mats
    g, rows_all, w = u_t.shape
    rows = n_chunks * bsz
    assert row0 % rows == 0
    blk = row0 // rows
    aliased = y_prev is not None
    body = functools.partial(_s5_body, n_chunks=n_chunks, bsz=bsz, aliased=aliased)
    per_g = lambda a: pl.BlockSpec((1,) + a.shape[1:], lambda i: (i, 0, 0))
    in_specs = [pl.BlockSpec((1, rows, w), lambda i: (i, blk, 0)),
                per_g(h0), per_g(m_mat), per_g(bm), per_g(cm), per_g(coef)]
    args = [u_t, h0, m_mat, bm, cm, coef]
    if aliased:
        in_specs.append(pl.BlockSpec(memory_space=pl.ANY))
        args.append(y_prev)
    return pl.pallas_call(
        body,
        grid=(g,),
        in_specs=in_specs,
        out_specs=[pl.BlockSpec((1, rows, w), lambda i: (i, blk, 0)),
                   pl.BlockSpec((1, bsz, 2 * P_A), lambda i: (i, 0, 0))],
        out_shape=[jax.ShapeDtypeStruct((g, rows_all, w), F32),
                   jax.ShapeDtypeStruct((g, bsz, 2 * P_A), F32)],
        input_output_aliases={6: 0} if aliased else {},
        scratch_shapes=[pltpu.VMEM((2, rows, 2 * P_A), F32), pltpu.VMEM((rows, 2 * P_A), F32)],
        compiler_params=pltpu.CompilerParams(
            dimension_semantics=("parallel",), vmem_limit_bytes=VMEM_LIMIT),
    )(*args)


def _s5_matrices(a_re, a_im, log_step, b_re, b_im, c_re, c_im, d):
    g = a_re.shape[0]
    t = S5_STEP
    lam = lax.complex(a_re, a_im)
    dl = lam * jnp.exp(log_step)[:, None]
    lam_bar = jnp.exp(dl)
    b_bar = ((lam_bar - 1.0) / lam)[..., None] * lax.complex(b_re, b_im)
    c = lax.complex(c_re, c_im)
    pw = jnp.exp(dl[:, None, :] * jnp.arange(t + 1, dtype=F32)[None, :, None])
    bmc = pw[:, t - 1::-1][:, :, :, None] * b_bar[:, None]
    bmc = jnp.swapaxes(bmc, 2, 3).reshape(g, t * S5_GROUP, P_A)
    bm = jnp.concatenate([bmc.real, bmc.imag, bmc.imag, bmc.real], axis=-1)
    kk = jnp.einsum('gcp,gkp,gpd->gkcd', c, pw[:, :t], b_bar).real
    kk = kk.at[:, 0].add(d.reshape(g, S5_GROUP)[:, :, None] * jnp.eye(S5_GROUP, dtype=F32))
    lag = jnp.arange(t)[None, :] - jnp.arange(t)[:, None]
    toep = jnp.where((lag >= 0)[None, :, :, None, None], kk[:, jnp.clip(lag, 0, t - 1)], 0.0)
    m_mat = jnp.transpose(toep, (0, 1, 4, 2, 3)).reshape(g, t * S5_GROUP, t * S5_GROUP)
    cp = c[:, None] * pw[:, 1:, None, :]
    cpm = jnp.transpose(cp, (0, 3, 1, 2)).reshape(g, P_A, t * S5_GROUP)
    cm = jnp.concatenate([cpm.real, -cpm.imag], axis=1)
    a_t = pw[:, t]
    ar, ai = a_t.real, a_t.imag
    zeros = jnp.zeros_like(ar)
    coef = jnp.stack([jnp.concatenate([ar, ar], -1), jnp.concatenate([-ai, ai], -1),
                      jnp.concatenate([ai, -ai], -1), jnp.concatenate([zeros, zeros], -1)], axis=1)
    return m_mat.astype(BF16), bm.astype(BF16), cm.astype(BF16), coef.astype(F32)


def _online(logits, v, e, m_ref, l_ref, acc_ref, p_ref, sum_in_v):
    tq, tk = p_ref.shape[1], p_ref.shape[2]
    for r0 in range(0, tq, ATT_ROWS):
        rows = slice(r0, r0 + ATT_ROWS)
        m_prev = m_ref[e, rows]
        m_new = jnp.maximum(m_prev, jnp.max(logits(r0, ATT_ROWS), axis=-1, keepdims=True))
        alpha = jnp.exp2(m_prev - m_new)
        if not sum_in_v:
            l_ref[e, rows] = alpha * l_ref[e, rows]
        acc_ref[e, rows] = alpha * acc_ref[e, rows]
        m_ref[e, rows] = m_new
    for r0 in range(0, tq, ATT_ROWS):
        rows = slice(r0, r0 + ATT_ROWS)
        p = jnp.exp2(logits(r0, ATT_ROWS) - jnp.concatenate([m_ref[e, rows]] * (tk // 128), axis=1))
        if not sum_in_v:
            l_ref[e, rows] += jnp.sum(p, axis=-1, keepdims=True)
        p_ref[e, rows] = p.astype(BF16)
    acc_ref[e] += _dot(p_ref[e], v)


def _prompt_attn_body(*refs, kind, front, tq, ns):
    if kind == "diff":
        par_ref, q_ref, k_ref, v_ref, kb_ref, g_ref, o_ref, m_ref, l_ref, acc_ref, s_ref, p_ref = refs
    else:
        q_ref, k_ref, v_ref, kb_ref, o_ref, m_ref, l_ref, acc_ref, s_ref, p_ref = refs
    sum_in_v = kind != "diff"
    hg = pl.program_id(1)
    i = pl.program_id(2)
    m_ref[...] = jnp.full(m_ref.shape, NEG, F32)
    l_ref[...] = jnp.zeros(l_ref.shape, F32)
    acc_ref[...] = jnp.zeros(acc_ref.shape, F32)

    q = q_ref[0]
    lane = lax.broadcasted_iota(jnp.int32, (1, 128), 1)
    qs = []
    for e in range(ns):
        if kind == "mla":
            qs.append(q[:, e * HEAD_PAD:(e + 1) * HEAD_PAD])
        else:
            qp = q[:, (e // 2) * 128:(e // 2 + 1) * 128]
            qs.append(jnp.where((lane < 64) if e % 2 == 0 else (lane >= 64), qp, jnp.zeros_like(qp)))
    qstart = pl.multiple_of(i * tq, tq)
    qend = qstart + (tq - 1)
    qpos = qstart + lax.broadcasted_iota(jnp.int32, (tq, 1), 0)
    ref = [-kb_ref[0, 0, e:e + 1, pl.ds(qstart + (tq - 128), 128)][:, 127:128] for e in range(ns)]
    if kind == "diff":
        slope = [LOG2E * par_ref[1 + hg * (ns // 2) + p] for p in range(ns // 2)]

    tk = tq

    def k_slot(k, e):
        if kind == "mla":
            return k[:, e * HEAD_PAD:(e + 1) * HEAD_PAD]
        return k[:, (e // 2) * 128:(e // 2 + 1) * 128]

    def v_slot(v, e):
        if sum_in_v:
            return v[:, e * 128:(e + 1) * 128]
        return v[:, (e // 2) * 128:(e // 2 + 1) * 128]

    def scores(j, slot):
        k0 = pl.multiple_of(j * tk, tk)
        k = k_ref[0, pl.ds(k0, tk), :]
        for e in range(ns):
            s_ref[slot, e] = _dot_nt(qs[e], k_slot(k, e)) + (kb_ref[0, 0, e:e + 1, pl.ds(k0, tk)] + ref[e])

    def softmax_pv(j, slot, diag):
        k0 = pl.multiple_of(j * tk, tk)
        v = v_ref[0, pl.ds(k0, tk), :]
        if diag:
            kpos = k0 + lax.broadcasted_iota(jnp.int32, (1, tk), 1)
            valid = kpos >= front
        for e in range(ns):
            def logits(r0, n, e=e):
                s = s_ref[slot, e, r0:r0 + n, :]
                if diag:
                    qp = qpos[r0:r0 + n]
                if diag and kind == "diff":
                    s = s - (2.0 * slope[e // 2]) * jnp.maximum(kpos - qp, 0).astype(F32)
                if diag:
                    if kind == "fox":
                        mask = jnp.logical_and(kpos <= qp, valid)
                    else:
                        mask = jnp.logical_and((kpos >> 6) <= (qp >> 6), valid)
                    s = jnp.where(mask, s, NEG)
                return s

            _online(logits, v_slot(v, e), e, m_ref, l_ref, acc_ref, p_ref, sum_in_v)

    scores(0, 0)

    def pair_body(jj, c):
        j = 2 * jj
        scores(j + 1, 1)
        softmax_pv(j, 0, False)
        scores(j + 2, 0)
        softmax_pv(j + 1, 1, False)
        return c

    lax.fori_loop(0, i // 2, pair_body, 0)

    @pl.when(i % 2 == 0)
    def _():
        softmax_pv(i, 0, True)

    @pl.when(i % 2 == 1)
    def _():
        scores(i, 1)
        softmax_pv(i - 1, 0, False)
        softmax_pv(i, 1, True)

    outs = []
    for p in range(ns // 2):
        e0, e1 = 2 * p, 2 * p + 1
        if kind == "diff":
            o = acc_ref[e0] / l_ref[e0] - par_ref[0] * (acc_ref[e1] / l_ref[e1])
            ms = jnp.mean(o * o, axis=-1, keepdims=True)
            outs.append(o * lax.rsqrt(ms + EPS) * g_ref[:, p * 128:(p + 1) * 128])
        else:
            r0 = acc_ref[e0] / pltpu.roll(acc_ref[e0], 64, axis=1)
            r1 = acc_ref[e1] / pltpu.roll(acc_ref[e1], 64, axis=1)
            outs.append(jnp.where(lane < 64, r0, pltpu.roll(r1, 64, axis=1)))
    o_ref[0] = (outs[0] if len(outs) == 1 else jnp.concatenate(outs, axis=1)).astype(o_ref.dtype)


def _prompt_attn(kind, q, k, v, kb, bsz, lp, front, par=None, gain=None):
    rows = q.shape[0]
    tq = ATT_BLOCK
    ns = ATT_STREAMS
    wq = (ns // 2) * (2 * HEAD_PAD if kind == "mla" else 128)
    wv = (ns // 2) * (128 if kind == "diff" else 256)
    wo = (ns // 2) * 128
    n_hg = v.shape[1] // wv
    nq = lp // tq
    kb_b, kb_h = kb.shape[0] > 1, kb.shape[1] > 1
    in_specs = [pl.BlockSpec((1, tq, wq), lambda b, h, i: (0, b * nq + i, h)),
                pl.BlockSpec((1, lp, wq), lambda b, h, i: (0, b, h)),
                pl.BlockSpec((1, lp, wv), lambda b, h, i: (0, b, h)),
                pl.BlockSpec((1, 1, ns, lp), lambda b, h, i: (b if kb_b else 0, h if kb_h else 0, 0, 0))]
    args = [q[None], k[None], v[None], kb]
    if kind == "diff":
        in_specs = [pl.BlockSpec(memory_space=pltpu.SMEM)] + in_specs + [pl.BlockSpec((1, wo), lambda b, h, i: (0, h))]
        args = [par] + args + [gain]
    body = functools.partial(_prompt_attn_body, kind=kind, front=front, tq=tq, ns=ns)
    return pl.pallas_call(
        body,
        grid=(bsz, n_hg, nq),
        in_specs=in_specs,
        out_specs=pl.BlockSpec((1, tq, wo), lambda b, h, i: (0, b * nq + i, h)),
        out_shape=jax.ShapeDtypeStruct((1, rows, n_hg * wo), BF16),
        scratch_shapes=[pltpu.VMEM((ns, tq, 128), F32), pltpu.VMEM((ns, tq, 128), F32),
                        pltpu.VMEM((ns, tq, 128), F32), pltpu.VMEM((2, ns, tq, tq), F32),
                        pltpu.VMEM((ns, tq, tq), BF16)],
        compiler_params=pltpu.CompilerParams(
            dimension_semantics=("parallel", "parallel", "arbitrary"), vmem_limit_bytes=VMEM_LIMIT),
    )(*args)[0]


def _cumsum_body(x_ref, tri_ref, o_ref, carry_ref):
    @pl.when(pl.program_id(0) == 0)
    def _():
        carry_ref[...] = jnp.zeros(carry_ref.shape, F32)

    y = jnp.dot(x_ref[...], tri_ref[...], preferred_element_type=F32,
                precision=lax.Precision.HIGHEST) + carry_ref[...]
    o_ref[...] = y
    carry_ref[...] = y[:, -1:]


def _cumsum_lanes(x, blk=256):
    rows, n = x.shape
    tri = (jnp.arange(blk)[:, None] <= jnp.arange(blk)[None, :]).astype(F32)
    return pl.pallas_call(
        _cumsum_body,
        grid=(n // blk,),
        in_specs=[pl.BlockSpec((rows, blk), lambda j: (0, j)), _const_spec((blk, blk))],
        out_specs=pl.BlockSpec((rows, blk), lambda j: (0, j)),
        out_shape=jax.ShapeDtypeStruct((rows, n), F32),
        scratch_shapes=[pltpu.VMEM((rows, 1), F32)],
        compiler_params=pltpu.CompilerParams(dimension_semantics=("arbitrary",)),
    )(x, tri)


def _expand_rows(x, rep):
    h, n = x.shape
    return jnp.broadcast_to(x[:, None, :], (h, rep, n)).reshape(h * rep, n)


def _dec_online(s, v16, m_ref, l_ref, acc_ref):
    m_prev = m_ref[...]
    m_new = jnp.maximum(m_prev, jnp.max(s, axis=-1, keepdims=True))
    alpha = jnp.exp2(m_prev - m_new)
    p = jnp.exp2(s - m_new)
    l_ref[...] = alpha * l_ref[...] + jnp.sum(p, axis=-1, keepdims=True)
    acc_ref[...] = alpha * acc_ref[...] + _dot(p.astype(BF16), v16)
    m_ref[...] = m_new


def _dec_init(m_ref, l_ref, acc_ref):
    m_ref[...] = jnp.full(m_ref.shape, NEG, F32)
    l_ref[...] = jnp.zeros(l_ref.shape, F32)
    acc_ref[...] = jnp.zeros(acc_ref.shape, F32)


def _diag_blocks(o, n_heads, ds, width):
    return jnp.concatenate([o[h * ds:(h + 1) * ds, h * width:(h + 1) * width] for h in range(n_heads)], axis=1)


def _diff_dec_body(par_ref, q_ref, kc_ref, vc_ref, kn_ref, vn_ref, g_ref, _, o_ref, m_ref, l_ref, acc_ref,
                   *, past, kb, ds, n_heads):
    jb = pl.program_id(1)
    hr = 2 * ds
    rows = n_heads * hr
    dv = 2 * DH_B
    r = lax.broadcasted_iota(jnp.int32, (rows, 1), 0)
    head = r // hr
    slope = LOG2E * jnp.exp2(-8.0 * (head + 1).astype(F32) / n_heads)
    qpos = past + (r % ds)
    q = q_ref[0]

    def key_block(k16, v16, key0):
        col = lax.broadcasted_iota(jnp.int32, (1, k16.shape[0]), 1)
        kpos = key0 + col // n_heads
        s = _dot_nt(q, k16) - slope * jnp.abs(qpos - kpos).astype(F32)
        s = jnp.where(col % n_heads == head, s, NEG)
        _dec_online(s, v16, m_ref, l_ref, acc_ref)

    @pl.when(jb == 0)
    def _():
        _dec_init(m_ref, l_ref, acc_ref)
        key_block(kn_ref[0], vn_ref[0], past)

    key_block(kc_ref[0, 0].astype(BF16), vc_ref[0, 0].astype(BF16), jb * kb)

    @pl.when(jb == pl.num_programs(1) - 1)
    def _():
        o = acc_ref[...] / l_ref[...]
        outs = []
        for h in range(n_heads):
            oh = o[h * hr:h * hr + ds] - par_ref[0] * o[h * hr + ds:(h + 1) * hr]
            ms = jnp.mean(oh * oh, axis=-1, keepdims=True)
            outs.append(oh * lax.rsqrt(ms + EPS) * g_ref[:, h * dv:(h + 1) * dv])
        o_ref[0] = jnp.concatenate(outs, axis=1).astype(o_ref.dtype)


def _fox_dec_body(q_ref, kc_ref, vc_ref, kn_ref, vn_ref, fc_ref, fn_ref, _, o_ref, m_ref, l_ref, acc_ref,
                  *, ds, n_heads):
    jb = pl.program_id(1)
    rows = n_heads * ds
    r = lax.broadcasted_iota(jnp.int32, (rows, 1), 0)
    head = r // ds
    q = q_ref[0]
    fnew = fn_ref[0]
    fref = jnp.zeros((rows, 1), F32)
    for h in range(n_heads):
        fref = jnp.where(head == h, fnew[:, h:h + 1], fref)

    def key_block(k16, v16, f_cols, causal):
        col = lax.broadcasted_iota(jnp.int32, (1, k16.shape[0]), 1)
        s = _dot_nt(q, k16) + LOG2E * (fref - f_cols)
        ok = col % n_heads == head
        if causal:
            ok = jnp.logical_and(ok, col // n_heads <= r % ds)
        _dec_online(jnp.where(ok, s, NEG), v16, m_ref, l_ref, acc_ref)

    @pl.when(jb == 0)
    def _():
        _dec_init(m_ref, l_ref, acc_ref)
        key_block(kn_ref[0], vn_ref[0].astype(BF16), fnew[:, :rows], True)

    key_block(kc_ref[0, 0].astype(BF16), vc_ref[0, 0].astype(BF16), fc_ref[0], False)

    @pl.when(jb == pl.num_programs(1) - 1)
    def _():
        o = acc_ref[...] / l_ref[...]
        o_ref[0] = jnp.concatenate([o[h * ds:(h + 1) * ds] for h in range(n_heads)], axis=1).astype(o_ref.dtype)


def _mla_dec_body(qn_ref, qp_ref, cc_ref, pc_ref, cn_ref, pn_ref, wk_ref, wv_ref, ones_ref, _,
                  o_ref, m_ref, l_ref, acc_ref, *, ds, n_heads):
    jb = pl.program_id(1)
    qn = qn_ref[0]
    qp = qp_ref[0]
    ones_h = ones_ref[...]

    def key_block(ckv, kpe):
        c16 = ckv.astype(BF16)
        kn = _dot(c16, wk_ref[...])
        v = _dot(c16, wv_ref[...])
        n = kpe.shape[0]
        ss = _dot_nt(ones_h, (kn * kn).astype(BF16)) + _dot_nt(jnp.ones((n_heads, ROPE_D), BF16),
                                                               (kpe * kpe).astype(BF16))
        rinv = lax.rsqrt(ss * (1.0 / (NOPE_D + ROPE_D)) + EPS)
        s = _dot_nt(qn, kn.astype(BF16)) + _dot_nt(qp, kpe.astype(BF16))
        s = s * _expand_rows(rinv, ds)
        _dec_online(s, v.astype(BF16), m_ref, l_ref, acc_ref)

    @pl.when(jb == 0)
    def _():
        _dec_init(m_ref, l_ref, acc_ref)
        key_block(cn_ref[0], pn_ref[0])

    key_block(cc_ref[0, 0], pc_ref[0, 0])

    @pl.when(jb == pl.num_programs(1) - 1)
    def _():
        o = acc_ref[...] / l_ref[...]
        o_ref[0] = _diag_blocks(o, n_heads, ds, V_D).astype(o_ref.dtype)


def _per_seq(a):
    return (a, pl.BlockSpec((1,) + a.shape[1:], lambda b, j: (b, 0, 0)))


def _dec_const(a):
    return (a, _const_spec(a.shape))


def _dec_call(body, ins, prev, row0, ds, rows, acc_w, nb, n_kb, smem=None):
    in_specs = [spec for _, spec in ins] + [pl.BlockSpec(memory_space=pl.ANY)]
    args = [a for a, _ in ins] + [prev[None]]
    if smem is not None:
        in_specs = [pl.BlockSpec(memory_space=pltpu.SMEM)] + in_specs
        args = [smem] + args
    blk0 = row0 // ds
    return pl.pallas_call(
        body,
        grid=(nb, n_kb),
        in_specs=in_specs,
        out_specs=pl.BlockSpec((1, ds, prev.shape[1]), lambda b, j: (0, blk0 + b, 0)),
        out_shape=jax.ShapeDtypeStruct((1,) + prev.shape, prev.dtype),
        input_output_aliases={len(args) - 1: 0},
        scratch_shapes=[pltpu.VMEM((rows, 1), F32), pltpu.VMEM((rows, 1), F32), pltpu.VMEM((rows, acc_w), F32)],
        compiler_params=pltpu.CompilerParams(
            dimension_semantics=("parallel", "arbitrary"), vmem_limit_bytes=VMEM_LIMIT),
    )(*args)[0]


def kernel(x_prompt, x_sample, state_s5_re, state_s5_im, cache_diff_k, cache_diff_v, cache_fox_k, cache_fox_v, cache_fox_logf, cache_mla_ckv, cache_mla_kpe, meta_tokens, ffn_norm, ffn_w_in, ffn_w_out, mix_norm, ab_w_in, ab_w_out, s5_a_re, s5_a_im, s5_log_step, s5_b_re, s5_b_im, s5_c_re, s5_c_im, s5_d, s5_glu_w, s5_glu_b, diff_q_norm, diff_k_norm, diff_lam, diff_sub_norm, cd_w_in, cd_w_out, fox_q_norm, fox_k_norm, fox_f_bias, mla_q_a_norm, mla_q_b, mla_kv_a_norm, mla_kv_b, mla_q_norm, mla_k_norm):
    bsz, seq, dm = x_prompt.shape
    nb, ds, _ = x_sample.shape
    n_meta = meta_tokens.shape[0]
    past = cache_diff_k.shape[2]
    front = ROW_ALIGN - n_meta
    lp = front + n_meta + seq
    ltot = n_meta + seq
    assert n_meta + front == ROW_ALIGN and lp % ATT_BLOCK == 0 and front % CHUNK == CHUNK - n_meta
    assert ds == S5_STEP and past % CHUNK == 0 and ds <= CHUNK
    kb = min(DEC_KB, past)
    assert past % kb == 0
    n_kb = past // kb
    assert ffn_norm.shape[0] == 2 and ab_w_in.shape[0] == 1 and cd_w_in.shape[0] == 1

    h_b = cache_diff_k.shape[3]
    h_c = cache_fox_k.shape[3]
    h_d = mla_q_b.shape[2] // (NOPE_D + ROPE_D)
    s5w = s5_glu_w.shape[1]
    n_grp = s5w // S5_GROUP
    qkw = h_b * 2 * DH_B
    fox_w = h_c * DH_C
    q_lora = mla_q_a_norm.shape[1]
    kv_lora = mla_kv_a_norm.shape[1]
    d_qk = NOPE_D + ROPE_D

    n_p = bsz * lp
    head_rows_x = jnp.concatenate([jnp.zeros((front, dm), F32), meta_tokens.astype(F32)], axis=0)
    pieces = []
    for b in range(bsz):
        pieces += [head_rows_x, x_prompt[b]]
    x = jnp.concatenate(pieces + [x_sample.reshape(nb * ds, dm)], axis=0)
    rows = x.shape[0]
    tm = _row_tile(rows)

    ones64 = _block_diag_ones(64)
    ones128 = _block_diag_ones(128)

    x = _ffn(x, ffn_norm[0, 0], ffn_w_in[0, 0], ffn_w_out[0, 0], tm)

    gq = (jnp.tile(diff_q_norm[0], 2 * h_b) * (DH_B ** -0.5 * LOG2E)).reshape(1, qkw)
    gk = jnp.tile(diff_k_norm[0], 2 * h_b).reshape(1, qkw)
    s5_cols = S5_STEP * S5_GROUP
    q16, k32, k16, v32, v16, u_t = _row_call(
        functools.partial(_ab_in_body, widths=(s5w, qkw)),
        [x], [mix_norm[0].reshape(1, dm), ab_w_in[0].astype(BF16), gq, gk, ones64],
        [qkw, qkw, qkw, h_b * 2 * DH_B, h_b * 2 * DH_B], [BF16, F32, BF16, F32, BF16], tm,
        group_outs=[jax.ShapeDtypeStruct((n_grp, rows // S5_STEP, s5_cols), BF16)],
        scratch=[pltpu.VMEM((s5w // 128, tm, 128), F32)])

    mats = _s5_matrices(s5_a_re[0], s5_a_im[0], s5_log_step[0], s5_b_re[0], s5_b_im[0],
                        s5_c_re[0], s5_c_im[0], s5_d[0])
    n_ch = lp // S5_STEP
    y_t, st_p = _s5_scan(u_t, jnp.zeros((n_grp, bsz, 4 * P_A), F32), mats, n_ch, bsz, 0)
    h_re = jnp.transpose(state_s5_re[0].astype(F32), (1, 0, 2))
    h_im = jnp.transpose(state_s5_im[0].astype(F32), (1, 0, 2))
    y_t, st_s = _s5_scan(u_t, jnp.concatenate([h_re, h_im, h_im, h_re], axis=-1), mats, 1, nb,
                         n_p // S5_STEP, y_prev=y_t)

    lv = diff_lam[0].astype(F32)
    lam_init = 0.8 - 0.6 * math.exp(-0.3 * 0)
    lam = jnp.exp(jnp.sum(lv[0] * lv[1])) - jnp.exp(jnp.sum(lv[2] * lv[3])) + lam_init
    slopes = jnp.exp2(-8.0 * jnp.arange(1, h_b + 1, dtype=F32) / h_b)
    par = jnp.concatenate([lam[None], slopes]).astype(F32)
    subg = (jnp.tile(diff_sub_norm[0], h_b) * (1.0 - lam_init)).reshape(1, qkw)
    kpad = jnp.arange(lp) < front
    kb_diff = jnp.where(kpad[None, :], NEG, LOG2E * slopes[:, None] * jnp.arange(lp, dtype=F32)[None, :])
    kb_diff = jnp.broadcast_to(kb_diff[None, :, None, :], (1, h_b, 2, lp)).reshape(
        1, 2 * h_b // ATT_STREAMS, ATT_STREAMS, lp)
    o_all = _prompt_attn("diff", q16, k16, v16, kb_diff, bsz, lp, front, par=par, gain=subg)
    qs = q16[n_p:].reshape(nb, ds, h_b, 2, DH_B)
    eye_2 = jnp.eye(2, dtype=BF16)
    qbd = jnp.einsum('bqhmd,mM->bhmqMd', qs, eye_2).reshape(nb, h_b * 2 * ds, 2 * DH_B)
    cache_spec = lambda w: pl.BlockSpec((1, 1, kb, w), lambda b, j: (0, b, j, 0))
    head_rows = lambda a: a.reshape(a.shape[0], nb, past * h_b, 2 * DH_B)
    head_cache_spec = pl.BlockSpec((1, 1, kb * h_b, 2 * DH_B), lambda b, j: (0, b, j, 0))
    o_all = _dec_call(
        functools.partial(_diff_dec_body, past=past, kb=kb, ds=ds, n_heads=h_b),
        [_per_seq(qbd), (head_rows(cache_diff_k), head_cache_spec), (head_rows(cache_diff_v), head_cache_spec),
         _per_seq(k16[n_p:].reshape(nb, ds * h_b, 2 * DH_B)), _per_seq(v16[n_p:].reshape(nb, ds * h_b, 2 * DH_B)),
         _dec_const(subg)],
        o_all, n_p, ds, 2 * h_b * ds, 2 * DH_B, nb, n_kb, smem=par)

    x = _row_call(
        functools.partial(_ab_out_body, s5w=s5w),
        [x, o_all],
        [s5_glu_w[0].astype(BF16), s5_glu_b[0].reshape(1, s5w), ab_w_out[0].astype(BF16)],
        [dm], [F32], tm, group_ins=[y_t], scratch=[pltpu.VMEM((s5w // 128, tm, 128), F32)])[0]

    x = _ffn(x, ffn_norm[0, 1], ffn_w_in[0, 1], ffn_w_out[0, 1], tm)

    x = _ffn(x, ffn_norm[1, 0], ffn_w_in[1, 0], ffn_w_out[1, 0], tm)

    half = ROPE_D // 2
    inv = ROPE_THETA ** (-jnp.arange(half, dtype=F32) / half)
    pos = jnp.concatenate([jnp.tile(jnp.arange(lp, dtype=jnp.int32) - front, bsz),
                           jnp.tile(past + jnp.arange(ds, dtype=jnp.int32), nb)]).astype(F32)
    ang = pos[:, None] * inv[None, :]
    pad_r = HEAD_PAD - NOPE_D - ROPE_D
    cos_t = jnp.concatenate([jnp.ones((rows, NOPE_D), F32), jnp.cos(ang), jnp.cos(ang),
                             jnp.zeros((rows, pad_r), F32)], axis=1)
    sin_t = jnp.concatenate([jnp.zeros((rows, NOPE_D), F32), jnp.sin(ang), jnp.sin(ang),
                             jnp.zeros((rows, pad_r), F32)], axis=1)

    wcd = cd_w_in[0]
    c_fg = 3 * fox_w
    c_qa = c_fg + h_c
    c_kva = c_qa + q_lora
    c_pe = c_kva + kv_lora
    w_pe = wcd[:, c_pe:c_pe + ROPE_D]
    zc = lambda n: jnp.zeros((dm, n), F32)
    w_cd = jnp.concatenate([
        wcd[:, :3 * fox_w], wcd[:, c_qa:c_qa + q_lora], wcd[:, c_kva:c_kva + kv_lora],
        zc(NOPE_D), w_pe, zc(pad_r),
        zc(NOPE_D), -w_pe[:, half:], w_pe[:, :half], zc(pad_r),
        wcd[:, c_fg:c_fg + h_c], zc(HEAD_PAD - h_c)], axis=1).astype(BF16)
    qb = mla_q_b[0].reshape(q_lora, h_d, d_qk)
    zq = lambda n: jnp.zeros((q_lora, h_d, n), F32)
    qb_pad = jnp.concatenate([qb, zq(pad_r)], axis=-1).reshape(q_lora, h_d * HEAD_PAD)
    qb_rot = jnp.concatenate([zq(NOPE_D), -qb[..., NOPE_D + half:], qb[..., NOPE_D:NOPE_D + half], zq(pad_r)],
                             axis=-1).reshape(q_lora, h_d * HEAD_PAD)
    wq2 = jnp.concatenate([qb_pad, qb_rot], axis=1).astype(BF16)
    kvb = mla_kv_b[0].reshape(kv_lora, h_d, NOPE_D + V_D)
    wk_pad = jnp.concatenate([kvb[..., :NOPE_D], jnp.zeros((kv_lora, h_d, HEAD_PAD - NOPE_D), F32)],
                             axis=-1).reshape(kv_lora, h_d * HEAD_PAD).astype(BF16)
    wk_cmp = kvb[..., :NOPE_D].reshape(kv_lora, h_d * NOPE_D).astype(BF16)
    wv_cmp = kvb[..., NOPE_D:].reshape(kv_lora, h_d * V_D).astype(BF16)
    gfq = (jnp.tile(fox_q_norm[0], h_c) * (DH_C ** -0.5 * LOG2E)).reshape(1, fox_w)
    gfk = jnp.tile(fox_k_norm[0], h_c).reshape(1, fox_w)
    fbias = jnp.concatenate([fox_f_bias[0], jnp.zeros((HEAD_PAD - h_c,), F32)]).reshape(1, HEAD_PAD)
    gmq = jnp.tile(jnp.concatenate([mla_q_norm[0] * mla_k_norm[0] * (d_qk ** -0.5 * LOG2E), jnp.zeros((pad_r,), F32)]),
                   h_d).reshape(1, h_d * HEAD_PAD)

    (fq16, fk32, fk16, fv32, fv16, logf, qm16, ckv32, kpe32, km16, vm16) = _row_call(
        functools.partial(_cd_in_body, fox_w=fox_w, q_lora=q_lora, kv_lora=kv_lora, n_heads=h_d),
        [x, cos_t, sin_t],
        [mix_norm[1].reshape(1, dm), w_cd, gfq, gfk, fbias, mla_q_a_norm[0].reshape(1, q_lora), wq2,
         mla_kv_a_norm[0].reshape(1, kv_lora), wk_pad, wv_cmp, gmq, ones64, ones128],
        [fox_w, fox_w, fox_w, fox_w, 2 * fox_w, h_c, h_d * HEAD_PAD, kv_lora, ROPE_D, h_d * HEAD_PAD, 2 * h_d * V_D],
        [BF16, F32, BF16, F32, BF16, F32, BF16, F32, F32, BF16, BF16], tm)

    logf_p = jnp.transpose(logf[:n_p].reshape(bsz, lp, h_c), (0, 2, 1)).reshape(bsz * h_c, lp)
    f_p = _cumsum_lanes(logf_p).reshape(bsz, h_c, lp)
    logf_s = jnp.concatenate([
        jnp.transpose(cache_fox_logf[0].astype(F32), (0, 2, 1)),
        jnp.transpose(logf[n_p:].reshape(nb, ds, h_c), (0, 2, 1)),
        jnp.zeros((nb, h_c, 256 - ds), F32)], axis=2).reshape(nb * h_c, past + 256)
    f_s = _cumsum_lanes(logf_s).reshape(nb, h_c, past + 256)

    kb_fox = jnp.where(kpad[None, None, :], NEG, -LOG2E * f_p).reshape(bsz, h_c // ATT_STREAMS, ATT_STREAMS, lp)
    oc_all = _prompt_attn("fox", fq16, fk16, fv16, kb_fox, bsz, lp, front)
    kb_mla = jnp.broadcast_to(jnp.where(kpad, NEG, 0.0).astype(F32)[None, None, None, :], (1, 1, ATT_STREAMS, lp))
    od_all = _prompt_attn("mla", qm16, km16, vm16, kb_mla, bsz, lp, front)

    fqs = jnp.transpose(fq16[n_p:].reshape(nb, ds, h_c, DH_C), (0, 2, 1, 3)).reshape(nb, h_c * ds, DH_C)
    fox_rows = lambda a: a.reshape(a.shape[0], nb, past * h_c, DH_C)
    fox_cache_spec = pl.BlockSpec((1, 1, kb * h_c, DH_C), lambda b, j: (0, b, j, 0))
    f_cols = jnp.transpose(f_s, (0, 2, 1)).reshape(nb, 1, (past + 256) * h_c)
    oc_all = _dec_call(
        functools.partial(_fox_dec_body, ds=ds, n_heads=h_c),
        [_per_seq(fqs), (fox_rows(cache_fox_k), fox_cache_spec), (fox_rows(cache_fox_v), fox_cache_spec),
         _per_seq(fk16[n_p:].reshape(nb, ds * h_c, DH_C)), _per_seq(fv32[n_p:].reshape(nb, ds * h_c, DH_C)),
         (f_cols, pl.BlockSpec((1, 1, kb * h_c), lambda b, j: (b, 0, j))),
         (f_cols, pl.BlockSpec((1, 1, 128), lambda b, j: (b, 0, past * h_c // 128)))],
        oc_all, n_p, ds, h_c * ds, DH_C, nb, n_kb)

    eye_d = jnp.eye(h_d, dtype=BF16)
    qms = qm16[n_p:].reshape(nb, ds, h_d, HEAD_PAD)
    qn_bd = jnp.einsum('bqhd,hH->bhqHd', qms[..., :NOPE_D], eye_d).reshape(nb, h_d * ds, h_d * NOPE_D)
    qp_s = jnp.transpose(qms[..., NOPE_D:NOPE_D + ROPE_D], (0, 2, 1, 3)).reshape(nb, h_d * ds, ROPE_D)
    ones_h = jnp.repeat(jnp.eye(h_d, dtype=BF16), NOPE_D, axis=1)
    od_all = _dec_call(
        functools.partial(_mla_dec_body, ds=ds, n_heads=h_d),
        [_per_seq(qn_bd), _per_seq(qp_s),
         (cache_mla_ckv, pl.BlockSpec((1, 1, kb, kv_lora), lambda b, j: (0, b, j, 0))),
         (cache_mla_kpe, pl.BlockSpec((1, 1, kb, ROPE_D), lambda b, j: (0, b, j, 0))),
         _per_seq(ckv32[n_p:].reshape(nb, ds, kv_lora)), _per_seq(kpe32[n_p:].reshape(nb, ds, ROPE_D)),
         _dec_const(wk_cmp), _dec_const(wv_cmp), _dec_const(ones_h)],
        od_all, n_p, ds, h_d * ds, h_d * V_D, nb, n_kb)

    x = _row_call(functools.partial(_cd_out_body, fox_w=fox_w), [x, oc_all, od_all],
                  [cd_w_out[0].astype(BF16)], [dm], [F32], tm)[0]

    x = _ffn(x, ffn_norm[1, 1], ffn_w_in[1, 1], ffn_w_out[1, 1], tm)

    def p_rows(a, shape):
        w = a.shape[1]
        return a[:n_p].reshape(bsz, lp, w)[:, front:front + ltot].reshape((1, bsz, ltot) + shape)

    def s_rows(a, shape):
        return a[n_p:].reshape((1, nb, ds) + shape)

    def s5_state(st):
        st = jnp.transpose(st, (1, 0, 2))
        return st[None, :, :, :P_A], st[None, :, :, P_A:]

    y_prompt = x[:n_p].reshape(bsz, lp, dm)[:, front + n_meta:]
    y_sample = x[n_p:].reshape(nb, ds, dm)
    s5_re_p, s5_im_p = s5_state(st_p)
    s5_re_s, s5_im_s = s5_state(st_s)
    return (y_prompt, y_sample,
            s5_re_p, s5_im_p, p_rows(k32, (h_b, 2 * DH_B)), p_rows(v32, (h_b, 2 * DH_B)),
            p_rows(fk32, (h_c, DH_C)), p_rows(fv32, (h_c, DH_C)), p_rows(logf, (h_c,)),
            p_rows(ckv32, (kv_lora,)), p_rows(kpe32, (ROPE_D,)),
            s5_re_s, s5_im_s, s_rows(k32, (h_b, 2 * DH_B)), s_rows(v32, (h_b, 2 * DH_B)),
            s_rows(fk32, (h_c, DH_C)), s_rows(fv32, (h_c, DH_C)), s_rows(logf, (h_c,)),
            s_rows(ckv32, (kv_lora,)), s_rows(kpe32, (ROPE_D,)))
```

```python
import functools
import math

import jax
import jax.numpy as jnp
from jax import lax
from jax.experimental import pallas as pl
from jax.experimental.pallas import tpu as pltpu

F32 = jnp.float32
BF16 = jnp.bfloat16

EPS = 1e-6
CHUNK = 64
ROW_ALIGN = 256
S5_GROUP = 16
S5_STEP = 16
P_A = 64
DH_B = 64
DH_C = 64
NOPE_D = 64
ROPE_D = 32
V_D = 64
HEAD_PAD = 128
ROPE_THETA = 10000.0
NEG = -1e30
LOG2E = math.log2(math.e)
VMEM_LIMIT = 56 * 1024 * 1024
ATT_BLOCK = 256
ATT_STREAMS = 4
ATT_ROWS = 64
DEC_KB = 1024


def _dot(a, b):
    return jnp.dot(a, b, preferred_element_type=F32)


def _dot_nt(a, b):
    return lax.dot_general(a, b, (((1,), (1,)), ((), ())), preferred_element_type=F32)


def _rms_rows(x, g):
    ms = jnp.mean(x * x, axis=-1, keepdims=True)
    return x * lax.rsqrt(ms + EPS) * g


def _group_sumsq(x, ones_bd):
    w = x.shape[-1]
    parts = [_dot((x[:, c:c + 256] * x[:, c:c + 256]).astype(BF16), ones_bd) for c in range(0, w, 256)]
    return parts[0] if len(parts) == 1 else jnp.concatenate(parts, axis=1)


def _block_diag_ones(group, n=256):
    r = jnp.arange(n) // group
    return (r[:, None] == r[None, :]).astype(BF16)


def _const_spec(shape):
    nd = len(shape)
    return pl.BlockSpec(shape, lambda *_: (0,) * nd, pipeline_mode=pl.Buffered(1))


def _row_tile(rows, cap=512):
    t = cap
    while rows % t:
        t //= 2
    return t


def _group_spec(a, tm):
    return pl.BlockSpec((a.shape[0], tm // S5_STEP, a.shape[2]), lambda i: (0, i, 0))


def _row_call(body, row_ins, consts, out_widths, out_dtypes, tm, group_ins=(), group_outs=(), scratch=(),
              col_outs=()):
    rows = row_ins[0].shape[0]
    in_specs = [pl.BlockSpec((tm, a.shape[1]), lambda i: (i, 0)) for a in row_ins]
    in_specs += [_group_spec(a, tm) for a in group_ins]
    in_specs += [_const_spec(c.shape) for c in consts]
    out_specs = [pl.BlockSpec((tm, w), lambda i: (i, 0)) for w in out_widths]
    out_specs += [_group_spec(a, tm) for a in group_outs]
    out_specs += [pl.BlockSpec((w, tm), lambda i: (0, i)) for w, _ in col_outs]
    out_shape = [jax.ShapeDtypeStruct((rows, w), d) for w, d in zip(out_widths, out_dtypes)]
    out_shape += list(group_outs)
    out_shape += [jax.ShapeDtypeStruct((w, rows), d) for w, d in col_outs]
    return pl.pallas_call(
        body,
        grid=(rows // tm,),
        in_specs=in_specs,
        out_specs=out_specs,
        out_shape=out_shape,
        scratch_shapes=list(scratch),
        compiler_params=pltpu.CompilerParams(
            dimension_semantics=("parallel",), vmem_limit_bytes=VMEM_LIMIT),
    )(*row_ins, *group_ins, *consts)


def _ffn_body(x_ref, g_ref, win_ref, wout_ref, o_ref, *, d_ff, tf):
    x = x_ref[...]
    xn = _rms_rows(x, g_ref[...]).astype(BF16)
    acc = jnp.zeros(x.shape, F32)
    for c in range(0, d_ff, tf):
        gate = _dot(xn, win_ref[:, c:c + tf])
        up = _dot(xn, win_ref[:, d_ff + c:d_ff + c + tf])
        a = (gate * jax.nn.sigmoid(gate) * up).astype(BF16)
        acc = acc + _dot(a, wout_ref[c:c + tf, :])
    o_ref[...] = x + 0.5 * acc


def _ffn(x, g, w_in, w_out, tm):
    d_ff = w_out.shape[0]
    body = functools.partial(_ffn_body, d_ff=d_ff, tf=256)
    return _row_call(body, [x], [g.reshape(1, -1), w_in.astype(BF16), w_out.astype(BF16)],
                     [x.shape[1]], [F32], tm)[0]


def _ab_in_body(x_ref, g_ref, w_ref, gq_ref, gk_ref, ones_ref,
                q_ref, k32_ref, k16_ref, v32_ref, ut_ref, vt_ref, us_ref, *, widths):
    s5w, qkw = widths
    xn = _rms_rows(x_ref[...], g_ref[...]).astype(BF16)
    h = _dot(xn, w_ref[...])
    n_chunk = us_ref.shape[1] // S5_STEP
    per_col = 128 // S5_GROUP
    for v in range(s5w // 128):
        us_ref[v] = h[:, v * 128:(v + 1) * 128]
        steps = [us_ref[v, pl.ds(t, n_chunk, stride=S5_STEP), :] for t in range(S5_STEP)]
        for gl in range(per_col):
            ut_ref[v * per_col + gl] = jnp.concatenate(
                [x[:, gl * S5_GROUP:(gl + 1) * S5_GROUP] for x in steps], axis=1).astype(BF16)
    q = h[:, s5w:s5w + qkw]
    k = h[:, s5w + qkw:s5w + 2 * qkw]
    v = h[:, s5w + 2 * qkw:]
    ones_bd = ones_ref[...]
    qn = q * lax.rsqrt(_group_sumsq(q, ones_bd) * (1.0 / DH_B) + EPS) * gq_ref[...]
    kn = k * lax.rsqrt(_group_sumsq(k, ones_bd) * (1.0 / DH_B) + EPS) * gk_ref[...]
    q_ref[...] = qn.astype(BF16)
    k32_ref[...] = kn
    k16_ref[...] = kn.astype(BF16)
    v32_ref[...] = v
    vt_ref[...] = v.T.astype(BF16)


def _ab_out_body(x_ref, o_ref, yt_ref, gluw_ref, glub_ref, wout_ref, out_ref, ys_ref, *, s5w):
    n_chunk = ys_ref.shape[1] // S5_STEP
    per_col = 128 // S5_GROUP
    for v in range(s5w // 128):
        for t in range(S5_STEP):
            ys_ref[v, pl.ds(t, n_chunk, stride=S5_STEP), :] = jnp.concatenate(
                [yt_ref[v * per_col + gl][:, t * S5_GROUP:(t + 1) * S5_GROUP] for gl in range(per_col)], axis=1)
    y = jnp.concatenate([ys_ref[v] for v in range(s5w // 128)], axis=1)
    g = 0.5 * y * (1.0 + jnp.tanh(math.sqrt(2.0 / math.pi) * (y + 0.044715 * (y * y * y))))
    z = _dot(g.astype(BF16), gluw_ref[...]) + glub_ref[...]
    s5o = g * jax.nn.sigmoid(z)
    m = _dot(s5o.astype(BF16), wout_ref[:s5w, :]) + _dot(o_ref[...], wout_ref[s5w:, :])
    out_ref[...] = x_ref[...] + m


def _heads_with_ones_t(vt):
    ones = jnp.ones((64, vt.shape[1]), vt.dtype)
    outs = []
    for h in range(vt.shape[0] // 64):
        outs += [vt[h * 64:(h + 1) * 64], ones]
    return jnp.concatenate(outs, axis=0)


def _cd_in_body(x_ref, cos_ref, sin_ref, g_ref, w_ref, gfq_ref, gfk_ref, fb_ref, gqa_ref, wq2_ref,
                gkva_ref, wk_ref, wv_ref, gmq_ref, ones64_ref, ones128_ref,
                fq_ref, fk32_ref, fk16_ref, fv32_ref, logf_ref, qm_ref, ckv_ref, kpe_ref,
                km_ref, fvt_ref, vmt_ref, *, fox_w, q_lora, kv_lora, n_heads):
    xn = _rms_rows(x_ref[...], g_ref[...]).astype(BF16)
    h = _dot(xn, w_ref[...])
    ones64 = ones64_ref[...]
    ones128 = ones128_ref[...]
    fq = h[:, :fox_w]
    fk = h[:, fox_w:2 * fox_w]
    fv = h[:, 2 * fox_w:3 * fox_w]
    c0 = 3 * fox_w
    qa = h[:, c0:c0 + q_lora]
    kva = h[:, c0 + q_lora:c0 + q_lora + kv_lora]
    c1 = c0 + q_lora + kv_lora
    pe_a = h[:, c1:c1 + HEAD_PAD]
    pe_b = h[:, c1 + HEAD_PAD:c1 + 2 * HEAD_PAD]
    fg = h[:, c1 + 2 * HEAD_PAD:c1 + 3 * HEAD_PAD]

    fqn = fq * lax.rsqrt(_group_sumsq(fq, ones64) * (1.0 / DH_C) + EPS) * gfq_ref[...]
    fkn = fk * lax.rsqrt(_group_sumsq(fk, ones64) * (1.0 / DH_C) + EPS) * gfk_ref[...]
    fq_ref[...] = fqn.astype(BF16)
    fk32_ref[...] = fkn
    fk16_ref[...] = fkn.astype(BF16)
    fv32_ref[...] = fv
    fvt_ref[...] = _heads_with_ones_t(fv.T).astype(BF16)

    z = fg + fb_ref[...]
    logf = jnp.minimum(z, 0.0) - jnp.log1p(jnp.exp(-jnp.abs(z)))
    logf_ref[...] = logf[:, :logf_ref.shape[1]]

    cos = cos_ref[...]
    sin = sin_ref[...]
    qan = _rms_rows(qa, gqa_ref[...]).astype(BF16)
    q2 = _dot(qan, wq2_ref[...])
    hw = n_heads * HEAD_PAD
    cos_t = jnp.concatenate([cos] * n_heads, axis=1)
    sin_t = jnp.concatenate([sin] * n_heads, axis=1)
    qr = q2[:, :hw] * cos_t + q2[:, hw:] * sin_t
    d_qk = NOPE_D + ROPE_D
    qm = qr * lax.rsqrt(_group_sumsq(qr, ones128) * (1.0 / d_qk) + EPS) * gmq_ref[...]
    qm_ref[...] = qm.astype(BF16)

    ckv = _rms_rows(kva, gkva_ref[...])
    ckv_ref[...] = ckv
    pe = pe_a * cos + pe_b * sin
    kpe_ref[...] = pe[:, NOPE_D:NOPE_D + ROPE_D]
    ckv16 = ckv.astype(BF16)
    kraw = _dot(ckv16, wk_ref[...]) + jnp.concatenate([pe] * n_heads, axis=1)
    km = kraw * lax.rsqrt(_group_sumsq(kraw, ones128) * (1.0 / d_qk) + EPS)
    km_ref[...] = km.astype(BF16)
    vmt_ref[...] = _heads_with_ones_t(_dot_nt(wv_ref[...], ckv16)).astype(BF16)


def _cd_out_body(x_ref, oc_ref, od_ref, wout_ref, out_ref, *, fox_w):
    m = _dot(oc_ref[...], wout_ref[:fox_w, :]) + _dot(od_ref[...], wout_ref[fox_w:, :])
    out_ref[...] = x_ref[...] + m


def _s5_body(*refs, n_chunks, bsz, aliased):
    if aliased:
        u_ref, h0_ref, m_ref, bm_ref, cm_ref, coef_ref, _, y_ref, st_ref, s2_ref, hp_ref = refs
    else:
        u_ref, h0_ref, m_ref, bm_ref, cm_ref, coef_ref, y_ref, st_ref, s2_ref, hp_ref = refs
    u = u_ref[0]
    half = 2 * P_A
    s2 = _dot(u, bm_ref[0])
    s2_ref[0] = s2[:, :half]
    s2_ref[1] = s2[:, half:]
    c1 = coef_ref[0, 0:1, :]
    c2 = coef_ref[0, 1:2, :]
    c3 = coef_ref[0, 2:3, :]

    def step(j, carry):
        ha, hb = carry
        hp_ref[pl.ds(j, bsz, stride=n_chunks), :] = ha
        sa = s2_ref[0, pl.ds(j, bsz, stride=n_chunks), :]
        sb = s2_ref[1, pl.ds(j, bsz, stride=n_chunks), :]
        return ha * c1 + hb * c2 + sa, hb * c1 + ha * c3 + sb

    h0 = h0_ref[0]
    ha, _ = lax.fori_loop(0, n_chunks, step, (h0[:, :half], h0[:, half:]))
    st_ref[0] = ha
    y_ref[0] = _dot(u, m_ref[0]) + _dot(hp_ref[...].astype(BF16), cm_ref[0])


def _s5_scan(u_t, h0, mats, n_chunks, bsz, row0, y_prev=None):
    m_mat, bm, cm, coef = mats
    g, rows_all, w = u_t.shape
    rows = n_chunks * bsz
    assert row0 % rows == 0
    blk = row0 // rows
    aliased = y_prev is not None
    body = functools.partial(_s5_body, n_chunks=n_chunks, bsz=bsz, aliased=aliased)
    per_g = lambda a: pl.BlockSpec((1,) + a.shape[1:], lambda i: (i, 0, 0))
    in_specs = [pl.BlockSpec((1, rows, w), lambda i: (i, blk, 0)),
                per_g(h0), per_g(m_mat), per_g(bm), per_g(cm), per_g(coef)]
    args = [u_t, h0, m_mat, bm, cm, coef]
    if aliased:
        in_specs.append(pl.BlockSpec(memory_space=pl.ANY))
        args.append(y_prev)
    return pl.pallas_call(
        body,
        grid=(g,),
        in_specs=in_specs,
        out_specs=[pl.BlockSpec((1, rows, w), lambda i: (i, blk, 0)),
                   pl.BlockSpec((1, bsz, 2 * P_A), lambda i: (i, 0, 0))],
        out_shape=[jax.ShapeDtypeStruct((g, rows_all, w), F32),
                   jax.ShapeDtypeStruct((g, bsz, 2 * P_A), F32)],
        input_output_aliases={6: 0} if aliased else {},
        scratch_shapes=[pltpu.VMEM((2, rows, 2 * P_A), F32), pltpu.VMEM((rows, 2 * P_A), F32)],
        compiler_params=pltpu.CompilerParams(
            dimension_semantics=("parallel",), vmem_limit_bytes=VMEM_LIMIT),
    )(*args)


def _s5_matrices(a_re, a_im, log_step, b_re, b_im, c_re, c_im, d):
    g = a_re.shape[0]
    t = S5_STEP
    lam = lax.complex(a_re, a_im)
    dl = lam * jnp.exp(log_step)[:, None]
    lam_bar = jnp.exp(dl)
    b_bar = ((lam_bar - 1.0) / lam)[..., None] * lax.complex(b_re, b_im)
    c = lax.complex(c_re, c_im)
    pw = jnp.exp(dl[:, None, :] * jnp.arange(t + 1, dtype=F32)[None, :, None])
    bmc = pw[:, t - 1::-1][:, :, :, None] * b_bar[:, None]
    bmc = jnp.swapaxes(bmc, 2, 3).reshape(g, t * S5_GROUP, P_A)
    bm = jnp.concatenate([bmc.real, bmc.imag, bmc.imag, bmc.real], axis=-1)
    kk = jnp.einsum('gcp,gkp,gpd->gkcd', c, pw[:, :t], b_bar).real
    kk = kk.at[:, 0].add(d.reshape(g, S5_GROUP)[:, :, None] * jnp.eye(S5_GROUP, dtype=F32))
    lag = jnp.arange(t)[None, :] - jnp.arange(t)[:, None]
    toep = jnp.where((lag >= 0)[None, :, :, None, None], kk[:, jnp.clip(lag, 0, t - 1)], 0.0)
    m_mat = jnp.transpose(toep, (0, 1, 4, 2, 3)).reshape(g, t * S5_GROUP, t * S5_GROUP)
    cp = c[:, None] * pw[:, 1:, None, :]
    cpm = jnp.transpose(cp, (0, 3, 1, 2)).reshape(g, P_A, t * S5_GROUP)
    cm = jnp.concatenate([cpm.real, -cpm.imag], axis=1)
    a_t = pw[:, t]
    ar, ai = a_t.real, a_t.imag
    zeros = jnp.zeros_like(ar)
    coef = jnp.stack([jnp.concatenate([ar, ar], -1), jnp.concatenate([-ai, ai], -1),
                      jnp.concatenate([ai, -ai], -1), jnp.concatenate([zeros, zeros], -1)], axis=1)
    return m_mat.astype(BF16), bm.astype(BF16), cm.astype(BF16), coef.astype(F32)


def _online(logits, vt, e, m_ref, l_ref, acc_ref, p_ref, sum_in_v):
    tk = p_ref.shape[1]
    part = logits(0, ATT_ROWS)
    for r0 in range(ATT_ROWS, tk, ATT_ROWS):
        part = jnp.maximum(part, logits(r0, ATT_ROWS))
    m_prev = m_ref[e]
    m_new = jnp.maximum(m_prev, jnp.max(part, axis=0, keepdims=True))
    alpha = jnp.exp2(m_prev - m_new)
    m_ref[e] = m_new
    psum = None
    for r0 in range(0, tk, ATT_ROWS):
        p = jnp.exp2(logits(r0, ATT_ROWS) - m_new)
        if not sum_in_v:
            psum = p if psum is None else psum + p
        p_ref[e, r0:r0 + ATT_ROWS] = p.astype(BF16)
    if not sum_in_v:
        l_ref[e] = alpha * l_ref[e] + jnp.sum(psum, axis=0, keepdims=True)
    acc_ref[e] = alpha * acc_ref[e] + _dot(vt, p_ref[e])


def _prompt_attn_body(*refs, kind, front, tq, ns):
    if kind == "diff":
        par_ref, q_ref, k_ref, vt_ref, kb_ref, g_ref, o_ref, m_ref, l_ref, acc_ref, s_ref, p_ref, kbc_ref = refs
    else:
        q_ref, k_ref, vt_ref, kb_ref, o_ref, m_ref, l_ref, acc_ref, s_ref, p_ref, kbc_ref = refs
    sum_in_v = kind != "diff"
    i = pl.program_id(2)
    if kind == "fox":
        hg = pl.program_id(1)
        new_bias = i == 0
    else:
        hg = pl.program_id(0)
        new_bias = jnp.logical_and(pl.program_id(1) == 0, i == 0)
    m_ref[...] = jnp.full(m_ref.shape, NEG, F32)
    l_ref[...] = jnp.zeros(l_ref.shape, F32)
    acc_ref[...] = jnp.zeros(acc_ref.shape, F32)

    @pl.when(new_bias)
    def _():
        def fill(c, carry):
            c0 = pl.multiple_of(c * 128, 128)
            for e in range(ns):
                row = kb_ref[0, 0, e:e + 1, pl.ds(c0, 128)]
                kbc_ref[e, pl.ds(c0, 128), :] = jnp.broadcast_to(row, (128, 128)).T
            return carry

        lax.fori_loop(0, kbc_ref.shape[1] // 128, fill, 0)

    q = q_ref[0]
    lane = lax.broadcasted_iota(jnp.int32, (1, 128), 1)
    qs = []
    for e in range(ns):
        if kind == "mla":
            qs.append(q[:, e * HEAD_PAD:(e + 1) * HEAD_PAD])
        else:
            qp = q[:, (e // 2) * 128:(e // 2 + 1) * 128]
            qs.append(jnp.where((lane < 64) if e % 2 == 0 else (lane >= 64), qp, jnp.zeros_like(qp)))
    qstart = pl.multiple_of(i * tq, tq)
    qpos = qstart + lax.broadcasted_iota(jnp.int32, (1, tq), 1)
    ref = [-kb_ref[0, 0, e:e + 1, pl.ds(qstart + (tq - 128), 128)][:, 127:128] for e in range(ns)]
    if kind == "diff":
        slope = [LOG2E * par_ref[1 + hg * (ns // 2) + p] for p in range(ns // 2)]

    tk = tq

    def k_slot(k, e):
        if kind == "mla":
            return k[:, e * HEAD_PAD:(e + 1) * HEAD_PAD]
        return k[:, (e // 2) * 128:(e // 2 + 1) * 128]

    def vt_slot(vt, e):
        r0 = (e if sum_in_v else e // 2) * 128
        return vt[r0:r0 + 128]

    def scores(j, slot):
        k0 = pl.multiple_of(j * tk, tk)
        k = k_ref[0, pl.ds(k0, tk), :]
        for e in range(ns):
            bias = kbc_ref[e, pl.ds(k0, tk), :] + ref[e]
            s_ref[slot, e] = _dot_nt(k_slot(k, e), qs[e]) + jnp.concatenate([bias] * (tq // 128), axis=1)

    def softmax_pv(j, slot, diag):
        k0 = pl.multiple_of(j * tk, tk)
        vt = vt_ref[:, pl.ds(k0, tk)]
        if diag:
            kpos = k0 + lax.broadcasted_iota(jnp.int32, (tk, 1), 0)
        for e in range(ns):
            def logits(r0, n, e=e):
                s = s_ref[slot, e, r0:r0 + n, :]
                if diag:
                    kp = kpos[r0:r0 + n]
                    if kind == "diff":
                        s = s - (2.0 * slope[e // 2]) * jnp.maximum(kp - qpos, 0).astype(F32)
                    if kind == "fox":
                        mask = jnp.logical_and(kp <= qpos, kp >= front)
                    else:
                        mask = jnp.logical_and((kp >> 6) <= (qpos >> 6), kp >= front)
                    s = jnp.where(mask, s, NEG)
                return s

            _online(logits, vt_slot(vt, e), e, m_ref, l_ref, acc_ref, p_ref, sum_in_v)

    scores(0, 0)

    def pair_body(jj, c):
        j = 2 * jj
        scores(j + 1, 1)
        softmax_pv(j, 0, False)
        scores(j + 2, 0)
        softmax_pv(j + 1, 1, False)
        return c

    lax.fori_loop(0, i // 2, pair_body, 0)

    @pl.when(i % 2 == 0)
    def _():
        softmax_pv(i, 0, True)

    @pl.when(i % 2 == 1)
    def _():
        scores(i, 1)
        softmax_pv(i - 1, 0, False)
        softmax_pv(i, 1, True)

    outs = []
    for p in range(ns // 2):
        e0, e1 = 2 * p, 2 * p + 1
        if kind == "diff":
            o = (acc_ref[e0] / l_ref[e0] - par_ref[0] * (acc_ref[e1] / l_ref[e1])).T
            ms = jnp.mean(o * o, axis=-1, keepdims=True)
            outs.append(o * lax.rsqrt(ms + EPS) * g_ref[:, p * 128:(p + 1) * 128])
        else:
            a0, a1 = acc_ref[e0], acc_ref[e1]
            outs.append(jnp.concatenate([a0[:64] / a0[64:], a1[:64] / a1[64:]], axis=0).T)
    o_ref[0] = (outs[0] if len(outs) == 1 else jnp.concatenate(outs, axis=1)).astype(o_ref.dtype)


def _prompt_attn(kind, q, k, vt, kb, bsz, lp, front, par=None, gain=None):
    rows = q.shape[0]
    tq = ATT_BLOCK
    ns = ATT_STREAMS
    wq = (ns // 2) * (2 * HEAD_PAD if kind == "mla" else 128)
    wv = (ns // 2) * (128 if kind == "diff" else 256)
    wo = (ns // 2) * 128
    n_hg = vt.shape[0] // wv
    nq = lp // tq
    kb_b, kb_h = kb.shape[0] > 1, kb.shape[1] > 1
    seq_major = kind == "fox"

    def bh(f):
        return (lambda b, h, i: f(b, h, i)) if seq_major else (lambda h, b, i: f(b, h, i))

    in_specs = [pl.BlockSpec((1, tq, wq), bh(lambda b, h, i: (0, b * nq + i, h))),
                pl.BlockSpec((1, lp, wq), bh(lambda b, h, i: (0, b, h))),
                pl.BlockSpec((wv, lp), bh(lambda b, h, i: (h, b))),
                pl.BlockSpec((1, 1, ns, lp), bh(lambda b, h, i: (b if kb_b else 0, h if kb_h else 0, 0, 0)))]
    args = [q[None], k[None], vt, kb]
    if kind == "diff":
        in_specs = ([pl.BlockSpec(memory_space=pltpu.SMEM)] + in_specs
                    + [pl.BlockSpec((1, wo), bh(lambda b, h, i: (0, h)))])
        args = [par] + args + [gain]
    body = functools.partial(_prompt_attn_body, kind=kind, front=front, tq=tq, ns=ns)
    return pl.pallas_call(
        body,
        grid=(bsz, n_hg, nq) if seq_major else (n_hg, bsz, nq),
        in_specs=in_specs,
        out_specs=pl.BlockSpec((1, tq, wo), bh(lambda b, h, i: (0, b * nq + i, h))),
        out_shape=jax.ShapeDtypeStruct((1, rows, n_hg * wo), BF16),
        scratch_shapes=[pltpu.VMEM((ns, 1, tq), F32), pltpu.VMEM((ns, 1, tq), F32),
                        pltpu.VMEM((ns, 128, tq), F32), pltpu.VMEM((2, ns, tq, tq), F32),
                        pltpu.VMEM((ns, tq, tq), BF16), pltpu.VMEM((ns, lp, 128), F32)],
        compiler_params=pltpu.CompilerParams(
            dimension_semantics=("arbitrary", "arbitrary", "arbitrary"), vmem_limit_bytes=VMEM_LIMIT),
    )(*args)[0]


def _cumsum_body(x_ref, tri_ref, o_ref, carry_ref):
    @pl.when(pl.program_id(0) == 0)
    def _():
        carry_ref[...] = jnp.zeros(carry_ref.shape, F32)

    y = jnp.dot(x_ref[...], tri_ref[...], preferred_element_type=F32,
                precision=lax.Precision.HIGHEST) + carry_ref[...]
    o_ref[...] = y
    carry_ref[...] = y[:, -1:]


def _cumsum_lanes(x, blk=256):
    rows, n = x.shape
    tri = (jnp.arange(blk)[:, None] <= jnp.arange(blk)[None, :]).astype(F32)
    return pl.pallas_call(
        _cumsum_body,
        grid=(n // blk,),
        in_specs=[pl.BlockSpec((rows, blk), lambda j: (0, j)), _const_spec((blk, blk))],
        out_specs=pl.BlockSpec((rows, blk), lambda j: (0, j)),
        out_shape=jax.ShapeDtypeStruct((rows, n), F32),
        scratch_shapes=[pltpu.VMEM((rows, 1), F32)],
        compiler_params=pltpu.CompilerParams(dimension_semantics=("arbitrary",)),
    )(x, tri)


def _expand_rows(x, rep):
    h, n = x.shape
    return jnp.broadcast_to(x[:, None, :], (h, rep, n)).reshape(h * rep, n)


def _dec_online(s, v16, m_ref, l_ref, acc_ref):
    m_prev = m_ref[...]
    m_new = jnp.maximum(m_prev, jnp.max(s, axis=-1, keepdims=True))
    alpha = jnp.exp2(m_prev - m_new)
    p = jnp.exp2(s - m_new)
    l_ref[...] = alpha * l_ref[...] + jnp.sum(p, axis=-1, keepdims=True)
    acc_ref[...] = alpha * acc_ref[...] + _dot(p.astype(BF16), v16)
    m_ref[...] = m_new


def _dec_init(m_ref, l_ref, acc_ref):
    m_ref[...] = jnp.full(m_ref.shape, NEG, F32)
    l_ref[...] = jnp.zeros(l_ref.shape, F32)
    acc_ref[...] = jnp.zeros(acc_ref.shape, F32)


def _diag_blocks(o, n_heads, ds, width):
    return jnp.concatenate([o[h * ds:(h + 1) * ds, h * width:(h + 1) * width] for h in range(n_heads)], axis=1)


def _diff_dec_body(par_ref, q_ref, kc_ref, vc_ref, kn_ref, vn_ref, g_ref, _, o_ref, m_ref, l_ref, acc_ref,
                   *, past, kb, ds, n_heads):
    jb = pl.program_id(1)
    hr = 2 * ds
    rows = n_heads * hr
    dv = 2 * DH_B
    r = lax.broadcasted_iota(jnp.int32, (rows, 1), 0)
    head = r // hr
    slope = LOG2E * jnp.exp2(-8.0 * (head + 1).astype(F32) / n_heads)
    qpos = past + (r % ds)
    q = q_ref[0]

    def key_block(k16, v16, key0):
        col = lax.broadcasted_iota(jnp.int32, (1, k16.shape[0]), 1)
        kpos = key0 + col // n_heads
        s = _dot_nt(q, k16) - slope * jnp.abs(qpos - kpos).astype(F32)
        s = jnp.where(col % n_heads == head, s, NEG)
        _dec_online(s, v16, m_ref, l_ref, acc_ref)

    @pl.when(jb == 0)
    def _():
        _dec_init(m_ref, l_ref, acc_ref)
        key_block(kn_ref[0], vn_ref[0], past)

    key_block(kc_ref[0, 0].astype(BF16), vc_ref[0, 0].astype(BF16), jb * kb)

    @pl.when(jb == pl.num_programs(1) - 1)
    def _():
        o = acc_ref[...] / l_ref[...]
        outs = []
        for h in range(n_heads):
            oh = o[h * hr:h * hr + ds] - par_ref[0] * o[h * hr + ds:(h + 1) * hr]
            ms = jnp.mean(oh * oh, axis=-1, keepdims=True)
            outs.append(oh * lax.rsqrt(ms + EPS) * g_ref[:, h * dv:(h + 1) * dv])
        o_ref[0] = jnp.concatenate(outs, axis=1).astype(o_ref.dtype)


def _fox_dec_body(q_ref, kc_ref, vc_ref, kn_ref, vn_ref, fc_ref, fn_ref, _, o_ref, m_ref, l_ref, acc_ref,
                  *, ds, n_heads):
    jb = pl.program_id(1)
    rows = n_heads * ds
    q = q_ref[0]
    fnew = fn_ref[0][:, :ds]
    fref = _expand_rows(fn_ref[0][:, 0:1], ds)

    @pl.when(jb == 0)
    def _():
        _dec_init(m_ref, l_ref, acc_ref)
        r = lax.broadcasted_iota(jnp.int32, (rows, 1), 0)
        kidx = lax.broadcasted_iota(jnp.int32, (1, ds), 1)
        s = _dot_nt(q, kn_ref[0]) + LOG2E * (fref - _expand_rows(fnew, ds))
        s = jnp.where(kidx <= (r % ds), s, NEG)
        _dec_online(s, vn_ref[0].astype(BF16), m_ref, l_ref, acc_ref)

    s = _dot_nt(q, kc_ref[0, 0].astype(BF16)) + LOG2E * (fref - _expand_rows(fc_ref[0], ds))
    _dec_online(s, vc_ref[0, 0].astype(BF16), m_ref, l_ref, acc_ref)

    @pl.when(jb == pl.num_programs(1) - 1)
    def _():
        o = acc_ref[...] / l_ref[...]
        o_ref[0] = _diag_blocks(o, n_heads, ds, DH_C).astype(o_ref.dtype)


def _mla_dec_body(qn_ref, qp_ref, cc_ref, pc_ref, cn_ref, pn_ref, wk_ref, wv_ref, ones_ref, _,
                  o_ref, m_ref, l_ref, acc_ref, *, ds, n_heads):
    jb = pl.program_id(1)
    qn = qn_ref[0]
    qp = qp_ref[0]
    ones_h = ones_ref[...]

    def key_block(ckv, kpe):
        c16 = ckv.astype(BF16)
        kn = _dot(c16, wk_ref[...])
        v = _dot(c16, wv_ref[...])
        n = kpe.shape[0]
        ss = _dot_nt(ones_h, (kn * kn).astype(BF16)) + _dot_nt(jnp.ones((n_heads, ROPE_D), BF16),
                                                               (kpe * kpe).astype(BF16))
        rinv = lax.rsqrt(ss * (1.0 / (NOPE_D + ROPE_D)) + EPS)
        s = _dot_nt(qn, kn.astype(BF16)) + _dot_nt(qp, kpe.astype(BF16))
        s = s * _expand_rows(rinv, ds)
        _dec_online(s, v.astype(BF16), m_ref, l_ref, acc_ref)

    @pl.when(jb == 0)
    def _():
        _dec_init(m_ref, l_ref, acc_ref)
        key_block(cn_ref[0], pn_ref[0])

    key_block(cc_ref[0, 0], pc_ref[0, 0])

    @pl.when(jb == pl.num_programs(1) - 1)
    def _():
        o = acc_ref[...] / l_ref[...]
        o_ref[0] = _diag_blocks(o, n_heads, ds, V_D).astype(o_ref.dtype)


def _per_seq(a):
    return (a, pl.BlockSpec((1,) + a.shape[1:], lambda b, j: (b, 0, 0)))


def _dec_const(a):
    return (a, _const_spec(a.shape))


def _dec_call(body, ins, prev, row0, ds, rows, acc_w, nb, n_kb, smem=None):
    in_specs = [spec for _, spec in ins] + [pl.BlockSpec(memory_space=pl.ANY)]
    args = [a for a, _ in ins] + [prev[None]]
    if smem is not None:
        in_specs = [pl.BlockSpec(memory_space=pltpu.SMEM)] + in_specs
        args = [smem] + args
    blk0 = row0 // ds
    return pl.pallas_call(
        body,
        grid=(nb, n_kb),
        in_specs=in_specs,
        out_specs=pl.BlockSpec((1, ds, prev.shape[1]), lambda b, j: (0, blk0 + b, 0)),
        out_shape=jax.ShapeDtypeStruct((1,) + prev.shape, prev.dtype),
        input_output_aliases={len(args) - 1: 0},
        scratch_shapes=[pltpu.VMEM((rows, 1), F32), pltpu.VMEM((rows, 1), F32), pltpu.VMEM((rows, acc_w), F32)],
        compiler_params=pltpu.CompilerParams(
            dimension_semantics=("parallel", "arbitrary"), vmem_limit_bytes=VMEM_LIMIT),
    )(*args)[0]


def kernel(x_prompt, x_sample, state_s5_re, state_s5_im, cache_diff_k, cache_diff_v, cache_fox_k, cache_fox_v, cache_fox_logf, cache_mla_ckv, cache_mla_kpe, meta_tokens, ffn_norm, ffn_w_in, ffn_w_out, mix_norm, ab_w_in, ab_w_out, s5_a_re, s5_a_im, s5_log_step, s5_b_re, s5_b_im, s5_c_re, s5_c_im, s5_d, s5_glu_w, s5_glu_b, diff_q_norm, diff_k_norm, diff_lam, diff_sub_norm, cd_w_in, cd_w_out, fox_q_norm, fox_k_norm, fox_f_bias, mla_q_a_norm, mla_q_b, mla_kv_a_norm, mla_kv_b, mla_q_norm, mla_k_norm):
    bsz, seq, dm = x_prompt.shape
    nb, ds, _ = x_sample.shape
    n_meta = meta_tokens.shape[0]
    past = cache_diff_k.shape[2]
    front = ROW_ALIGN - n_meta
    lp = front + n_meta + seq
    ltot = n_meta + seq
    assert n_meta + front == ROW_ALIGN and lp % ATT_BLOCK == 0 and front % CHUNK == CHUNK - n_meta
    assert ds == S5_STEP and past % CHUNK == 0 and ds <= CHUNK
    kb = min(DEC_KB, past)
    assert past % kb == 0
    n_kb = past // kb
    assert ffn_norm.shape[0] == 2 and ab_w_in.shape[0] == 1 and cd_w_in.shape[0] == 1

    h_b = cache_diff_k.shape[3]
    h_c = cache_fox_k.shape[3]
    h_d = mla_q_b.shape[2] // (NOPE_D + ROPE_D)
    s5w = s5_glu_w.shape[1]
    n_grp = s5w // S5_GROUP
    qkw = h_b * 2 * DH_B
    fox_w = h_c * DH_C
    q_lora = mla_q_a_norm.shape[1]
    kv_lora = mla_kv_a_norm.shape[1]
    d_qk = NOPE_D + ROPE_D

    n_p = bsz * lp
    head_rows_x = jnp.concatenate([jnp.zeros((front, dm), F32), meta_tokens.astype(F32)], axis=0)
    pieces = []
    for b in range(bsz):
        pieces += [head_rows_x, x_prompt[b]]
    x = jnp.concatenate(pieces + [x_sample.reshape(nb * ds, dm)], axis=0)
    rows = x.shape[0]
    tm = _row_tile(rows)

    ones64 = _block_diag_ones(64)
    ones128 = _block_diag_ones(128)

    x = _ffn(x, ffn_norm[0, 0], ffn_w_in[0, 0], ffn_w_out[0, 0], tm)

    gq = (jnp.tile(diff_q_norm[0], 2 * h_b) * (DH_B ** -0.5 * LOG2E)).reshape(1, qkw)
    gk = jnp.tile(diff_k_norm[0], 2 * h_b).reshape(1, qkw)
    s5_cols = S5_STEP * S5_GROUP
    q16, k32, k16, v32, u_t, vt_diff = _row_call(
        functools.partial(_ab_in_body, widths=(s5w, qkw)),
        [x], [mix_norm[0].reshape(1, dm), ab_w_in[0].astype(BF16), gq, gk, ones64],
        [qkw, qkw, qkw, h_b * 2 * DH_B], [BF16, F32, BF16, F32], tm,
        group_outs=[jax.ShapeDtypeStruct((n_grp, rows // S5_STEP, s5_cols), BF16)],
        scratch=[pltpu.VMEM((s5w // 128, tm, 128), F32)],
        col_outs=[(h_b * 2 * DH_B, BF16)])

    mats = _s5_matrices(s5_a_re[0], s5_a_im[0], s5_log_step[0], s5_b_re[0], s5_b_im[0],
                        s5_c_re[0], s5_c_im[0], s5_d[0])
    n_ch = lp // S5_STEP
    y_t, st_p = _s5_scan(u_t, jnp.zeros((n_grp, bsz, 4 * P_A), F32), mats, n_ch, bsz, 0)
    h_re = jnp.transpose(state_s5_re[0].astype(F32), (1, 0, 2))
    h_im = jnp.transpose(state_s5_im[0].astype(F32), (1, 0, 2))
    y_t, st_s = _s5_scan(u_t, jnp.concatenate([h_re, h_im, h_im, h_re], axis=-1), mats, 1, nb,
                         n_p // S5_STEP, y_prev=y_t)

    lv = diff_lam[0].astype(F32)
    lam_init = 0.8 - 0.6 * math.exp(-0.3 * 0)
    lam = jnp.exp(jnp.sum(lv[0] * lv[1])) - jnp.exp(jnp.sum(lv[2] * lv[3])) + lam_init
    slopes = jnp.exp2(-8.0 * jnp.arange(1, h_b + 1, dtype=F32) / h_b)
    par = jnp.concatenate([lam[None], slopes]).astype(F32)
    subg = (jnp.tile(diff_sub_norm[0], h_b) * (1.0 - lam_init)).reshape(1, qkw)
    kpad = jnp.arange(lp) < front
    kb_diff = jnp.where(kpad[None, :], NEG, LOG2E * slopes[:, None] * jnp.arange(lp, dtype=F32)[None, :])
    kb_diff = jnp.broadcast_to(kb_diff[None, :, None, :], (1, h_b, 2, lp)).reshape(
        1, 2 * h_b // ATT_STREAMS, ATT_STREAMS, lp)
    o_all = _prompt_attn("diff", q16, k16, vt_diff, kb_diff, bsz, lp, front, par=par, gain=subg)
    qs = q16[n_p:].reshape(nb, ds, h_b, 2, DH_B)
    eye_2 = jnp.eye(2, dtype=BF16)
    qbd = jnp.einsum('bqhmd,mM->bhmqMd', qs, eye_2).reshape(nb, h_b * 2 * ds, 2 * DH_B)
    cache_spec = lambda w: pl.BlockSpec((1, 1, kb, w), lambda b, j: (0, b, j, 0))
    head_rows = lambda a: a.reshape(a.shape[0], nb, past * h_b, 2 * DH_B)
    head_cache_spec = pl.BlockSpec((1, 1, kb * h_b, 2 * DH_B), lambda b, j: (0, b, j, 0))
    o_all = _dec_call(
        functools.partial(_diff_dec_body, past=past, kb=kb, ds=ds, n_heads=h_b),
        [_per_seq(qbd), (head_rows(cache_diff_k), head_cache_spec), (head_rows(cache_diff_v), head_cache_spec),
         _per_seq(k16[n_p:].reshape(nb, ds * h_b, 2 * DH_B)), _per_seq(v32[n_p:].astype(BF16).reshape(nb, ds * h_b, 2 * DH_B)),
         _dec_const(subg)],
        o_all, n_p, ds, 2 * h_b * ds, 2 * DH_B, nb, n_kb, smem=par)

    x = _row_call(
        functools.partial(_ab_out_body, s5w=s5w),
        [x, o_all],
        [s5_glu_w[0].astype(BF16), s5_glu_b[0].reshape(1, s5w), ab_w_out[0].astype(BF16)],
        [dm], [F32], tm, group_ins=[y_t], scratch=[pltpu.VMEM((s5w // 128, tm, 128), F32)])[0]

    x = _ffn(x, ffn_norm[0, 1], ffn_w_in[0, 1], ffn_w_out[0, 1], tm)

    x = _ffn(x, ffn_norm[1, 0], ffn_w_in[1, 0], ffn_w_out[1, 0], tm)

    half = ROPE_D // 2
    inv = ROPE_THETA ** (-jnp.arange(half, dtype=F32) / half)
    pos = jnp.concatenate([jnp.tile(jnp.arange(lp, dtype=jnp.int32) - front, bsz),
                           jnp.tile(past + jnp.arange(ds, dtype=jnp.int32), nb)]).astype(F32)
    ang = pos[:, None] * inv[None, :]
    pad_r = HEAD_PAD - NOPE_D - ROPE_D
    cos_t = jnp.concatenate([jnp.ones((rows, NOPE_D), F32), jnp.cos(ang), jnp.cos(ang),
                             jnp.zeros((rows, pad_r), F32)], axis=1)
    sin_t = jnp.concatenate([jnp.zeros((rows, NOPE_D), F32), jnp.sin(ang), jnp.sin(ang),
                             jnp.zeros((rows, pad_r), F32)], axis=1)

    wcd = cd_w_in[0]
    c_fg = 3 * fox_w
    c_qa = c_fg + h_c
    c_kva = c_qa + q_lora
    c_pe = c_kva + kv_lora
    w_pe = wcd[:, c_pe:c_pe + ROPE_D]
    zc = lambda n: jnp.zeros((dm, n), F32)
    w_cd = jnp.concatenate([
        wcd[:, :3 * fox_w], wcd[:, c_qa:c_qa + q_lora], wcd[:, c_kva:c_kva + kv_lora],
        zc(NOPE_D), w_pe, zc(pad_r),
        zc(NOPE_D), -w_pe[:, half:], w_pe[:, :half], zc(pad_r),
        wcd[:, c_fg:c_fg + h_c], zc(HEAD_PAD - h_c)], axis=1).astype(BF16)
    qb = mla_q_b[0].reshape(q_lora, h_d, d_qk)
    zq = lambda n: jnp.zeros((q_lora, h_d, n), F32)
    qb_pad = jnp.concatenate([qb, zq(pad_r)], axis=-1).reshape(q_lora, h_d * HEAD_PAD)
    qb_rot = jnp.concatenate([zq(NOPE_D), -qb[..., NOPE_D + half:], qb[..., NOPE_D:NOPE_D + half], zq(pad_r)],
                             axis=-1).reshape(q_lora, h_d * HEAD_PAD)
    wq2 = jnp.concatenate([qb_pad, qb_rot], axis=1).astype(BF16)
    kvb = mla_kv_b[0].reshape(kv_lora, h_d, NOPE_D + V_D)
    wk_pad = jnp.concatenate([kvb[..., :NOPE_D], jnp.zeros((kv_lora, h_d, HEAD_PAD - NOPE_D), F32)],
                             axis=-1).reshape(kv_lora, h_d * HEAD_PAD).astype(BF16)
    wk_cmp = kvb[..., :NOPE_D].reshape(kv_lora, h_d * NOPE_D).astype(BF16)
    wv_cmp = kvb[..., NOPE_D:].reshape(kv_lora, h_d * V_D).astype(BF16)
    gfq = (jnp.tile(fox_q_norm[0], h_c) * (DH_C ** -0.5 * LOG2E)).reshape(1, fox_w)
    gfk = jnp.tile(fox_k_norm[0], h_c).reshape(1, fox_w)
    fbias = jnp.concatenate([fox_f_bias[0], jnp.zeros((HEAD_PAD - h_c,), F32)]).reshape(1, HEAD_PAD)
    gmq = jnp.tile(jnp.concatenate([mla_q_norm[0] * mla_k_norm[0] * (d_qk ** -0.5 * LOG2E), jnp.zeros((pad_r,), F32)]),
                   h_d).reshape(1, h_d * HEAD_PAD)

    (fq16, fk32, fk16, fv32, logf, qm16, ckv32, kpe32, km16, fvt, vmt) = _row_call(
        functools.partial(_cd_in_body, fox_w=fox_w, q_lora=q_lora, kv_lora=kv_lora, n_heads=h_d),
        [x, cos_t, sin_t],
        [mix_norm[1].reshape(1, dm), w_cd, gfq, gfk, fbias, mla_q_a_norm[0].reshape(1, q_lora), wq2,
         mla_kv_a_norm[0].reshape(1, kv_lora), wk_pad, wv_cmp.T, gmq, ones64, ones128],
        [fox_w, fox_w, fox_w, fox_w, h_c, h_d * HEAD_PAD, kv_lora, ROPE_D, h_d * HEAD_PAD],
        [BF16, F32, BF16, F32, F32, BF16, F32, F32, BF16], tm,
        col_outs=[(2 * fox_w, BF16), (2 * h_d * V_D, BF16)])

    logf_p = jnp.transpose(logf[:n_p].reshape(bsz, lp, h_c), (0, 2, 1)).reshape(bsz * h_c, lp)
    f_p = _cumsum_lanes(logf_p).reshape(bsz, h_c, lp)
    logf_s = jnp.concatenate([
        jnp.transpose(cache_fox_logf[0].astype(F32), (0, 2, 1)),
        jnp.transpose(logf[n_p:].reshape(nb, ds, h_c), (0, 2, 1)),
        jnp.zeros((nb, h_c, 256 - ds), F32)], axis=2).reshape(nb * h_c, past + 256)
    f_s = _cumsum_lanes(logf_s).reshape(nb, h_c, past + 256)

    kb_fox = jnp.where(kpad[None, None, :], NEG, -LOG2E * f_p).reshape(bsz, h_c // ATT_STREAMS, ATT_STREAMS, lp)
    oc_all = _prompt_attn("fox", fq16, fk16, fvt, kb_fox, bsz, lp, front)
    kb_mla = jnp.broadcast_to(jnp.where(kpad, NEG, 0.0).astype(F32)[None, None, None, :], (1, 1, ATT_STREAMS, lp))
    od_all = _prompt_attn("mla", qm16, km16, vmt, kb_mla, bsz, lp, front)

    eye_c = jnp.eye(h_c, dtype=BF16)
    fqs = fq16[n_p:].reshape(nb, ds, h_c, DH_C)
    fq_bd = jnp.einsum('bqhd,hH->bhqHd', fqs, eye_c).reshape(nb, h_c * ds, fox_w)
    fkc = cache_fox_k.reshape(cache_fox_k.shape[0], nb, past, fox_w)
    fvc = cache_fox_v.reshape(cache_fox_v.shape[0], nb, past, fox_w)
    oc_all = _dec_call(
        functools.partial(_fox_dec_body, ds=ds, n_heads=h_c),
        [_per_seq(fq_bd), (fkc, cache_spec(fox_w)), (fvc, cache_spec(fox_w)),
         _per_seq(fk16[n_p:].reshape(nb, ds, fox_w)), _per_seq(fv32[n_p:].reshape(nb, ds, fox_w)),
         (f_s, pl.BlockSpec((1, h_c, kb), lambda b, j: (b, 0, j))),
         (f_s, pl.BlockSpec((1, h_c, 128), lambda b, j: (b, 0, past // 128)))],
        oc_all, n_p, ds, h_c * ds, fox_w, nb, n_kb)

    eye_d = jnp.eye(h_d, dtype=BF16)
    qms = qm16[n_p:].reshape(nb, ds, h_d, HEAD_PAD)
    qn_bd = jnp.einsum('bqhd,hH->bhqHd', qms[..., :NOPE_D], eye_d).reshape(nb, h_d * ds, h_d * NOPE_D)
    qp_s = jnp.transpose(qms[..., NOPE_D:NOPE_D + ROPE_D], (0, 2, 1, 3)).reshape(nb, h_d * ds, ROPE_D)
    ones_h = jnp.repeat(jnp.eye(h_d, dtype=BF16), NOPE_D, axis=1)
    od_all = _dec_call(
        functools.partial(_mla_dec_body, ds=ds, n_heads=h_d),
        [_per_seq(qn_bd), _per_seq(qp_s),
         (cache_mla_ckv, pl.BlockSpec((1, 1, kb, kv_lora), lambda b, j: (0, b, j, 0))),
         (cache_mla_kpe, pl.BlockSpec((1, 1, kb, ROPE_D), lambda b, j: (0, b, j, 0))),
         _per_seq(ckv32[n_p:].reshape(nb, ds, kv_lora)), _per_seq(kpe32[n_p:].reshape(nb, ds, ROPE_D)),
         _dec_const(wk_cmp), _dec_const(wv_cmp), _dec_const(ones_h)],
        od_all, n_p, ds, h_d * ds, h_d * V_D, nb, n_kb)

    x = _row_call(functools.partial(_cd_out_body, fox_w=fox_w), [x, oc_all, od_all],
                  [cd_w_out[0].astype(BF16)], [dm], [F32], tm)[0]

    x = _ffn(x, ffn_norm[1, 1], ffn_w_in[1, 1], ffn_w_out[1, 1], tm)

    def p_rows(a, shape):
        w = a.shape[1]
        return a[:n_p].reshape(bsz, lp, w)[:, front:front + ltot].reshape((1, bsz, ltot) + shape)

    def s_rows(a, shape):
        return a[n_p:].reshape((1, nb, ds) + shape)

    def s5_state(st):
        st = jnp.transpose(st, (1, 0, 2))
        return st[None, :, :, :P_A], st[None, :, :, P_A:]

    y_prompt = x[:n_p].reshape(bsz, lp, dm)[:, front + n_meta:]
    y_sample = x[n_p:].reshape(nb, ds, dm)
    s5_re_p, s5_im_p = s5_state(st_p)
    s5_re_s, s5_im_s = s5_state(st_s)
    return (y_prompt, y_sample,
            s5_re_p, s5_im_p, p_rows(k32, (h_b, 2 * DH_B)), p_rows(v32, (h_b, 2 * DH_B)),
            p_rows(fk32, (h_c, DH_C)), p_rows(fv32, (h_c, DH_C)), p_rows(logf, (h_c,)),
            p_rows(ckv32, (kv_lora,)), p_rows(kpe32, (ROPE_D,)),
            s5_re_s, s5_im_s, s_rows(k32, (h_b, 2 * DH_B)), s_rows(v32, (h_b, 2 * DH_B)),
            s_rows(fk32, (h_c, DH_C)), s_rows(fv32, (h_c, DH_C)), s_rows(logf, (h_c,)),
            s_rows(ckv32, (kv_lora,)), s_rows(kpe32, (ROPE_D,)))
```

```python
import functools
import math

import jax
import jax.numpy as jnp
from jax import lax
from jax.experimental import pallas as pl
from jax.experimental.pallas import tpu as pltpu

F32 = jnp.float32
BF16 = jnp.bfloat16

EPS = 1e-6
CHUNK = 64
ROW_ALIGN = 256
S5_GROUP = 16
S5_STEP = 16
P_A = 64
DH_B = 64
DH_C = 64
NOPE_D = 64
ROPE_D = 32
V_D = 64
HEAD_PAD = 128
ROPE_THETA = 10000.0
NEG = -1e30
LOG2E = math.log2(math.e)
VMEM_LIMIT = 56 * 1024 * 1024
ATT_BLOCK = 256
ATT_STREAMS = 4
ATT_ROWS = 64
DIFF_VT_ROWS = 2 * DH_B + 16
DEC_KB = 1024


def _dot(a, b):
    return jnp.dot(a, b, preferred_element_type=F32)


def _dot_nt(a, b):
    return lax.dot_general(a, b, (((1,), (1,)), ((), ())), preferred_element_type=F32)


def _rms_rows(x, g):
    ms = jnp.mean(x * x, axis=-1, keepdims=True)
    return x * lax.rsqrt(ms + EPS) * g


def _group_sumsq(x, ones_bd):
    w = x.shape[-1]
    parts = [_dot((x[:, c:c + 256] * x[:, c:c + 256]).astype(BF16), ones_bd) for c in range(0, w, 256)]
    return parts[0] if len(parts) == 1 else jnp.concatenate(parts, axis=1)


def _block_diag_ones(group, n=256):
    r = jnp.arange(n) // group
    return (r[:, None] == r[None, :]).astype(BF16)


def _const_spec(shape):
    nd = len(shape)
    return pl.BlockSpec(shape, lambda *_: (0,) * nd, pipeline_mode=pl.Buffered(1))


def _row_tile(rows, cap=512):
    t = cap
    while rows % t:
        t //= 2
    return t


def _group_spec(a, tm):
    return pl.BlockSpec((a.shape[0], tm // S5_STEP, a.shape[2]), lambda i: (0, i, 0))


def _row_call(body, row_ins, consts, out_widths, out_dtypes, tm, group_ins=(), group_outs=(), scratch=(),
              col_outs=()):
    rows = row_ins[0].shape[0]
    in_specs = [pl.BlockSpec((tm, a.shape[1]), lambda i: (i, 0)) for a in row_ins]
    in_specs += [_group_spec(a, tm) for a in group_ins]
    in_specs += [_const_spec(c.shape) for c in consts]
    out_specs = [pl.BlockSpec((tm, w), lambda i: (i, 0)) for w in out_widths]
    out_specs += [_group_spec(a, tm) for a in group_outs]
    out_specs += [pl.BlockSpec((w, tm), lambda i: (0, i)) for w, _ in col_outs]
    out_shape = [jax.ShapeDtypeStruct((rows, w), d) for w, d in zip(out_widths, out_dtypes)]
    out_shape += list(group_outs)
    out_shape += [jax.ShapeDtypeStruct((w, rows), d) for w, d in col_outs]
    return pl.pallas_call(
        body,
        grid=(rows // tm,),
        in_specs=in_specs,
        out_specs=out_specs,
        out_shape=out_shape,
        scratch_shapes=list(scratch),
        compiler_params=pltpu.CompilerParams(
            dimension_semantics=("parallel",), vmem_limit_bytes=VMEM_LIMIT),
    )(*row_ins, *group_ins, *consts)


def _ffn_body(x_ref, g_ref, win_ref, wout_ref, o_ref, *, d_ff, tf):
    x = x_ref[...]
    xn = _rms_rows(x, g_ref[...]).astype(BF16)
    acc = jnp.zeros(x.shape, F32)
    for c in range(0, d_ff, tf):
        gate = _dot(xn, win_ref[:, c:c + tf])
        up = _dot(xn, win_ref[:, d_ff + c:d_ff + c + tf])
        a = (gate * jax.nn.sigmoid(gate) * up).astype(BF16)
        acc = acc + _dot(a, wout_ref[c:c + tf, :])
    o_ref[...] = x + 0.5 * acc


def _ffn(x, g, w_in, w_out, tm):
    d_ff = w_out.shape[0]
    body = functools.partial(_ffn_body, d_ff=d_ff, tf=256)
    return _row_call(body, [x], [g.reshape(1, -1), w_in.astype(BF16), w_out.astype(BF16)],
                     [x.shape[1]], [F32], tm)[0]


def _ab_in_body(x_ref, g_ref, w_ref, gq_ref, gk_ref, ones_ref,
                q_ref, k32_ref, k16_ref, v32_ref, ut_ref, vt_ref, us_ref, *, widths):
    s5w, qkw = widths
    xn = _rms_rows(x_ref[...], g_ref[...]).astype(BF16)
    h = _dot(xn, w_ref[...])
    n_chunk = us_ref.shape[1] // S5_STEP
    per_col = 128 // S5_GROUP
    for v in range(s5w // 128):
        us_ref[v] = h[:, v * 128:(v + 1) * 128]
        steps = [us_ref[v, pl.ds(t, n_chunk, stride=S5_STEP), :] for t in range(S5_STEP)]
        for gl in range(per_col):
            ut_ref[v * per_col + gl] = jnp.concatenate(
                [x[:, gl * S5_GROUP:(gl + 1) * S5_GROUP] for x in steps], axis=1).astype(BF16)
    q = h[:, s5w:s5w + qkw]
    k = h[:, s5w + qkw:s5w + 2 * qkw]
    v = h[:, s5w + 2 * qkw:]
    ones_bd = ones_ref[...]
    qn = q * lax.rsqrt(_group_sumsq(q, ones_bd) * (1.0 / DH_B) + EPS) * gq_ref[...]
    kn = k * lax.rsqrt(_group_sumsq(k, ones_bd) * (1.0 / DH_B) + EPS) * gk_ref[...]
    q_ref[...] = qn.astype(BF16)
    k32_ref[...] = kn
    k16_ref[...] = kn.astype(BF16)
    v32_ref[...] = v
    vt = v.T
    dv = 2 * DH_B
    ones = jnp.ones((DIFF_VT_ROWS - dv, vt.shape[1]), F32)
    vt_ref[...] = jnp.concatenate(
        [a for h in range(vt.shape[0] // dv) for a in (vt[h * dv:(h + 1) * dv], ones)], axis=0).astype(BF16)


def _ab_out_body(x_ref, o_ref, yt_ref, gluw_ref, glub_ref, wout_ref, out_ref, ys_ref, *, s5w):
    n_chunk = ys_ref.shape[1] // S5_STEP
    per_col = 128 // S5_GROUP
    for v in range(s5w // 128):
        for t in range(S5_STEP):
            ys_ref[v, pl.ds(t, n_chunk, stride=S5_STEP), :] = jnp.concatenate(
                [yt_ref[v * per_col + gl][:, t * S5_GROUP:(t + 1) * S5_GROUP] for gl in range(per_col)], axis=1)
    y = jnp.concatenate([ys_ref[v] for v in range(s5w // 128)], axis=1)
    g = 0.5 * y * (1.0 + jnp.tanh(math.sqrt(2.0 / math.pi) * (y + 0.044715 * (y * y * y))))
    z = _dot(g.astype(BF16), gluw_ref[...]) + glub_ref[...]
    s5o = g * jax.nn.sigmoid(z)
    m = _dot(s5o.astype(BF16), wout_ref[:s5w, :]) + _dot(o_ref[...], wout_ref[s5w:, :])
    out_ref[...] = x_ref[...] + m


def _heads_with_ones_t(vt):
    ones = jnp.ones((64, vt.shape[1]), vt.dtype)
    outs = []
    for h in range(vt.shape[0] // 64):
        outs += [vt[h * 64:(h + 1) * 64], ones]
    return jnp.concatenate(outs, axis=0)


def _cd_in_body(x_ref, cos_ref, sin_ref, g_ref, w_ref, gfq_ref, gfk_ref, fb_ref, gqa_ref, wq2_ref,
                gkva_ref, wk_ref, wv_ref, gmq_ref, ones64_ref, ones128_ref,
                fq_ref, fk32_ref, fk16_ref, fv32_ref, logf_ref, qm_ref, ckv_ref, kpe_ref,
                km_ref, fvt_ref, vmt_ref, *, fox_w, q_lora, kv_lora, n_heads):
    xn = _rms_rows(x_ref[...], g_ref[...]).astype(BF16)
    h = _dot(xn, w_ref[...])
    ones64 = ones64_ref[...]
    ones128 = ones128_ref[...]
    fq = h[:, :fox_w]
    fk = h[:, fox_w:2 * fox_w]
    fv = h[:, 2 * fox_w:3 * fox_w]
    c0 = 3 * fox_w
    qa = h[:, c0:c0 + q_lora]
    kva = h[:, c0 + q_lora:c0 + q_lora + kv_lora]
    c1 = c0 + q_lora + kv_lora
    pe_a = h[:, c1:c1 + HEAD_PAD]
    pe_b = h[:, c1 + HEAD_PAD:c1 + 2 * HEAD_PAD]
    fg = h[:, c1 + 2 * HEAD_PAD:c1 + 3 * HEAD_PAD]

    fqn = fq * lax.rsqrt(_group_sumsq(fq, ones64) * (1.0 / DH_C) + EPS) * gfq_ref[...]
    fkn = fk * lax.rsqrt(_group_sumsq(fk, ones64) * (1.0 / DH_C) + EPS) * gfk_ref[...]
    fq_ref[...] = fqn.astype(BF16)
    fk32_ref[...] = fkn
    fk16_ref[...] = fkn.astype(BF16)
    fv32_ref[...] = fv
    fvt_ref[...] = _heads_with_ones_t(fv.T).astype(BF16)

    z = fg + fb_ref[...]
    logf = jnp.minimum(z, 0.0) - jnp.log1p(jnp.exp(-jnp.abs(z)))
    logf_ref[...] = logf[:, :logf_ref.shape[1]]

    cos = cos_ref[...]
    sin = sin_ref[...]
    qan = _rms_rows(qa, gqa_ref[...]).astype(BF16)
    q2 = _dot(qan, wq2_ref[...])
    hw = n_heads * HEAD_PAD
    cos_t = jnp.concatenate([cos] * n_heads, axis=1)
    sin_t = jnp.concatenate([sin] * n_heads, axis=1)
    qr = q2[:, :hw] * cos_t + q2[:, hw:] * sin_t
    d_qk = NOPE_D + ROPE_D
    qm = qr * lax.rsqrt(_group_sumsq(qr, ones128) * (1.0 / d_qk) + EPS) * gmq_ref[...]
    qm_ref[...] = qm.astype(BF16)

    ckv = _rms_rows(kva, gkva_ref[...])
    ckv_ref[...] = ckv
    pe = pe_a * cos + pe_b * sin
    kpe_ref[...] = pe[:, NOPE_D:NOPE_D + ROPE_D]
    ckv16 = ckv.astype(BF16)
    kraw = _dot(ckv16, wk_ref[...]) + jnp.concatenate([pe] * n_heads, axis=1)
    km = kraw * lax.rsqrt(_group_sumsq(kraw, ones128) * (1.0 / d_qk) + EPS)
    km_ref[...] = km.astype(BF16)
    vmt_ref[...] = _heads_with_ones_t(_dot_nt(wv_ref[...], ckv16)).astype(BF16)


def _cd_out_body(x_ref, oc_ref, od_ref, wout_ref, out_ref, *, fox_w):
    m = _dot(oc_ref[...], wout_ref[:fox_w, :]) + _dot(od_ref[...], wout_ref[fox_w:, :])
    out_ref[...] = x_ref[...] + m


def _s5_body(*refs, n_chunks, bsz, aliased):
    if aliased:
        u_ref, h0_ref, m_ref, bm_ref, cm_ref, coef_ref, _, y_ref, st_ref, s2_ref, hp_ref = refs
    else:
        u_ref, h0_ref, m_ref, bm_ref, cm_ref, coef_ref, y_ref, st_ref, s2_ref, hp_ref = refs
    u = u_ref[0]
    half = 2 * P_A
    s2 = _dot(u, bm_ref[0])
    s2_ref[0] = s2[:, :half]
    s2_ref[1] = s2[:, half:]
    c1 = coef_ref[0, 0:1, :]
    c2 = coef_ref[0, 1:2, :]
    c3 = coef_ref[0, 2:3, :]

    def step(j, carry):
        ha, hb = carry
        hp_ref[pl.ds(j, bsz, stride=n_chunks), :] = ha
        sa = s2_ref[0, pl.ds(j, bsz, stride=n_chunks), :]
        sb = s2_ref[1, pl.ds(j, bsz, stride=n_chunks), :]
        return ha * c1 + hb * c2 + sa, hb * c1 + ha * c3 + sb

    h0 = h0_ref[0]
    ha, _ = lax.fori_loop(0, n_chunks, step, (h0[:, :half], h0[:, half:]))
    st_ref[0] = ha
    y_ref[0] = _dot(u, m_ref[0]) + _dot(hp_ref[...].astype(BF16), cm_ref[0])


def _s5_scan(u_t, h0, mats, n_chunks, bsz, row0, y_prev=None):
    m_mat, bm, cm, coef = mats
    g, rows_all, w = u_t.shape
    rows = n_chunks * bsz
    assert row0 % rows == 0
    blk = row0 // rows
    aliased = y_prev is not None
    body = functools.partial(_s5_body, n_chunks=n_chunks, bsz=bsz, aliased=aliased)
    per_g = lambda a: pl.BlockSpec((1,) + a.shape[1:], lambda i: (i, 0, 0))
    in_specs = [pl.BlockSpec((1, rows, w), lambda i: (i, blk, 0)),
                per_g(h0), per_g(m_mat), per_g(bm), per_g(cm), per_g(coef)]
    args = [u_t, h0, m_mat, bm, cm, coef]
    if aliased:
        in_specs.append(pl.BlockSpec(memory_space=pl.ANY))
        args.append(y_prev)
    return pl.pallas_call(
        body,
        grid=(g,),
        in_specs=in_specs,
        out_specs=[pl.BlockSpec((1, rows, w), lambda i: (i, blk, 0)),
                   pl.BlockSpec((1, bsz, 2 * P_A), lambda i: (i, 0, 0))],
        out_shape=[jax.ShapeDtypeStruct((g, rows_all, w), F32),
                   jax.ShapeDtypeStruct((g, bsz, 2 * P_A), F32)],
        input_output_aliases={6: 0} if aliased else {},
        scratch_shapes=[pltpu.VMEM((2, rows, 2 * P_A), F32), pltpu.VMEM((rows, 2 * P_A), F32)],
        compiler_params=pltpu.CompilerParams(
            dimension_semantics=("parallel",), vmem_limit_bytes=VMEM_LIMIT),
    )(*args)


def _s5_matrices(a_re, a_im, log_step, b_re, b_im, c_re, c_im, d):
    g = a_re.shape[0]
    t = S5_STEP
    lam = lax.complex(a_re, a_im)
    dl = lam * jnp.exp(log_step)[:, None]
    lam_bar = jnp.exp(dl)
    b_bar = ((lam_bar - 1.0) / lam)[..., None] * lax.complex(b_re, b_im)
    c = lax.complex(c_re, c_im)
    pw = jnp.exp(dl[:, None, :] * jnp.arange(t + 1, dtype=F32)[None, :, None])
    bmc = pw[:, t - 1::-1][:, :, :, None] * b_bar[:, None]
    bmc = jnp.swapaxes(bmc, 2, 3).reshape(g, t * S5_GROUP, P_A)
    bm = jnp.concatenate([bmc.real, bmc.imag, bmc.imag, bmc.real], axis=-1)
    kk = jnp.einsum('gcp,gkp,gpd->gkcd', c, pw[:, :t], b_bar).real
    kk = kk.at[:, 0].add(d.reshape(g, S5_GROUP)[:, :, None] * jnp.eye(S5_GROUP, dtype=F32))
    lag = jnp.arange(t)[None, :] - jnp.arange(t)[:, None]
    toep = jnp.where((lag >= 0)[None, :, :, None, None], kk[:, jnp.clip(lag, 0, t - 1)], 0.0)
    m_mat = jnp.transpose(toep, (0, 1, 4, 2, 3)).reshape(g, t * S5_GROUP, t * S5_GROUP)
    cp = c[:, None] * pw[:, 1:, None, :]
    cpm = jnp.transpose(cp, (0, 3, 1, 2)).reshape(g, P_A, t * S5_GROUP)
    cm = jnp.concatenate([cpm.real, -cpm.imag], axis=1)
    a_t = pw[:, t]
    ar, ai = a_t.real, a_t.imag
    zeros = jnp.zeros_like(ar)
    coef = jnp.stack([jnp.concatenate([ar, ar], -1), jnp.concatenate([-ai, ai], -1),
                      jnp.concatenate([ai, -ai], -1), jnp.concatenate([zeros, zeros], -1)], axis=1)
    return m_mat.astype(BF16), bm.astype(BF16), cm.astype(BF16), coef.astype(F32)


def _online(logits, vt, e, m_ref, acc_ref, p_ref, block_max=None):
    tk = p_ref.shape[1]
    if block_max is None:
        part = logits(0, ATT_ROWS)
        for r0 in range(ATT_ROWS, tk, ATT_ROWS):
            part = jnp.maximum(part, logits(r0, ATT_ROWS))
    else:
        part = block_max
    m_prev = m_ref[e]
    m_new = jnp.maximum(m_prev, jnp.max(part, axis=0, keepdims=True))
    alpha = jnp.exp2(m_prev - m_new)
    m_ref[e] = m_new
    for r0 in range(0, tk, ATT_ROWS):
        p_ref[e, r0:r0 + ATT_ROWS] = jnp.exp2(logits(r0, ATT_ROWS) - m_new).astype(BF16)
    acc_ref[e] = alpha * acc_ref[e] + _dot(vt, p_ref[e])


def _prompt_attn_body(*refs, kind, front, tq, ns):
    if kind == "diff":
        (par_ref, q_ref, k_ref, vt_ref, kb_ref, g_ref, o_ref,
         m_ref, acc_ref, s_ref, p_ref, kbc_ref, mx_ref) = refs
    else:
        q_ref, k_ref, vt_ref, kb_ref, o_ref, m_ref, acc_ref, s_ref, p_ref, kbc_ref, mx_ref = refs
    vrows = acc_ref.shape[1]
    i = pl.program_id(2)
    if kind == "fox":
        hg = pl.program_id(1)
        new_bias = i == 0
    else:
        hg = pl.program_id(0)
        new_bias = jnp.logical_and(pl.program_id(1) == 0, i == 0)
    m_ref[...] = jnp.full(m_ref.shape, NEG, F32)
    acc_ref[...] = jnp.zeros(acc_ref.shape, F32)

    @pl.when(new_bias)
    def _():
        def fill(c, carry):
            c0 = pl.multiple_of(c * 128, 128)
            for e in range(ns):
                row = kb_ref[0, 0, e:e + 1, pl.ds(c0, 128)]
                kbc_ref[e, pl.ds(c0, 128), :] = jnp.broadcast_to(row, (128, 128)).T
            return carry

        lax.fori_loop(0, kbc_ref.shape[1] // 128, fill, 0)

    q = q_ref[0]
    lane = lax.broadcasted_iota(jnp.int32, (1, 128), 1)
    qs = []
    for e in range(ns):
        if kind == "mla":
            qs.append(q[:, e * HEAD_PAD:(e + 1) * HEAD_PAD])
        else:
            qp = q[:, (e // 2) * 128:(e // 2 + 1) * 128]
            qs.append(jnp.where((lane < 64) if e % 2 == 0 else (lane >= 64), qp, jnp.zeros_like(qp)))
    qstart = pl.multiple_of(i * tq, tq)
    qpos = qstart + lax.broadcasted_iota(jnp.int32, (1, tq), 1)
    ref = [-kb_ref[0, 0, e:e + 1, pl.ds(qstart + (tq - 128), 128)][:, 127:128] for e in range(ns)]
    if kind == "diff":
        slope = [LOG2E * par_ref[1 + hg * (ns // 2) + p] for p in range(ns // 2)]

    tk = tq

    def k_slot(k, e):
        if kind == "mla":
            return k[:, e * HEAD_PAD:(e + 1) * HEAD_PAD]
        return k[:, (e // 2) * 128:(e // 2 + 1) * 128]

    def vt_slot(vt, e):
        r0 = (e // 2 if kind == "diff" else e) * vrows
        return vt[r0:r0 + vrows]

    def scores(j, slot, first=False):
        k0 = pl.multiple_of(j * tk, tk)
        k = k_ref[0, pl.ds(k0, tk), :]
        for e in range(ns):
            s = _dot_nt(k_slot(k, e), qs[e])
            if first or kind != "mla":
                bias = kbc_ref[e, pl.ds(k0, tk), :] + ref[e]
                s = s + jnp.concatenate([bias] * (tq // 128), axis=1)
            s_ref[slot, e] = s
            part = s[0:8]
            for r0 in range(8, tk, 8):
                part = jnp.maximum(part, s[r0:r0 + 8])
            mx_ref[slot, e] = part

    def softmax_pv(j, slot, diag):
        k0 = pl.multiple_of(j * tk, tk)
        vt = vt_ref[:, pl.ds(k0, tk)]
        if diag:
            kpos = k0 + lax.broadcasted_iota(jnp.int32, (tk, 1), 0)
        for e in range(ns):
            def logits(r0, n, e=e):
                s = s_ref[slot, e, r0:r0 + n, :]
                if diag:
                    kp = kpos[r0:r0 + n]
                    if kind == "diff":
                        s = s - (2.0 * slope[e // 2]) * jnp.maximum(kp - qpos, 0).astype(F32)
                    if kind == "fox":
                        mask = jnp.logical_and(kp <= qpos, kp >= front)
                    else:
                        mask = jnp.logical_and((kp >> 6) <= (qpos >> 6), kp >= front)
                    s = jnp.where(mask, s, NEG)
                return s

            _online(logits, vt_slot(vt, e), e, m_ref, acc_ref, p_ref,
                    block_max=None if diag else mx_ref[slot, e])

    scores(0, 0, first=True)

    def pair_body(jj, c):
        j = 2 * jj
        scores(j + 1, 1)
        softmax_pv(j, 0, False)
        scores(j + 2, 0)
        softmax_pv(j + 1, 1, False)
        return c

    lax.fori_loop(0, i // 2, pair_body, 0)

    @pl.when(i % 2 == 0)
    def _():
        softmax_pv(i, 0, True)

    @pl.when(i % 2 == 1)
    def _():
        scores(i, 1)
        softmax_pv(i - 1, 0, False)
        softmax_pv(i, 1, True)

    outs = []
    for p in range(ns // 2):
        e0, e1 = 2 * p, 2 * p + 1
        if kind == "diff":
            a0, a1 = acc_ref[e0], acc_ref[e1]
            dv = 2 * DH_B
            o = (a0[:dv] / a0[dv:dv + 1] - par_ref[0] * (a1[:dv] / a1[dv:dv + 1])).T
            ms = jnp.mean(o * o, axis=-1, keepdims=True)
            outs.append(o * lax.rsqrt(ms + EPS) * g_ref[:, p * 128:(p + 1) * 128])
        else:
            a0, a1 = acc_ref[e0], acc_ref[e1]
            outs.append(jnp.concatenate([a0[:64] / a0[64:], a1[:64] / a1[64:]], axis=0).T)
    o_ref[0] = (outs[0] if len(outs) == 1 else jnp.concatenate(outs, axis=1)).astype(o_ref.dtype)


def _prompt_attn(kind, q, k, vt, kb, bsz, lp, front, par=None, gain=None):
    rows = q.shape[0]
    tq = ATT_BLOCK
    ns = ATT_STREAMS
    wq = (ns // 2) * (2 * HEAD_PAD if kind == "mla" else 128)
    vrows = DIFF_VT_ROWS if kind == "diff" else 128
    wv = (ns // 2) * (vrows if kind == "diff" else 2 * vrows)
    wo = (ns // 2) * 128
    n_hg = vt.shape[0] // wv
    nq = lp // tq
    kb_b, kb_h = kb.shape[0] > 1, kb.shape[1] > 1
    seq_major = kind == "fox"

    def bh(f):
        return (lambda b, h, i: f(b, h, i)) if seq_major else (lambda h, b, i: f(b, h, i))

    in_specs = [pl.BlockSpec((1, tq, wq), bh(lambda b, h, i: (0, b * nq + i, h))),
                pl.BlockSpec((1, lp, wq), bh(lambda b, h, i: (0, b, h))),
                pl.BlockSpec((wv, lp), bh(lambda b, h, i: (h, b))),
                pl.BlockSpec((1, 1, ns, lp), bh(lambda b, h, i: (b if kb_b else 0, h if kb_h else 0, 0, 0)))]
    args = [q[None], k[None], vt, kb]
    if kind == "diff":
        in_specs = ([pl.BlockSpec(memory_space=pltpu.SMEM)] + in_specs
                    + [pl.BlockSpec((1, wo), bh(lambda b, h, i: (0, h)))])
        args = [par] + args + [gain]
    body = functools.partial(_prompt_attn_body, kind=kind, front=front, tq=tq, ns=ns)
    return pl.pallas_call(
        body,
        grid=(bsz, n_hg, nq) if seq_major else (n_hg, bsz, nq),
        in_specs=in_specs,
        out_specs=pl.BlockSpec((1, tq, wo), bh(lambda b, h, i: (0, b * nq + i, h))),
        out_shape=jax.ShapeDtypeStruct((1, rows, n_hg * wo), BF16),
        scratch_shapes=[pltpu.VMEM((ns, 1, tq), F32),
                        pltpu.VMEM((ns, vrows, tq), F32), pltpu.VMEM((2, ns, tq, tq), F32),
                        pltpu.VMEM((ns, tq, tq), BF16), pltpu.VMEM((ns, lp, 128), F32),
                        pltpu.VMEM((2, ns, 8, tq), F32)],
        compiler_params=pltpu.CompilerParams(
            dimension_semantics=("arbitrary", "arbitrary", "arbitrary"), vmem_limit_bytes=VMEM_LIMIT),
    )(*args)[0]


def _cumsum_body(x_ref, tri_ref, o_ref, carry_ref):
    @pl.when(pl.program_id(0) == 0)
    def _():
        carry_ref[...] = jnp.zeros(carry_ref.shape, F32)

    y = jnp.dot(x_ref[...], tri_ref[...], preferred_element_type=F32,
                precision=lax.Precision.HIGHEST) + carry_ref[...]
    o_ref[...] = y
    carry_ref[...] = y[:, -1:]


def _cumsum_lanes(x, blk=256):
    rows, n = x.shape
    tri = (jnp.arange(blk)[:, None] <= jnp.arange(blk)[None, :]).astype(F32)
    return pl.pallas_call(
        _cumsum_body,
        grid=(n // blk,),
        in_specs=[pl.BlockSpec((rows, blk), lambda j: (0, j)), _const_spec((blk, blk))],
        out_specs=pl.BlockSpec((rows, blk), lambda j: (0, j)),
        out_shape=jax.ShapeDtypeStruct((rows, n), F32),
        scratch_shapes=[pltpu.VMEM((rows, 1), F32)],
        compiler_params=pltpu.CompilerParams(dimension_semantics=("arbitrary",)),
    )(x, tri)


def _expand_rows(x, rep):
    h, n = x.shape
    return jnp.broadcast_to(x[:, None, :], (h, rep, n)).reshape(h * rep, n)


def _dec_online(s, v16, m_ref, l_ref, acc_ref):
    m_prev = m_ref[...]
    m_new = jnp.maximum(m_prev, jnp.max(s, axis=-1, keepdims=True))
    alpha = jnp.exp2(m_prev - m_new)
    p = jnp.exp2(s - m_new)
    l_ref[...] = alpha * l_ref[...] + jnp.sum(p, axis=-1, keepdims=True)
    acc_ref[...] = alpha * acc_ref[...] + _dot(p.astype(BF16), v16)
    m_ref[...] = m_new


def _dec_init(m_ref, l_ref, acc_ref):
    m_ref[...] = jnp.full(m_ref.shape, NEG, F32)
    l_ref[...] = jnp.zeros(l_ref.shape, F32)
    acc_ref[...] = jnp.zeros(acc_ref.shape, F32)


def _diag_blocks(o, n_heads, ds, width):
    return jnp.concatenate([o[h * ds:(h + 1) * ds, h * width:(h + 1) * width] for h in range(n_heads)], axis=1)


def _diff_dec_body(par_ref, q_ref, kc_ref, vc_ref, kn_ref, vn_ref, g_ref, _, o_ref, m_ref, l_ref, acc_ref,
                   *, past, kb, ds, n_heads):
    jb = pl.program_id(1)
    hr = 2 * ds
    rows = n_heads * hr
    dv = 2 * DH_B
    r = lax.broadcasted_iota(jnp.int32, (rows, 1), 0)
    head = r // hr
    slope = LOG2E * jnp.exp2(-8.0 * (head + 1).astype(F32) / n_heads)
    qpos = past + (r % ds)
    q = q_ref[0]

    def key_block(k16, v16, key0):
        col = lax.broadcasted_iota(jnp.int32, (1, k16.shape[0]), 1)
        kpos = key0 + col // n_heads
        s = _dot_nt(q, k16) - slope * jnp.abs(qpos - kpos).astype(F32)
        s = jnp.where(col % n_heads == head, s, NEG)
        _dec_online(s, v16, m_ref, l_ref, acc_ref)

    @pl.when(jb == 0)
    def _():
        _dec_init(m_ref, l_ref, acc_ref)
        key_block(kn_ref[0], vn_ref[0], past)

    key_block(kc_ref[0, 0].astype(BF16), vc_ref[0, 0].astype(BF16), jb * kb)

    @pl.when(jb == pl.num_programs(1) - 1)
    def _():
        o = acc_ref[...] / l_ref[...]
        outs = []
        for h in range(n_heads):
            oh = o[h * hr:h * hr + ds] - par_ref[0] * o[h * hr + ds:(h + 1) * hr]
            ms = jnp.mean(oh * oh, axis=-1, keepdims=True)
            outs.append(oh * lax.rsqrt(ms + EPS) * g_ref[:, h * dv:(h + 1) * dv])
        o_ref[0] = jnp.concatenate(outs, axis=1).astype(o_ref.dtype)


def _fox_dec_body(q_ref, kc_ref, vc_ref, kn_ref, vn_ref, fc_ref, fn_ref, _, o_ref, m_ref, l_ref, acc_ref,
                  *, ds, n_heads):
    jb = pl.program_id(1)
    rows = n_heads * ds
    q = q_ref[0]
    fnew = fn_ref[0][:, :ds]
    fref = _expand_rows(fn_ref[0][:, 0:1], ds)

    @pl.when(jb == 0)
    def _():
        _dec_init(m_ref, l_ref, acc_ref)
        r = lax.broadcasted_iota(jnp.int32, (rows, 1), 0)
        kidx = lax.broadcasted_iota(jnp.int32, (1, ds), 1)
        s = _dot_nt(q, kn_ref[0]) + LOG2E * (fref - _expand_rows(fnew, ds))
        s = jnp.where(kidx <= (r % ds), s, NEG)
        _dec_online(s, vn_ref[0].astype(BF16), m_ref, l_ref, acc_ref)

    s = _dot_nt(q, kc_ref[0, 0].astype(BF16)) + LOG2E * (fref - _expand_rows(fc_ref[0], ds))
    _dec_online(s, vc_ref[0, 0].astype(BF16), m_ref, l_ref, acc_ref)

    @pl.when(jb == pl.num_programs(1) - 1)
    def _():
        o = acc_ref[...] / l_ref[...]
        o_ref[0] = _diag_blocks(o, n_heads, ds, DH_C).astype(o_ref.dtype)


def _mla_dec_body(qn_ref, qp_ref, cc_ref, pc_ref, cn_ref, pn_ref, wk_ref, wv_ref, ones_ref, _,
                  o_ref, m_ref, l_ref, acc_ref, *, ds, n_heads):
    jb = pl.program_id(1)
    qn = qn_ref[0]
    qp = qp_ref[0]
    ones_h = ones_ref[...]

    def key_block(ckv, kpe):
        c16 = ckv.astype(BF16)
        kn = _dot(c16, wk_ref[...])
        v = _dot(c16, wv_ref[...])
        n = kpe.shape[0]
        ss = _dot_nt(ones_h, (kn * kn).astype(BF16)) + _dot_nt(jnp.ones((n_heads, ROPE_D), BF16),
                                                               (kpe * kpe).astype(BF16))
        rinv = lax.rsqrt(ss * (1.0 / (NOPE_D + ROPE_D)) + EPS)
        s = _dot_nt(qn, kn.astype(BF16)) + _dot_nt(qp, kpe.astype(BF16))
        s = s * _expand_rows(rinv, ds)
        _dec_online(s, v.astype(BF16), m_ref, l_ref, acc_ref)

    @pl.when(jb == 0)
    def _():
        _dec_init(m_ref, l_ref, acc_ref)
        key_block(cn_ref[0], pn_ref[0])

    key_block(cc_ref[0, 0], pc_ref[0, 0])

    @pl.when(jb == pl.num_programs(1) - 1)
    def _():
        o = acc_ref[...] / l_ref[...]
        o_ref[0] = _diag_blocks(o, n_heads, ds, V_D).astype(o_ref.dtype)


def _per_seq(a):
    return (a, pl.BlockSpec((1,) + a.shape[1:], lambda b, j: (b, 0, 0)))


def _dec_const(a):
    return (a, _const_spec(a.shape))


def _dec_call(body, ins, prev, row0, ds, rows, acc_w, nb, n_kb, smem=None):
    in_specs = [spec for _, spec in ins] + [pl.BlockSpec(memory_space=pl.ANY)]
    args = [a for a, _ in ins] + [prev[None]]
    if smem is not None:
        in_specs = [pl.BlockSpec(memory_space=pltpu.SMEM)] + in_specs
        args = [smem] + args
    blk0 = row0 // ds
    return pl.pallas_call(
        body,
        grid=(nb, n_kb),
        in_specs=in_specs,
        out_specs=pl.BlockSpec((1, ds, prev.shape[1]), lambda b, j: (0, blk0 + b, 0)),
        out_shape=jax.ShapeDtypeStruct((1,) + prev.shape, prev.dtype),
        input_output_aliases={len(args) - 1: 0},
        scratch_shapes=[pltpu.VMEM((rows, 1), F32), pltpu.VMEM((rows, 1), F32), pltpu.VMEM((rows, acc_w), F32)],
        compiler_params=pltpu.CompilerParams(
            dimension_semantics=("parallel", "arbitrary"), vmem_limit_bytes=VMEM_LIMIT),
    )(*args)[0]


def kernel(x_prompt, x_sample, state_s5_re, state_s5_im, cache_diff_k, cache_diff_v, cache_fox_k, cache_fox_v, cache_fox_logf, cache_mla_ckv, cache_mla_kpe, meta_tokens, ffn_norm, ffn_w_in, ffn_w_out, mix_norm, ab_w_in, ab_w_out, s5_a_re, s5_a_im, s5_log_step, s5_b_re, s5_b_im, s5_c_re, s5_c_im, s5_d, s5_glu_w, s5_glu_b, diff_q_norm, diff_k_norm, diff_lam, diff_sub_norm, cd_w_in, cd_w_out, fox_q_norm, fox_k_norm, fox_f_bias, mla_q_a_norm, mla_q_b, mla_kv_a_norm, mla_kv_b, mla_q_norm, mla_k_norm):
    bsz, seq, dm = x_prompt.shape
    nb, ds, _ = x_sample.shape
    n_meta = meta_tokens.shape[0]
    past = cache_diff_k.shape[2]
    front = ROW_ALIGN - n_meta
    lp = front + n_meta + seq
    ltot = n_meta + seq
    assert n_meta + front == ROW_ALIGN and lp % ATT_BLOCK == 0 and front % CHUNK == CHUNK - n_meta
    assert ds == S5_STEP and past % CHUNK == 0 and ds <= CHUNK
    kb = min(DEC_KB, past)
    assert past % kb == 0
    n_kb = past // kb
    assert ffn_norm.shape[0] == 2 and ab_w_in.shape[0] == 1 and cd_w_in.shape[0] == 1

    h_b = cache_diff_k.shape[3]
    h_c = cache_fox_k.shape[3]
    h_d = mla_q_b.shape[2] // (NOPE_D + ROPE_D)
    s5w = s5_glu_w.shape[1]
    n_grp = s5w // S5_GROUP
    qkw = h_b * 2 * DH_B
    fox_w = h_c * DH_C
    q_lora = mla_q_a_norm.shape[1]
    kv_lora = mla_kv_a_norm.shape[1]
    d_qk = NOPE_D + ROPE_D

    n_p = bsz * lp
    head_rows_x = jnp.concatenate([jnp.zeros((front, dm), F32), meta_tokens.astype(F32)], axis=0)
    pieces = []
    for b in range(bsz):
        pieces += [head_rows_x, x_prompt[b]]
    x = jnp.concatenate(pieces + [x_sample.reshape(nb * ds, dm)], axis=0)
    rows = x.shape[0]
    tm = _row_tile(rows)

    ones64 = _block_diag_ones(64)
    ones128 = _block_diag_ones(128)

    x = _ffn(x, ffn_norm[0, 0], ffn_w_in[0, 0], ffn_w_out[0, 0], tm)

    gq = (jnp.tile(diff_q_norm[0], 2 * h_b) * (DH_B ** -0.5 * LOG2E)).reshape(1, qkw)
    gk = jnp.tile(diff_k_norm[0], 2 * h_b).reshape(1, qkw)
    s5_cols = S5_STEP * S5_GROUP
    q16, k32, k16, v32, u_t, vt_diff = _row_call(
        functools.partial(_ab_in_body, widths=(s5w, qkw)),
        [x], [mix_norm[0].reshape(1, dm), ab_w_in[0].astype(BF16), gq, gk, ones64],
        [qkw, qkw, qkw, h_b * 2 * DH_B], [BF16, F32, BF16, F32], tm,
        group_outs=[jax.ShapeDtypeStruct((n_grp, rows // S5_STEP, s5_cols), BF16)],
        scratch=[pltpu.VMEM((s5w // 128, tm, 128), F32)],
        col_outs=[(h_b * DIFF_VT_ROWS, BF16)])

    mats = _s5_matrices(s5_a_re[0], s5_a_im[0], s5_log_step[0], s5_b_re[0], s5_b_im[0],
                        s5_c_re[0], s5_c_im[0], s5_d[0])
    n_ch = lp // S5_STEP
    y_t, st_p = _s5_scan(u_t, jnp.zeros((n_grp, bsz, 4 * P_A), F32), mats, n_ch, bsz, 0)
    h_re = jnp.transpose(state_s5_re[0].astype(F32), (1, 0, 2))
    h_im = jnp.transpose(state_s5_im[0].astype(F32), (1, 0, 2))
    y_t, st_s = _s5_scan(u_t, jnp.concatenate([h_re, h_im, h_im, h_re], axis=-1), mats, 1, nb,
                         n_p // S5_STEP, y_prev=y_t)

    lv = diff_lam[0].astype(F32)
    lam_init = 0.8 - 0.6 * math.exp(-0.3 * 0)
    lam = jnp.exp(jnp.sum(lv[0] * lv[1])) - jnp.exp(jnp.sum(lv[2] * lv[3])) + lam_init
    slopes = jnp.exp2(-8.0 * jnp.arange(1, h_b + 1, dtype=F32) / h_b)
    par = jnp.concatenate([lam[None], slopes]).astype(F32)
    subg = (jnp.tile(diff_sub_norm[0], h_b) * (1.0 - lam_init)).reshape(1, qkw)
    kpad = jnp.arange(lp) < front
    kb_diff = jnp.where(kpad[None, :], NEG, LOG2E * slopes[:, None] * jnp.arange(lp, dtype=F32)[None, :])
    kb_diff = jnp.broadcast_to(kb_diff[None, :, None, :], (1, h_b, 2, lp)).reshape(
        1, 2 * h_b // ATT_STREAMS, ATT_STREAMS, lp)
    o_all = _prompt_attn("diff", q16, k16, vt_diff, kb_diff, bsz, lp, front, par=par, gain=subg)
    qs = q16[n_p:].reshape(nb, ds, h_b, 2, DH_B)
    eye_2 = jnp.eye(2, dtype=BF16)
    qbd = jnp.einsum('bqhmd,mM->bhmqMd', qs, eye_2).reshape(nb, h_b * 2 * ds, 2 * DH_B)
    cache_spec = lambda w: pl.BlockSpec((1, 1, kb, w), lambda b, j: (0, b, j, 0))
    head_rows = lambda a: a.reshape(a.shape[0], nb, past * h_b, 2 * DH_B)
    head_cache_spec = pl.BlockSpec((1, 1, kb * h_b, 2 * DH_B), lambda b, j: (0, b, j, 0))
    o_all = _dec_call(
        functools.partial(_diff_dec_body, past=past, kb=kb, ds=ds, n_heads=h_b),
        [_per_seq(qbd), (head_rows(cache_diff_k), head_cache_spec), (head_rows(cache_diff_v), head_cache_spec),
         _per_seq(k16[n_p:].reshape(nb, ds * h_b, 2 * DH_B)), _per_seq(v32[n_p:].astype(BF16).reshape(nb, ds * h_b, 2 * DH_B)),
         _dec_const(subg)],
        o_all, n_p, ds, 2 * h_b * ds, 2 * DH_B, nb, n_kb, smem=par)

    x = _row_call(
        functools.partial(_ab_out_body, s5w=s5w),
        [x, o_all],
        [s5_glu_w[0].astype(BF16), s5_glu_b[0].reshape(1, s5w), ab_w_out[0].astype(BF16)],
        [dm], [F32], tm, group_ins=[y_t], scratch=[pltpu.VMEM((s5w // 128, tm, 128), F32)])[0]

    x = _ffn(x, ffn_norm[0, 1], ffn_w_in[0, 1], ffn_w_out[0, 1], tm)

    x = _ffn(x, ffn_norm[1, 0], ffn_w_in[1, 0], ffn_w_out[1, 0], tm)

    half = ROPE_D // 2
    inv = ROPE_THETA ** (-jnp.arange(half, dtype=F32) / half)
    pos = jnp.concatenate([jnp.tile(jnp.arange(lp, dtype=jnp.int32) - front, bsz),
                           jnp.tile(past + jnp.arange(ds, dtype=jnp.int32), nb)]).astype(F32)
    ang = pos[:, None] * inv[None, :]
    pad_r = HEAD_PAD - NOPE_D - ROPE_D
    cos_t = jnp.concatenate([jnp.ones((rows, NOPE_D), F32), jnp.cos(ang), jnp.cos(ang),
                             jnp.zeros((rows, pad_r), F32)], axis=1)
    sin_t = jnp.concatenate([jnp.zeros((rows, NOPE_D), F32), jnp.sin(ang), jnp.sin(ang),
                             jnp.zeros((rows, pad_r), F32)], axis=1)

    wcd = cd_w_in[0]
    c_fg = 3 * fox_w
    c_qa = c_fg + h_c
    c_kva = c_qa + q_lora
    c_pe = c_kva + kv_lora
    w_pe = wcd[:, c_pe:c_pe + ROPE_D]
    zc = lambda n: jnp.zeros((dm, n), F32)
    w_cd = jnp.concatenate([
        wcd[:, :3 * fox_w], wcd[:, c_qa:c_qa + q_lora], wcd[:, c_kva:c_kva + kv_lora],
        zc(NOPE_D), w_pe, zc(pad_r),
        zc(NOPE_D), -w_pe[:, half:], w_pe[:, :half], zc(pad_r),
        wcd[:, c_fg:c_fg + h_c], zc(HEAD_PAD - h_c)], axis=1).astype(BF16)
    qb = mla_q_b[0].reshape(q_lora, h_d, d_qk)
    zq = lambda n: jnp.zeros((q_lora, h_d, n), F32)
    qb_pad = jnp.concatenate([qb, zq(pad_r)], axis=-1).reshape(q_lora, h_d * HEAD_PAD)
    qb_rot = jnp.concatenate([zq(NOPE_D), -qb[..., NOPE_D + half:], qb[..., NOPE_D:NOPE_D + half], zq(pad_r)],
                             axis=-1).reshape(q_lora, h_d * HEAD_PAD)
    wq2 = jnp.concatenate([qb_pad, qb_rot], axis=1).astype(BF16)
    kvb = mla_kv_b[0].reshape(kv_lora, h_d, NOPE_D + V_D)
    wk_pad = jnp.concatenate([kvb[..., :NOPE_D], jnp.zeros((kv_lora, h_d, HEAD_PAD - NOPE_D), F32)],
                             axis=-1).reshape(kv_lora, h_d * HEAD_PAD).astype(BF16)
    wk_cmp = kvb[..., :NOPE_D].reshape(kv_lora, h_d * NOPE_D).astype(BF16)
    wv_cmp = kvb[..., NOPE_D:].reshape(kv_lora, h_d * V_D).astype(BF16)
    gfq = (jnp.tile(fox_q_norm[0], h_c) * (DH_C ** -0.5 * LOG2E)).reshape(1, fox_w)
    gfk = jnp.tile(fox_k_norm[0], h_c).reshape(1, fox_w)
    fbias = jnp.concatenate([fox_f_bias[0], jnp.zeros((HEAD_PAD - h_c,), F32)]).reshape(1, HEAD_PAD)
    gmq = jnp.tile(jnp.concatenate([mla_q_norm[0] * mla_k_norm[0] * (d_qk ** -0.5 * LOG2E), jnp.zeros((pad_r,), F32)]),
                   h_d).reshape(1, h_d * HEAD_PAD)

    (fq16, fk32, fk16, fv32, logf, qm16, ckv32, kpe32, km16, fvt, vmt) = _row_call(
        functools.partial(_cd_in_body, fox_w=fox_w, q_lora=q_lora, kv_lora=kv_lora, n_heads=h_d),
        [x, cos_t, sin_t],
        [mix_norm[1].reshape(1, dm), w_cd, gfq, gfk, fbias, mla_q_a_norm[0].reshape(1, q_lora), wq2,
         mla_kv_a_norm[0].reshape(1, kv_lora), wk_pad, wv_cmp.T, gmq, ones64, ones128],
        [fox_w, fox_w, fox_w, fox_w, h_c, h_d * HEAD_PAD, kv_lora, ROPE_D, h_d * HEAD_PAD],
        [BF16, F32, BF16, F32, F32, BF16, F32, F32, BF16], tm,
        col_outs=[(2 * fox_w, BF16), (2 * h_d * V_D, BF16)])

    logf_p = jnp.transpose(logf[:n_p].reshape(bsz, lp, h_c), (0, 2, 1)).reshape(bsz * h_c, lp)
    f_p = _cumsum_lanes(logf_p).reshape(bsz, h_c, lp)
    logf_s = jnp.concatenate([
        jnp.transpose(cache_fox_logf[0].astype(F32), (0, 2, 1)),
        jnp.transpose(logf[n_p:].reshape(nb, ds, h_c), (0, 2, 1)),
        jnp.zeros((nb, h_c, 256 - ds), F32)], axis=2).reshape(nb * h_c, past + 256)
    f_s = _cumsum_lanes(logf_s).reshape(nb, h_c, past + 256)

    kb_fox = jnp.where(kpad[None, None, :], NEG, -LOG2E * f_p).reshape(bsz, h_c // ATT_STREAMS, ATT_STREAMS, lp)
    oc_all = _prompt_attn("fox", fq16, fk16, fvt, kb_fox, bsz, lp, front)
    kb_mla = jnp.broadcast_to(jnp.where(kpad, NEG, 0.0).astype(F32)[None, None, None, :], (1, 1, ATT_STREAMS, lp))
    od_all = _prompt_attn("mla", qm16, km16, vmt, kb_mla, bsz, lp, front)

    eye_c = jnp.eye(h_c, dtype=BF16)
    fqs = fq16[n_p:].reshape(nb, ds, h_c, DH_C)
    fq_bd = jnp.einsum('bqhd,hH->bhqHd', fqs, eye_c).reshape(nb, h_c * ds, fox_w)
    fkc = cache_fox_k.reshape(cache_fox_k.shape[0], nb, past, fox_w).astype(BF16)
    fvc = cache_fox_v.reshape(cache_fox_v.shape[0], nb, past, fox_w).astype(BF16)
    oc_all = _dec_call(
        functools.partial(_fox_dec_body, ds=ds, n_heads=h_c),
        [_per_seq(fq_bd), (fkc, cache_spec(fox_w)), (fvc, cache_spec(fox_w)),
         _per_seq(fk16[n_p:].reshape(nb, ds, fox_w)), _per_seq(fv32[n_p:].reshape(nb, ds, fox_w)),
         (f_s, pl.BlockSpec((1, h_c, kb), lambda b, j: (b, 0, j))),
         (f_s, pl.BlockSpec((1, h_c, 128), lambda b, j: (b, 0, past // 128)))],
        oc_all, n_p, ds, h_c * ds, fox_w, nb, n_kb)

    eye_d = jnp.eye(h_d, dtype=BF16)
    qms = qm16[n_p:].reshape(nb, ds, h_d, HEAD_PAD)
    qn_bd = jnp.einsum('bqhd,hH->bhqHd', qms[..., :NOPE_D], eye_d).reshape(nb, h_d * ds, h_d * NOPE_D)
    qp_s = jnp.transpose(qms[..., NOPE_D:NOPE_D + ROPE_D], (0, 2, 1, 3)).reshape(nb, h_d * ds, ROPE_D)
    ones_h = jnp.repeat(jnp.eye(h_d, dtype=BF16), NOPE_D, axis=1)
    od_all = _dec_call(
        functools.partial(_mla_dec_body, ds=ds, n_heads=h_d),
        [_per_seq(qn_bd), _per_seq(qp_s),
         (cache_mla_ckv, pl.BlockSpec((1, 1, kb, kv_lora), lambda b, j: (0, b, j, 0))),
         (cache_mla_kpe, pl.BlockSpec((1, 1, kb, ROPE_D), lambda b, j: (0, b, j, 0))),
         _per_seq(ckv32[n_p:].reshape(nb, ds, kv_lora)), _per_seq(kpe32[n_p:].reshape(nb, ds, ROPE_D)),
         _dec_const(wk_cmp), _dec_const(wv_cmp), _dec_const(ones_h)],
        od_all, n_p, ds, h_d * ds, h_d * V_D, nb, n_kb)

    x = _row_call(functools.partial(_cd_out_body, fox_w=fox_w), [x, oc_all, od_all],
                  [cd_w_out[0].astype(BF16)], [dm], [F32], tm)[0]

    x = _ffn(x, ffn_norm[1, 1], ffn_w_in[1, 1], ffn_w_out[1, 1], tm)

    def p_rows(a, shape):
        w = a.shape[1]
        return a[:n_p].reshape(bsz, lp, w)[:, front:front + ltot].reshape((1, bsz, ltot) + shape)

    def s_rows(a, shape):
        return a[n_p:].reshape((1, nb, ds) + shape)

    def s5_state(st):
        st = jnp.transpose(st, (1, 0, 2))
        return st[None, :, :, :P_A], st[None, :, :, P_A:]

    y_prompt = x[:n_p].reshape(bsz, lp, dm)[:, front + n_meta:]
    y_sample = x[n_p:].reshape(nb, ds, dm)
    s5_re_p, s5_im_p = s5_state(st_p)
    s5_re_s, s5_im_s = s5_state(st_s)
    return (y_prompt, y_sample,
            s5_re_p, s5_im_p, p_rows(k32, (h_b, 2 * DH_B)), p_rows(v32, (h_b, 2 * DH_B)),
            p_rows(fk32, (h_c, DH_C)), p_rows(fv32, (h_c, DH_C)), p_rows(logf, (h_c,)),
            p_rows(ckv32, (kv_lora,)), p_rows(kpe32, (ROPE_D,)),
            s5_re_s, s5_im_s, s_rows(k32, (h_b, 2 * DH_B)), s_rows(v32, (h_b, 2 * DH_B)),
            s_rows(fk32, (h_c, DH_C)), s_rows(fv32, (h_c, DH_C)), s_rows(logf, (h_c,)),
            s_rows(ckv32, (kv_lora,)), s_rows(kpe32, (ROPE_D,)))
```

```python
import functools
import math

import jax
import jax.numpy as jnp
from jax import lax
from jax.experimental import pallas as pl
from jax.experimental.pallas import tpu as pltpu

F32 = jnp.float32
BF16 = jnp.bfloat16

EPS = 1e-6
CHUNK = 64
ROW_ALIGN = 256
S5_GROUP = 16
S5_STEP = 16
P_A = 64
DH_B = 64
DH_C = 64
NOPE_D = 64
ROPE_D = 32
V_D = 64
HEAD_PAD = 128
ROPE_THETA = 10000.0
NEG = -1e30
LOG2E = math.log2(math.e)
VMEM_LIMIT = 56 * 1024 * 1024
ATT_BLOCK = 256
ATT_STREAMS = 4
ATT_ROWS = 64
DIFF_VT_ROWS = 2 * DH_B + 16
DEC_KB = 1024


def _dot(a, b):
    return jnp.dot(a, b, preferred_element_type=F32)


def _dot_nt(a, b):
    return lax.dot_general(a, b, (((1,), (1,)), ((), ())), preferred_element_type=F32)


def _rms_rows(x, g):
    ms = jnp.mean(x * x, axis=-1, keepdims=True)
    return x * lax.rsqrt(ms + EPS) * g


def _group_sumsq(x, ones_bd):
    w = x.shape[-1]
    parts = [_dot((x[:, c:c + 256] * x[:, c:c + 256]).astype(BF16), ones_bd) for c in range(0, w, 256)]
    return parts[0] if len(parts) == 1 else jnp.concatenate(parts, axis=1)


def _block_diag_ones(group, n=256):
    r = jnp.arange(n) // group
    return (r[:, None] == r[None, :]).astype(BF16)


def _const_spec(shape):
    nd = len(shape)
    return pl.BlockSpec(shape, lambda *_: (0,) * nd, pipeline_mode=pl.Buffered(1))


def _row_tile(rows, cap=512):
    t = cap
    while rows % t:
        t //= 2
    return t


def _group_spec(a, tm):
    return pl.BlockSpec((a.shape[0], tm // S5_STEP, a.shape[2]), lambda i: (0, i, 0))


def _row_call(body, row_ins, consts, out_widths, out_dtypes, tm, group_ins=(), group_outs=(), scratch=(),
              col_outs=(), head_outs=()):
    rows = row_ins[0].shape[0]
    in_specs = [pl.BlockSpec((tm, a.shape[1]), lambda i: (i, 0)) for a in row_ins]
    in_specs += [_group_spec(a, tm) for a in group_ins]
    in_specs += [_const_spec(c.shape) for c in consts]
    out_specs = [pl.BlockSpec((tm, w), lambda i: (i, 0)) for w in out_widths]
    out_specs += [_group_spec(a, tm) for a in group_outs]
    out_specs += [pl.BlockSpec((w, tm), lambda i: (0, i)) for w, _ in col_outs]
    out_shape = [jax.ShapeDtypeStruct((rows, w), d) for w, d in zip(out_widths, out_dtypes)]
    out_shape += list(group_outs)
    out_shape += [jax.ShapeDtypeStruct((w, rows), d) for w, d in col_outs]
    out_specs += [pl.BlockSpec((tm * h, 128), lambda i: (i, 0)) for h, _ in head_outs]
    out_shape += [jax.ShapeDtypeStruct((rows * h, 128), d) for h, d in head_outs]
    return pl.pallas_call(
        body,
        grid=(rows // tm,),
        in_specs=in_specs,
        out_specs=out_specs,
        out_shape=out_shape,
        scratch_shapes=list(scratch),
        compiler_params=pltpu.CompilerParams(
            dimension_semantics=("parallel",), vmem_limit_bytes=VMEM_LIMIT),
    )(*row_ins, *group_ins, *consts)


def _ffn_body(x_ref, g_ref, win_ref, wout_ref, o_ref, *, d_ff, tf):
    x = x_ref[...]
    xn = _rms_rows(x, g_ref[...]).astype(BF16)
    acc = jnp.zeros(x.shape, F32)
    for c in range(0, d_ff, tf):
        gate = _dot(xn, win_ref[:, c:c + tf])
        up = _dot(xn, win_ref[:, d_ff + c:d_ff + c + tf])
        a = (gate * jax.nn.sigmoid(gate) * up).astype(BF16)
        acc = acc + _dot(a, wout_ref[c:c + tf, :])
    o_ref[...] = x + 0.5 * acc


def _ffn(x, g, w_in, w_out, tm):
    d_ff = w_out.shape[0]
    body = functools.partial(_ffn_body, d_ff=d_ff, tf=256)
    return _row_call(body, [x], [g.reshape(1, -1), w_in.astype(BF16), w_out.astype(BF16)],
                     [x.shape[1]], [F32], tm)[0]


def _ab_in_body(x_ref, g_ref, w_ref, gq_ref, gk_ref, ones_ref,
                q_ref, k16_ref, ut_ref, vt_ref, k32_ref, v32_ref, us_ref, *, widths):
    s5w, qkw = widths
    xn = _rms_rows(x_ref[...], g_ref[...]).astype(BF16)
    h = _dot(xn, w_ref[...])
    n_chunk = us_ref.shape[1] // S5_STEP
    per_col = 128 // S5_GROUP
    for v in range(s5w // 128):
        us_ref[v] = h[:, v * 128:(v + 1) * 128]
        steps = [us_ref[v, pl.ds(t, n_chunk, stride=S5_STEP), :] for t in range(S5_STEP)]
        for gl in range(per_col):
            ut_ref[v * per_col + gl] = jnp.concatenate(
                [x[:, gl * S5_GROUP:(gl + 1) * S5_GROUP] for x in steps], axis=1).astype(BF16)
    q = h[:, s5w:s5w + qkw]
    k = h[:, s5w + qkw:s5w + 2 * qkw]
    v = h[:, s5w + 2 * qkw:]
    ones_bd = ones_ref[...]
    qn = q * lax.rsqrt(_group_sumsq(q, ones_bd) * (1.0 / DH_B) + EPS) * gq_ref[...]
    kn = k * lax.rsqrt(_group_sumsq(k, ones_bd) * (1.0 / DH_B) + EPS) * gk_ref[...]
    q_ref[...] = qn.astype(BF16)
    k16_ref[...] = kn.astype(BF16)
    dv = 2 * DH_B
    n_head = qkw // dv
    tm = kn.shape[0]
    for hd in range(n_head):
        k32_ref[pl.ds(hd, tm, stride=n_head), :] = kn[:, hd * dv:(hd + 1) * dv]
        v32_ref[pl.ds(hd, tm, stride=n_head), :] = v[:, hd * dv:(hd + 1) * dv]
    vt = v.T
    ones = jnp.ones((DIFF_VT_ROWS - dv, vt.shape[1]), F32)
    vt_ref[...] = jnp.concatenate(
        [a for h in range(vt.shape[0] // dv) for a in (vt[h * dv:(h + 1) * dv], ones)], axis=0).astype(BF16)


def _ab_out_body(x_ref, o_ref, yt_ref, gluw_ref, glub_ref, wout_ref, out_ref, ys_ref, *, s5w):
    n_chunk = ys_ref.shape[1] // S5_STEP
    per_col = 128 // S5_GROUP
    for v in range(s5w // 128):
        for t in range(S5_STEP):
            ys_ref[v, pl.ds(t, n_chunk, stride=S5_STEP), :] = jnp.concatenate(
                [yt_ref[v * per_col + gl][:, t * S5_GROUP:(t + 1) * S5_GROUP] for gl in range(per_col)], axis=1)
    y = jnp.concatenate([ys_ref[v] for v in range(s5w // 128)], axis=1)
    g = 0.5 * y * (1.0 + jnp.tanh(math.sqrt(2.0 / math.pi) * (y + 0.044715 * (y * y * y))))
    z = _dot(g.astype(BF16), gluw_ref[...]) + glub_ref[...]
    s5o = g * jax.nn.sigmoid(z)
    m = _dot(s5o.astype(BF16), wout_ref[:s5w, :]) + _dot(o_ref[...], wout_ref[s5w:, :])
    out_ref[...] = x_ref[...] + m


def _heads_with_ones_t(vt):
    ones = jnp.ones((64, vt.shape[1]), vt.dtype)
    outs = []
    for h in range(vt.shape[0] // 64):
        outs += [vt[h * 64:(h + 1) * 64], ones]
    return jnp.concatenate(outs, axis=0)


def _cd_in_body(x_ref, cos_ref, sin_ref, g_ref, w_ref, gfq_ref, gfk_ref, fb_ref, gqa_ref, wq2_ref,
                gkva_ref, wk_ref, wv_ref, gmq_ref, ones64_ref, ones128_ref,
                fq_ref, fk32_ref, fk16_ref, fv32_ref, logf_ref, qm_ref, ckv_ref, kpe_ref,
                km_ref, fvt_ref, vmt_ref, *, fox_w, q_lora, kv_lora, n_heads):
    xn = _rms_rows(x_ref[...], g_ref[...]).astype(BF16)
    h = _dot(xn, w_ref[...])
    ones64 = ones64_ref[...]
    ones128 = ones128_ref[...]
    fq = h[:, :fox_w]
    fk = h[:, fox_w:2 * fox_w]
    fv = h[:, 2 * fox_w:3 * fox_w]
    c0 = 3 * fox_w
    qa = h[:, c0:c0 + q_lora]
    kva = h[:, c0 + q_lora:c0 + q_lora + kv_lora]
    c1 = c0 + q_lora + kv_lora
    pe_a = h[:, c1:c1 + HEAD_PAD]
    pe_b = h[:, c1 + HEAD_PAD:c1 + 2 * HEAD_PAD]
    fg = h[:, c1 + 2 * HEAD_PAD:c1 + 3 * HEAD_PAD]

    fqn = fq * lax.rsqrt(_group_sumsq(fq, ones64) * (1.0 / DH_C) + EPS) * gfq_ref[...]
    fkn = fk * lax.rsqrt(_group_sumsq(fk, ones64) * (1.0 / DH_C) + EPS) * gfk_ref[...]
    fq_ref[...] = fqn.astype(BF16)
    fk32_ref[...] = fkn
    fk16_ref[...] = fkn.astype(BF16)
    fv32_ref[...] = fv
    fvt_ref[...] = _heads_with_ones_t(fv.T).astype(BF16)

    z = fg + fb_ref[...]
    logf = jnp.minimum(z, 0.0) - jnp.log1p(jnp.exp(-jnp.abs(z)))
    logf_ref[...] = logf[:, :logf_ref.shape[1]]

    cos = cos_ref[...]
    sin = sin_ref[...]
    qan = _rms_rows(qa, gqa_ref[...]).astype(BF16)
    q2 = _dot(qan, wq2_ref[...])
    hw = n_heads * HEAD_PAD
    cos_t = jnp.concatenate([cos] * n_heads, axis=1)
    sin_t = jnp.concatenate([sin] * n_heads, axis=1)
    qr = q2[:, :hw] * cos_t + q2[:, hw:] * sin_t
    d_qk = NOPE_D + ROPE_D
    qm = qr * lax.rsqrt(_group_sumsq(qr, ones128) * (1.0 / d_qk) + EPS) * gmq_ref[...]
    qm_ref[...] = qm.astype(BF16)

    ckv = _rms_rows(kva, gkva_ref[...])
    ckv_ref[...] = ckv
    pe = pe_a * cos + pe_b * sin
    kpe_ref[...] = pe[:, NOPE_D:NOPE_D + ROPE_D]
    ckv16 = ckv.astype(BF16)
    kraw = _dot(ckv16, wk_ref[...]) + jnp.concatenate([pe] * n_heads, axis=1)
    km = kraw * lax.rsqrt(_group_sumsq(kraw, ones128) * (1.0 / d_qk) + EPS)
    km_ref[...] = km.astype(BF16)
    vmt_ref[...] = _heads_with_ones_t(_dot_nt(wv_ref[...], ckv16)).astype(BF16)


def _cd_out_body(x_ref, oc_ref, od_ref, wout_ref, out_ref, *, fox_w):
    m = _dot(oc_ref[...], wout_ref[:fox_w, :]) + _dot(od_ref[...], wout_ref[fox_w:, :])
    out_ref[...] = x_ref[...] + m


def _s5_body(*refs, n_chunks, bsz, aliased):
    if aliased:
        u_ref, h0_ref, m_ref, bm_ref, cm_ref, coef_ref, _, y_ref, st_ref, s2_ref, hp_ref = refs
    else:
        u_ref, h0_ref, m_ref, bm_ref, cm_ref, coef_ref, y_ref, st_ref, s2_ref, hp_ref = refs
    u = u_ref[0]
    half = 2 * P_A
    s2 = _dot(u, bm_ref[0])
    s2_ref[0] = s2[:, :half]
    s2_ref[1] = s2[:, half:]
    c1 = coef_ref[0, 0:1, :]
    c2 = coef_ref[0, 1:2, :]
    c3 = coef_ref[0, 2:3, :]

    def step(j, carry):
        ha, hb = carry
        hp_ref[pl.ds(j, bsz, stride=n_chunks), :] = ha
        sa = s2_ref[0, pl.ds(j, bsz, stride=n_chunks), :]
        sb = s2_ref[1, pl.ds(j, bsz, stride=n_chunks), :]
        return ha * c1 + hb * c2 + sa, hb * c1 + ha * c3 + sb

    h0 = h0_ref[0]
    ha, _ = lax.fori_loop(0, n_chunks, step, (h0[:, :half], h0[:, half:]))
    st_ref[0] = ha
    y_ref[0] = _dot(u, m_ref[0]) + _dot(hp_ref[...].astype(BF16), cm_ref[0])


def _s5_scan(u_t, h0, mats, n_chunks, bsz, row0, y_prev=None):
    m_mat, bm, cm, coef = mats
    g, rows_all, w = u_t.shape
    rows = n_chunks * bsz
    assert row0 % rows == 0
    blk = row0 // rows
    aliased = y_prev is not None
    body = functools.partial(_s5_body, n_chunks=n_chunks, bsz=bsz, aliased=aliased)
    per_g = lambda a: pl.BlockSpec((1,) + a.shape[1:], lambda i: (i, 0, 0))
    in_specs = [pl.BlockSpec((1, rows, w), lambda i: (i, blk, 0)),
                per_g(h0), per_g(m_mat), per_g(bm), per_g(cm), per_g(coef)]
    args = [u_t, h0, m_mat, bm, cm, coef]
    if aliased:
        in_specs.append(pl.BlockSpec(memory_space=pl.ANY))
        args.append(y_prev)
    return pl.pallas_call(
        body,
        grid=(g,),
        in_specs=in_specs,
        out_specs=[pl.BlockSpec((1, rows, w), lambda i: (i, blk, 0)),
                   pl.BlockSpec((1, bsz, 2 * P_A), lambda i: (i, 0, 0))],
        out_shape=[jax.ShapeDtypeStruct((g, rows_all, w), F32),
                   jax.ShapeDtypeStruct((g, bsz, 2 * P_A), F32)],
        input_output_aliases={6: 0} if aliased else {},
        scratch_shapes=[pltpu.VMEM((2, rows, 2 * P_A), F32), pltpu.VMEM((rows, 2 * P_A), F32)],
        compiler_params=pltpu.CompilerParams(
            dimension_semantics=("parallel",), vmem_limit_bytes=VMEM_LIMIT),
    )(*args)


def _s5_matrices(a_re, a_im, log_step, b_re, b_im, c_re, c_im, d):
    g = a_re.shape[0]
    t = S5_STEP
    lam = lax.complex(a_re, a_im)
    dl = lam * jnp.exp(log_step)[:, None]
    lam_bar = jnp.exp(dl)
    b_bar = ((lam_bar - 1.0) / lam)[..., None] * lax.complex(b_re, b_im)
    c = lax.complex(c_re, c_im)
    pw = jnp.exp(dl[:, None, :] * jnp.arange(t + 1, dtype=F32)[None, :, None])
    bmc = pw[:, t - 1::-1][:, :, :, None] * b_bar[:, None]
    bmc = jnp.swapaxes(bmc, 2, 3).reshape(g, t * S5_GROUP, P_A)
    bm = jnp.concatenate([bmc.real, bmc.imag, bmc.imag, bmc.real], axis=-1)
    kk = jnp.einsum('gcp,gkp,gpd->gkcd', c, pw[:, :t], b_bar).real
    kk = kk.at[:, 0].add(d.reshape(g, S5_GROUP)[:, :, None] * jnp.eye(S5_GROUP, dtype=F32))
    lag = jnp.arange(t)[None, :] - jnp.arange(t)[:, None]
    toep = jnp.where((lag >= 0)[None, :, :, None, None], kk[:, jnp.clip(lag, 0, t - 1)], 0.0)
    m_mat = jnp.transpose(toep, (0, 1, 4, 2, 3)).reshape(g, t * S5_GROUP, t * S5_GROUP)
    cp = c[:, None] * pw[:, 1:, None, :]
    cpm = jnp.transpose(cp, (0, 3, 1, 2)).reshape(g, P_A, t * S5_GROUP)
    cm = jnp.concatenate([cpm.real, -cpm.imag], axis=1)
    a_t = pw[:, t]
    ar, ai = a_t.real, a_t.imag
    zeros = jnp.zeros_like(ar)
    coef = jnp.stack([jnp.concatenate([ar, ar], -1), jnp.concatenate([-ai, ai], -1),
                      jnp.concatenate([ai, -ai], -1), jnp.concatenate([zeros, zeros], -1)], axis=1)
    return m_mat.astype(BF16), bm.astype(BF16), cm.astype(BF16), coef.astype(F32)


def _online(logits, vt, e, m_ref, acc_ref, p_ref, block_max=None):
    tk = p_ref.shape[1]
    if block_max is None:
        part = logits(0, ATT_ROWS)
        for r0 in range(ATT_ROWS, tk, ATT_ROWS):
            part = jnp.maximum(part, logits(r0, ATT_ROWS))
    else:
        part = block_max
    m_prev = m_ref[e]
    m_new = jnp.maximum(m_prev, jnp.max(part, axis=0, keepdims=True))
    alpha = jnp.exp2(m_prev - m_new)
    m_ref[e] = m_new
    for r0 in range(0, tk, ATT_ROWS):
        p_ref[e, r0:r0 + ATT_ROWS] = jnp.exp2(logits(r0, ATT_ROWS) - m_new).astype(BF16)
    acc_ref[e] = alpha * acc_ref[e] + _dot(vt, p_ref[e])


def _prompt_attn_body(*refs, kind, front, tq, ns):
    if kind == "diff":
        (par_ref, q_ref, k_ref, vt_ref, kb_ref, g_ref, o_ref,
         m_ref, acc_ref, s_ref, p_ref, kbc_ref, mx_ref) = refs
    else:
        q_ref, k_ref, vt_ref, kb_ref, o_ref, m_ref, acc_ref, s_ref, p_ref, kbc_ref, mx_ref = refs
    vrows = acc_ref.shape[1]
    i = pl.program_id(2)
    if kind == "fox":
        hg = pl.program_id(1)
        new_bias = i == 0
    else:
        hg = pl.program_id(0)
        new_bias = jnp.logical_and(pl.program_id(1) == 0, i == 0)
    m_ref[...] = jnp.full(m_ref.shape, NEG, F32)
    acc_ref[...] = jnp.zeros(acc_ref.shape, F32)

    @pl.when(new_bias)
    def _():
        def fill(c, carry):
            c0 = pl.multiple_of(c * 128, 128)
            for e in range(ns):
                row = kb_ref[0, 0, e:e + 1, pl.ds(c0, 128)]
                kbc_ref[e, pl.ds(c0, 128), :] = jnp.broadcast_to(row, (128, 128)).T
            return carry

        lax.fori_loop(0, kbc_ref.shape[1] // 128, fill, 0)

    q = q_ref[0]
    lane = lax.broadcasted_iota(jnp.int32, (1, 128), 1)
    qs = []
    for e in range(ns):
        if kind == "mla":
            qs.append(q[:, e * HEAD_PAD:(e + 1) * HEAD_PAD])
        else:
            qp = q[:, (e // 2) * 128:(e // 2 + 1) * 128]
            qs.append(jnp.where((lane < 64) if e % 2 == 0 else (lane >= 64), qp, jnp.zeros_like(qp)))
    qstart = pl.multiple_of(i * tq, tq)
    qpos = qstart + lax.broadcasted_iota(jnp.int32, (1, tq), 1)
    ref = [-kb_ref[0, 0, e:e + 1, pl.ds(qstart + (tq - 128), 128)][:, 127:128] for e in range(ns)]
    if kind == "diff":
        slope = [LOG2E * par_ref[1 + hg * (ns // 2) + p] for p in range(ns // 2)]

    tk = tq

    def k_slot(k, e):
        if kind == "mla":
            return k[:, e * HEAD_PAD:(e + 1) * HEAD_PAD]
        return k[:, (e // 2) * 128:(e // 2 + 1) * 128]

    def vt_slot(vt, e):
        r0 = (e // 2 if kind == "diff" else e) * vrows
        return vt[r0:r0 + vrows]

    def scores(j, slot, first=False):
        k0 = pl.multiple_of(j * tk, tk)
        k = k_ref[0, pl.ds(k0, tk), :]
        for e in range(ns):
            s = _dot_nt(k_slot(k, e), qs[e])
            if first or kind != "mla":
                bias = kbc_ref[e, pl.ds(k0, tk), :] + ref[e]
                s = s + jnp.concatenate([bias] * (tq // 128), axis=1)
            s_ref[slot, e] = s
            part = s[0:8]
            for r0 in range(8, tk, 8):
                part = jnp.maximum(part, s[r0:r0 + 8])
            mx_ref[slot, e] = part

    def softmax_pv(j, slot, diag):
        k0 = pl.multiple_of(j * tk, tk)
        vt = vt_ref[:, pl.ds(k0, tk)]
        if diag:
            kpos = k0 + lax.broadcasted_iota(jnp.int32, (tk, 1), 0)
        for e in range(ns):
            def logits(r0, n, e=e):
                s = s_ref[slot, e, r0:r0 + n, :]
                if diag:
                    kp = kpos[r0:r0 + n]
                    if kind == "diff":
                        s = s - (2.0 * slope[e // 2]) * jnp.maximum(kp - qpos, 0).astype(F32)
                    if kind == "fox":
                        mask = jnp.logical_and(kp <= qpos, kp >= front)
                    else:
                        mask = jnp.logical_and((kp >> 6) <= (qpos >> 6), kp >= front)
                    s = jnp.where(mask, s, NEG)
                return s

            _online(logits, vt_slot(vt, e), e, m_ref, acc_ref, p_ref,
                    block_max=None if diag else mx_ref[slot, e])

    scores(0, 0, first=True)

    def pair_body(jj, c):
        j = 2 * jj
        scores(j + 1, 1)
        softmax_pv(j, 0, False)
        scores(j + 2, 0)
        softmax_pv(j + 1, 1, False)
        return c

    lax.fori_loop(0, i // 2, pair_body, 0)

    @pl.when(i % 2 == 0)
    def _():
        softmax_pv(i, 0, True)

    @pl.when(i % 2 == 1)
    def _():
        scores(i, 1)
        softmax_pv(i - 1, 0, False)
        softmax_pv(i, 1, True)

    outs = []
    for p in range(ns // 2):
        e0, e1 = 2 * p, 2 * p + 1
        if kind == "diff":
            a0, a1 = acc_ref[e0], acc_ref[e1]
            dv = 2 * DH_B
            o = (a0[:dv] / a0[dv:dv + 1] - par_ref[0] * (a1[:dv] / a1[dv:dv + 1])).T
            ms = jnp.mean(o * o, axis=-1, keepdims=True)
            outs.append(o * lax.rsqrt(ms + EPS) * g_ref[:, p * 128:(p + 1) * 128])
        else:
            a0, a1 = acc_ref[e0], acc_ref[e1]
            outs.append(jnp.concatenate([a0[:64] / a0[64:], a1[:64] / a1[64:]], axis=0).T)
    o_ref[0] = (outs[0] if len(outs) == 1 else jnp.concatenate(outs, axis=1)).astype(o_ref.dtype)


def _prompt_attn(kind, q, k, vt, kb, bsz, lp, front, par=None, gain=None):
    rows = q.shape[0]
    tq = ATT_BLOCK
    ns = ATT_STREAMS
    wq = (ns // 2) * (2 * HEAD_PAD if kind == "mla" else 128)
    vrows = DIFF_VT_ROWS if kind == "diff" else 128
    wv = (ns // 2) * (vrows if kind == "diff" else 2 * vrows)
    wo = (ns // 2) * 128
    n_hg = vt.shape[0] // wv
    nq = lp // tq
    kb_b, kb_h = kb.shape[0] > 1, kb.shape[1] > 1
    seq_major = kind == "fox"

    def bh(f):
        return (lambda b, h, i: f(b, h, i)) if seq_major else (lambda h, b, i: f(b, h, i))

    in_specs = [pl.BlockSpec((1, tq, wq), bh(lambda b, h, i: (0, b * nq + i, h))),
                pl.BlockSpec((1, lp, wq), bh(lambda b, h, i: (0, b, h))),
                pl.BlockSpec((wv, lp), bh(lambda b, h, i: (h, b))),
                pl.BlockSpec((1, 1, ns, lp), bh(lambda b, h, i: (b if kb_b else 0, h if kb_h else 0, 0, 0)))]
    args = [q[None], k[None], vt, kb]
    if kind == "diff":
        in_specs = ([pl.BlockSpec(memory_space=pltpu.SMEM)] + in_specs
                    + [pl.BlockSpec((1, wo), bh(lambda b, h, i: (0, h)))])
        args = [par] + args + [gain]
    body = functools.partial(_prompt_attn_body, kind=kind, front=front, tq=tq, ns=ns)
    return pl.pallas_call(
        body,
        grid=(bsz, n_hg, nq) if seq_major else (n_hg, bsz, nq),
        in_specs=in_specs,
        out_specs=pl.BlockSpec((1, tq, wo), bh(lambda b, h, i: (0, b * nq + i, h))),
        out_shape=jax.ShapeDtypeStruct((1, rows, n_hg * wo), BF16),
        scratch_shapes=[pltpu.VMEM((ns, 1, tq), F32),
                        pltpu.VMEM((ns, vrows, tq), F32), pltpu.VMEM((2, ns, tq, tq), F32),
                        pltpu.VMEM((ns, tq, tq), BF16), pltpu.VMEM((ns, lp, 128), F32),
                        pltpu.VMEM((2, ns, 8, tq), F32)],
        compiler_params=pltpu.CompilerParams(
            dimension_semantics=("arbitrary", "arbitrary", "arbitrary"), vmem_limit_bytes=VMEM_LIMIT),
    )(*args)[0]


def _cumsum_body(x_ref, tri_ref, o_ref, carry_ref):
    @pl.when(pl.program_id(0) == 0)
    def _():
        carry_ref[...] = jnp.zeros(carry_ref.shape, F32)

    y = jnp.dot(x_ref[...], tri_ref[...], preferred_element_type=F32,
                precision=lax.Precision.HIGHEST) + carry_ref[...]
    o_ref[...] = y
    carry_ref[...] = y[:, -1:]


def _cumsum_lanes(x, blk=256):
    rows, n = x.shape
    tri = (jnp.arange(blk)[:, None] <= jnp.arange(blk)[None, :]).astype(F32)
    return pl.pallas_call(
        _cumsum_body,
        grid=(n // blk,),
        in_specs=[pl.BlockSpec((rows, blk), lambda j: (0, j)), _const_spec((blk, blk))],
        out_specs=pl.BlockSpec((rows, blk), lambda j: (0, j)),
        out_shape=jax.ShapeDtypeStruct((rows, n), F32),
        scratch_shapes=[pltpu.VMEM((rows, 1), F32)],
        compiler_params=pltpu.CompilerParams(dimension_semantics=("arbitrary",)),
    )(x, tri)


def _expand_rows(x, rep):
    h, n = x.shape
    return jnp.broadcast_to(x[:, None, :], (h, rep, n)).reshape(h * rep, n)


def _dec_online(s, v16, m_ref, l_ref, acc_ref):
    m_prev = m_ref[...]
    m_new = jnp.maximum(m_prev, jnp.max(s, axis=-1, keepdims=True))
    alpha = jnp.exp2(m_prev - m_new)
    p = jnp.exp2(s - m_new)
    l_ref[...] = alpha * l_ref[...] + jnp.sum(p, axis=-1, keepdims=True)
    acc_ref[...] = alpha * acc_ref[...] + _dot(p.astype(BF16), v16)
    m_ref[...] = m_new


def _dec_init(m_ref, l_ref, acc_ref):
    m_ref[...] = jnp.full(m_ref.shape, NEG, F32)
    l_ref[...] = jnp.zeros(l_ref.shape, F32)
    acc_ref[...] = jnp.zeros(acc_ref.shape, F32)


def _diag_blocks(o, n_heads, ds, width):
    return jnp.concatenate([o[h * ds:(h + 1) * ds, h * width:(h + 1) * width] for h in range(n_heads)], axis=1)


def _diff_dec_body(par_ref, q_ref, kc_ref, vc_ref, kn_ref, vn_ref, g_ref, _, o_ref, m_ref, l_ref, acc_ref,
                   *, past, kb, ds, n_heads):
    jb = pl.program_id(1)
    hr = 2 * ds
    rows = n_heads * hr
    dv = 2 * DH_B
    r = lax.broadcasted_iota(jnp.int32, (rows, 1), 0)
    head = r // hr
    slope = LOG2E * jnp.exp2(-8.0 * (head + 1).astype(F32) / n_heads)
    qpos = past + (r % ds)
    q = q_ref[0]

    def key_block(k16, v16, key0):
        col = lax.broadcasted_iota(jnp.int32, (1, k16.shape[0]), 1)
        kpos = key0 + col // n_heads
        s = _dot_nt(q, k16) - slope * jnp.abs(qpos - kpos).astype(F32)
        s = jnp.where(col % n_heads == head, s, NEG)
        _dec_online(s, v16, m_ref, l_ref, acc_ref)

    @pl.when(jb == 0)
    def _():
        _dec_init(m_ref, l_ref, acc_ref)
        key_block(kn_ref[0], vn_ref[0], past)

    key_block(kc_ref[0, 0].astype(BF16), vc_ref[0, 0].astype(BF16), jb * kb)

    @pl.when(jb == pl.num_programs(1) - 1)
    def _():
        o = acc_ref[...] / l_ref[...]
        outs = []
        for h in range(n_heads):
            oh = o[h * hr:h * hr + ds] - par_ref[0] * o[h * hr + ds:(h + 1) * hr]
            ms = jnp.mean(oh * oh, axis=-1, keepdims=True)
            outs.append(oh * lax.rsqrt(ms + EPS) * g_ref[:, h * dv:(h + 1) * dv])
        o_ref[0] = jnp.concatenate(outs, axis=1).astype(o_ref.dtype)


def _fox_dec_body(q_ref, kc_ref, vc_ref, kn_ref, vn_ref, fc_ref, fn_ref, _, o_ref, m_ref, l_ref, acc_ref,
                  *, ds, n_heads):
    jb = pl.program_id(1)
    rows = n_heads * ds
    q = q_ref[0]
    fnew = fn_ref[0][:, :ds]
    fref = _expand_rows(fn_ref[0][:, 0:1], ds)

    @pl.when(jb == 0)
    def _():
        _dec_init(m_ref, l_ref, acc_ref)
        r = lax.broadcasted_iota(jnp.int32, (rows, 1), 0)
        kidx = lax.broadcasted_iota(jnp.int32, (1, ds), 1)
        s = _dot_nt(q, kn_ref[0]) + LOG2E * (fref - _expand_rows(fnew, ds))
        s = jnp.where(kidx <= (r % ds), s, NEG)
        _dec_online(s, vn_ref[0].astype(BF16), m_ref, l_ref, acc_ref)

    s = _dot_nt(q, kc_ref[0, 0].astype(BF16)) + LOG2E * (fref - _expand_rows(fc_ref[0], ds))
    _dec_online(s, vc_ref[0, 0].astype(BF16), m_ref, l_ref, acc_ref)

    @pl.when(jb == pl.num_programs(1) - 1)
    def _():
        o = acc_ref[...] / l_ref[...]
        o_ref[0] = _diag_blocks(o, n_heads, ds, DH_C).astype(o_ref.dtype)


def _mla_dec_body(qn_ref, qp_ref, cc_ref, pc_ref, cn_ref, pn_ref, wk_ref, wv_ref, ones_ref, _,
                  o_ref, m_ref, l_ref, acc_ref, *, ds, n_heads):
    jb = pl.program_id(1)
    qn = qn_ref[0]
    qp = qp_ref[0]
    ones_h = ones_ref[...]

    def key_block(ckv, kpe):
        c16 = ckv.astype(BF16)
        kn = _dot(c16, wk_ref[...])
        v = _dot(c16, wv_ref[...])
        n = kpe.shape[0]
        ss = _dot_nt(ones_h, (kn * kn).astype(BF16)) + _dot_nt(jnp.ones((n_heads, ROPE_D), BF16),
                                                               (kpe * kpe).astype(BF16))
        rinv = lax.rsqrt(ss * (1.0 / (NOPE_D + ROPE_D)) + EPS)
        s = _dot_nt(qn, kn.astype(BF16)) + _dot_nt(qp, kpe.astype(BF16))
        s = s * _expand_rows(rinv, ds)
        _dec_online(s, v.astype(BF16), m_ref, l_ref, acc_ref)

    @pl.when(jb == 0)
    def _():
        _dec_init(m_ref, l_ref, acc_ref)
        key_block(cn_ref[0], pn_ref[0])

    key_block(cc_ref[0, 0], pc_ref[0, 0])

    @pl.when(jb == pl.num_programs(1) - 1)
    def _():
        o = acc_ref[...] / l_ref[...]
        o_ref[0] = _diag_blocks(o, n_heads, ds, V_D).astype(o_ref.dtype)


def _per_seq(a):
    return (a, pl.BlockSpec((1,) + a.shape[1:], lambda b, j: (b, 0, 0)))


def _dec_const(a):
    return (a, _const_spec(a.shape))


def _dec_call(body, ins, prev, row0, ds, rows, acc_w, nb, n_kb, smem=None):
    in_specs = [spec for _, spec in ins] + [pl.BlockSpec(memory_space=pl.ANY)]
    args = [a for a, _ in ins] + [prev[None]]
    if smem is not None:
        in_specs = [pl.BlockSpec(memory_space=pltpu.SMEM)] + in_specs
        args = [smem] + args
    blk0 = row0 // ds
    return pl.pallas_call(
        body,
        grid=(nb, n_kb),
        in_specs=in_specs,
        out_specs=pl.BlockSpec((1, ds, prev.shape[1]), lambda b, j: (0, blk0 + b, 0)),
        out_shape=jax.ShapeDtypeStruct((1,) + prev.shape, prev.dtype),
        input_output_aliases={len(args) - 1: 0},
        scratch_shapes=[pltpu.VMEM((rows, 1), F32), pltpu.VMEM((rows, 1), F32), pltpu.VMEM((rows, acc_w), F32)],
        compiler_params=pltpu.CompilerParams(
            dimension_semantics=("parallel", "arbitrary"), vmem_limit_bytes=VMEM_LIMIT),
    )(*args)[0]


def kernel(x_prompt, x_sample, state_s5_re, state_s5_im, cache_diff_k, cache_diff_v, cache_fox_k, cache_fox_v, cache_fox_logf, cache_mla_ckv, cache_mla_kpe, meta_tokens, ffn_norm, ffn_w_in, ffn_w_out, mix_norm, ab_w_in, ab_w_out, s5_a_re, s5_a_im, s5_log_step, s5_b_re, s5_b_im, s5_c_re, s5_c_im, s5_d, s5_glu_w, s5_glu_b, diff_q_norm, diff_k_norm, diff_lam, diff_sub_norm, cd_w_in, cd_w_out, fox_q_norm, fox_k_norm, fox_f_bias, mla_q_a_norm, mla_q_b, mla_kv_a_norm, mla_kv_b, mla_q_norm, mla_k_norm):
    bsz, seq, dm = x_prompt.shape
    nb, ds, _ = x_sample.shape
    n_meta = meta_tokens.shape[0]
    past = cache_diff_k.shape[2]
    front = ROW_ALIGN - n_meta
    lp = front + n_meta + seq
    ltot = n_meta + seq
    assert n_meta + front == ROW_ALIGN and lp % ATT_BLOCK == 0 and front % CHUNK == CHUNK - n_meta
    assert ds == S5_STEP and past % CHUNK == 0 and ds <= CHUNK
    kb = min(DEC_KB, past)
    assert past % kb == 0
    n_kb = past // kb
    assert ffn_norm.shape[0] == 2 and ab_w_in.shape[0] == 1 and cd_w_in.shape[0] == 1

    h_b = cache_diff_k.shape[3]
    h_c = cache_fox_k.shape[3]
    h_d = mla_q_b.shape[2] // (NOPE_D + ROPE_D)
    s5w = s5_glu_w.shape[1]
    n_grp = s5w // S5_GROUP
    qkw = h_b * 2 * DH_B
    fox_w = h_c * DH_C
    q_lora = mla_q_a_norm.shape[1]
    kv_lora = mla_kv_a_norm.shape[1]
    d_qk = NOPE_D + ROPE_D

    n_p = bsz * lp
    head_rows_x = jnp.concatenate([jnp.zeros((front, dm), F32), meta_tokens.astype(F32)], axis=0)
    pieces = []
    for b in range(bsz):
        pieces += [head_rows_x, x_prompt[b]]
    x = jnp.concatenate(pieces + [x_sample.reshape(nb * ds, dm)], axis=0)
    rows = x.shape[0]
    tm = _row_tile(rows)

    ones64 = _block_diag_ones(64)
    ones128 = _block_diag_ones(128)

    x = _ffn(x, ffn_norm[0, 0], ffn_w_in[0, 0], ffn_w_out[0, 0], tm)

    gq = (jnp.tile(diff_q_norm[0], 2 * h_b) * (DH_B ** -0.5 * LOG2E)).reshape(1, qkw)
    gk = jnp.tile(diff_k_norm[0], 2 * h_b).reshape(1, qkw)
    s5_cols = S5_STEP * S5_GROUP
    q16, k16, u_t, vt_diff, k32h, v32h = _row_call(
        functools.partial(_ab_in_body, widths=(s5w, qkw)),
        [x], [mix_norm[0].reshape(1, dm), ab_w_in[0].astype(BF16), gq, gk, ones64],
        [qkw, qkw], [BF16, BF16], tm,
        group_outs=[jax.ShapeDtypeStruct((n_grp, rows // S5_STEP, s5_cols), BF16)],
        scratch=[pltpu.VMEM((s5w // 128, tm, 128), F32)],
        col_outs=[(h_b * DIFF_VT_ROWS, BF16)], head_outs=[(h_b, F32), (h_b, F32)])
    k32 = k32h.reshape(rows, h_b, 2 * DH_B)
    v32 = v32h.reshape(rows, h_b, 2 * DH_B)

    mats = _s5_matrices(s5_a_re[0], s5_a_im[0], s5_log_step[0], s5_b_re[0], s5_b_im[0],
                        s5_c_re[0], s5_c_im[0], s5_d[0])
    n_ch = lp // S5_STEP
    y_t, st_p = _s5_scan(u_t, jnp.zeros((n_grp, bsz, 4 * P_A), F32), mats, n_ch, bsz, 0)
    h_re = jnp.transpose(state_s5_re[0].astype(F32), (1, 0, 2))
    h_im = jnp.transpose(state_s5_im[0].astype(F32), (1, 0, 2))
    y_t, st_s = _s5_scan(u_t, jnp.concatenate([h_re, h_im, h_im, h_re], axis=-1), mats, 1, nb,
                         n_p // S5_STEP, y_prev=y_t)

    lv = diff_lam[0].astype(F32)
    lam_init = 0.8 - 0.6 * math.exp(-0.3 * 0)
    lam = jnp.exp(jnp.sum(lv[0] * lv[1])) - jnp.exp(jnp.sum(lv[2] * lv[3])) + lam_init
    slopes = jnp.exp2(-8.0 * jnp.arange(1, h_b + 1, dtype=F32) / h_b)
    par = jnp.concatenate([lam[None], slopes]).astype(F32)
    subg = (jnp.tile(diff_sub_norm[0], h_b) * (1.0 - lam_init)).reshape(1, qkw)
    kpad = jnp.arange(lp) < front
    kb_diff = jnp.where(kpad[None, :], NEG, LOG2E * slopes[:, None] * jnp.arange(lp, dtype=F32)[None, :])
    kb_diff = jnp.broadcast_to(kb_diff[None, :, None, :], (1, h_b, 2, lp)).reshape(
        1, 2 * h_b // ATT_STREAMS, ATT_STREAMS, lp)
    o_all = _prompt_attn("diff", q16, k16, vt_diff, kb_diff, bsz, lp, front, par=par, gain=subg)
    qs = q16[n_p:].reshape(nb, ds, h_b, 2, DH_B)
    eye_2 = jnp.eye(2, dtype=BF16)
    qbd = jnp.einsum('bqhmd,mM->bhmqMd', qs, eye_2).reshape(nb, h_b * 2 * ds, 2 * DH_B)
    cache_spec = lambda w: pl.BlockSpec((1, 1, kb, w), lambda b, j: (0, b, j, 0))
    head_rows = lambda a: a.reshape(a.shape[0], nb, past * h_b, 2 * DH_B)
    head_cache_spec = pl.BlockSpec((1, 1, kb * h_b, 2 * DH_B), lambda b, j: (0, b, j, 0))
    o_all = _dec_call(
        functools.partial(_diff_dec_body, past=past, kb=kb, ds=ds, n_heads=h_b),
        [_per_seq(qbd), (head_rows(cache_diff_k), head_cache_spec), (head_rows(cache_diff_v), head_cache_spec),
         _per_seq(k16[n_p:].reshape(nb, ds * h_b, 2 * DH_B)), _per_seq(v32h[n_p * h_b:].astype(BF16).reshape(nb, ds * h_b, 2 * DH_B)),
         _dec_const(subg)],
        o_all, n_p, ds, 2 * h_b * ds, 2 * DH_B, nb, n_kb, smem=par)

    x = _row_call(
        functools.partial(_ab_out_body, s5w=s5w),
        [x, o_all],
        [s5_glu_w[0].astype(BF16), s5_glu_b[0].reshape(1, s5w), ab_w_out[0].astype(BF16)],
        [dm], [F32], tm, group_ins=[y_t], scratch=[pltpu.VMEM((s5w // 128, tm, 128), F32)])[0]

    x = _ffn(x, ffn_norm[0, 1], ffn_w_in[0, 1], ffn_w_out[0, 1], tm)

    x = _ffn(x, ffn_norm[1, 0], ffn_w_in[1, 0], ffn_w_out[1, 0], tm)

    half = ROPE_D // 2
    inv = ROPE_THETA ** (-jnp.arange(half, dtype=F32) / half)
    pos = jnp.concatenate([jnp.tile(jnp.arange(lp, dtype=jnp.int32) - front, bsz),
                           jnp.tile(past + jnp.arange(ds, dtype=jnp.int32), nb)]).astype(F32)
    ang = pos[:, None] * inv[None, :]
    pad_r = HEAD_PAD - NOPE_D - ROPE_D
    cos_t = jnp.concatenate([jnp.ones((rows, NOPE_D), F32), jnp.cos(ang), jnp.cos(ang),
                             jnp.zeros((rows, pad_r), F32)], axis=1)
    sin_t = jnp.concatenate([jnp.zeros((rows, NOPE_D), F32), jnp.sin(ang), jnp.sin(ang),
                             jnp.zeros((rows, pad_r), F32)], axis=1)

    wcd = cd_w_in[0]
    c_fg = 3 * fox_w
    c_qa = c_fg + h_c
    c_kva = c_qa + q_lora
    c_pe = c_kva + kv_lora
    w_pe = wcd[:, c_pe:c_pe + ROPE_D]
    zc = lambda n: jnp.zeros((dm, n), F32)
    w_cd = jnp.concatenate([
        wcd[:, :3 * fox_w], wcd[:, c_qa:c_qa + q_lora], wcd[:, c_kva:c_kva + kv_lora],
        zc(NOPE_D), w_pe, zc(pad_r),
        zc(NOPE_D), -w_pe[:, half:], w_pe[:, :half], zc(pad_r),
        wcd[:, c_fg:c_fg + h_c], zc(HEAD_PAD - h_c)], axis=1).astype(BF16)
    qb = mla_q_b[0].reshape(q_lora, h_d, d_qk)
    zq = lambda n: jnp.zeros((q_lora, h_d, n), F32)
    qb_pad = jnp.concatenate([qb, zq(pad_r)], axis=-1).reshape(q_lora, h_d * HEAD_PAD)
    qb_rot = jnp.concatenate([zq(NOPE_D), -qb[..., NOPE_D + half:], qb[..., NOPE_D:NOPE_D + half], zq(pad_r)],
                             axis=-1).reshape(q_lora, h_d * HEAD_PAD)
    wq2 = jnp.concatenate([qb_pad, qb_rot], axis=1).astype(BF16)
    kvb = mla_kv_b[0].reshape(kv_lora, h_d, NOPE_D + V_D)
    wk_pad = jnp.concatenate([kvb[..., :NOPE_D], jnp.zeros((kv_lora, h_d, HEAD_PAD - NOPE_D), F32)],
                             axis=-1).reshape(kv_lora, h_d * HEAD_PAD).astype(BF16)
    wk_cmp = kvb[..., :NOPE_D].reshape(kv_lora, h_d * NOPE_D).astype(BF16)
    wv_cmp = kvb[..., NOPE_D:].reshape(kv_lora, h_d * V_D).astype(BF16)
    gfq = (jnp.tile(fox_q_norm[0], h_c) * (DH_C ** -0.5 * LOG2E)).reshape(1, fox_w)
    gfk = jnp.tile(fox_k_norm[0], h_c).reshape(1, fox_w)
    fbias = jnp.concatenate([fox_f_bias[0], jnp.zeros((HEAD_PAD - h_c,), F32)]).reshape(1, HEAD_PAD)
    gmq = jnp.tile(jnp.concatenate([mla_q_norm[0] * mla_k_norm[0] * (d_qk ** -0.5 * LOG2E), jnp.zeros((pad_r,), F32)]),
                   h_d).reshape(1, h_d * HEAD_PAD)

    (fq16, fk32, fk16, fv32, logf, qm16, ckv32, kpe32, km16, fvt, vmt) = _row_call(
        functools.partial(_cd_in_body, fox_w=fox_w, q_lora=q_lora, kv_lora=kv_lora, n_heads=h_d),
        [x, cos_t, sin_t],
        [mix_norm[1].reshape(1, dm), w_cd, gfq, gfk, fbias, mla_q_a_norm[0].reshape(1, q_lora), wq2,
         mla_kv_a_norm[0].reshape(1, kv_lora), wk_pad, wv_cmp.T, gmq, ones64, ones128],
        [fox_w, fox_w, fox_w, fox_w, h_c, h_d * HEAD_PAD, kv_lora, ROPE_D, h_d * HEAD_PAD],
        [BF16, F32, BF16, F32, F32, BF16, F32, F32, BF16], tm,
        col_outs=[(2 * fox_w, BF16), (2 * h_d * V_D, BF16)])

    logf_p = jnp.transpose(logf[:n_p].reshape(bsz, lp, h_c), (0, 2, 1)).reshape(bsz * h_c, lp)
    f_p = _cumsum_lanes(logf_p).reshape(bsz, h_c, lp)
    logf_s = jnp.concatenate([
        jnp.transpose(cache_fox_logf[0].astype(F32), (0, 2, 1)),
        jnp.transpose(logf[n_p:].reshape(nb, ds, h_c), (0, 2, 1)),
        jnp.zeros((nb, h_c, 256 - ds), F32)], axis=2).reshape(nb * h_c, past + 256)
    f_s = _cumsum_lanes(logf_s).reshape(nb, h_c, past + 256)

    kb_fox = jnp.where(kpad[None, None, :], NEG, -LOG2E * f_p).reshape(bsz, h_c // ATT_STREAMS, ATT_STREAMS, lp)
    oc_all = _prompt_attn("fox", fq16, fk16, fvt, kb_fox, bsz, lp, front)
    kb_mla = jnp.broadcast_to(jnp.where(kpad, NEG, 0.0).astype(F32)[None, None, None, :], (1, 1, ATT_STREAMS, lp))
    od_all = _prompt_attn("mla", qm16, km16, vmt, kb_mla, bsz, lp, front)

    eye_c = jnp.eye(h_c, dtype=BF16)
    fqs = fq16[n_p:].reshape(nb, ds, h_c, DH_C)
    fq_bd = jnp.einsum('bqhd,hH->bhqHd', fqs, eye_c).reshape(nb, h_c * ds, fox_w)
    fkc = cache_fox_k.reshape(cache_fox_k.shape[0], nb, past, fox_w)
    fvc = cache_fox_v.reshape(cache_fox_v.shape[0], nb, past, fox_w)
    oc_all = _dec_call(
        functools.partial(_fox_dec_body, ds=ds, n_heads=h_c),
        [_per_seq(fq_bd), (fkc, cache_spec(fox_w)), (fvc, cache_spec(fox_w)),
         _per_seq(fk16[n_p:].reshape(nb, ds, fox_w)), _per_seq(fv32[n_p:].reshape(nb, ds, fox_w)),
         (f_s, pl.BlockSpec((1, h_c, kb), lambda b, j: (b, 0, j))),
         (f_s, pl.BlockSpec((1, h_c, 128), lambda b, j: (b, 0, past // 128)))],
        oc_all, n_p, ds, h_c * ds, fox_w, nb, n_kb)

    eye_d = jnp.eye(h_d, dtype=BF16)
    qms = qm16[n_p:].reshape(nb, ds, h_d, HEAD_PAD)
    qn_bd = jnp.einsum('bqhd,hH->bhqHd', qms[..., :NOPE_D], eye_d).reshape(nb, h_d * ds, h_d * NOPE_D)
    qp_s = jnp.transpose(qms[..., NOPE_D:NOPE_D + ROPE_D], (0, 2, 1, 3)).reshape(nb, h_d * ds, ROPE_D)
    ones_h = jnp.repeat(jnp.eye(h_d, dtype=BF16), NOPE_D, axis=1)
    od_all = _dec_call(
        functools.partial(_mla_dec_body, ds=ds, n_heads=h_d),
        [_per_seq(qn_bd), _per_seq(qp_s),
         (cache_mla_ckv, pl.BlockSpec((1, 1, kb, kv_lora), lambda b, j: (0, b, j, 0))),
         (cache_mla_kpe, pl.BlockSpec((1, 1, kb, ROPE_D), lambda b, j: (0, b, j, 0))),
         _per_seq(ckv32[n_p:].reshape(nb, ds, kv_lora)), _per_seq(kpe32[n_p:].reshape(nb, ds, ROPE_D)),
         _dec_const(wk_cmp), _dec_const(wv_cmp), _dec_const(ones_h)],
        od_all, n_p, ds, h_d * ds, h_d * V_D, nb, n_kb)

    x = _row_call(functools.partial(_cd_out_body, fox_w=fox_w), [x, oc_all, od_all],
                  [cd_w_out[0].astype(BF16)], [dm], [F32], tm)[0]

    x = _ffn(x, ffn_norm[1, 1], ffn_w_in[1, 1], ffn_w_out[1, 1], tm)

    def p_rows(a, shape):
        return a[:n_p].reshape((bsz, lp) + a.shape[1:])[:, front:front + ltot].reshape((1, bsz, ltot) + shape)

    def s_rows(a, shape):
        return a[n_p:].reshape((1, nb, ds) + shape)

    def s5_state(st):
        st = jnp.transpose(st, (1, 0, 2))
        return st[None, :, :, :P_A], st[None, :, :, P_A:]

    y_prompt = x[:n_p].reshape(bsz, lp, dm)[:, front + n_meta:]
    y_sample = x[n_p:].reshape(nb, ds, dm)
    s5_re_p, s5_im_p = s5_state(st_p)
    s5_re_s, s5_im_s = s5_state(st_s)
    return (y_prompt, y_sample,
            s5_re_p, s5_im_p, p_rows(k32, (h_b, 2 * DH_B)), p_rows(v32, (h_b, 2 * DH_B)),
            p_rows(fk32, (h_c, DH_C)), p_rows(fv32, (h_c, DH_C)), p_rows(logf, (h_c,)),
            p_rows(ckv32, (kv_lora,)), p_rows(kpe32, (ROPE_D,)),
            s5_re_s, s5_im_s, s_rows(k32, (h_b, 2 * DH_B)), s_rows(v32, (h_b, 2 * DH_B)),
            s_rows(fk32, (h_c, DH_C)), s_rows(fv32, (h_c, DH_C)), s_rows(logf, (h_c,)),
            s_rows(ckv32, (kv_lora,)), s_rows(kpe32, (ROPE_D,)))
```

```python
import functools
import math

import jax
import jax.numpy as jnp
from jax import lax
from jax.experimental import pallas as pl
from jax.experimental.pallas import tpu as pltpu

F32 = jnp.float32
BF16 = jnp.bfloat16

EPS = 1e-6
CHUNK = 64
ROW_ALIGN = 256
S5_GROUP = 16
S5_STEP = 16
P_A = 64
DH_B = 64
DH_C = 64
NOPE_D = 64
ROPE_D = 32
V_D = 64
HEAD_PAD = 128
ROPE_THETA = 10000.0
NEG = -1e30
LOG2E = math.log2(math.e)
VMEM_LIMIT = 56 * 1024 * 1024
ATT_BLOCK = 256
ATT_STREAMS = 4
ATT_ROWS = 64
DIFF_VT_ROWS = 2 * DH_B + 16
DEC_KB = 1024


def _dot(a, b):
    return jnp.dot(a, b, preferred_element_type=F32)


def _dot_nt(a, b):
    return lax.dot_general(a, b, (((1,), (1,)), ((), ())), preferred_element_type=F32)


def _rms_rows(x, g):
    ms = jnp.mean(x * x, axis=-1, keepdims=True)
    return x * lax.rsqrt(ms + EPS) * g


def _group_sumsq(x, ones_bd):
    w = x.shape[-1]
    parts = [_dot((x[:, c:c + 256] * x[:, c:c + 256]).astype(BF16), ones_bd) for c in range(0, w, 256)]
    return parts[0] if len(parts) == 1 else jnp.concatenate(parts, axis=1)


def _block_diag_ones(group, n=256):
    r = jnp.arange(n) // group
    return (r[:, None] == r[None, :]).astype(BF16)


def _const_spec(shape):
    nd = len(shape)
    return pl.BlockSpec(shape, lambda *_: (0,) * nd, pipeline_mode=pl.Buffered(1))


def _row_tile(rows, cap=512):
    t = cap
    while rows % t:
        t //= 2
    return t


def _group_spec(a, tm):
    return pl.BlockSpec((a.shape[0], tm // S5_STEP, a.shape[2]), lambda i: (0, i, 0))


def _row_call(body, row_ins, consts, out_widths, out_dtypes, tm, group_ins=(), group_outs=(), scratch=(),
              col_outs=(), head_outs=()):
    rows = row_ins[0].shape[0]
    in_specs = [pl.BlockSpec((tm, a.shape[1]), lambda i: (i, 0)) for a in row_ins]
    in_specs += [_group_spec(a, tm) for a in group_ins]
    in_specs += [_const_spec(c.shape) for c in consts]
    out_specs = [pl.BlockSpec((tm, w), lambda i: (i, 0)) for w in out_widths]
    out_specs += [_group_spec(a, tm) for a in group_outs]
    out_specs += [pl.BlockSpec((w, tm), lambda i: (0, i)) for w, _ in col_outs]
    out_shape = [jax.ShapeDtypeStruct((rows, w), d) for w, d in zip(out_widths, out_dtypes)]
    out_shape += list(group_outs)
    out_shape += [jax.ShapeDtypeStruct((w, rows), d) for w, d in col_outs]
    out_specs += [pl.BlockSpec((tm * h, 128), lambda i: (i, 0)) for h, _ in head_outs]
    out_shape += [jax.ShapeDtypeStruct((rows * h, 128), d) for h, d in head_outs]
    return pl.pallas_call(
        body,
        grid=(rows // tm,),
        in_specs=in_specs,
        out_specs=out_specs,
        out_shape=out_shape,
        scratch_shapes=list(scratch),
        compiler_params=pltpu.CompilerParams(
            dimension_semantics=("parallel",), vmem_limit_bytes=VMEM_LIMIT),
    )(*row_ins, *group_ins, *consts)


def _ffn_body(x_ref, g_ref, win_ref, wout_ref, o_ref, *, d_ff, tf):
    x = x_ref[...]
    xn = _rms_rows(x, g_ref[...]).astype(BF16)
    acc = jnp.zeros(x.shape, F32)
    for c in range(0, d_ff, tf):
        gate = _dot(xn, win_ref[:, c:c + tf])
        up = _dot(xn, win_ref[:, d_ff + c:d_ff + c + tf])
        a = (gate * jax.nn.sigmoid(gate) * up).astype(BF16)
        acc = acc + _dot(a, wout_ref[c:c + tf, :])
    o_ref[...] = x + 0.5 * acc


def _ffn(x, g, w_in, w_out, tm):
    d_ff = w_out.shape[0]
    body = functools.partial(_ffn_body, d_ff=d_ff, tf=256)
    return _row_call(body, [x], [g.reshape(1, -1), w_in.astype(BF16), w_out.astype(BF16)],
                     [x.shape[1]], [F32], tm)[0]


def _ab_in_body(x_ref, g_ref, w_ref, gq_ref, gk_ref, ones_ref,
                q_ref, k16_ref, ut_ref, vt_ref, k32_ref, v32_ref, us_ref, *, widths):
    s5w, qkw = widths
    xn = _rms_rows(x_ref[...], g_ref[...]).astype(BF16)
    h = _dot(xn, w_ref[...])
    n_chunk = us_ref.shape[1] // S5_STEP
    per_col = 128 // S5_GROUP
    for v in range(s5w // 128):
        us_ref[v] = h[:, v * 128:(v + 1) * 128]
        steps = [us_ref[v, pl.ds(t, n_chunk, stride=S5_STEP), :] for t in range(S5_STEP)]
        for gl in range(per_col):
            ut_ref[v * per_col + gl] = jnp.concatenate(
                [x[:, gl * S5_GROUP:(gl + 1) * S5_GROUP] for x in steps], axis=1).astype(BF16)
    q = h[:, s5w:s5w + qkw]
    k = h[:, s5w + qkw:s5w + 2 * qkw]
    v = h[:, s5w + 2 * qkw:]
    ones_bd = ones_ref[...]
    qn = q * lax.rsqrt(_group_sumsq(q, ones_bd) * (1.0 / DH_B) + EPS) * gq_ref[...]
    kn = k * lax.rsqrt(_group_sumsq(k, ones_bd) * (1.0 / DH_B) + EPS) * gk_ref[...]
    q_ref[...] = qn.astype(BF16)
    k16_ref[...] = kn.astype(BF16)
    dv = 2 * DH_B
    n_head = qkw // dv
    tm = kn.shape[0]
    for hd in range(n_head):
        k32_ref[pl.ds(hd, tm, stride=n_head), :] = kn[:, hd * dv:(hd + 1) * dv]
        v32_ref[pl.ds(hd, tm, stride=n_head), :] = v[:, hd * dv:(hd + 1) * dv]
    vt = v.T
    ones = jnp.ones((DIFF_VT_ROWS - dv, vt.shape[1]), F32)
    vt_ref[...] = jnp.concatenate(
        [a for h in range(vt.shape[0] // dv) for a in (vt[h * dv:(h + 1) * dv], ones)], axis=0).astype(BF16)


def _ab_out_body(x_ref, o_ref, yt_ref, gluw_ref, glub_ref, wout_ref, out_ref, ys_ref, *, s5w):
    n_chunk = ys_ref.shape[1] // S5_STEP
    per_col = 128 // S5_GROUP
    for v in range(s5w // 128):
        for t in range(S5_STEP):
            ys_ref[v, pl.ds(t, n_chunk, stride=S5_STEP), :] = jnp.concatenate(
                [yt_ref[v * per_col + gl][:, t * S5_GROUP:(t + 1) * S5_GROUP] for gl in range(per_col)], axis=1)
    y = jnp.concatenate([ys_ref[v] for v in range(s5w // 128)], axis=1)
    g = 0.5 * y * (1.0 + jnp.tanh(math.sqrt(2.0 / math.pi) * (y + 0.044715 * (y * y * y))))
    z = _dot(g.astype(BF16), gluw_ref[...]) + glub_ref[...]
    s5o = g * jax.nn.sigmoid(z)
    m = _dot(s5o.astype(BF16), wout_ref[:s5w, :]) + _dot(o_ref[...], wout_ref[s5w:, :])
    out_ref[...] = x_ref[...] + m


def _heads_with_ones_t(vt):
    ones = jnp.ones((64, vt.shape[1]), vt.dtype)
    outs = []
    for h in range(vt.shape[0] // 64):
        outs += [vt[h * 64:(h + 1) * 64], ones]
    return jnp.concatenate(outs, axis=0)


def _cd_in_body(x_ref, cos_ref, sin_ref, g_ref, w_ref, gfq_ref, gfk_ref, fb_ref, gqa_ref, wq2_ref,
                gkva_ref, wk_ref, wv_ref, gmq_ref, ones64_ref, ones128_ref,
                fq_ref, fk32_ref, fk16_ref, fv32_ref, logf_ref, qm_ref, ckv_ref, kpe_ref,
                km_ref, fvt_ref, vmt_ref, *, fox_w, q_lora, kv_lora, n_heads):
    xn = _rms_rows(x_ref[...], g_ref[...]).astype(BF16)
    h = _dot(xn, w_ref[...])
    ones64 = ones64_ref[...]
    ones128 = ones128_ref[...]
    fq = h[:, :fox_w]
    fk = h[:, fox_w:2 * fox_w]
    fv = h[:, 2 * fox_w:3 * fox_w]
    c0 = 3 * fox_w
    qa = h[:, c0:c0 + q_lora]
    kva = h[:, c0 + q_lora:c0 + q_lora + kv_lora]
    c1 = c0 + q_lora + kv_lora
    pe_a = h[:, c1:c1 + HEAD_PAD]
    pe_b = h[:, c1 + HEAD_PAD:c1 + 2 * HEAD_PAD]
    fg = h[:, c1 + 2 * HEAD_PAD:c1 + 3 * HEAD_PAD]

    fqn = fq * lax.rsqrt(_group_sumsq(fq, ones64) * (1.0 / DH_C) + EPS) * gfq_ref[...]
    fkn = fk * lax.rsqrt(_group_sumsq(fk, ones64) * (1.0 / DH_C) + EPS) * gfk_ref[...]
    fq_ref[...] = fqn.astype(BF16)
    fk32_ref[...] = fkn
    fk16_ref[...] = fkn.astype(BF16)
    fv32_ref[...] = fv
    fvt_ref[...] = _heads_with_ones_t(fv.T).astype(BF16)

    z = fg + fb_ref[...]
    logf = jnp.minimum(z, 0.0) - jnp.log1p(jnp.exp(-jnp.abs(z)))
    logf_ref[...] = logf[:, :logf_ref.shape[1]]

    cos = cos_ref[...]
    sin = sin_ref[...]
    qan = _rms_rows(qa, gqa_ref[...]).astype(BF16)
    q2 = _dot(qan, wq2_ref[...])
    hw = n_heads * HEAD_PAD
    cos_t = jnp.concatenate([cos] * n_heads, axis=1)
    sin_t = jnp.concatenate([sin] * n_heads, axis=1)
    qr = q2[:, :hw] * cos_t + q2[:, hw:] * sin_t
    d_qk = NOPE_D + ROPE_D
    qm = qr * lax.rsqrt(_group_sumsq(qr, ones128) * (1.0 / d_qk) + EPS) * gmq_ref[...]
    qm_ref[...] = qm.astype(BF16)

    ckv = _rms_rows(kva, gkva_ref[...])
    ckv_ref[...] = ckv
    pe = pe_a * cos + pe_b * sin
    kpe_ref[...] = pe[:, NOPE_D:NOPE_D + ROPE_D]
    ckv16 = ckv.astype(BF16)
    kraw = _dot(ckv16, wk_ref[...]) + jnp.concatenate([pe] * n_heads, axis=1)
    km = kraw * lax.rsqrt(_group_sumsq(kraw, ones128) * (1.0 / d_qk) + EPS)
    km_ref[...] = km.astype(BF16)
    vmt_ref[...] = _heads_with_ones_t(_dot_nt(wv_ref[...], ckv16)).astype(BF16)


def _cd_out_body(x_ref, oc_ref, od_ref, wout_ref, out_ref, *, fox_w):
    m = _dot(oc_ref[...], wout_ref[:fox_w, :]) + _dot(od_ref[...], wout_ref[fox_w:, :])
    out_ref[...] = x_ref[...] + m


def _s5_body(*refs, n_chunks, bsz, aliased):
    if aliased:
        u_ref, h0_ref, m_ref, bm_ref, cm_ref, coef_ref, _, y_ref, st_ref, s2_ref, hp_ref = refs
    else:
        u_ref, h0_ref, m_ref, bm_ref, cm_ref, coef_ref, y_ref, st_ref, s2_ref, hp_ref = refs
    u = u_ref[0]
    half = 2 * P_A
    s2 = _dot(u, bm_ref[0])
    s2_ref[0] = s2[:, :half]
    s2_ref[1] = s2[:, half:]
    c1 = coef_ref[0, 0:1, :]
    c2 = coef_ref[0, 1:2, :]
    c3 = coef_ref[0, 2:3, :]

    def step(j, carry):
        ha, hb = carry
        hp_ref[pl.ds(j, bsz, stride=n_chunks), :] = ha
        sa = s2_ref[0, pl.ds(j, bsz, stride=n_chunks), :]
        sb = s2_ref[1, pl.ds(j, bsz, stride=n_chunks), :]
        return ha * c1 + hb * c2 + sa, hb * c1 + ha * c3 + sb

    h0 = h0_ref[0]
    ha, _ = lax.fori_loop(0, n_chunks, step, (h0[:, :half], h0[:, half:]))
    st_ref[0] = ha
    y_ref[0] = _dot(u, m_ref[0]) + _dot(hp_ref[...].astype(BF16), cm_ref[0])


def _s5_scan(u_t, h0, mats, n_chunks, bsz, row0, y_prev=None):
    m_mat, bm, cm, coef = mats
    g, rows_all, w = u_t.shape
    rows = n_chunks * bsz
    assert row0 % rows == 0
    blk = row0 // rows
    aliased = y_prev is not None
    body = functools.partial(_s5_body, n_chunks=n_chunks, bsz=bsz, aliased=aliased)
    per_g = lambda a: pl.BlockSpec((1,) + a.shape[1:], lambda i: (i, 0, 0))
    in_specs = [pl.BlockSpec((1, rows, w), lambda i: (i, blk, 0)),
                per_g(h0), per_g(m_mat), per_g(bm), per_g(cm), per_g(coef)]
    args = [u_t, h0, m_mat, bm, cm, coef]
    if aliased:
        in_specs.append(pl.BlockSpec(memory_space=pl.ANY))
        args.append(y_prev)
    return pl.pallas_call(
        body,
        grid=(g,),
        in_specs=in_specs,
        out_specs=[pl.BlockSpec((1, rows, w), lambda i: (i, blk, 0)),
                   pl.BlockSpec((1, bsz, 2 * P_A), lambda i: (i, 0, 0))],
        out_shape=[jax.ShapeDtypeStruct((g, rows_all, w), F32),
                   jax.ShapeDtypeStruct((g, bsz, 2 * P_A), F32)],
        input_output_aliases={6: 0} if aliased else {},
        scratch_shapes=[pltpu.VMEM((2, rows, 2 * P_A), F32), pltpu.VMEM((rows, 2 * P_A), F32)],
        compiler_params=pltpu.CompilerParams(
            dimension_semantics=("parallel",), vmem_limit_bytes=VMEM_LIMIT),
    )(*args)


def _s5_matrices(a_re, a_im, log_step, b_re, b_im, c_re, c_im, d):
    g = a_re.shape[0]
    t = S5_STEP
    lam = lax.complex(a_re, a_im)
    dl = lam * jnp.exp(log_step)[:, None]
    lam_bar = jnp.exp(dl)
    b_bar = ((lam_bar - 1.0) / lam)[..., None] * lax.complex(b_re, b_im)
    c = lax.complex(c_re, c_im)
    pw = jnp.exp(dl[:, None, :] * jnp.arange(t + 1, dtype=F32)[None, :, None])
    bmc = pw[:, t - 1::-1][:, :, :, None] * b_bar[:, None]
    bmc = jnp.swapaxes(bmc, 2, 3).reshape(g, t * S5_GROUP, P_A)
    bm = jnp.concatenate([bmc.real, bmc.imag, bmc.imag, bmc.real], axis=-1)
    kk = jnp.einsum('gcp,gkp,gpd->gkcd', c, pw[:, :t], b_bar).real
    kk = kk.at[:, 0].add(d.reshape(g, S5_GROUP)[:, :, None] * jnp.eye(S5_GROUP, dtype=F32))
    lag = jnp.arange(t)[None, :] - jnp.arange(t)[:, None]
    toep = jnp.where((lag >= 0)[None, :, :, None, None], kk[:, jnp.clip(lag, 0, t - 1)], 0.0)
    m_mat = jnp.transpose(toep, (0, 1, 4, 2, 3)).reshape(g, t * S5_GROUP, t * S5_GROUP)
    cp = c[:, None] * pw[:, 1:, None, :]
    cpm = jnp.transpose(cp, (0, 3, 1, 2)).reshape(g, P_A, t * S5_GROUP)
    cm = jnp.concatenate([cpm.real, -cpm.imag], axis=1)
    a_t = pw[:, t]
    ar, ai = a_t.real, a_t.imag
    zeros = jnp.zeros_like(ar)
    coef = jnp.stack([jnp.concatenate([ar, ar], -1), jnp.concatenate([-ai, ai], -1),
                      jnp.concatenate([ai, -ai], -1), jnp.concatenate([zeros, zeros], -1)], axis=1)
    return m_mat.astype(BF16), bm.astype(BF16), cm.astype(BF16), coef.astype(F32)


def _online(logits, vt, e, m_ref, acc_ref, p_ref, block_max=None):
    tk = p_ref.shape[1]
    if block_max is None:
        part = logits(0, ATT_ROWS)
        for r0 in range(ATT_ROWS, tk, ATT_ROWS):
            part = jnp.maximum(part, logits(r0, ATT_ROWS))
    else:
        part = block_max
    m_prev = m_ref[e]
    m_new = jnp.maximum(m_prev, jnp.max(part, axis=0, keepdims=True))
    alpha = jnp.exp2(m_prev - m_new)
    m_ref[e] = m_new
    for r0 in range(0, tk, ATT_ROWS):
        p_ref[e, r0:r0 + ATT_ROWS] = jnp.exp2(logits(r0, ATT_ROWS) - m_new).astype(BF16)
    acc_ref[e] = alpha * acc_ref[e] + _dot(vt, p_ref[e])


def _prompt_attn_body(*refs, kind, tq, ns):
    if kind == "diff":
        (par_ref, q_ref, k_ref, vt_ref, kb_ref, g_ref, o_ref,
         m_ref, acc_ref, s_ref, p_ref, kbc_ref, mx_ref) = refs
    else:
        q_ref, k_ref, vt_ref, kb_ref, o_ref, m_ref, acc_ref, s_ref, p_ref, kbc_ref, mx_ref = refs
    vrows = acc_ref.shape[1]
    i = pl.program_id(2)
    if kind == "fox":
        hg = pl.program_id(1)
        new_bias = i == 0
    else:
        hg = pl.program_id(0)
        new_bias = jnp.logical_and(pl.program_id(1) == 0, i == 0)
    m_ref[...] = jnp.full(m_ref.shape, NEG, F32)
    acc_ref[...] = jnp.zeros(acc_ref.shape, F32)

    @pl.when(new_bias)
    def _():
        def fill(c, carry):
            c0 = pl.multiple_of(c * 128, 128)
            for e in range(ns):
                row = kb_ref[0, 0, e:e + 1, pl.ds(c0, 128)]
                kbc_ref[e, pl.ds(c0, 128), :] = jnp.broadcast_to(row, (128, 128)).T
            return carry

        lax.fori_loop(0, kbc_ref.shape[1] // 128, fill, 0)

    q = q_ref[0]
    lane = lax.broadcasted_iota(jnp.int32, (1, 128), 1)
    qs = []
    for e in range(ns):
        if kind == "mla":
            qs.append(q[:, e * HEAD_PAD:(e + 1) * HEAD_PAD])
        else:
            qp = q[:, (e // 2) * 128:(e // 2 + 1) * 128]
            qs.append(jnp.where((lane < 64) if e % 2 == 0 else (lane >= 64), qp, jnp.zeros_like(qp)))
    qstart = pl.multiple_of(i * tq, tq)
    lane_q = lax.broadcasted_iota(jnp.int32, (1, tq), 1)
    ref = [-kb_ref[0, 0, e:e + 1, pl.ds(qstart + (tq - 128), 128)][:, 127:128] for e in range(ns)]
    if kind == "diff":
        slope = [LOG2E * par_ref[1 + hg * (ns // 2) + p] for p in range(ns // 2)]

    tk = tq

    def k_slot(k, e):
        if kind == "mla":
            return k[:, e * HEAD_PAD:(e + 1) * HEAD_PAD]
        return k[:, (e // 2) * 128:(e // 2 + 1) * 128]

    def vt_slot(vt, e):
        r0 = (e // 2 if kind == "diff" else e) * vrows
        return vt[r0:r0 + vrows]

    def scores(j, slot, first=False):
        k0 = pl.multiple_of(j * tk, tk)
        k = k_ref[0, pl.ds(k0, tk), :]
        for e in range(ns):
            s = _dot_nt(k_slot(k, e), qs[e])
            if first or kind != "mla":
                bias = kbc_ref[e, pl.ds(k0, tk), :] + ref[e]
                s = s + jnp.concatenate([bias] * (tq // 128), axis=1)
            s_ref[slot, e] = s
            part = s[0:8]
            for r0 in range(8, tk, 8):
                part = jnp.maximum(part, s[r0:r0 + 8])
            mx_ref[slot, e] = part

    def softmax_pv(j, slot, diag):
        k0 = pl.multiple_of(j * tk, tk)
        vt = vt_ref[:, pl.ds(k0, tk)]
        for e in range(ns):
            def logits(r0, n, e=e):
                s = s_ref[slot, e, r0:r0 + n, :]
                if diag:
                    row_i = r0 + lax.broadcasted_iota(jnp.int32, (n, 1), 0)
                    if kind == "diff":
                        s = s - (2.0 * slope[e // 2]) * jnp.maximum(row_i - lane_q, 0).astype(F32)
                    if kind == "fox":
                        s = jnp.where(row_i <= lane_q, s, NEG)
                    elif r0 > 0:
                        s = jnp.where(lane_q >= r0, s, NEG)
                return s

            _online(logits, vt_slot(vt, e), e, m_ref, acc_ref, p_ref,
                    block_max=None if diag else mx_ref[slot, e])

    scores(0, 0, first=True)

    def pair_body(jj, c):
        j = 2 * jj
        scores(j + 1, 1)
        softmax_pv(j, 0, False)
        scores(j + 2, 0)
        softmax_pv(j + 1, 1, False)
        return c

    lax.fori_loop(0, i // 2, pair_body, 0)

    @pl.when(i % 2 == 0)
    def _():
        softmax_pv(i, 0, True)

    @pl.when(i % 2 == 1)
    def _():
        scores(i, 1)
        softmax_pv(i - 1, 0, False)
        softmax_pv(i, 1, True)

    outs = []
    for p in range(ns // 2):
        e0, e1 = 2 * p, 2 * p + 1
        if kind == "diff":
            a0, a1 = acc_ref[e0], acc_ref[e1]
            dv = 2 * DH_B
            o = (a0[:dv] / a0[dv:dv + 1] - par_ref[0] * (a1[:dv] / a1[dv:dv + 1])).T
            ms = jnp.mean(o * o, axis=-1, keepdims=True)
            outs.append(o * lax.rsqrt(ms + EPS) * g_ref[:, p * 128:(p + 1) * 128])
        else:
            a0, a1 = acc_ref[e0], acc_ref[e1]
            outs.append(jnp.concatenate([a0[:64] / a0[64:], a1[:64] / a1[64:]], axis=0).T)
    o_ref[0] = (outs[0] if len(outs) == 1 else jnp.concatenate(outs, axis=1)).astype(o_ref.dtype)


def _prompt_attn(kind, q, k, vt, kb, bsz, lp, par=None, gain=None):
    rows = q.shape[0]
    tq = ATT_BLOCK
    ns = ATT_STREAMS
    wq = (ns // 2) * (2 * HEAD_PAD if kind == "mla" else 128)
    vrows = DIFF_VT_ROWS if kind == "diff" else 128
    wv = (ns // 2) * (vrows if kind == "diff" else 2 * vrows)
    wo = (ns // 2) * 128
    n_hg = vt.shape[0] // wv
    nq = lp // tq
    kb_b, kb_h = kb.shape[0] > 1, kb.shape[1] > 1
    seq_major = kind == "fox"

    def bh(f):
        return (lambda b, h, i: f(b, h, i)) if seq_major else (lambda h, b, i: f(b, h, i))

    in_specs = [pl.BlockSpec((1, tq, wq), bh(lambda b, h, i: (0, b * nq + i, h))),
                pl.BlockSpec((1, lp, wq), bh(lambda b, h, i: (0, b, h))),
                pl.BlockSpec((wv, lp), bh(lambda b, h, i: (h, b))),
                pl.BlockSpec((1, 1, ns, lp), bh(lambda b, h, i: (b if kb_b else 0, h if kb_h else 0, 0, 0)))]
    args = [q[None], k[None], vt, kb]
    if kind == "diff":
        in_specs = ([pl.BlockSpec(memory_space=pltpu.SMEM)] + in_specs
                    + [pl.BlockSpec((1, wo), bh(lambda b, h, i: (0, h)))])
        args = [par] + args + [gain]
    assert ATT_ROWS == CHUNK
    body = functools.partial(_prompt_attn_body, kind=kind, tq=tq, ns=ns)
    return pl.pallas_call(
        body,
        grid=(bsz, n_hg, nq) if seq_major else (n_hg, bsz, nq),
        in_specs=in_specs,
        out_specs=pl.BlockSpec((1, tq, wo), bh(lambda b, h, i: (0, b * nq + i, h))),
        out_shape=jax.ShapeDtypeStruct((1, rows, n_hg * wo), BF16),
        scratch_shapes=[pltpu.VMEM((ns, 1, tq), F32),
                        pltpu.VMEM((ns, vrows, tq), F32), pltpu.VMEM((2, ns, tq, tq), F32),
                        pltpu.VMEM((ns, tq, tq), BF16), pltpu.VMEM((ns, lp, 128), F32),
                        pltpu.VMEM((2, ns, 8, tq), F32)],
        compiler_params=pltpu.CompilerParams(
            dimension_semantics=("arbitrary", "arbitrary", "arbitrary"), vmem_limit_bytes=VMEM_LIMIT),
    )(*args)[0]


def _cumsum_body(x_ref, tri_ref, o_ref, carry_ref):
    @pl.when(pl.program_id(0) == 0)
    def _():
        carry_ref[...] = jnp.zeros(carry_ref.shape, F32)

    y = jnp.dot(x_ref[...], tri_ref[...], preferred_element_type=F32,
                precision=lax.Precision.HIGHEST) + carry_ref[...]
    o_ref[...] = y
    carry_ref[...] = y[:, -1:]


def _cumsum_lanes(x, blk=256):
    rows, n = x.shape
    tri = (jnp.arange(blk)[:, None] <= jnp.arange(blk)[None, :]).astype(F32)
    return pl.pallas_call(
        _cumsum_body,
        grid=(n // blk,),
        in_specs=[pl.BlockSpec((rows, blk), lambda j: (0, j)), _const_spec((blk, blk))],
        out_specs=pl.BlockSpec((rows, blk), lambda j: (0, j)),
        out_shape=jax.ShapeDtypeStruct((rows, n), F32),
        scratch_shapes=[pltpu.VMEM((rows, 1), F32)],
        compiler_params=pltpu.CompilerParams(dimension_semantics=("arbitrary",)),
    )(x, tri)


def _expand_rows(x, rep):
    h, n = x.shape
    return jnp.broadcast_to(x[:, None, :], (h, rep, n)).reshape(h * rep, n)


def _dec_online(s, v16, m_ref, l_ref, acc_ref):
    m_prev = m_ref[...]
    m_new = jnp.maximum(m_prev, jnp.max(s, axis=-1, keepdims=True))
    alpha = jnp.exp2(m_prev - m_new)
    p = jnp.exp2(s - m_new)
    l_ref[...] = alpha * l_ref[...] + jnp.sum(p, axis=-1, keepdims=True)
    acc_ref[...] = alpha * acc_ref[...] + _dot(p.astype(BF16), v16)
    m_ref[...] = m_new


def _dec_init(m_ref, l_ref, acc_ref):
    m_ref[...] = jnp.full(m_ref.shape, NEG, F32)
    l_ref[...] = jnp.zeros(l_ref.shape, F32)
    acc_ref[...] = jnp.zeros(acc_ref.shape, F32)


def _diag_blocks(o, n_heads, ds, width):
    return jnp.concatenate([o[h * ds:(h + 1) * ds, h * width:(h + 1) * width] for h in range(n_heads)], axis=1)


def _diff_dec_body(par_ref, q_ref, kc_ref, vc_ref, kn_ref, vn_ref, g_ref, _, o_ref, m_ref, l_ref, acc_ref,
                   *, past, kb, ds, n_heads):
    jb = pl.program_id(1)
    hr = 2 * ds
    rows = n_heads * hr
    dv = 2 * DH_B
    r = lax.broadcasted_iota(jnp.int32, (rows, 1), 0)
    head = r // hr
    slope = LOG2E * jnp.exp2(-8.0 * (head + 1).astype(F32) / n_heads)
    qpos = past + (r % ds)
    q = q_ref[0]

    def key_block(k16, v16, key0):
        col = lax.broadcasted_iota(jnp.int32, (1, k16.shape[0]), 1)
        kpos = key0 + col // n_heads
        s = _dot_nt(q, k16) - slope * jnp.abs(qpos - kpos).astype(F32)
        s = jnp.where(col % n_heads == head, s, NEG)
        _dec_online(s, v16, m_ref, l_ref, acc_ref)

    @pl.when(jb == 0)
    def _():
        _dec_init(m_ref, l_ref, acc_ref)
        key_block(kn_ref[0], vn_ref[0], past)

    key_block(kc_ref[0, 0].astype(BF16), vc_ref[0, 0].astype(BF16), jb * kb)

    @pl.when(jb == pl.num_programs(1) - 1)
    def _():
        o = acc_ref[...] / l_ref[...]
        outs = []
        for h in range(n_heads):
            oh = o[h * hr:h * hr + ds] - par_ref[0] * o[h * hr + ds:(h + 1) * hr]
            ms = jnp.mean(oh * oh, axis=-1, keepdims=True)
            outs.append(oh * lax.rsqrt(ms + EPS) * g_ref[:, h * dv:(h + 1) * dv])
        o_ref[0] = jnp.concatenate(outs, axis=1).astype(o_ref.dtype)


def _fox_dec_body(q_ref, kc_ref, vc_ref, kn_ref, vn_ref, fc_ref, fn_ref, _, o_ref, m_ref, l_ref, acc_ref,
                  *, ds, n_heads):
    jb = pl.program_id(1)
    rows = n_heads * ds
    q = q_ref[0]
    fnew = fn_ref[0][:, :ds]
    fref = _expand_rows(fn_ref[0][:, 0:1], ds)

    @pl.when(jb == 0)
    def _():
        _dec_init(m_ref, l_ref, acc_ref)
        r = lax.broadcasted_iota(jnp.int32, (rows, 1), 0)
        kidx = lax.broadcasted_iota(jnp.int32, (1, ds), 1)
        s = _dot_nt(q, kn_ref[0]) + LOG2E * (fref - _expand_rows(fnew, ds))
        s = jnp.where(kidx <= (r % ds), s, NEG)
        _dec_online(s, vn_ref[0].astype(BF16), m_ref, l_ref, acc_ref)

    s = _dot_nt(q, kc_ref[0, 0].astype(BF16)) + LOG2E * (fref - _expand_rows(fc_ref[0], ds))
    _dec_online(s, vc_ref[0, 0].astype(BF16), m_ref, l_ref, acc_ref)

    @pl.when(jb == pl.num_programs(1) - 1)
    def _():
        o = acc_ref[...] / l_ref[...]
        o_ref[0] = _diag_blocks(o, n_heads, ds, DH_C).astype(o_ref.dtype)


def _mla_dec_body(qn_ref, qp_ref, cc_ref, pc_ref, cn_ref, pn_ref, wk_ref, wv_ref, ones_ref, _,
                  o_ref, m_ref, l_ref, acc_ref, *, ds, n_heads):
    jb = pl.program_id(1)
    qn = qn_ref[0]
    qp = qp_ref[0]
    ones_h = ones_ref[...]

    def key_block(ckv, kpe):
        c16 = ckv.astype(BF16)
        kn = _dot(c16, wk_ref[...])
        v = _dot(c16, wv_ref[...])
        n = kpe.shape[0]
        ss = _dot_nt(ones_h, (kn * kn).astype(BF16)) + _dot_nt(jnp.ones((n_heads, ROPE_D), BF16),
                                                               (kpe * kpe).astype(BF16))
        rinv = lax.rsqrt(ss * (1.0 / (NOPE_D + ROPE_D)) + EPS)
        s = _dot_nt(qn, kn.astype(BF16)) + _dot_nt(qp, kpe.astype(BF16))
        s = s * _expand_rows(rinv, ds)
        _dec_online(s, v.astype(BF16), m_ref, l_ref, acc_ref)

    @pl.when(jb == 0)
    def _():
        _dec_init(m_ref, l_ref, acc_ref)
        key_block(cn_ref[0], pn_ref[0])

    key_block(cc_ref[0, 0], pc_ref[0, 0])

    @pl.when(jb == pl.num_programs(1) - 1)
    def _():
        o = acc_ref[...] / l_ref[...]
        o_ref[0] = _diag_blocks(o, n_heads, ds, V_D).astype(o_ref.dtype)


def _per_seq(a):
    return (a, pl.BlockSpec((1,) + a.shape[1:], lambda b, j: (b, 0, 0)))


def _dec_const(a):
    return (a, _const_spec(a.shape))


def _dec_call(body, ins, prev, row0, ds, rows, acc_w, nb, n_kb, smem=None):
    in_specs = [spec for _, spec in ins] + [pl.BlockSpec(memory_space=pl.ANY)]
    args = [a for a, _ in ins] + [prev[None]]
    if smem is not None:
        in_specs = [pl.BlockSpec(memory_space=pltpu.SMEM)] + in_specs
        args = [smem] + args
    blk0 = row0 // ds
    return pl.pallas_call(
        body,
        grid=(nb, n_kb),
        in_specs=in_specs,
        out_specs=pl.BlockSpec((1, ds, prev.shape[1]), lambda b, j: (0, blk0 + b, 0)),
        out_shape=jax.ShapeDtypeStruct((1,) + prev.shape, prev.dtype),
        input_output_aliases={len(args) - 1: 0},
        scratch_shapes=[pltpu.VMEM((rows, 1), F32), pltpu.VMEM((rows, 1), F32), pltpu.VMEM((rows, acc_w), F32)],
        compiler_params=pltpu.CompilerParams(
            dimension_semantics=("parallel", "arbitrary"), vmem_limit_bytes=VMEM_LIMIT),
    )(*args)[0]


def kernel(x_prompt, x_sample, state_s5_re, state_s5_im, cache_diff_k, cache_diff_v, cache_fox_k, cache_fox_v, cache_fox_logf, cache_mla_ckv, cache_mla_kpe, meta_tokens, ffn_norm, ffn_w_in, ffn_w_out, mix_norm, ab_w_in, ab_w_out, s5_a_re, s5_a_im, s5_log_step, s5_b_re, s5_b_im, s5_c_re, s5_c_im, s5_d, s5_glu_w, s5_glu_b, diff_q_norm, diff_k_norm, diff_lam, diff_sub_norm, cd_w_in, cd_w_out, fox_q_norm, fox_k_norm, fox_f_bias, mla_q_a_norm, mla_q_b, mla_kv_a_norm, mla_kv_b, mla_q_norm, mla_k_norm):
    bsz, seq, dm = x_prompt.shape
    nb, ds, _ = x_sample.shape
    n_meta = meta_tokens.shape[0]
    past = cache_diff_k.shape[2]
    front = ROW_ALIGN - n_meta
    lp = front + n_meta + seq
    ltot = n_meta + seq
    assert n_meta + front == ROW_ALIGN and lp % ATT_BLOCK == 0 and front % CHUNK == CHUNK - n_meta
    assert ds == S5_STEP and past % CHUNK == 0 and ds <= CHUNK
    kb = min(DEC_KB, past)
    assert past % kb == 0
    n_kb = past // kb
    assert ffn_norm.shape[0] == 2 and ab_w_in.shape[0] == 1 and cd_w_in.shape[0] == 1

    h_b = cache_diff_k.shape[3]
    h_c = cache_fox_k.shape[3]
    h_d = mla_q_b.shape[2] // (NOPE_D + ROPE_D)
    s5w = s5_glu_w.shape[1]
    n_grp = s5w // S5_GROUP
    qkw = h_b * 2 * DH_B
    fox_w = h_c * DH_C
    q_lora = mla_q_a_norm.shape[1]
    kv_lora = mla_kv_a_norm.shape[1]
    d_qk = NOPE_D + ROPE_D

    n_p = bsz * lp
    head_rows_x = jnp.concatenate([jnp.zeros((front, dm), F32), meta_tokens.astype(F32)], axis=0)
    pieces = []
    for b in range(bsz):
        pieces += [head_rows_x, x_prompt[b]]
    x = jnp.concatenate(pieces + [x_sample.reshape(nb * ds, dm)], axis=0)
    rows = x.shape[0]
    tm = _row_tile(rows)

    ones64 = _block_diag_ones(64)
    ones128 = _block_diag_ones(128)

    x = _ffn(x, ffn_norm[0, 0], ffn_w_in[0, 0], ffn_w_out[0, 0], tm)

    gq = (jnp.tile(diff_q_norm[0], 2 * h_b) * (DH_B ** -0.5 * LOG2E)).reshape(1, qkw)
    gk = jnp.tile(diff_k_norm[0], 2 * h_b).reshape(1, qkw)
    s5_cols = S5_STEP * S5_GROUP
    q16, k16, u_t, vt_diff, k32h, v32h = _row_call(
        functools.partial(_ab_in_body, widths=(s5w, qkw)),
        [x], [mix_norm[0].reshape(1, dm), ab_w_in[0].astype(BF16), gq, gk, ones64],
        [qkw, qkw], [BF16, BF16], tm,
        group_outs=[jax.ShapeDtypeStruct((n_grp, rows // S5_STEP, s5_cols), BF16)],
        scratch=[pltpu.VMEM((s5w // 128, tm, 128), F32)],
        col_outs=[(h_b * DIFF_VT_ROWS, BF16)], head_outs=[(h_b, F32), (h_b, F32)])
    k32 = k32h.reshape(rows, h_b, 2 * DH_B)
    v32 = v32h.reshape(rows, h_b, 2 * DH_B)

    mats = _s5_matrices(s5_a_re[0], s5_a_im[0], s5_log_step[0], s5_b_re[0], s5_b_im[0],
                        s5_c_re[0], s5_c_im[0], s5_d[0])
    n_ch = lp // S5_STEP
    y_t, st_p = _s5_scan(u_t, jnp.zeros((n_grp, bsz, 4 * P_A), F32), mats, n_ch, bsz, 0)
    h_re = jnp.transpose(state_s5_re[0].astype(F32), (1, 0, 2))
    h_im = jnp.transpose(state_s5_im[0].astype(F32), (1, 0, 2))
    y_t, st_s = _s5_scan(u_t, jnp.concatenate([h_re, h_im, h_im, h_re], axis=-1), mats, 1, nb,
                         n_p // S5_STEP, y_prev=y_t)

    lv = diff_lam[0].astype(F32)
    lam_init = 0.8 - 0.6 * math.exp(-0.3 * 0)
    lam = jnp.exp(jnp.sum(lv[0] * lv[1])) - jnp.exp(jnp.sum(lv[2] * lv[3])) + lam_init
    slopes = jnp.exp2(-8.0 * jnp.arange(1, h_b + 1, dtype=F32) / h_b)
    par = jnp.concatenate([lam[None], slopes]).astype(F32)
    subg = (jnp.tile(diff_sub_norm[0], h_b) * (1.0 - lam_init)).reshape(1, qkw)
    kpad = jnp.arange(lp) < front
    kb_diff = jnp.where(kpad[None, :], NEG, LOG2E * slopes[:, None] * jnp.arange(lp, dtype=F32)[None, :])
    kb_diff = jnp.broadcast_to(kb_diff[None, :, None, :], (1, h_b, 2, lp)).reshape(
        1, 2 * h_b // ATT_STREAMS, ATT_STREAMS, lp)
    o_all = _prompt_attn("diff", q16, k16, vt_diff, kb_diff, bsz, lp, par=par, gain=subg)
    qs = q16[n_p:].reshape(nb, ds, h_b, 2, DH_B)
    eye_2 = jnp.eye(2, dtype=BF16)
    qbd = jnp.einsum('bqhmd,mM->bhmqMd', qs, eye_2).reshape(nb, h_b * 2 * ds, 2 * DH_B)
    cache_spec = lambda w: pl.BlockSpec((1, 1, kb, w), lambda b, j: (0, b, j, 0))
    head_rows = lambda a: a.reshape(a.shape[0], nb, past * h_b, 2 * DH_B)
    head_cache_spec = pl.BlockSpec((1, 1, kb * h_b, 2 * DH_B), lambda b, j: (0, b, j, 0))
    o_all = _dec_call(
        functools.partial(_diff_dec_body, past=past, kb=kb, ds=ds, n_heads=h_b),
        [_per_seq(qbd), (head_rows(cache_diff_k), head_cache_spec), (head_rows(cache_diff_v), head_cache_spec),
         _per_seq(k16[n_p:].reshape(nb, ds * h_b, 2 * DH_B)), _per_seq(v32h[n_p * h_b:].astype(BF16).reshape(nb, ds * h_b, 2 * DH_B)),
         _dec_const(subg)],
        o_all, n_p, ds, 2 * h_b * ds, 2 * DH_B, nb, n_kb, smem=par)

    x = _row_call(
        functools.partial(_ab_out_body, s5w=s5w),
        [x, o_all],
        [s5_glu_w[0].astype(BF16), s5_glu_b[0].reshape(1, s5w), ab_w_out[0].astype(BF16)],
        [dm], [F32], tm, group_ins=[y_t], scratch=[pltpu.VMEM((s5w // 128, tm, 128), F32)])[0]

    x = _ffn(x, ffn_norm[0, 1], ffn_w_in[0, 1], ffn_w_out[0, 1], tm)

    x = _ffn(x, ffn_norm[1, 0], ffn_w_in[1, 0], ffn_w_out[1, 0], tm)

    half = ROPE_D // 2
    inv = ROPE_THETA ** (-jnp.arange(half, dtype=F32) / half)
    pos = jnp.concatenate([jnp.tile(jnp.arange(lp, dtype=jnp.int32) - front, bsz),
                           jnp.tile(past + jnp.arange(ds, dtype=jnp.int32), nb)]).astype(F32)
    ang = pos[:, None] * inv[None, :]
    pad_r = HEAD_PAD - NOPE_D - ROPE_D
    cos_t = jnp.concatenate([jnp.ones((rows, NOPE_D), F32), jnp.cos(ang), jnp.cos(ang),
                             jnp.zeros((rows, pad_r), F32)], axis=1)
    sin_t = jnp.concatenate([jnp.zeros((rows, NOPE_D), F32), jnp.sin(ang), jnp.sin(ang),
                             jnp.zeros((rows, pad_r), F32)], axis=1)

    wcd = cd_w_in[0]
    c_fg = 3 * fox_w
    c_qa = c_fg + h_c
    c_kva = c_qa + q_lora
    c_pe = c_kva + kv_lora
    w_pe = wcd[:, c_pe:c_pe + ROPE_D]
    zc = lambda n: jnp.zeros((dm, n), F32)
    w_cd = jnp.concatenate([
        wcd[:, :3 * fox_w], wcd[:, c_qa:c_qa + q_lora], wcd[:, c_kva:c_kva + kv_lora],
        zc(NOPE_D), w_pe, zc(pad_r),
        zc(NOPE_D), -w_pe[:, half:], w_pe[:, :half], zc(pad_r),
        wcd[:, c_fg:c_fg + h_c], zc(HEAD_PAD - h_c)], axis=1).astype(BF16)
    qb = mla_q_b[0].reshape(q_lora, h_d, d_qk)
    zq = lambda n: jnp.zeros((q_lora, h_d, n), F32)
    qb_pad = jnp.concatenate([qb, zq(pad_r)], axis=-1).reshape(q_lora, h_d * HEAD_PAD)
    qb_rot = jnp.concatenate([zq(NOPE_D), -qb[..., NOPE_D + half:], qb[..., NOPE_D:NOPE_D + half], zq(pad_r)],
                             axis=-1).reshape(q_lora, h_d * HEAD_PAD)
    wq2 = jnp.concatenate([qb_pad, qb_rot], axis=1).astype(BF16)
    kvb = mla_kv_b[0].reshape(kv_lora, h_d, NOPE_D + V_D)
    wk_pad = jnp.concatenate([kvb[..., :NOPE_D], jnp.zeros((kv_lora, h_d, HEAD_PAD - NOPE_D), F32)],
                             axis=-1).reshape(kv_lora, h_d * HEAD_PAD).astype(BF16)
    wk_cmp = kvb[..., :NOPE_D].reshape(kv_lora, h_d * NOPE_D).astype(BF16)
    wv_cmp = kvb[..., NOPE_D:].reshape(kv_lora, h_d * V_D).astype(BF16)
    gfq = (jnp.tile(fox_q_norm[0], h_c) * (DH_C ** -0.5 * LOG2E)).reshape(1, fox_w)
    gfk = jnp.tile(fox_k_norm[0], h_c).reshape(1, fox_w)
    fbias = jnp.concatenate([fox_f_bias[0], jnp.zeros((HEAD_PAD - h_c,), F32)]).reshape(1, HEAD_PAD)
    gmq = jnp.tile(jnp.concatenate([mla_q_norm[0] * mla_k_norm[0] * (d_qk ** -0.5 * LOG2E), jnp.zeros((pad_r,), F32)]),
                   h_d).reshape(1, h_d * HEAD_PAD)

    (fq16, fk32, fk16, fv32, logf, qm16, ckv32, kpe32, km16, fvt, vmt) = _row_call(
        functools.partial(_cd_in_body, fox_w=fox_w, q_lora=q_lora, kv_lora=kv_lora, n_heads=h_d),
        [x, cos_t, sin_t],
        [mix_norm[1].reshape(1, dm), w_cd, gfq, gfk, fbias, mla_q_a_norm[0].reshape(1, q_lora), wq2,
         mla_kv_a_norm[0].reshape(1, kv_lora), wk_pad, wv_cmp.T, gmq, ones64, ones128],
        [fox_w, fox_w, fox_w, fox_w, h_c, h_d * HEAD_PAD, kv_lora, ROPE_D, h_d * HEAD_PAD],
        [BF16, F32, BF16, F32, F32, BF16, F32, F32, BF16], tm,
        col_outs=[(2 * fox_w, BF16), (2 * h_d * V_D, BF16)])

    logf_p = jnp.transpose(logf[:n_p].reshape(bsz, lp, h_c), (0, 2, 1)).reshape(bsz * h_c, lp)
    f_p = _cumsum_lanes(logf_p).reshape(bsz, h_c, lp)
    logf_s = jnp.concatenate([
        jnp.transpose(cache_fox_logf[0].astype(F32), (0, 2, 1)),
        jnp.transpose(logf[n_p:].reshape(nb, ds, h_c), (0, 2, 1)),
        jnp.zeros((nb, h_c, 256 - ds), F32)], axis=2).reshape(nb * h_c, past + 256)
    f_s = _cumsum_lanes(logf_s).reshape(nb, h_c, past + 256)

    kb_fox = jnp.where(kpad[None, None, :], NEG, -LOG2E * f_p).reshape(bsz, h_c // ATT_STREAMS, ATT_STREAMS, lp)
    oc_all = _prompt_attn("fox", fq16, fk16, fvt, kb_fox, bsz, lp)
    kb_mla = jnp.broadcast_to(jnp.where(kpad, NEG, 0.0).astype(F32)[None, None, None, :], (1, 1, ATT_STREAMS, lp))
    od_all = _prompt_attn("mla", qm16, km16, vmt, kb_mla, bsz, lp)

    eye_c = jnp.eye(h_c, dtype=BF16)
    fqs = fq16[n_p:].reshape(nb, ds, h_c, DH_C)
    fq_bd = jnp.einsum('bqhd,hH->bhqHd', fqs, eye_c).reshape(nb, h_c * ds, fox_w)
    fkc = cache_fox_k.reshape(cache_fox_k.shape[0], nb, past, fox_w)
    fvc = cache_fox_v.reshape(cache_fox_v.shape[0], nb, past, fox_w)
    oc_all = _dec_call(
        functools.partial(_fox_dec_body, ds=ds, n_heads=h_c),
        [_per_seq(fq_bd), (fkc, cache_spec(fox_w)), (fvc, cache_spec(fox_w)),
         _per_seq(fk16[n_p:].reshape(nb, ds, fox_w)), _per_seq(fv32[n_p:].reshape(nb, ds, fox_w)),
         (f_s, pl.BlockSpec((1, h_c, kb), lambda b, j: (b, 0, j))),
         (f_s, pl.BlockSpec((1, h_c, 128), lambda b, j: (b, 0, past // 128)))],
        oc_all, n_p, ds, h_c * ds, fox_w, nb, n_kb)

    eye_d = jnp.eye(h_d, dtype=BF16)
    qms = qm16[n_p:].reshape(nb, ds, h_d, HEAD_PAD)
    qn_bd = jnp.einsum('bqhd,hH->bhqHd', qms[..., :NOPE_D], eye_d).reshape(nb, h_d * ds, h_d * NOPE_D)
    qp_s = jnp.transpose(qms[..., NOPE_D:NOPE_D + ROPE_D], (0, 2, 1, 3)).reshape(nb, h_d * ds, ROPE_D)
    ones_h = jnp.repeat(jnp.eye(h_d, dtype=BF16), NOPE_D, axis=1)
    od_all = _dec_call(
        functools.partial(_mla_dec_body, ds=ds, n_heads=h_d),
        [_per_seq(qn_bd), _per_seq(qp_s),
         (cache_mla_ckv, pl.BlockSpec((1, 1, kb, kv_lora), lambda b, j: (0, b, j, 0))),
         (cache_mla_kpe, pl.BlockSpec((1, 1, kb, ROPE_D), lambda b, j: (0, b, j, 0))),
         _per_seq(ckv32[n_p:].reshape(nb, ds, kv_lora)), _per_seq(kpe32[n_p:].reshape(nb, ds, ROPE_D)),
         _dec_const(wk_cmp), _dec_const(wv_cmp), _dec_const(ones_h)],
        od_all, n_p, ds, h_d * ds, h_d * V_D, nb, n_kb)

    x = _row_call(functools.partial(_cd_out_body, fox_w=fox_w), [x, oc_all, od_all],
                  [cd_w_out[0].astype(BF16)], [dm], [F32], tm)[0]

    x = _ffn(x, ffn_norm[1, 1], ffn_w_in[1, 1], ffn_w_out[1, 1], tm)

    def p_rows(a, shape):
        return a[:n_p].reshape((bsz, lp) + a.shape[1:])[:, front:front + ltot].reshape((1, bsz, ltot) + shape)

    def s_rows(a, shape):
        return a[n_p:].reshape((1, nb, ds) + shape)

    def s5_state(st):
        st = jnp.transpose(st, (1, 0, 2))
        return st[None, :, :, :P_A], st[None, :, :, P_A:]

    y_prompt = jnp.stack([x[b * lp + front + n_meta:(b + 1) * lp] for b in range(bsz)])
    y_sample = x[n_p:].reshape(nb, ds, dm)
    s5_re_p, s5_im_p = s5_state(st_p)
    s5_re_s, s5_im_s = s5_state(st_s)
    return (y_prompt, y_sample,
            s5_re_p, s5_im_p, p_rows(k32, (h_b, 2 * DH_B)), p_rows(v32, (h_b, 2 * DH_B)),
            p_rows(fk32, (h_c, DH_C)), p_rows(fv32, (h_c, DH_C)), p_rows(logf, (h_c,)),
            p_rows(ckv32, (kv_lora,)), p_rows(kpe32, (ROPE_D,)),
            s5_re_s, s5_im_s, s_rows(k32, (h_b, 2 * DH_B)), s_rows(v32, (h_b, 2 * DH_B)),
            s_rows(fk32, (h_c, DH_C)), s_rows(fv32, (h_c, DH_C)), s_rows(logf, (h_c,)),
            s_rows(ckv32, (kv_lora,)), s_rows(kpe32, (ROPE_D,)))
```

```python
import functools
import math

import jax
import jax.numpy as jnp
from jax import lax
from jax.experimental import pallas as pl
from jax.experimental.pallas import tpu as pltpu

F32 = jnp.float32
BF16 = jnp.bfloat16

EPS = 1e-6
CHUNK = 64
ROW_ALIGN = 256
S5_GROUP = 16
S5_STEP = 16
P_A = 64
DH_B = 64
DH_C = 64
NOPE_D = 64
ROPE_D = 32
V_D = 64
HEAD_PAD = 128
ROPE_THETA = 10000.0
NEG = -1e30
LOG2E = math.log2(math.e)
VMEM_LIMIT = 56 * 1024 * 1024
ATT_BLOCK = 256
ATT_STREAMS = 4
ATT_ROWS = 64
DIFF_VT_ROWS = 2 * DH_B + 16
DEC_KB = 1024


def _dot(a, b):
    return jnp.dot(a, b, preferred_element_type=F32)


def _dot_nt(a, b):
    return lax.dot_general(a, b, (((1,), (1,)), ((), ())), preferred_element_type=F32)


def _rms_rows(x, g):
    ms = jnp.mean(x * x, axis=-1, keepdims=True)
    return x * lax.rsqrt(ms + EPS) * g


def _group_sumsq(x, ones_bd):
    w = x.shape[-1]
    parts = [_dot((x[:, c:c + 256] * x[:, c:c + 256]).astype(BF16), ones_bd) for c in range(0, w, 256)]
    return parts[0] if len(parts) == 1 else jnp.concatenate(parts, axis=1)


def _block_diag_ones(group, n=256):
    r = jnp.arange(n) // group
    return (r[:, None] == r[None, :]).astype(BF16)


def _const_spec(shape):
    nd = len(shape)
    return pl.BlockSpec(shape, lambda *_: (0,) * nd, pipeline_mode=pl.Buffered(1))


def _row_tile(rows, cap=512):
    t = cap
    while rows % t:
        t //= 2
    return t


def _group_spec(a, tm):
    return pl.BlockSpec((a.shape[0], tm // S5_STEP, a.shape[2]), lambda i: (0, i, 0))


def _row_call(body, row_ins, consts, out_widths, out_dtypes, tm, group_ins=(), group_outs=(), scratch=(),
              col_outs=(), head_outs=()):
    rows = row_ins[0].shape[0]
    in_specs = [pl.BlockSpec((tm, a.shape[1]), lambda i: (i, 0)) for a in row_ins]
    in_specs += [_group_spec(a, tm) for a in group_ins]
    in_specs += [_const_spec(c.shape) for c in consts]
    out_specs = [pl.BlockSpec((tm, w), lambda i: (i, 0)) for w in out_widths]
    out_specs += [_group_spec(a, tm) for a in group_outs]
    out_specs += [pl.BlockSpec((w, tm), lambda i: (0, i)) for w, _ in col_outs]
    out_shape = [jax.ShapeDtypeStruct((rows, w), d) for w, d in zip(out_widths, out_dtypes)]
    out_shape += list(group_outs)
    out_shape += [jax.ShapeDtypeStruct((w, rows), d) for w, d in col_outs]
    out_specs += [pl.BlockSpec((tm * h, 128), lambda i: (i, 0)) for h, _ in head_outs]
    out_shape += [jax.ShapeDtypeStruct((rows * h, 128), d) for h, d in head_outs]
    return pl.pallas_call(
        body,
        grid=(rows // tm,),
        in_specs=in_specs,
        out_specs=out_specs,
        out_shape=out_shape,
        scratch_shapes=list(scratch),
        compiler_params=pltpu.CompilerParams(
            dimension_semantics=("parallel",), vmem_limit_bytes=VMEM_LIMIT),
    )(*row_ins, *group_ins, *consts)


def _ffn_body(x_ref, g_ref, win_ref, wout_ref, o_ref, *, d_ff, tf):
    x = x_ref[...]
    xn = _rms_rows(x, g_ref[...]).astype(BF16)
    acc = jnp.zeros(x.shape, F32)
    for c in range(0, d_ff, tf):
        gate = _dot(xn, win_ref[:, c:c + tf])
        up = _dot(xn, win_ref[:, d_ff + c:d_ff + c + tf])
        a = (gate * jax.nn.sigmoid(gate) * up).astype(BF16)
        acc = acc + _dot(a, wout_ref[c:c + tf, :])
    o_ref[...] = x + 0.5 * acc


def _ffn(x, g, w_in, w_out, tm):
    d_ff = w_out.shape[0]
    body = functools.partial(_ffn_body, d_ff=d_ff, tf=256)
    return _row_call(body, [x], [g.reshape(1, -1), w_in.astype(BF16), w_out.astype(BF16)],
                     [x.shape[1]], [F32], tm)[0]


def _ab_in_body(x_ref, g_ref, w_ref, gq_ref, gk_ref, ones_ref,
                q_ref, k16_ref, ut_ref, vt_ref, k32_ref, v32_ref, us_ref, *, widths):
    s5w, qkw = widths
    xn = _rms_rows(x_ref[...], g_ref[...]).astype(BF16)
    h = _dot(xn, w_ref[...])
    n_chunk = us_ref.shape[1] // S5_STEP
    per_col = 128 // S5_GROUP
    for v in range(s5w // 128):
        us_ref[v] = h[:, v * 128:(v + 1) * 128]
        steps = [us_ref[v, pl.ds(t, n_chunk, stride=S5_STEP), :] for t in range(S5_STEP)]
        for gl in range(per_col):
            ut_ref[v * per_col + gl] = jnp.concatenate(
                [x[:, gl * S5_GROUP:(gl + 1) * S5_GROUP] for x in steps], axis=1).astype(BF16)
    q = h[:, s5w:s5w + qkw]
    k = h[:, s5w + qkw:s5w + 2 * qkw]
    v = h[:, s5w + 2 * qkw:]
    ones_bd = ones_ref[...]
    qn = q * lax.rsqrt(_group_sumsq(q, ones_bd) * (1.0 / DH_B) + EPS) * gq_ref[...]
    kn = k * lax.rsqrt(_group_sumsq(k, ones_bd) * (1.0 / DH_B) + EPS) * gk_ref[...]
    q_ref[...] = qn.astype(BF16)
    k16_ref[...] = kn.astype(BF16)
    dv = 2 * DH_B
    n_head = qkw // dv
    tm = kn.shape[0]
    for hd in range(n_head):
        k32_ref[pl.ds(hd, tm, stride=n_head), :] = kn[:, hd * dv:(hd + 1) * dv]
        v32_ref[pl.ds(hd, tm, stride=n_head), :] = v[:, hd * dv:(hd + 1) * dv]
    vt = v.T
    ones = jnp.ones((DIFF_VT_ROWS - dv, vt.shape[1]), F32)
    vt_ref[...] = jnp.concatenate(
        [a for h in range(vt.shape[0] // dv) for a in (vt[h * dv:(h + 1) * dv], ones)], axis=0).astype(BF16)


def _ab_out_body(x_ref, o_ref, yt_ref, gluw_ref, glub_ref, wout_ref, out_ref, ys_ref, *, s5w):
    n_chunk = ys_ref.shape[1] // S5_STEP
    per_col = 128 // S5_GROUP
    for v in range(s5w // 128):
        for t in range(S5_STEP):
            ys_ref[v, pl.ds(t, n_chunk, stride=S5_STEP), :] = jnp.concatenate(
                [yt_ref[v * per_col + gl][:, t * S5_GROUP:(t + 1) * S5_GROUP] for gl in range(per_col)], axis=1)
    y = jnp.concatenate([ys_ref[v] for v in range(s5w // 128)], axis=1)
    g = 0.5 * y * (1.0 + jnp.tanh(math.sqrt(2.0 / math.pi) * (y + 0.044715 * (y * y * y))))
    z = _dot(g.astype(BF16), gluw_ref[...]) + glub_ref[...]
    s5o = g * jax.nn.sigmoid(z)
    m = _dot(s5o.astype(BF16), wout_ref[:s5w, :]) + _dot(o_ref[...], wout_ref[s5w:, :])
    out_ref[...] = x_ref[...] + m


def _heads_with_ones_t(vt):
    ones = jnp.ones((64, vt.shape[1]), vt.dtype)
    outs = []
    for h in range(vt.shape[0] // 64):
        outs += [vt[h * 64:(h + 1) * 64], ones]
    return jnp.concatenate(outs, axis=0)


def _cd_in_body(x_ref, cos_ref, sin_ref, g_ref, w_ref, gfq_ref, gfk_ref, fb_ref, gqa_ref, wq2_ref,
                gkva_ref, wk_ref, wv_ref, gmq_ref, ones64_ref, ones128_ref,
                fq_ref, fk32_ref, fk16_ref, fv32_ref, logf_ref, qm_ref, ckv_ref, kpe_ref,
                km_ref, fvt_ref, vmt_ref, *, fox_w, q_lora, kv_lora, n_heads):
    xn = _rms_rows(x_ref[...], g_ref[...]).astype(BF16)
    h = _dot(xn, w_ref[...])
    ones64 = ones64_ref[...]
    ones128 = ones128_ref[...]
    fq = h[:, :fox_w]
    fk = h[:, fox_w:2 * fox_w]
    fv = h[:, 2 * fox_w:3 * fox_w]
    c0 = 3 * fox_w
    qa = h[:, c0:c0 + q_lora]
    kva = h[:, c0 + q_lora:c0 + q_lora + kv_lora]
    c1 = c0 + q_lora + kv_lora
    pe_a = h[:, c1:c1 + HEAD_PAD]
    pe_b = h[:, c1 + HEAD_PAD:c1 + 2 * HEAD_PAD]
    fg = h[:, c1 + 2 * HEAD_PAD:c1 + 3 * HEAD_PAD]

    fqn = fq * lax.rsqrt(_group_sumsq(fq, ones64) * (1.0 / DH_C) + EPS) * gfq_ref[...]
    fkn = fk * lax.rsqrt(_group_sumsq(fk, ones64) * (1.0 / DH_C) + EPS) * gfk_ref[...]
    fq_ref[...] = fqn.astype(BF16)
    fk32_ref[...] = fkn
    fk16_ref[...] = fkn.astype(BF16)
    fv32_ref[...] = fv
    fvt_ref[...] = _heads_with_ones_t(fv.T).astype(BF16)

    z = fg + fb_ref[...]
    logf = jnp.minimum(z, 0.0) - jnp.log1p(jnp.exp(-jnp.abs(z)))
    logf_ref[...] = logf[:, :logf_ref.shape[1]]

    cos = cos_ref[...]
    sin = sin_ref[...]
    qan = _rms_rows(qa, gqa_ref[...]).astype(BF16)
    q2 = _dot(qan, wq2_ref[...])
    hw = n_heads * HEAD_PAD
    cos_t = jnp.concatenate([cos] * n_heads, axis=1)
    sin_t = jnp.concatenate([sin] * n_heads, axis=1)
    qr = q2[:, :hw] * cos_t + q2[:, hw:] * sin_t
    d_qk = NOPE_D + ROPE_D
    qm = qr * lax.rsqrt(_group_sumsq(qr, ones128) * (1.0 / d_qk) + EPS) * gmq_ref[...]
    qm_ref[...] = qm.astype(BF16)

    ckv = _rms_rows(kva, gkva_ref[...])
    ckv_ref[...] = ckv
    pe = pe_a * cos + pe_b * sin
    kpe_ref[...] = pe[:, NOPE_D:NOPE_D + ROPE_D]
    ckv16 = ckv.astype(BF16)
    kraw = _dot(ckv16, wk_ref[...]) + jnp.concatenate([pe] * n_heads, axis=1)
    km = kraw * lax.rsqrt(_group_sumsq(kraw, ones128) * (1.0 / d_qk) + EPS)
    km_ref[...] = km.astype(BF16)
    vmt_ref[...] = _heads_with_ones_t(_dot_nt(wv_ref[...], ckv16)).astype(BF16)


def _cd_out_body(x_ref, oc_ref, od_ref, wout_ref, out_ref, *, fox_w):
    m = _dot(oc_ref[...], wout_ref[:fox_w, :]) + _dot(od_ref[...], wout_ref[fox_w:, :])
    out_ref[...] = x_ref[...] + m


def _s5_body(*refs, n_chunks, bsz, aliased):
    if aliased:
        u_ref, h0_ref, m_ref, bm_ref, cm_ref, coef_ref, _, y_ref, st_ref, s2_ref, hp_ref = refs
    else:
        u_ref, h0_ref, m_ref, bm_ref, cm_ref, coef_ref, y_ref, st_ref, s2_ref, hp_ref = refs
    u = u_ref[0]
    half = 2 * P_A
    s2 = _dot(u, bm_ref[0])
    s2_ref[0] = s2[:, :half]
    s2_ref[1] = s2[:, half:]
    c1 = coef_ref[0, 0:1, :]
    c2 = coef_ref[0, 1:2, :]
    c3 = coef_ref[0, 2:3, :]

    def step(j, carry):
        ha, hb = carry
        hp_ref[pl.ds(j, bsz, stride=n_chunks), :] = ha
        sa = s2_ref[0, pl.ds(j, bsz, stride=n_chunks), :]
        sb = s2_ref[1, pl.ds(j, bsz, stride=n_chunks), :]
        return ha * c1 + hb * c2 + sa, hb * c1 + ha * c3 + sb

    h0 = h0_ref[0]
    ha, _ = lax.fori_loop(0, n_chunks, step, (h0[:, :half], h0[:, half:]))
    st_ref[0] = ha
    y_ref[0] = _dot(u, m_ref[0]) + _dot(hp_ref[...].astype(BF16), cm_ref[0])


def _s5_scan(u_t, h0, mats, n_chunks, bsz, row0, y_prev=None):
    m_mat, bm, cm, coef = mats
    g, rows_all, w = u_t.shape
    rows = n_chunks * bsz
    assert row0 % rows == 0
    blk = row0 // rows
    aliased = y_prev is not None
    body = functools.partial(_s5_body, n_chunks=n_chunks, bsz=bsz, aliased=aliased)
    per_g = lambda a: pl.BlockSpec((1,) + a.shape[1:], lambda i: (i, 0, 0))
    in_specs = [pl.BlockSpec((1, rows, w), lambda i: (i, blk, 0)),
                per_g(h0), per_g(m_mat), per_g(bm), per_g(cm), per_g(coef)]
    args = [u_t, h0, m_mat, bm, cm, coef]
    if aliased:
        in_specs.append(pl.BlockSpec(memory_space=pl.ANY))
        args.append(y_prev)
    return pl.pallas_call(
        body,
        grid=(g,),
        in_specs=in_specs,
        out_specs=[pl.BlockSpec((1, rows, w), lambda i: (i, blk, 0)),
                   pl.BlockSpec((1, bsz, 2 * P_A), lambda i: (i, 0, 0))],
        out_shape=[jax.ShapeDtypeStruct((g, rows_all, w), F32),
                   jax.ShapeDtypeStruct((g, bsz, 2 * P_A), F32)],
        input_output_aliases={6: 0} if aliased else {},
        scratch_shapes=[pltpu.VMEM((2, rows, 2 * P_A), F32), pltpu.VMEM((rows, 2 * P_A), F32)],
        compiler_params=pltpu.CompilerParams(
            dimension_semantics=("parallel",), vmem_limit_bytes=VMEM_LIMIT),
    )(*args)


def _s5_matrices(a_re, a_im, log_step, b_re, b_im, c_re, c_im, d):
    g = a_re.shape[0]
    t = S5_STEP
    lam = lax.complex(a_re, a_im)
    dl = lam * jnp.exp(log_step)[:, None]
    lam_bar = jnp.exp(dl)
    b_bar = ((lam_bar - 1.0) / lam)[..., None] * lax.complex(b_re, b_im)
    c = lax.complex(c_re, c_im)
    pw = jnp.exp(dl[:, None, :] * jnp.arange(t + 1, dtype=F32)[None, :, None])
    bmc = pw[:, t - 1::-1][:, :, :, None] * b_bar[:, None]
    bmc = jnp.swapaxes(bmc, 2, 3).reshape(g, t * S5_GROUP, P_A)
    bm = jnp.concatenate([bmc.real, bmc.imag, bmc.imag, bmc.real], axis=-1)
    kk = jnp.einsum('gcp,gkp,gpd->gkcd', c, pw[:, :t], b_bar).real
    kk = kk.at[:, 0].add(d.reshape(g, S5_GROUP)[:, :, None] * jnp.eye(S5_GROUP, dtype=F32))
    lag = jnp.arange(t)[None, :] - jnp.arange(t)[:, None]
    toep = jnp.where((lag >= 0)[None, :, :, None, None], kk[:, jnp.clip(lag, 0, t - 1)], 0.0)
    m_mat = jnp.transpose(toep, (0, 1, 4, 2, 3)).reshape(g, t * S5_GROUP, t * S5_GROUP)
    cp = c[:, None] * pw[:, 1:, None, :]
    cpm = jnp.transpose(cp, (0, 3, 1, 2)).reshape(g, P_A, t * S5_GROUP)
    cm = jnp.concatenate([cpm.real, -cpm.imag], axis=1)
    a_t = pw[:, t]
    ar, ai = a_t.real, a_t.imag
    zeros = jnp.zeros_like(ar)
    coef = jnp.stack([jnp.concatenate([ar, ar], -1), jnp.concatenate([-ai, ai], -1),
                      jnp.concatenate([ai, -ai], -1), jnp.concatenate([zeros, zeros], -1)], axis=1)
    return m_mat.astype(BF16), bm.astype(BF16), cm.astype(BF16), coef.astype(F32)


def _online(logits, vt, e, m_ref, acc_ref, p_ref, block_max=None):
    tk = p_ref.shape[1]
    if block_max is None:
        part = logits(0, ATT_ROWS)
        for r0 in range(ATT_ROWS, tk, ATT_ROWS):
            part = jnp.maximum(part, logits(r0, ATT_ROWS))
    else:
        part = block_max
    m_prev = m_ref[e]
    m_new = jnp.maximum(m_prev, jnp.max(part, axis=0, keepdims=True))
    alpha = jnp.exp2(m_prev - m_new)
    m_ref[e] = m_new
    for r0 in range(0, tk, ATT_ROWS):
        p_ref[e, r0:r0 + ATT_ROWS] = jnp.exp2(logits(r0, ATT_ROWS) - m_new).astype(BF16)
    acc_ref[e] = alpha * acc_ref[e] + _dot(vt, p_ref[e])


def _attn_query_block(i, hg, refs, kind, tq, ns):
    if kind == "diff":
        (par_ref, q_ref, k_ref, vt_ref, kb_ref, g_ref, o_ref,
         m_ref, acc_ref, s_ref, p_ref, kbc_ref, mx_ref) = refs
    else:
        q_ref, k_ref, vt_ref, kb_ref, o_ref, m_ref, acc_ref, s_ref, p_ref, kbc_ref, mx_ref = refs
    vrows = acc_ref.shape[1]
    m_ref[...] = jnp.full(m_ref.shape, NEG, F32)
    acc_ref[...] = jnp.zeros(acc_ref.shape, F32)

    qstart = pl.multiple_of(i * tq, tq)
    q = q_ref[0, pl.ds(qstart, tq), :]
    lane = lax.broadcasted_iota(jnp.int32, (1, 128), 1)
    qs = []
    for e in range(ns):
        if kind == "mla":
            qs.append(q[:, e * HEAD_PAD:(e + 1) * HEAD_PAD])
        else:
            qp = q[:, (e // 2) * 128:(e // 2 + 1) * 128]
            qs.append(jnp.where((lane < 64) if e % 2 == 0 else (lane >= 64), qp, jnp.zeros_like(qp)))
    lane_q = lax.broadcasted_iota(jnp.int32, (1, tq), 1)
    ref = [-kb_ref[0, 0, e:e + 1, pl.ds(qstart + (tq - 128), 128)][:, 127:128] for e in range(ns)]
    if kind == "diff":
        slope = [LOG2E * par_ref[1 + hg * (ns // 2) + p] for p in range(ns // 2)]

    tk = tq

    def k_slot(k, e):
        if kind == "mla":
            return k[:, e * HEAD_PAD:(e + 1) * HEAD_PAD]
        return k[:, (e // 2) * 128:(e // 2 + 1) * 128]

    def vt_slot(vt, e):
        r0 = (e // 2 if kind == "diff" else e) * vrows
        return vt[r0:r0 + vrows]

    def scores(j, slot, first=False):
        k0 = pl.multiple_of(j * tk, tk)
        k = k_ref[0, pl.ds(k0, tk), :]
        for e in range(ns):
            s = _dot_nt(k_slot(k, e), qs[e])
            if first or kind != "mla":
                bias = kbc_ref[e, pl.ds(k0, tk), :] + ref[e]
                s = s + jnp.concatenate([bias] * (tq // 128), axis=1)
            s_ref[slot, e] = s
            part = s[0:8]
            for r0 in range(8, tk, 8):
                part = jnp.maximum(part, s[r0:r0 + 8])
            mx_ref[slot, e] = part

    def softmax_pv(j, slot, diag):
        k0 = pl.multiple_of(j * tk, tk)
        vt = vt_ref[:, pl.ds(k0, tk)]
        for e in range(ns):
            def logits(r0, n, e=e):
                s = s_ref[slot, e, r0:r0 + n, :]
                if diag:
                    row_i = r0 + lax.broadcasted_iota(jnp.int32, (n, 1), 0)
                    if kind == "diff":
                        s = s - (2.0 * slope[e // 2]) * jnp.maximum(row_i - lane_q, 0).astype(F32)
                    if kind == "fox":
                        s = jnp.where(row_i <= lane_q, s, NEG)
                    elif r0 > 0:
                        s = jnp.where(lane_q >= r0, s, NEG)
                return s

            _online(logits, vt_slot(vt, e), e, m_ref, acc_ref, p_ref,
                    block_max=None if diag else mx_ref[slot, e])

    scores(0, 0, first=True)

    def pair_body(jj, c):
        j = 2 * jj
        scores(j + 1, 1)
        softmax_pv(j, 0, False)
        scores(j + 2, 0)
        softmax_pv(j + 1, 1, False)
        return c

    lax.fori_loop(0, i // 2, pair_body, 0)

    @pl.when(i % 2 == 0)
    def _():
        softmax_pv(i, 0, True)

    @pl.when(i % 2 == 1)
    def _():
        scores(i, 1)
        softmax_pv(i - 1, 0, False)
        softmax_pv(i, 1, True)

    outs = []
    for p in range(ns // 2):
        e0, e1 = 2 * p, 2 * p + 1
        if kind == "diff":
            a0, a1 = acc_ref[e0], acc_ref[e1]
            dv = 2 * DH_B
            o = (a0[:dv] / a0[dv:dv + 1] - par_ref[0] * (a1[:dv] / a1[dv:dv + 1])).T
            ms = jnp.mean(o * o, axis=-1, keepdims=True)
            outs.append(o * lax.rsqrt(ms + EPS) * g_ref[:, p * 128:(p + 1) * 128])
        else:
            a0, a1 = acc_ref[e0], acc_ref[e1]
            outs.append(jnp.concatenate([a0[:64] / a0[64:], a1[:64] / a1[64:]], axis=0).T)
    o_ref[0, pl.ds(qstart, tq), :] = (
        outs[0] if len(outs) == 1 else jnp.concatenate(outs, axis=1)).astype(o_ref.dtype)


def _prompt_attn_body(*refs, kind, tq, ns, nq):
    kb_ref, kbc_ref = (refs[4], refs[11]) if kind == "diff" else (refs[3], refs[9])
    def fill_bias_columns():
        def fill(c, carry):
            c0 = pl.multiple_of(c * 128, 128)
            for e in range(ns):
                row = kb_ref[0, 0, e:e + 1, pl.ds(c0, 128)]
                kbc_ref[e, pl.ds(c0, 128), :] = jnp.broadcast_to(row, (128, 128)).T
            return carry

        lax.fori_loop(0, kbc_ref.shape[1] // 128, fill, 0)

    if kind == "fox":
        hg = pl.program_id(1)
        fill_bias_columns()
    else:
        hg = pl.program_id(0)
        pl.when(pl.program_id(1) == 0)(fill_bias_columns)

    def query_block(i, carry):
        _attn_query_block(i, hg, refs, kind, tq, ns)
        return carry

    lax.fori_loop(0, nq, query_block, 0)


def _prompt_attn(kind, q, k, vt, kb, bsz, lp, par=None, gain=None):
    rows = q.shape[0]
    tq = ATT_BLOCK
    ns = ATT_STREAMS
    wq = (ns // 2) * (2 * HEAD_PAD if kind == "mla" else 128)
    vrows = DIFF_VT_ROWS if kind == "diff" else 128
    wv = (ns // 2) * (vrows if kind == "diff" else 2 * vrows)
    wo = (ns // 2) * 128
    n_hg = vt.shape[0] // wv
    nq = lp // tq
    kb_b, kb_h = kb.shape[0] > 1, kb.shape[1] > 1
    seq_major = kind == "fox"

    def bh(f):
        return (lambda b, h: f(b, h)) if seq_major else (lambda h, b: f(b, h))

    in_specs = [pl.BlockSpec((1, lp, wq), bh(lambda b, h: (0, b, h))),
                pl.BlockSpec((1, lp, wq), bh(lambda b, h: (0, b, h))),
                pl.BlockSpec((wv, lp), bh(lambda b, h: (h, b))),
                pl.BlockSpec((1, 1, ns, lp), bh(lambda b, h: (b if kb_b else 0, h if kb_h else 0, 0, 0)))]
    args = [q[None], k[None], vt, kb]
    if kind == "diff":
        in_specs = ([pl.BlockSpec(memory_space=pltpu.SMEM)] + in_specs
                    + [pl.BlockSpec((1, wo), bh(lambda b, h: (0, h)))])
        args = [par] + args + [gain]
    assert ATT_ROWS == CHUNK
    body = functools.partial(_prompt_attn_body, kind=kind, tq=tq, ns=ns, nq=nq)
    return pl.pallas_call(
        body,
        grid=(bsz, n_hg) if seq_major else (n_hg, bsz),
        in_specs=in_specs,
        out_specs=pl.BlockSpec((1, lp, wo), bh(lambda b, h: (0, b, h))),
        out_shape=jax.ShapeDtypeStruct((1, rows, n_hg * wo), BF16),
        scratch_shapes=[pltpu.VMEM((ns, 1, tq), F32),
                        pltpu.VMEM((ns, vrows, tq), F32), pltpu.VMEM((2, ns, tq, tq), F32),
                        pltpu.VMEM((ns, tq, tq), BF16), pltpu.VMEM((ns, lp, 128), F32),
                        pltpu.VMEM((2, ns, 8, tq), F32)],
        compiler_params=pltpu.CompilerParams(
            dimension_semantics=("arbitrary", "arbitrary"), vmem_limit_bytes=VMEM_LIMIT),
    )(*args)[0]


def _cumsum_body(x_ref, tri_ref, o_ref, carry_ref):
    @pl.when(pl.program_id(0) == 0)
    def _():
        carry_ref[...] = jnp.zeros(carry_ref.shape, F32)

    y = jnp.dot(x_ref[...], tri_ref[...], preferred_element_type=F32,
                precision=lax.Precision.HIGHEST) + carry_ref[...]
    o_ref[...] = y
    carry_ref[...] = y[:, -1:]


def _cumsum_lanes(x, blk=256):
    rows, n = x.shape
    tri = (jnp.arange(blk)[:, None] <= jnp.arange(blk)[None, :]).astype(F32)
    return pl.pallas_call(
        _cumsum_body,
        grid=(n // blk,),
        in_specs=[pl.BlockSpec((rows, blk), lambda j: (0, j)), _const_spec((blk, blk))],
        out_specs=pl.BlockSpec((rows, blk), lambda j: (0, j)),
        out_shape=jax.ShapeDtypeStruct((rows, n), F32),
        scratch_shapes=[pltpu.VMEM((rows, 1), F32)],
        compiler_params=pltpu.CompilerParams(dimension_semantics=("arbitrary",)),
    )(x, tri)


def _expand_rows(x, rep):
    h, n = x.shape
    return jnp.broadcast_to(x[:, None, :], (h, rep, n)).reshape(h * rep, n)


def _dec_online(s, v16, m_ref, l_ref, acc_ref):
    m_prev = m_ref[...]
    m_new = jnp.maximum(m_prev, jnp.max(s, axis=-1, keepdims=True))
    alpha = jnp.exp2(m_prev - m_new)
    p = jnp.exp2(s - m_new)
    l_ref[...] = alpha * l_ref[...] + jnp.sum(p, axis=-1, keepdims=True)
    acc_ref[...] = alpha * acc_ref[...] + _dot(p.astype(BF16), v16)
    m_ref[...] = m_new


def _dec_init(m_ref, l_ref, acc_ref):
    m_ref[...] = jnp.full(m_ref.shape, NEG, F32)
    l_ref[...] = jnp.zeros(l_ref.shape, F32)
    acc_ref[...] = jnp.zeros(acc_ref.shape, F32)


def _diag_blocks(o, n_heads, ds, width):
    return jnp.concatenate([o[h * ds:(h + 1) * ds, h * width:(h + 1) * width] for h in range(n_heads)], axis=1)


def _diff_dec_body(par_ref, q_ref, kc_ref, vc_ref, kn_ref, vn_ref, g_ref, _, o_ref, m_ref, l_ref, acc_ref,
                   *, past, kb, ds, n_heads):
    jb = pl.program_id(1)
    hr = 2 * ds
    rows = n_heads * hr
    dv = 2 * DH_B
    r = lax.broadcasted_iota(jnp.int32, (rows, 1), 0)
    head = r // hr
    slope = LOG2E * jnp.exp2(-8.0 * (head + 1).astype(F32) / n_heads)
    qpos = past + (r % ds)
    q = q_ref[0]

    def key_block(k16, v16, key0):
        col = lax.broadcasted_iota(jnp.int32, (1, k16.shape[0]), 1)
        kpos = key0 + col // n_heads
        s = _dot_nt(q, k16) - slope * jnp.abs(qpos - kpos).astype(F32)
        s = jnp.where(col % n_heads == head, s, NEG)
        _dec_online(s, v16, m_ref, l_ref, acc_ref)

    @pl.when(jb == 0)
    def _():
        _dec_init(m_ref, l_ref, acc_ref)
        key_block(kn_ref[0], vn_ref[0], past)

    key_block(kc_ref[0, 0].astype(BF16), vc_ref[0, 0].astype(BF16), jb * kb)

    @pl.when(jb == pl.num_programs(1) - 1)
    def _():
        o = acc_ref[...] / l_ref[...]
        outs = []
        for h in range(n_heads):
            oh = o[h * hr:h * hr + ds] - par_ref[0] * o[h * hr + ds:(h + 1) * hr]
            ms = jnp.mean(oh * oh, axis=-1, keepdims=True)
            outs.append(oh * lax.rsqrt(ms + EPS) * g_ref[:, h * dv:(h + 1) * dv])
        o_ref[0] = jnp.concatenate(outs, axis=1).astype(o_ref.dtype)


def _fox_dec_body(q_ref, kc_ref, vc_ref, kn_ref, vn_ref, fc_ref, fn_ref, _, o_ref, m_ref, l_ref, acc_ref,
                  *, ds, n_heads):
    jb = pl.program_id(1)
    rows = n_heads * ds
    q = q_ref[0]
    fnew = fn_ref[0][:, :ds]
    fref = _expand_rows(fn_ref[0][:, 0:1], ds)

    @pl.when(jb == 0)
    def _():
        _dec_init(m_ref, l_ref, acc_ref)
        r = lax.broadcasted_iota(jnp.int32, (rows, 1), 0)
        kidx = lax.broadcasted_iota(jnp.int32, (1, ds), 1)
        s = _dot_nt(q, kn_ref[0]) + LOG2E * (fref - _expand_rows(fnew, ds))
        s = jnp.where(kidx <= (r % ds), s, NEG)
        _dec_online(s, vn_ref[0].astype(BF16), m_ref, l_ref, acc_ref)

    s = _dot_nt(q, kc_ref[0, 0].astype(BF16)) + LOG2E * (fref - _expand_rows(fc_ref[0], ds))
    _dec_online(s, vc_ref[0, 0].astype(BF16), m_ref, l_ref, acc_ref)

    @pl.when(jb == pl.num_programs(1) - 1)
    def _():
        o = acc_ref[...] / l_ref[...]
        o_ref[0] = _diag_blocks(o, n_heads, ds, DH_C).astype(o_ref.dtype)


def _mla_dec_body(qn_ref, qp_ref, cc_ref, pc_ref, cn_ref, pn_ref, wk_ref, wv_ref, ones_ref, _,
                  o_ref, m_ref, l_ref, acc_ref, *, ds, n_heads):
    jb = pl.program_id(1)
    qn = qn_ref[0]
    qp = qp_ref[0]
    ones_h = ones_ref[...]

    def key_block(ckv, kpe):
        c16 = ckv.astype(BF16)
        kn = _dot(c16, wk_ref[...])
        v = _dot(c16, wv_ref[...])
        n = kpe.shape[0]
        ss = _dot_nt(ones_h, (kn * kn).astype(BF16)) + _dot_nt(jnp.ones((n_heads, ROPE_D), BF16),
                                                               (kpe * kpe).astype(BF16))
        rinv = lax.rsqrt(ss * (1.0 / (NOPE_D + ROPE_D)) + EPS)
        s = _dot_nt(qn, kn.astype(BF16)) + _dot_nt(qp, kpe.astype(BF16))
        s = s * _expand_rows(rinv, ds)
        _dec_online(s, v.astype(BF16), m_ref, l_ref, acc_ref)

    @pl.when(jb == 0)
    def _():
        _dec_init(m_ref, l_ref, acc_ref)
        key_block(cn_ref[0], pn_ref[0])

    key_block(cc_ref[0, 0], pc_ref[0, 0])

    @pl.when(jb == pl.num_programs(1) - 1)
    def _():
        o = acc_ref[...] / l_ref[...]
        o_ref[0] = _diag_blocks(o, n_heads, ds, V_D).astype(o_ref.dtype)


def _per_seq(a):
    return (a, pl.BlockSpec((1,) + a.shape[1:], lambda b, j: (b, 0, 0)))


def _dec_const(a):
    return (a, _const_spec(a.shape))


def _dec_call(body, ins, prev, row0, ds, rows, acc_w, nb, n_kb, smem=None):
    in_specs = [spec for _, spec in ins] + [pl.BlockSpec(memory_space=pl.ANY)]
    args = [a for a, _ in ins] + [prev[None]]
    if smem is not None:
        in_specs = [pl.BlockSpec(memory_space=pltpu.SMEM)] + in_specs
        args = [smem] + args
    blk0 = row0 // ds
    return pl.pallas_call(
        body,
        grid=(nb, n_kb),
        in_specs=in_specs,
        out_specs=pl.BlockSpec((1, ds, prev.shape[1]), lambda b, j: (0, blk0 + b, 0)),
        out_shape=jax.ShapeDtypeStruct((1,) + prev.shape, prev.dtype),
        input_output_aliases={len(args) - 1: 0},
        scratch_shapes=[pltpu.VMEM((rows, 1), F32), pltpu.VMEM((rows, 1), F32), pltpu.VMEM((rows, acc_w), F32)],
        compiler_params=pltpu.CompilerParams(
            dimension_semantics=("parallel", "arbitrary"), vmem_limit_bytes=VMEM_LIMIT),
    )(*args)[0]


def kernel(x_prompt, x_sample, state_s5_re, state_s5_im, cache_diff_k, cache_diff_v, cache_fox_k, cache_fox_v, cache_fox_logf, cache_mla_ckv, cache_mla_kpe, meta_tokens, ffn_norm, ffn_w_in, ffn_w_out, mix_norm, ab_w_in, ab_w_out, s5_a_re, s5_a_im, s5_log_step, s5_b_re, s5_b_im, s5_c_re, s5_c_im, s5_d, s5_glu_w, s5_glu_b, diff_q_norm, diff_k_norm, diff_lam, diff_sub_norm, cd_w_in, cd_w_out, fox_q_norm, fox_k_norm, fox_f_bias, mla_q_a_norm, mla_q_b, mla_kv_a_norm, mla_kv_b, mla_q_norm, mla_k_norm):
    bsz, seq, dm = x_prompt.shape
    nb, ds, _ = x_sample.shape
    n_meta = meta_tokens.shape[0]
    past = cache_diff_k.shape[2]
    front = ROW_ALIGN - n_meta
    lp = front + n_meta + seq
    ltot = n_meta + seq
    assert n_meta + front == ROW_ALIGN and lp % ATT_BLOCK == 0 and front % CHUNK == CHUNK - n_meta
    assert ds == S5_STEP and past % CHUNK == 0 and ds <= CHUNK
    kb = min(DEC_KB, past)
    assert past % kb == 0
    n_kb = past // kb
    assert ffn_norm.shape[0] == 2 and ab_w_in.shape[0] == 1 and cd_w_in.shape[0] == 1

    h_b = cache_diff_k.shape[3]
    h_c = cache_fox_k.shape[3]
    h_d = mla_q_b.shape[2] // (NOPE_D + ROPE_D)
    s5w = s5_glu_w.shape[1]
    n_grp = s5w // S5_GROUP
    qkw = h_b * 2 * DH_B
    fox_w = h_c * DH_C
    q_lora = mla_q_a_norm.shape[1]
    kv_lora = mla_kv_a_norm.shape[1]
    d_qk = NOPE_D + ROPE_D

    n_p = bsz * lp
    head_rows_x = jnp.concatenate([jnp.zeros((front, dm), F32), meta_tokens.astype(F32)], axis=0)
    pieces = []
    for b in range(bsz):
        pieces += [head_rows_x, x_prompt[b]]
    x = jnp.concatenate(pieces + [x_sample.reshape(nb * ds, dm)], axis=0)
    rows = x.shape[0]
    tm = _row_tile(rows)

    ones64 = _block_diag_ones(64)
    ones128 = _block_diag_ones(128)

    x = _ffn(x, ffn_norm[0, 0], ffn_w_in[0, 0], ffn_w_out[0, 0], tm)

    gq = (jnp.tile(diff_q_norm[0], 2 * h_b) * (DH_B ** -0.5 * LOG2E)).reshape(1, qkw)
    gk = jnp.tile(diff_k_norm[0], 2 * h_b).reshape(1, qkw)
    s5_cols = S5_STEP * S5_GROUP
    q16, k16, u_t, vt_diff, k32h, v32h = _row_call(
        functools.partial(_ab_in_body, widths=(s5w, qkw)),
        [x], [mix_norm[0].reshape(1, dm), ab_w_in[0].astype(BF16), gq, gk, ones64],
        [qkw, qkw], [BF16, BF16], tm,
        group_outs=[jax.ShapeDtypeStruct((n_grp, rows // S5_STEP, s5_cols), BF16)],
        scratch=[pltpu.VMEM((s5w // 128, tm, 128), F32)],
        col_outs=[(h_b * DIFF_VT_ROWS, BF16)], head_outs=[(h_b, F32), (h_b, F32)])
    k32 = k32h.reshape(rows, h_b, 2 * DH_B)
    v32 = v32h.reshape(rows, h_b, 2 * DH_B)

    mats = _s5_matrices(s5_a_re[0], s5_a_im[0], s5_log_step[0], s5_b_re[0], s5_b_im[0],
                        s5_c_re[0], s5_c_im[0], s5_d[0])
    n_ch = lp // S5_STEP
    y_t, st_p = _s5_scan(u_t, jnp.zeros((n_grp, bsz, 4 * P_A), F32), mats, n_ch, bsz, 0)
    h_re = jnp.transpose(state_s5_re[0].astype(F32), (1, 0, 2))
    h_im = jnp.transpose(state_s5_im[0].astype(F32), (1, 0, 2))
    y_t, st_s = _s5_scan(u_t, jnp.concatenate([h_re, h_im, h_im, h_re], axis=-1), mats, 1, nb,
                         n_p // S5_STEP, y_prev=y_t)

    lv = diff_lam[0].astype(F32)
    lam_init = 0.8 - 0.6 * math.exp(-0.3 * 0)
    lam = jnp.exp(jnp.sum(lv[0] * lv[1])) - jnp.exp(jnp.sum(lv[2] * lv[3])) + lam_init
    slopes = jnp.exp2(-8.0 * jnp.arange(1, h_b + 1, dtype=F32) / h_b)
    par = jnp.concatenate([lam[None], slopes]).astype(F32)
    subg = (jnp.tile(diff_sub_norm[0], h_b) * (1.0 - lam_init)).reshape(1, qkw)
    kpad = jnp.arange(lp) < front
    kb_diff = jnp.where(kpad[None, :], NEG, LOG2E * slopes[:, None] * jnp.arange(lp, dtype=F32)[None, :])
    kb_diff = jnp.broadcast_to(kb_diff[None, :, None, :], (1, h_b, 2, lp)).reshape(
        1, 2 * h_b // ATT_STREAMS, ATT_STREAMS, lp)
    o_all = _prompt_attn("diff", q16, k16, vt_diff, kb_diff, bsz, lp, par=par, gain=subg)
    qs = q16[n_p:].reshape(nb, ds, h_b, 2, DH_B)
    eye_2 = jnp.eye(2, dtype=BF16)
    qbd = jnp.einsum('bqhmd,mM->bhmqMd', qs, eye_2).reshape(nb, h_b * 2 * ds, 2 * DH_B)
    cache_spec = lambda w: pl.BlockSpec((1, 1, kb, w), lambda b, j: (0, b, j, 0))
    head_rows = lambda a: a.reshape(a.shape[0], nb, past * h_b, 2 * DH_B)
    head_cache_spec = pl.BlockSpec((1, 1, kb * h_b, 2 * DH_B), lambda b, j: (0, b, j, 0))
    o_all = _dec_call(
        functools.partial(_diff_dec_body, past=past, kb=kb, ds=ds, n_heads=h_b),
        [_per_seq(qbd), (head_rows(cache_diff_k), head_cache_spec), (head_rows(cache_diff_v), head_cache_spec),
         _per_seq(k16[n_p:].reshape(nb, ds * h_b, 2 * DH_B)), _per_seq(v32h[n_p * h_b:].astype(BF16).reshape(nb, ds * h_b, 2 * DH_B)),
         _dec_const(subg)],
        o_all, n_p, ds, 2 * h_b * ds, 2 * DH_B, nb, n_kb, smem=par)

    x = _row_call(
        functools.partial(_ab_out_body, s5w=s5w),
        [x, o_all],
        [s5_glu_w[0].astype(BF16), s5_glu_b[0].reshape(1, s5w), ab_w_out[0].astype(BF16)],
        [dm], [F32], tm, group_ins=[y_t], scratch=[pltpu.VMEM((s5w // 128, tm, 128), F32)])[0]

    x = _ffn(x, ffn_norm[0, 1], ffn_w_in[0, 1], ffn_w_out[0, 1], tm)

    x = _ffn(x, ffn_norm[1, 0], ffn_w_in[1, 0], ffn_w_out[1, 0], tm)

    half = ROPE_D // 2
    inv = ROPE_THETA ** (-jnp.arange(half, dtype=F32) / half)
    pos = jnp.concatenate([jnp.tile(jnp.arange(lp, dtype=jnp.int32) - front, bsz),
                           jnp.tile(past + jnp.arange(ds, dtype=jnp.int32), nb)]).astype(F32)
    ang = pos[:, None] * inv[None, :]
    pad_r = HEAD_PAD - NOPE_D - ROPE_D
    cos_t = jnp.concatenate([jnp.ones((rows, NOPE_D), F32), jnp.cos(ang), jnp.cos(ang),
                             jnp.zeros((rows, pad_r), F32)], axis=1)
    sin_t = jnp.concatenate([jnp.zeros((rows, NOPE_D), F32), jnp.sin(ang), jnp.sin(ang),
                             jnp.zeros((rows, pad_r), F32)], axis=1)

    wcd = cd_w_in[0]
    c_fg = 3 * fox_w
    c_qa = c_fg + h_c
    c_kva = c_qa + q_lora
    c_pe = c_kva + kv_lora
    w_pe = wcd[:, c_pe:c_pe + ROPE_D]
    zc = lambda n: jnp.zeros((dm, n), F32)
    w_cd = jnp.concatenate([
        wcd[:, :3 * fox_w], wcd[:, c_qa:c_qa + q_lora], wcd[:, c_kva:c_kva + kv_lora],
        zc(NOPE_D), w_pe, zc(pad_r),
        zc(NOPE_D), -w_pe[:, half:], w_pe[:, :half], zc(pad_r),
        wcd[:, c_fg:c_fg + h_c], zc(HEAD_PAD - h_c)], axis=1).astype(BF16)
    qb = mla_q_b[0].reshape(q_lora, h_d, d_qk)
    zq = lambda n: jnp.zeros((q_lora, h_d, n), F32)
    qb_pad = jnp.concatenate([qb, zq(pad_r)], axis=-1).reshape(q_lora, h_d * HEAD_PAD)
    qb_rot = jnp.concatenate([zq(NOPE_D), -qb[..., NOPE_D + half:], qb[..., NOPE_D:NOPE_D + half], zq(pad_r)],
                             axis=-1).reshape(q_lora, h_d * HEAD_PAD)
    wq2 = jnp.concatenate([qb_pad, qb_rot], axis=1).astype(BF16)
    kvb = mla_kv_b[0].reshape(kv_lora, h_d, NOPE_D + V_D)
    wk_pad = jnp.concatenate([kvb[..., :NOPE_D], jnp.zeros((kv_lora, h_d, HEAD_PAD - NOPE_D), F32)],
                             axis=-1).reshape(kv_lora, h_d * HEAD_PAD).astype(BF16)
    wk_cmp = kvb[..., :NOPE_D].reshape(kv_lora, h_d * NOPE_D).astype(BF16)
    wv_cmp = kvb[..., NOPE_D:].reshape(kv_lora, h_d * V_D).astype(BF16)
    gfq = (jnp.tile(fox_q_norm[0], h_c) * (DH_C ** -0.5 * LOG2E)).reshape(1, fox_w)
    gfk = jnp.tile(fox_k_norm[0], h_c).reshape(1, fox_w)
    fbias = jnp.concatenate([fox_f_bias[0], jnp.zeros((HEAD_PAD - h_c,), F32)]).reshape(1, HEAD_PAD)
    gmq = jnp.tile(jnp.concatenate([mla_q_norm[0] * mla_k_norm[0] * (d_qk ** -0.5 * LOG2E), jnp.zeros((pad_r,), F32)]),
                   h_d).reshape(1, h_d * HEAD_PAD)

    (fq16, fk32, fk16, fv32, logf, qm16, ckv32, kpe32, km16, fvt, vmt) = _row_call(
        functools.partial(_cd_in_body, fox_w=fox_w, q_lora=q_lora, kv_lora=kv_lora, n_heads=h_d),
        [x, cos_t, sin_t],
        [mix_norm[1].reshape(1, dm), w_cd, gfq, gfk, fbias, mla_q_a_norm[0].reshape(1, q_lora), wq2,
         mla_kv_a_norm[0].reshape(1, kv_lora), wk_pad, wv_cmp.T, gmq, ones64, ones128],
        [fox_w, fox_w, fox_w, fox_w, h_c, h_d * HEAD_PAD, kv_lora, ROPE_D, h_d * HEAD_PAD],
        [BF16, F32, BF16, F32, F32, BF16, F32, F32, BF16], tm,
        col_outs=[(2 * fox_w, BF16), (2 * h_d * V_D, BF16)])

    logf_p = jnp.transpose(logf[:n_p].reshape(bsz, lp, h_c), (0, 2, 1)).reshape(bsz * h_c, lp)
    f_p = _cumsum_lanes(logf_p).reshape(bsz, h_c, lp)
    logf_s = jnp.concatenate([
        jnp.transpose(cache_fox_logf[0].astype(F32), (0, 2, 1)),
        jnp.transpose(logf[n_p:].reshape(nb, ds, h_c), (0, 2, 1)),
        jnp.zeros((nb, h_c, 256 - ds), F32)], axis=2).reshape(nb * h_c, past + 256)
    f_s = _cumsum_lanes(logf_s).reshape(nb, h_c, past + 256)

    kb_fox = jnp.where(kpad[None, None, :], NEG, -LOG2E * f_p).reshape(bsz, h_c // ATT_STREAMS, ATT_STREAMS, lp)
    oc_all = _prompt_attn("fox", fq16, fk16, fvt, kb_fox, bsz, lp)
    kb_mla = jnp.broadcast_to(jnp.where(kpad, NEG, 0.0).astype(F32)[None, None, None, :], (1, 1, ATT_STREAMS, lp))
    od_all = _prompt_attn("mla", qm16, km16, vmt, kb_mla, bsz, lp)

    eye_c = jnp.eye(h_c, dtype=BF16)
    fqs = fq16[n_p:].reshape(nb, ds, h_c, DH_C)
    fq_bd = jnp.einsum('bqhd,hH->bhqHd', fqs, eye_c).reshape(nb, h_c * ds, fox_w)
    fkc = cache_fox_k.reshape(cache_fox_k.shape[0], nb, past, fox_w)
    fvc = cache_fox_v.reshape(cache_fox_v.shape[0], nb, past, fox_w)
    oc_all = _dec_call(
        functools.partial(_fox_dec_body, ds=ds, n_heads=h_c),
        [_per_seq(fq_bd), (fkc, cache_spec(fox_w)), (fvc, cache_spec(fox_w)),
         _per_seq(fk16[n_p:].reshape(nb, ds, fox_w)), _per_seq(fv32[n_p:].reshape(nb, ds, fox_w)),
         (f_s, pl.BlockSpec((1, h_c, kb), lambda b, j: (b, 0, j))),
         (f_s, pl.BlockSpec((1, h_c, 128), lambda b, j: (b, 0, past // 128)))],
        oc_all, n_p, ds, h_c * ds, fox_w, nb, n_kb)

    eye_d = jnp.eye(h_d, dtype=BF16)
    qms = qm16[n_p:].reshape(nb, ds, h_d, HEAD_PAD)
    qn_bd = jnp.einsum('bqhd,hH->bhqHd', qms[..., :NOPE_D], eye_d).reshape(nb, h_d * ds, h_d * NOPE_D)
    qp_s = jnp.transpose(qms[..., NOPE_D:NOPE_D + ROPE_D], (0, 2, 1, 3)).reshape(nb, h_d * ds, ROPE_D)
    ones_h = jnp.repeat(jnp.eye(h_d, dtype=BF16), NOPE_D, axis=1)
    od_all = _dec_call(
        functools.partial(_mla_dec_body, ds=ds, n_heads=h_d),
        [_per_seq(qn_bd), _per_seq(qp_s),
         (cache_mla_ckv, pl.BlockSpec((1, 1, kb, kv_lora), lambda b, j: (0, b, j, 0))),
         (cache_mla_kpe, pl.BlockSpec((1, 1, kb, ROPE_D), lambda b, j: (0, b, j, 0))),
         _per_seq(ckv32[n_p:].reshape(nb, ds, kv_lora)), _per_seq(kpe32[n_p:].reshape(nb, ds, ROPE_D)),
         _dec_const(wk_cmp), _dec_const(wv_cmp), _dec_const(ones_h)],
        od_all, n_p, ds, h_d * ds, h_d * V_D, nb, n_kb)

    x = _row_call(functools.partial(_cd_out_body, fox_w=fox_w), [x, oc_all, od_all],
                  [cd_w_out[0].astype(BF16)], [dm], [F32], tm)[0]

    x = _ffn(x, ffn_norm[1, 1], ffn_w_in[1, 1], ffn_w_out[1, 1], tm)

    def p_rows(a, shape):
        return a[:n_p].reshape((bsz, lp) + a.shape[1:])[:, front:front + ltot].reshape((1, bsz, ltot) + shape)

    def s_rows(a, shape):
        return a[n_p:].reshape((1, nb, ds) + shape)

    def s5_state(st):
        st = jnp.transpose(st, (1, 0, 2))
        return st[None, :, :, :P_A], st[None, :, :, P_A:]

    y_prompt = jnp.stack([x[b * lp + front + n_meta:(b + 1) * lp] for b in range(bsz)])
    y_sample = x[n_p:].reshape(nb, ds, dm)
    s5_re_p, s5_im_p = s5_state(st_p)
    s5_re_s, s5_im_s = s5_state(st_s)
    return (y_prompt, y_sample,
            s5_re_p, s5_im_p, p_rows(k32, (h_b, 2 * DH_B)), p_rows(v32, (h_b, 2 * DH_B)),
            p_rows(fk32, (h_c, DH_C)), p_rows(fv32, (h_c, DH_C)), p_rows(logf, (h_c,)),
            p_rows(ckv32, (kv_lora,)), p_rows(kpe32, (ROPE_D,)),
            s5_re_s, s5_im_s, s_rows(k32, (h_b, 2 * DH_B)), s_rows(v32, (h_b, 2 * DH_B)),
            s_rows(fk32, (h_c, DH_C)), s_rows(fv32, (h_c, DH_C)), s_rows(logf, (h_c,)),
            s_rows(ckv32, (kv_lora,)), s_rows(kpe32, (ROPE_D,)))
```

```python
import functools
import math

import jax
import jax.numpy as jnp
from jax import lax
from jax.experimental import pallas as pl
from jax.experimental.pallas import tpu as pltpu

F32 = jnp.float32
BF16 = jnp.bfloat16

EPS = 1e-6
CHUNK = 64
ROW_ALIGN = 256
S5_GROUP = 16
S5_STEP = 16
P_A = 64
DH_B = 64
DH_C = 64
NOPE_D = 64
ROPE_D = 32
V_D = 64
HEAD_PAD = 128
ROPE_THETA = 10000.0
NEG = -1e30
LOG2E = math.log2(math.e)
VMEM_LIMIT = 56 * 1024 * 1024
OUT_PARTS = 4
ATT_BLOCK = 256
ATT_STREAMS = 4
ATT_ROWS = 64
DIFF_VT_ROWS = 2 * DH_B + 16
DEC_KB = 2048


def _dot(a, b):
    return jnp.dot(a, b, preferred_element_type=F32)


def _dot_nt(a, b):
    return lax.dot_general(a, b, (((1,), (1,)), ((), ())), preferred_element_type=F32)


def _rms_rows(x, g):
    ms = jnp.mean(x * x, axis=-1, keepdims=True)
    return x * lax.rsqrt(ms + EPS) * g


def _group_sumsq(x, ones_bd):
    w = x.shape[-1]
    parts = [_dot((x[:, c:c + 256] * x[:, c:c + 256]).astype(BF16), ones_bd) for c in range(0, w, 256)]
    return parts[0] if len(parts) == 1 else jnp.concatenate(parts, axis=1)


def _block_diag_ones(group, n=256):
    r = jnp.arange(n) // group
    return (r[:, None] == r[None, :]).astype(BF16)


def _const_spec(shape):
    nd = len(shape)
    return pl.BlockSpec(shape, lambda *_: (0,) * nd, pipeline_mode=pl.Buffered(1))


def _row_tile(rows, cap=512):
    t = cap
    while rows % t:
        t //= 2
    return t


def _group_spec(a, tm):
    return pl.BlockSpec((a.shape[0], tm // S5_STEP, a.shape[2]), lambda i: (0, i, 0))


def _row_call(body, row_ins, consts, out_widths, out_dtypes, tm, group_ins=(), group_outs=(), scratch=(),
              col_outs=(), head_outs=()):
    rows = row_ins[0].shape[0]
    in_specs = [pl.BlockSpec((tm, a.shape[1]), lambda i: (i, 0)) for a in row_ins]
    in_specs += [_group_spec(a, tm) for a in group_ins]
    in_specs += [_const_spec(c.shape) for c in consts]
    out_specs = [pl.BlockSpec((tm, w), lambda i: (i, 0)) for w in out_widths]
    out_specs += [_group_spec(a, tm) for a in group_outs]
    out_specs += [pl.BlockSpec((w, tm), lambda i: (0, i)) for w, _ in col_outs]
    out_shape = [jax.ShapeDtypeStruct((rows, w), d) for w, d in zip(out_widths, out_dtypes)]
    out_shape += list(group_outs)
    out_shape += [jax.ShapeDtypeStruct((w, rows), d) for w, d in col_outs]
    out_specs += [pl.BlockSpec((tm * h, 128), lambda i: (i, 0)) for h, _ in head_outs]
    out_shape += [jax.ShapeDtypeStruct((rows * h, 128), d) for h, d in head_outs]
    return pl.pallas_call(
        body,
        grid=(rows // tm,),
        in_specs=in_specs,
        out_specs=out_specs,
        out_shape=out_shape,
        scratch_shapes=list(scratch),
        compiler_params=pltpu.CompilerParams(
            dimension_semantics=("parallel",), vmem_limit_bytes=VMEM_LIMIT),
    )(*row_ins, *group_ins, *consts)


def _ffn_body(x_ref, g_ref, win_ref, wout_ref, o_ref, *, d_ff, tf):
    x = x_ref[...]
    xn = _rms_rows(x, g_ref[...]).astype(BF16)
    acc = jnp.zeros(x.shape, F32)
    for c in range(0, d_ff, tf):
        gate = _dot(xn, win_ref[:, c:c + tf])
        up = _dot(xn, win_ref[:, d_ff + c:d_ff + c + tf])
        a = (gate * jax.nn.sigmoid(gate) * up).astype(BF16)
        acc = acc + _dot(a, wout_ref[c:c + tf, :])
    o_ref[...] = x + 0.5 * acc


def _ffn(x, g, w_in, w_out, tm):
    d_ff = w_out.shape[0]
    body = functools.partial(_ffn_body, d_ff=d_ff, tf=256)
    return _row_call(body, [x], [g.reshape(1, -1), w_in.astype(BF16), w_out.astype(BF16)],
                     [x.shape[1]], [F32], tm)[0]


def _ab_in_body(x_ref, g_ref, w_ref, gq_ref, gk_ref, ones_ref,
                q_ref, k16_ref, ut_ref, vt_ref, k32_ref, v32_ref, us_ref, *, widths):
    s5w, qkw = widths
    xn = _rms_rows(x_ref[...], g_ref[...]).astype(BF16)
    h = _dot(xn, w_ref[...])
    n_chunk = us_ref.shape[1] // S5_STEP
    per_col = 128 // S5_GROUP
    for v in range(s5w // 128):
        us_ref[v] = h[:, v * 128:(v + 1) * 128]
        steps = [us_ref[v, pl.ds(t, n_chunk, stride=S5_STEP), :] for t in range(S5_STEP)]
        for gl in range(per_col):
            ut_ref[v * per_col + gl] = jnp.concatenate(
                [x[:, gl * S5_GROUP:(gl + 1) * S5_GROUP] for x in steps], axis=1).astype(BF16)
    q = h[:, s5w:s5w + qkw]
    k = h[:, s5w + qkw:s5w + 2 * qkw]
    v = h[:, s5w + 2 * qkw:]
    ones_bd = ones_ref[...]
    qn = q * lax.rsqrt(_group_sumsq(q, ones_bd) * (1.0 / DH_B) + EPS) * gq_ref[...]
    kn = k * lax.rsqrt(_group_sumsq(k, ones_bd) * (1.0 / DH_B) + EPS) * gk_ref[...]
    q_ref[...] = qn.astype(BF16)
    k16_ref[...] = kn.astype(BF16)
    dv = 2 * DH_B
    n_head = qkw // dv
    tm = kn.shape[0]
    for hd in range(n_head):
        k32_ref[pl.ds(hd, tm, stride=n_head), :] = kn[:, hd * dv:(hd + 1) * dv]
        v32_ref[pl.ds(hd, tm, stride=n_head), :] = v[:, hd * dv:(hd + 1) * dv]
    vt = v.T
    ones = jnp.ones((DIFF_VT_ROWS - dv, vt.shape[1]), F32)
    vt_ref[...] = jnp.concatenate(
        [a for h in range(vt.shape[0] // dv) for a in (vt[h * dv:(h + 1) * dv], ones)], axis=0).astype(BF16)


def _ab_out_body(x_ref, o_ref, yt_ref, gluw_ref, glub_ref, wout_ref, out_ref, ys_ref, *, s5w):
    tm = ys_ref.shape[1]
    part = tm // OUT_PARTS
    n_chunk = part // S5_STEP
    per_col = 128 // S5_GROUP
    for h in range(OUT_PARTS):
        c0 = h * n_chunk
        for v in range(s5w // 128):
            for t in range(S5_STEP):
                ys_ref[v, pl.ds(h * part + t, n_chunk, stride=S5_STEP), :] = jnp.concatenate(
                    [yt_ref[v * per_col + gl, c0:c0 + n_chunk, t * S5_GROUP:(t + 1) * S5_GROUP]
                     for gl in range(per_col)], axis=1)
        rows = slice(h * part, (h + 1) * part)
        y = jnp.concatenate([ys_ref[v, rows] for v in range(s5w // 128)], axis=1)
        g = 0.5 * y * (1.0 + jnp.tanh(math.sqrt(2.0 / math.pi) * (y + 0.044715 * (y * y * y))))
        z = _dot(g.astype(BF16), gluw_ref[...]) + glub_ref[...]
        s5o = g * jax.nn.sigmoid(z)
        m = _dot(s5o.astype(BF16), wout_ref[:s5w, :]) + _dot(o_ref[rows], wout_ref[s5w:, :])
        out_ref[rows] = x_ref[rows] + m


def _heads_with_ones_t(vt):
    ones = jnp.ones((64, vt.shape[1]), vt.dtype)
    outs = []
    for h in range(vt.shape[0] // 64):
        outs += [vt[h * 64:(h + 1) * 64], ones]
    return jnp.concatenate(outs, axis=0)


def _cd_in_body(x_ref, cos_ref, sin_ref, g_ref, w_ref, gfq_ref, gfk_ref, fb_ref, gqa_ref, wq2_ref,
                gkva_ref, wk_ref, wv_ref, gmq_ref, ones64_ref, ones128_ref,
                fq_ref, fk32_ref, fk16_ref, fv32_ref, logf_ref, qm_ref, ckv_ref, kpe_ref,
                km_ref, fvt_ref, vmt_ref, *, fox_w, q_lora, kv_lora, n_heads):
    xn = _rms_rows(x_ref[...], g_ref[...]).astype(BF16)
    h = _dot(xn, w_ref[...])
    ones64 = ones64_ref[...]
    ones128 = ones128_ref[...]
    fq = h[:, :fox_w]
    fk = h[:, fox_w:2 * fox_w]
    fv = h[:, 2 * fox_w:3 * fox_w]
    c0 = 3 * fox_w
    qa = h[:, c0:c0 + q_lora]
    kva = h[:, c0 + q_lora:c0 + q_lora + kv_lora]
    c1 = c0 + q_lora + kv_lora
    pe_a = h[:, c1:c1 + HEAD_PAD]
    pe_b = h[:, c1 + HEAD_PAD:c1 + 2 * HEAD_PAD]
    fg = h[:, c1 + 2 * HEAD_PAD:c1 + 3 * HEAD_PAD]

    fqn = fq * lax.rsqrt(_group_sumsq(fq, ones64) * (1.0 / DH_C) + EPS) * gfq_ref[...]
    fkn = fk * lax.rsqrt(_group_sumsq(fk, ones64) * (1.0 / DH_C) + EPS) * gfk_ref[...]
    fq_ref[...] = fqn.astype(BF16)
    fk32_ref[...] = fkn
    fk16_ref[...] = fkn.astype(BF16)
    fv32_ref[...] = fv
    fvt_ref[...] = _heads_with_ones_t(fv.T).astype(BF16)

    z = fg + fb_ref[...]
    logf = jnp.minimum(z, 0.0) - jnp.log1p(jnp.exp(-jnp.abs(z)))
    logf_ref[...] = logf[:, :logf_ref.shape[1]]

    cos = cos_ref[...]
    sin = sin_ref[...]
    qan = _rms_rows(qa, gqa_ref[...]).astype(BF16)
    q2 = _dot(qan, wq2_ref[...])
    hw = n_heads * HEAD_PAD
    cos_t = jnp.concatenate([cos] * n_heads, axis=1)
    sin_t = jnp.concatenate([sin] * n_heads, axis=1)
    qr = q2[:, :hw] * cos_t + q2[:, hw:] * sin_t
    d_qk = NOPE_D + ROPE_D
    qm = qr * lax.rsqrt(_group_sumsq(qr, ones128) * (1.0 / d_qk) + EPS) * gmq_ref[...]
    qm_ref[...] = qm.astype(BF16)

    ckv = _rms_rows(kva, gkva_ref[...])
    ckv_ref[...] = ckv
    pe = pe_a * cos + pe_b * sin
    kpe_ref[...] = pe[:, NOPE_D:NOPE_D + ROPE_D]
    ckv16 = ckv.astype(BF16)
    kraw = _dot(ckv16, wk_ref[...]) + jnp.concatenate([pe] * n_heads, axis=1)
    km = kraw * lax.rsqrt(_group_sumsq(kraw, ones128) * (1.0 / d_qk) + EPS)
    km_ref[...] = km.astype(BF16)
    vmt_ref[...] = _heads_with_ones_t(_dot_nt(wv_ref[...], ckv16)).astype(BF16)


def _cd_out_body(x_ref, oc_ref, od_ref, wout_ref, out_ref, *, fox_w):
    m = _dot(oc_ref[...], wout_ref[:fox_w, :]) + _dot(od_ref[...], wout_ref[fox_w:, :])
    out_ref[...] = x_ref[...] + m


def _s5_body(*refs, n_chunks, bsz, aliased):
    if aliased:
        u_ref, h0_ref, m_ref, bm_ref, cm_ref, coef_ref, _, y_ref, st_ref, s2_ref, hp_ref = refs
    else:
        u_ref, h0_ref, m_ref, bm_ref, cm_ref, coef_ref, y_ref, st_ref, s2_ref, hp_ref = refs
    u = u_ref[0]
    half = 2 * P_A
    s2 = _dot(u, bm_ref[0])
    s2_ref[0] = s2[:, :half]
    s2_ref[1] = s2[:, half:]
    c1 = coef_ref[0, 0:1, :]
    c2 = coef_ref[0, 1:2, :]
    c3 = coef_ref[0, 2:3, :]

    def step(j, carry):
        ha, hb = carry
        hp_ref[pl.ds(j, bsz, stride=n_chunks), :] = ha
        sa = s2_ref[0, pl.ds(j, bsz, stride=n_chunks), :]
        sb = s2_ref[1, pl.ds(j, bsz, stride=n_chunks), :]
        return ha * c1 + hb * c2 + sa, hb * c1 + ha * c3 + sb

    h0 = h0_ref[0]
    ha, _ = lax.fori_loop(0, n_chunks, step, (h0[:, :half], h0[:, half:]))
    st_ref[0] = ha
    y_ref[0] = _dot(u, m_ref[0]) + _dot(hp_ref[...].astype(BF16), cm_ref[0])


def _s5_scan(u_t, h0, mats, n_chunks, bsz, row0, y_prev=None):
    m_mat, bm, cm, coef = mats
    g, rows_all, w = u_t.shape
    rows = n_chunks * bsz
    assert row0 % rows == 0
    blk = row0 // rows
    aliased = y_prev is not None
    body = functools.partial(_s5_body, n_chunks=n_chunks, bsz=bsz, aliased=aliased)
    per_g = lambda a: pl.BlockSpec((1,) + a.shape[1:], lambda i: (i, 0, 0))
    in_specs = [pl.BlockSpec((1, rows, w), lambda i: (i, blk, 0)),
                per_g(h0), per_g(m_mat), per_g(bm), per_g(cm), per_g(coef)]
    args = [u_t, h0, m_mat, bm, cm, coef]
    if aliased:
        in_specs.append(pl.BlockSpec(memory_space=pl.ANY))
        args.append(y_prev)
    return pl.pallas_call(
        body,
        grid=(g,),
        in_specs=in_specs,
        out_specs=[pl.BlockSpec((1, rows, w), lambda i: (i, blk, 0)),
                   pl.BlockSpec((1, bsz, 2 * P_A), lambda i: (i, 0, 0))],
        out_shape=[jax.ShapeDtypeStruct((g, rows_all, w), F32),
                   jax.ShapeDtypeStruct((g, bsz, 2 * P_A), F32)],
        input_output_aliases={6: 0} if aliased else {},
        scratch_shapes=[pltpu.VMEM((2, rows, 2 * P_A), F32), pltpu.VMEM((rows, 2 * P_A), F32)],
        compiler_params=pltpu.CompilerParams(
            dimension_semantics=("parallel",), vmem_limit_bytes=VMEM_LIMIT),
    )(*args)


def _s5_matrices(a_re, a_im, log_step, b_re, b_im, c_re, c_im, d):
    g = a_re.shape[0]
    t = S5_STEP
    lam = lax.complex(a_re, a_im)
    dl = lam * jnp.exp(log_step)[:, None]
    lam_bar = jnp.exp(dl)
    b_bar = ((lam_bar - 1.0) / lam)[..., None] * lax.complex(b_re, b_im)
    c = lax.complex(c_re, c_im)
    pw = jnp.exp(dl[:, None, :] * jnp.arange(t + 1, dtype=F32)[None, :, None])
    bmc = pw[:, t - 1::-1][:, :, :, None] * b_bar[:, None]
    bmc = jnp.swapaxes(bmc, 2, 3).reshape(g, t * S5_GROUP, P_A)
    bm = jnp.concatenate([bmc.real, bmc.imag, bmc.imag, bmc.real], axis=-1)
    kk = jnp.einsum('gcp,gkp,gpd->gkcd', c, pw[:, :t], b_bar).real
    kk = kk.at[:, 0].add(d.reshape(g, S5_GROUP)[:, :, None] * jnp.eye(S5_GROUP, dtype=F32))
    lag = jnp.arange(t)[None, :] - jnp.arange(t)[:, None]
    toep = jnp.where((lag >= 0)[None, :, :, None, None], kk[:, jnp.clip(lag, 0, t - 1)], 0.0)
    m_mat = jnp.transpose(toep, (0, 1, 4, 2, 3)).reshape(g, t * S5_GROUP, t * S5_GROUP)
    cp = c[:, None] * pw[:, 1:, None, :]
    cpm = jnp.transpose(cp, (0, 3, 1, 2)).reshape(g, P_A, t * S5_GROUP)
    cm = jnp.concatenate([cpm.real, -cpm.imag], axis=1)
    a_t = pw[:, t]
    ar, ai = a_t.real, a_t.imag
    zeros = jnp.zeros_like(ar)
    coef = jnp.stack([jnp.concatenate([ar, ar], -1), jnp.concatenate([-ai, ai], -1),
                      jnp.concatenate([ai, -ai], -1), jnp.concatenate([zeros, zeros], -1)], axis=1)
    return m_mat.astype(BF16), bm.astype(BF16), cm.astype(BF16), coef.astype(F32)


def _online(logits, vt, e, m_ref, acc_ref, p_ref, block_max=None):
    tk = p_ref.shape[1]
    if block_max is None:
        part = logits(0, ATT_ROWS)
        for r0 in range(ATT_ROWS, tk, ATT_ROWS):
            part = jnp.maximum(part, logits(r0, ATT_ROWS))
    else:
        part = block_max
    m_prev = m_ref[e]
    m_new = jnp.maximum(m_prev, jnp.max(part, axis=0, keepdims=True))
    alpha = jnp.exp2(m_prev - m_new)
    m_ref[e] = m_new
    for r0 in range(0, tk, ATT_ROWS):
        p_ref[e, r0:r0 + ATT_ROWS] = jnp.exp2(logits(r0, ATT_ROWS) - m_new).astype(BF16)
    acc_ref[e] = alpha * acc_ref[e] + _dot(vt, p_ref[e])


def _attn_query_block(i, hg, refs, kind, tq, ns):
    if kind == "diff":
        (par_ref, q_ref, k_ref, vt_ref, kb_ref, g_ref, o_ref,
         m_ref, acc_ref, s_ref, p_ref, kbc_ref, mx_ref) = refs
    else:
        q_ref, k_ref, vt_ref, kb_ref, o_ref, m_ref, acc_ref, s_ref, p_ref, kbc_ref, mx_ref = refs
    vrows = acc_ref.shape[1]
    m_ref[...] = jnp.full(m_ref.shape, NEG, F32)
    acc_ref[...] = jnp.zeros(acc_ref.shape, F32)

    qstart = pl.multiple_of(i * tq, tq)
    q = q_ref[0, pl.ds(qstart, tq), :]
    lane = lax.broadcasted_iota(jnp.int32, (1, 128), 1)
    qs = []
    for e in range(ns):
        if kind == "mla":
            qs.append(q[:, e * HEAD_PAD:(e + 1) * HEAD_PAD])
        else:
            qp = q[:, (e // 2) * 128:(e // 2 + 1) * 128]
            qs.append(jnp.where((lane < 64) if e % 2 == 0 else (lane >= 64), qp, jnp.zeros_like(qp)))
    lane_q = lax.broadcasted_iota(jnp.int32, (1, tq), 1)
    ref = [-kb_ref[0, 0, e:e + 1, pl.ds(qstart + (tq - 128), 128)][:, 127:128] for e in range(ns)]
    if kind == "diff":
        slope = [LOG2E * par_ref[1 + hg * (ns // 2) + p] for p in range(ns // 2)]

    tk = tq

    def k_slot(k, e):
        if kind == "mla":
            return k[:, e * HEAD_PAD:(e + 1) * HEAD_PAD]
        return k[:, (e // 2) * 128:(e // 2 + 1) * 128]

    def vt_slot(vt, e):
        r0 = (e // 2 if kind == "diff" else e) * vrows
        return vt[r0:r0 + vrows]

    def scores(j, slot, first=False):
        k0 = pl.multiple_of(j * tk, tk)
        k = k_ref[0, pl.ds(k0, tk), :]
        for e in range(ns):
            s = _dot_nt(k_slot(k, e), qs[e])
            if first or kind != "mla":
                bias = kbc_ref[e, pl.ds(k0, tk), :] + ref[e]
                s = s + jnp.concatenate([bias] * (tq // 128), axis=1)
            s_ref[slot, e] = s
            part = s[0:8]
            for r0 in range(8, tk, 8):
                part = jnp.maximum(part, s[r0:r0 + 8])
            mx_ref[slot, e] = part

    def softmax_pv(j, slot, diag):
        k0 = pl.multiple_of(j * tk, tk)
        vt = vt_ref[:, pl.ds(k0, tk)]
        for e in range(ns):
            def logits(r0, n, e=e):
                s = s_ref[slot, e, r0:r0 + n, :]
                if diag:
                    row_i = r0 + lax.broadcasted_iota(jnp.int32, (n, 1), 0)
                    if kind == "diff":
                        s = s - (2.0 * slope[e // 2]) * jnp.maximum(row_i - lane_q, 0).astype(F32)
                    if kind == "fox":
                        s = jnp.where(row_i <= lane_q, s, NEG)
                    elif r0 > 0:
                        s = jnp.where(lane_q >= r0, s, NEG)
                return s

            _online(logits, vt_slot(vt, e), e, m_ref, acc_ref, p_ref,
                    block_max=None if diag else mx_ref[slot, e])

    scores(0, 0, first=True)

    def pair_body(jj, c):
        j = 2 * jj
        scores(j + 1, 1)
        softmax_pv(j, 0, False)
        scores(j + 2, 0)
        softmax_pv(j + 1, 1, False)
        return c

    lax.fori_loop(0, i // 2, pair_body, 0)

    @pl.when(i % 2 == 0)
    def _():
        softmax_pv(i, 0, True)

    @pl.when(i % 2 == 1)
    def _():
        scores(i, 1)
        softmax_pv(i - 1, 0, False)
        softmax_pv(i, 1, True)

    outs = []
    for p in range(ns // 2):
        e0, e1 = 2 * p, 2 * p + 1
        if kind == "diff":
            a0, a1 = acc_ref[e0], acc_ref[e1]
            dv = 2 * DH_B
            o = (a0[:dv] / a0[dv:dv + 1] - par_ref[0] * (a1[:dv] / a1[dv:dv + 1])).T
            ms = jnp.mean(o * o, axis=-1, keepdims=True)
            outs.append(o * lax.rsqrt(ms + EPS) * g_ref[:, p * 128:(p + 1) * 128])
        else:
            a0, a1 = acc_ref[e0], acc_ref[e1]
            outs.append(jnp.concatenate([a0[:64] / a0[64:], a1[:64] / a1[64:]], axis=0).T)
    o_ref[0, pl.ds(qstart, tq), :] = (
        outs[0] if len(outs) == 1 else jnp.concatenate(outs, axis=1)).astype(o_ref.dtype)


def _prompt_attn_body(*refs, kind, tq, ns, nq):
    kb_ref, kbc_ref = (refs[4], refs[11]) if kind == "diff" else (refs[3], refs[9])
    def fill_bias_columns():
        def fill(c, carry):
            c0 = pl.multiple_of(c * 128, 128)
            for e in range(ns):
                row = kb_ref[0, 0, e:e + 1, pl.ds(c0, 128)]
                kbc_ref[e, pl.ds(c0, 128), :] = jnp.broadcast_to(row, (128, 128)).T
            return carry

        lax.fori_loop(0, kbc_ref.shape[1] // 128, fill, 0)

    if kind == "fox":
        hg = pl.program_id(1)
        fill_bias_columns()
    else:
        hg = pl.program_id(0)
        pl.when(pl.program_id(1) == 0)(fill_bias_columns)

    def query_block(i, carry):
        _attn_query_block(i, hg, refs, kind, tq, ns)
        return carry

    lax.fori_loop(0, nq, query_block, 0)


def _prompt_attn(kind, q, k, vt, kb, bsz, lp, par=None, gain=None):
    rows = q.shape[0]
    tq = ATT_BLOCK
    ns = ATT_STREAMS
    wq = (ns // 2) * (2 * HEAD_PAD if kind == "mla" else 128)
    vrows = DIFF_VT_ROWS if kind == "diff" else 128
    wv = (ns // 2) * (vrows if kind == "diff" else 2 * vrows)
    wo = (ns // 2) * 128
    n_hg = vt.shape[0] // wv
    nq = lp // tq
    kb_b, kb_h = kb.shape[0] > 1, kb.shape[1] > 1
    seq_major = kind == "fox"

    def bh(f):
        return (lambda b, h: f(b, h)) if seq_major else (lambda h, b: f(b, h))

    in_specs = [pl.BlockSpec((1, lp, wq), bh(lambda b, h: (0, b, h))),
                pl.BlockSpec((1, lp, wq), bh(lambda b, h: (0, b, h))),
                pl.BlockSpec((wv, lp), bh(lambda b, h: (h, b))),
                pl.BlockSpec((1, 1, ns, lp), bh(lambda b, h: (b if kb_b else 0, h if kb_h else 0, 0, 0)))]
    args = [q[None], k[None], vt, kb]
    if kind == "diff":
        in_specs = ([pl.BlockSpec(memory_space=pltpu.SMEM)] + in_specs
                    + [pl.BlockSpec((1, wo), bh(lambda b, h: (0, h)))])
        args = [par] + args + [gain]
    assert ATT_ROWS == CHUNK
    body = functools.partial(_prompt_attn_body, kind=kind, tq=tq, ns=ns, nq=nq)
    return pl.pallas_call(
        body,
        grid=(bsz, n_hg) if seq_major else (n_hg, bsz),
        in_specs=in_specs,
        out_specs=pl.BlockSpec((1, lp, wo), bh(lambda b, h: (0, b, h))),
        out_shape=jax.ShapeDtypeStruct((1, rows, n_hg * wo), BF16),
        scratch_shapes=[pltpu.VMEM((ns, 1, tq), F32),
                        pltpu.VMEM((ns, vrows, tq), F32), pltpu.VMEM((2, ns, tq, tq), F32),
                        pltpu.VMEM((ns, tq, tq), BF16), pltpu.VMEM((ns, lp, 128), F32),
                        pltpu.VMEM((2, ns, 8, tq), F32)],
        compiler_params=pltpu.CompilerParams(
            dimension_semantics=("arbitrary", "arbitrary"), vmem_limit_bytes=VMEM_LIMIT),
    )(*args)[0]


def _cumsum_body(x_ref, tri_ref, o_ref, carry_ref):
    @pl.when(pl.program_id(0) == 0)
    def _():
        carry_ref[...] = jnp.zeros(carry_ref.shape, F32)

    y = jnp.dot(x_ref[...], tri_ref[...], preferred_element_type=F32,
                precision=lax.Precision.HIGHEST) + carry_ref[...]
    o_ref[...] = y
    carry_ref[...] = y[:, -1:]


def _cumsum_lanes(x, blk=256):
    rows, n = x.shape
    tri = (jnp.arange(blk)[:, None] <= jnp.arange(blk)[None, :]).astype(F32)
    return pl.pallas_call(
        _cumsum_body,
        grid=(n // blk,),
        in_specs=[pl.BlockSpec((rows, blk), lambda j: (0, j)), _const_spec((blk, blk))],
        out_specs=pl.BlockSpec((rows, blk), lambda j: (0, j)),
        out_shape=jax.ShapeDtypeStruct((rows, n), F32),
        scratch_shapes=[pltpu.VMEM((rows, 1), F32)],
        compiler_params=pltpu.CompilerParams(dimension_semantics=("arbitrary",)),
    )(x, tri)


def _expand_rows(x, rep):
    h, n = x.shape
    return jnp.broadcast_to(x[:, None, :], (h, rep, n)).reshape(h * rep, n)


def _dec_online(s, v16, m_ref, l_ref, acc_ref):
    m_prev = m_ref[...]
    m_new = jnp.maximum(m_prev, jnp.max(s, axis=-1, keepdims=True))
    alpha = jnp.exp2(m_prev - m_new)
    p = jnp.exp2(s - m_new)
    l_ref[...] = alpha * l_ref[...] + jnp.sum(p, axis=-1, keepdims=True)
    acc_ref[...] = alpha * acc_ref[...] + _dot(p.astype(BF16), v16)
    m_ref[...] = m_new


def _dec_init(m_ref, l_ref, acc_ref):
    m_ref[...] = jnp.full(m_ref.shape, NEG, F32)
    l_ref[...] = jnp.zeros(l_ref.shape, F32)
    acc_ref[...] = jnp.zeros(acc_ref.shape, F32)


def _diag_blocks(o, n_heads, ds, width):
    return jnp.concatenate([o[h * ds:(h + 1) * ds, h * width:(h + 1) * width] for h in range(n_heads)], axis=1)


def _diff_dec_body(par_ref, q_ref, kc_ref, vc_ref, kn_ref, vn_ref, g_ref, _, o_ref, m_ref, l_ref, acc_ref,
                   *, past, kb, ds, n_heads):
    jb = pl.program_id(1)
    hr = 2 * ds
    rows = n_heads * hr
    dv = 2 * DH_B
    r = lax.broadcasted_iota(jnp.int32, (rows, 1), 0)
    head = r // hr
    slope = LOG2E * jnp.exp2(-8.0 * (head + 1).astype(F32) / n_heads)
    qpos = past + (r % ds)
    q = q_ref[0]

    def key_block(k16, v16, key0):
        col = lax.broadcasted_iota(jnp.int32, (1, k16.shape[0]), 1)
        kpos = key0 + col // n_heads
        s = _dot_nt(q, k16) - slope * jnp.abs(qpos - kpos).astype(F32)
        s = jnp.where(col % n_heads == head, s, NEG)
        _dec_online(s, v16, m_ref, l_ref, acc_ref)

    @pl.when(jb == 0)
    def _():
        _dec_init(m_ref, l_ref, acc_ref)
        key_block(kn_ref[0], vn_ref[0], past)

    key_block(kc_ref[0, 0].astype(BF16), vc_ref[0, 0].astype(BF16), jb * kb)

    @pl.when(jb == pl.num_programs(1) - 1)
    def _():
        o = acc_ref[...] / l_ref[...]
        outs = []
        for h in range(n_heads):
            oh = o[h * hr:h * hr + ds] - par_ref[0] * o[h * hr + ds:(h + 1) * hr]
            ms = jnp.mean(oh * oh, axis=-1, keepdims=True)
            outs.append(oh * lax.rsqrt(ms + EPS) * g_ref[:, h * dv:(h + 1) * dv])
        o_ref[0] = jnp.concatenate(outs, axis=1).astype(o_ref.dtype)


def _fox_dec_body(q_ref, kc_ref, vc_ref, kn_ref, vn_ref, fc_ref, fn_ref, _, o_ref, m_ref, l_ref, acc_ref,
                  *, ds, n_heads):
    jb = pl.program_id(1)
    rows = n_heads * ds
    q = q_ref[0]
    fnew = fn_ref[0][:, :ds]
    fref = _expand_rows(fn_ref[0][:, 0:1], ds)

    @pl.when(jb == 0)
    def _():
        _dec_init(m_ref, l_ref, acc_ref)
        r = lax.broadcasted_iota(jnp.int32, (rows, 1), 0)
        kidx = lax.broadcasted_iota(jnp.int32, (1, ds), 1)
        s = _dot_nt(q, kn_ref[0]) + LOG2E * (fref - _expand_rows(fnew, ds))
        s = jnp.where(kidx <= (r % ds), s, NEG)
        _dec_online(s, vn_ref[0].astype(BF16), m_ref, l_ref, acc_ref)

    s = _dot_nt(q, kc_ref[0, 0].astype(BF16)) + LOG2E * (fref - _expand_rows(fc_ref[0], ds))
    _dec_online(s, vc_ref[0, 0].astype(BF16), m_ref, l_ref, acc_ref)

    @pl.when(jb == pl.num_programs(1) - 1)
    def _():
        o = acc_ref[...] / l_ref[...]
        o_ref[0] = _diag_blocks(o, n_heads, ds, DH_C).astype(o_ref.dtype)


def _mla_dec_body(qn_ref, qp_ref, cc_ref, pc_ref, cn_ref, pn_ref, wk_ref, wv_ref, ones_ref, _,
                  o_ref, m_ref, l_ref, acc_ref, *, ds, n_heads):
    jb = pl.program_id(1)
    qn = qn_ref[0]
    qp = qp_ref[0]
    ones_h = ones_ref[...]

    def key_block(ckv, kpe):
        c16 = ckv.astype(BF16)
        kn = _dot(c16, wk_ref[...])
        v = _dot(c16, wv_ref[...])
        n = kpe.shape[0]
        ss = _dot_nt(ones_h, (kn * kn).astype(BF16)) + _dot_nt(jnp.ones((n_heads, ROPE_D), BF16),
                                                               (kpe * kpe).astype(BF16))
        rinv = lax.rsqrt(ss * (1.0 / (NOPE_D + ROPE_D)) + EPS)
        s = _dot_nt(qn, kn.astype(BF16)) + _dot_nt(qp, kpe.astype(BF16))
        s = s * _expand_rows(rinv, ds)
        _dec_online(s, v.astype(BF16), m_ref, l_ref, acc_ref)

    @pl.when(jb == 0)
    def _():
        _dec_init(m_ref, l_ref, acc_ref)
        key_block(cn_ref[0], pn_ref[0])

    key_block(cc_ref[0, 0], pc_ref[0, 0])

    @pl.when(jb == pl.num_programs(1) - 1)
    def _():
        o = acc_ref[...] / l_ref[...]
        o_ref[0] = _diag_blocks(o, n_heads, ds, V_D).astype(o_ref.dtype)


def _per_seq(a):
    return (a, pl.BlockSpec((1,) + a.shape[1:], lambda b, j: (b, 0, 0)))


def _dec_const(a):
    return (a, _const_spec(a.shape))


def _dec_call(body, ins, prev, row0, ds, rows, acc_w, nb, n_kb, smem=None):
    in_specs = [spec for _, spec in ins] + [pl.BlockSpec(memory_space=pl.ANY)]
    args = [a for a, _ in ins] + [prev[None]]
    if smem is not None:
        in_specs = [pl.BlockSpec(memory_space=pltpu.SMEM)] + in_specs
        args = [smem] + args
    blk0 = row0 // ds
    return pl.pallas_call(
        body,
        grid=(nb, n_kb),
        in_specs=in_specs,
        out_specs=pl.BlockSpec((1, ds, prev.shape[1]), lambda b, j: (0, blk0 + b, 0)),
        out_shape=jax.ShapeDtypeStruct((1,) + prev.shape, prev.dtype),
        input_output_aliases={len(args) - 1: 0},
        scratch_shapes=[pltpu.VMEM((rows, 1), F32), pltpu.VMEM((rows, 1), F32), pltpu.VMEM((rows, acc_w), F32)],
        compiler_params=pltpu.CompilerParams(
            dimension_semantics=("parallel", "arbitrary"), vmem_limit_bytes=VMEM_LIMIT),
    )(*args)[0]


def kernel(x_prompt, x_sample, state_s5_re, state_s5_im, cache_diff_k, cache_diff_v, cache_fox_k, cache_fox_v, cache_fox_logf, cache_mla_ckv, cache_mla_kpe, meta_tokens, ffn_norm, ffn_w_in, ffn_w_out, mix_norm, ab_w_in, ab_w_out, s5_a_re, s5_a_im, s5_log_step, s5_b_re, s5_b_im, s5_c_re, s5_c_im, s5_d, s5_glu_w, s5_glu_b, diff_q_norm, diff_k_norm, diff_lam, diff_sub_norm, cd_w_in, cd_w_out, fox_q_norm, fox_k_norm, fox_f_bias, mla_q_a_norm, mla_q_b, mla_kv_a_norm, mla_kv_b, mla_q_norm, mla_k_norm):
    bsz, seq, dm = x_prompt.shape
    nb, ds, _ = x_sample.shape
    n_meta = meta_tokens.shape[0]
    past = cache_diff_k.shape[2]
    front = ROW_ALIGN - n_meta
    lp = front + n_meta + seq
    ltot = n_meta + seq
    assert n_meta + front == ROW_ALIGN and lp % ATT_BLOCK == 0 and front % CHUNK == CHUNK - n_meta
    assert ds == S5_STEP and past % CHUNK == 0 and ds <= CHUNK
    kb = min(DEC_KB, past)
    assert past % kb == 0
    n_kb = past // kb
    assert ffn_norm.shape[0] == 2 and ab_w_in.shape[0] == 1 and cd_w_in.shape[0] == 1

    h_b = cache_diff_k.shape[3]
    h_c = cache_fox_k.shape[3]
    h_d = mla_q_b.shape[2] // (NOPE_D + ROPE_D)
    s5w = s5_glu_w.shape[1]
    n_grp = s5w // S5_GROUP
    qkw = h_b * 2 * DH_B
    fox_w = h_c * DH_C
    q_lora = mla_q_a_norm.shape[1]
    kv_lora = mla_kv_a_norm.shape[1]
    d_qk = NOPE_D + ROPE_D

    n_p = bsz * lp
    head_rows_x = jnp.concatenate([jnp.zeros((front, dm), F32), meta_tokens.astype(F32)], axis=0)
    pieces = []
    for b in range(bsz):
        pieces += [head_rows_x, x_prompt[b]]
    x = jnp.concatenate(pieces + [x_sample.reshape(nb * ds, dm)], axis=0)
    rows = x.shape[0]
    tm = _row_tile(rows)

    ones64 = _block_diag_ones(64)
    ones128 = _block_diag_ones(128)

    x = _ffn(x, ffn_norm[0, 0], ffn_w_in[0, 0], ffn_w_out[0, 0], tm)

    gq = (jnp.tile(diff_q_norm[0], 2 * h_b) * (DH_B ** -0.5 * LOG2E)).reshape(1, qkw)
    gk = jnp.tile(diff_k_norm[0], 2 * h_b).reshape(1, qkw)
    s5_cols = S5_STEP * S5_GROUP
    q16, k16, u_t, vt_diff, k32h, v32h = _row_call(
        functools.partial(_ab_in_body, widths=(s5w, qkw)),
        [x], [mix_norm[0].reshape(1, dm), ab_w_in[0].astype(BF16), gq, gk, ones64],
        [qkw, qkw], [BF16, BF16], tm,
        group_outs=[jax.ShapeDtypeStruct((n_grp, rows // S5_STEP, s5_cols), BF16)],
        scratch=[pltpu.VMEM((s5w // 128, tm, 128), F32)],
        col_outs=[(h_b * DIFF_VT_ROWS, BF16)], head_outs=[(h_b, F32), (h_b, F32)])
    k32 = k32h.reshape(rows, h_b, 2 * DH_B)
    v32 = v32h.reshape(rows, h_b, 2 * DH_B)

    mats = _s5_matrices(s5_a_re[0], s5_a_im[0], s5_log_step[0], s5_b_re[0], s5_b_im[0],
                        s5_c_re[0], s5_c_im[0], s5_d[0])
    n_ch = lp // S5_STEP
    y_t, st_p = _s5_scan(u_t, jnp.zeros((n_grp, bsz, 4 * P_A), F32), mats, n_ch, bsz, 0)
    h_re = jnp.transpose(state_s5_re[0].astype(F32), (1, 0, 2))
    h_im = jnp.transpose(state_s5_im[0].astype(F32), (1, 0, 2))
    y_t, st_s = _s5_scan(u_t, jnp.concatenate([h_re, h_im, h_im, h_re], axis=-1), mats, 1, nb,
                         n_p // S5_STEP, y_prev=y_t)

    lv = diff_lam[0].astype(F32)
    lam_init = 0.8 - 0.6 * math.exp(-0.3 * 0)
    lam = jnp.exp(jnp.sum(lv[0] * lv[1])) - jnp.exp(jnp.sum(lv[2] * lv[3])) + lam_init
    slopes = jnp.exp2(-8.0 * jnp.arange(1, h_b + 1, dtype=F32) / h_b)
    par = jnp.concatenate([lam[None], slopes]).astype(F32)
    subg = (jnp.tile(diff_sub_norm[0], h_b) * (1.0 - lam_init)).reshape(1, qkw)
    kpad = jnp.arange(lp) < front
    kb_diff = jnp.where(kpad[None, :], NEG, LOG2E * slopes[:, None] * jnp.arange(lp, dtype=F32)[None, :])
    kb_diff = jnp.broadcast_to(kb_diff[None, :, None, :], (1, h_b, 2, lp)).reshape(
        1, 2 * h_b // ATT_STREAMS, ATT_STREAMS, lp)
    o_all = _prompt_attn("diff", q16, k16, vt_diff, kb_diff, bsz, lp, par=par, gain=subg)
    qs = q16[n_p:].reshape(nb, ds, h_b, 2, DH_B)
    eye_2 = jnp.eye(2, dtype=BF16)
    qbd = jnp.einsum('bqhmd,mM->bhmqMd', qs, eye_2).reshape(nb, h_b * 2 * ds, 2 * DH_B)
    cache_spec = lambda w: pl.BlockSpec((1, 1, kb, w), lambda b, j: (0, b, j, 0))
    head_rows = lambda a: a.reshape(a.shape[0], nb, past * h_b, 2 * DH_B)
    head_cache_spec = pl.BlockSpec((1, 1, kb * h_b, 2 * DH_B), lambda b, j: (0, b, j, 0))
    o_all = _dec_call(
        functools.partial(_diff_dec_body, past=past, kb=kb, ds=ds, n_heads=h_b),
        [_per_seq(qbd), (head_rows(cache_diff_k), head_cache_spec), (head_rows(cache_diff_v), head_cache_spec),
         _per_seq(k16[n_p:].reshape(nb, ds * h_b, 2 * DH_B)), _per_seq(v32h[n_p * h_b:].astype(BF16).reshape(nb, ds * h_b, 2 * DH_B)),
         _dec_const(subg)],
        o_all, n_p, ds, 2 * h_b * ds, 2 * DH_B, nb, n_kb, smem=par)

    x = _row_call(
        functools.partial(_ab_out_body, s5w=s5w),
        [x, o_all],
        [s5_glu_w[0].astype(BF16), s5_glu_b[0].reshape(1, s5w), ab_w_out[0].astype(BF16)],
        [dm], [F32], tm, group_ins=[y_t], scratch=[pltpu.VMEM((s5w // 128, tm, 128), F32)])[0]

    x = _ffn(x, ffn_norm[0, 1], ffn_w_in[0, 1], ffn_w_out[0, 1], tm)

    x = _ffn(x, ffn_norm[1, 0], ffn_w_in[1, 0], ffn_w_out[1, 0], tm)

    half = ROPE_D // 2
    inv = ROPE_THETA ** (-jnp.arange(half, dtype=F32) / half)
    pos = jnp.concatenate([jnp.tile(jnp.arange(lp, dtype=jnp.int32) - front, bsz),
                           jnp.tile(past + jnp.arange(ds, dtype=jnp.int32), nb)]).astype(F32)
    ang = pos[:, None] * inv[None, :]
    pad_r = HEAD_PAD - NOPE_D - ROPE_D
    cos_t = jnp.concatenate([jnp.ones((rows, NOPE_D), F32), jnp.cos(ang), jnp.cos(ang),
                             jnp.zeros((rows, pad_r), F32)], axis=1)
    sin_t = jnp.concatenate([jnp.zeros((rows, NOPE_D), F32), jnp.sin(ang), jnp.sin(ang),
                             jnp.zeros((rows, pad_r), F32)], axis=1)

    wcd = cd_w_in[0]
    c_fg = 3 * fox_w
    c_qa = c_fg + h_c
    c_kva = c_qa + q_lora
    c_pe = c_kva + kv_lora
    w_pe = wcd[:, c_pe:c_pe + ROPE_D]
    zc = lambda n: jnp.zeros((dm, n), F32)
    w_cd = jnp.concatenate([
        wcd[:, :3 * fox_w], wcd[:, c_qa:c_qa + q_lora], wcd[:, c_kva:c_kva + kv_lora],
        zc(NOPE_D), w_pe, zc(pad_r),
        zc(NOPE_D), -w_pe[:, half:], w_pe[:, :half], zc(pad_r),
        wcd[:, c_fg:c_fg + h_c], zc(HEAD_PAD - h_c)], axis=1).astype(BF16)
    qb = mla_q_b[0].reshape(q_lora, h_d, d_qk)
    zq = lambda n: jnp.zeros((q_lora, h_d, n), F32)
    qb_pad = jnp.concatenate([qb, zq(pad_r)], axis=-1).reshape(q_lora, h_d * HEAD_PAD)
    qb_rot = jnp.concatenate([zq(NOPE_D), -qb[..., NOPE_D + half:], qb[..., NOPE_D:NOPE_D + half], zq(pad_r)],
                             axis=-1).reshape(q_lora, h_d * HEAD_PAD)
    wq2 = jnp.concatenate([qb_pad, qb_rot], axis=1).astype(BF16)
    kvb = mla_kv_b[0].reshape(kv_lora, h_d, NOPE_D + V_D)
    wk_pad = jnp.concatenate([kvb[..., :NOPE_D], jnp.zeros((kv_lora, h_d, HEAD_PAD - NOPE_D), F32)],
                             axis=-1).reshape(kv_lora, h_d * HEAD_PAD).astype(BF16)
    wk_cmp = kvb[..., :NOPE_D].reshape(kv_lora, h_d * NOPE_D).astype(BF16)
    wv_cmp = kvb[..., NOPE_D:].reshape(kv_lora, h_d * V_D).astype(BF16)
    gfq = (jnp.tile(fox_q_norm[0], h_c) * (DH_C ** -0.5 * LOG2E)).reshape(1, fox_w)
    gfk = jnp.tile(fox_k_norm[0], h_c).reshape(1, fox_w)
    fbias = jnp.concatenate([fox_f_bias[0], jnp.zeros((HEAD_PAD - h_c,), F32)]).reshape(1, HEAD_PAD)
    gmq = jnp.tile(jnp.concatenate([mla_q_norm[0] * mla_k_norm[0] * (d_qk ** -0.5 * LOG2E), jnp.zeros((pad_r,), F32)]),
                   h_d).reshape(1, h_d * HEAD_PAD)

    (fq16, fk32, fk16, fv32, logf, qm16, ckv32, kpe32, km16, fvt, vmt) = _row_call(
        functools.partial(_cd_in_body, fox_w=fox_w, q_lora=q_lora, kv_lora=kv_lora, n_heads=h_d),
        [x, cos_t, sin_t],
        [mix_norm[1].reshape(1, dm), w_cd, gfq, gfk, fbias, mla_q_a_norm[0].reshape(1, q_lora), wq2,
         mla_kv_a_norm[0].reshape(1, kv_lora), wk_pad, wv_cmp.T, gmq, ones64, ones128],
        [fox_w, fox_w, fox_w, fox_w, h_c, h_d * HEAD_PAD, kv_lora, ROPE_D, h_d * HEAD_PAD],
        [BF16, F32, BF16, F32, F32, BF16, F32, F32, BF16], tm,
        col_outs=[(2 * fox_w, BF16), (2 * h_d * V_D, BF16)])

    logf_p = jnp.transpose(logf[:n_p].reshape(bsz, lp, h_c), (0, 2, 1)).reshape(bsz * h_c, lp)
    f_p = _cumsum_lanes(logf_p).reshape(bsz, h_c, lp)
    logf_s = jnp.concatenate([
        jnp.transpose(cache_fox_logf[0].astype(F32), (0, 2, 1)),
        jnp.transpose(logf[n_p:].reshape(nb, ds, h_c), (0, 2, 1)),
        jnp.zeros((nb, h_c, 256 - ds), F32)], axis=2).reshape(nb * h_c, past + 256)
    f_s = _cumsum_lanes(logf_s).reshape(nb, h_c, past + 256)

    kb_fox = jnp.where(kpad[None, None, :], NEG, -LOG2E * f_p).reshape(bsz, h_c // ATT_STREAMS, ATT_STREAMS, lp)
    oc_all = _prompt_attn("fox", fq16, fk16, fvt, kb_fox, bsz, lp)
    kb_mla = jnp.broadcast_to(jnp.where(kpad, NEG, 0.0).astype(F32)[None, None, None, :], (1, 1, ATT_STREAMS, lp))
    od_all = _prompt_attn("mla", qm16, km16, vmt, kb_mla, bsz, lp)

    eye_c = jnp.eye(h_c, dtype=BF16)
    fqs = fq16[n_p:].reshape(nb, ds, h_c, DH_C)
    fq_bd = jnp.einsum('bqhd,hH->bhqHd', fqs, eye_c).reshape(nb, h_c * ds, fox_w)
    fkc = cache_fox_k.reshape(cache_fox_k.shape[0], nb, past, fox_w)
    fvc = cache_fox_v.reshape(cache_fox_v.shape[0], nb, past, fox_w)
    oc_all = _dec_call(
        functools.partial(_fox_dec_body, ds=ds, n_heads=h_c),
        [_per_seq(fq_bd), (fkc, cache_spec(fox_w)), (fvc, cache_spec(fox_w)),
         _per_seq(fk16[n_p:].reshape(nb, ds, fox_w)), _per_seq(fv32[n_p:].reshape(nb, ds, fox_w)),
         (f_s, pl.BlockSpec((1, h_c, kb), lambda b, j: (b, 0, j))),
         (f_s, pl.BlockSpec((1, h_c, 128), lambda b, j: (b, 0, past // 128)))],
        oc_all, n_p, ds, h_c * ds, fox_w, nb, n_kb)

    eye_d = jnp.eye(h_d, dtype=BF16)
    qms = qm16[n_p:].reshape(nb, ds, h_d, HEAD_PAD)
    qn_bd = jnp.einsum('bqhd,hH->bhqHd', qms[..., :NOPE_D], eye_d).reshape(nb, h_d * ds, h_d * NOPE_D)
    qp_s = jnp.transpose(qms[..., NOPE_D:NOPE_D + ROPE_D], (0, 2, 1, 3)).reshape(nb, h_d * ds, ROPE_D)
    ones_h = jnp.repeat(jnp.eye(h_d, dtype=BF16), NOPE_D, axis=1)
    od_all = _dec_call(
        functools.partial(_mla_dec_body, ds=ds, n_heads=h_d),
        [_per_seq(qn_bd), _per_seq(qp_s),
         (cache_mla_ckv, pl.BlockSpec((1, 1, kb, kv_lora), lambda b, j: (0, b, j, 0))),
         (cache_mla_kpe, pl.BlockSpec((1, 1, kb, ROPE_D), lambda b, j: (0, b, j, 0))),
         _per_seq(ckv32[n_p:].reshape(nb, ds, kv_lora)), _per_seq(kpe32[n_p:].reshape(nb, ds, ROPE_D)),
         _dec_const(wk_cmp), _dec_const(wv_cmp), _dec_const(ones_h)],
        od_all, n_p, ds, h_d * ds, h_d * V_D, nb, n_kb)

    x = _row_call(functools.partial(_cd_out_body, fox_w=fox_w), [x, oc_all, od_all],
                  [cd_w_out[0].astype(BF16)], [dm], [F32], tm)[0]

    x = _ffn(x, ffn_norm[1, 1], ffn_w_in[1, 1], ffn_w_out[1, 1], tm)

    def p_rows(a, shape):
        return a[:n_p].reshape((bsz, lp) + a.shape[1:])[:, front:front + ltot].reshape((1, bsz, ltot) + shape)

    def s_rows(a, shape):
        return a[n_p:].reshape((1, nb, ds) + shape)

    def s5_state(st):
        st = jnp.transpose(st, (1, 0, 2))
        return st[None, :, :, :P_A], st[None, :, :, P_A:]

    y_prompt = jnp.stack([x[b * lp + front + n_meta:(b + 1) * lp] for b in range(bsz)])
    y_sample = x[n_p:].reshape(nb, ds, dm)
    s5_re_p, s5_im_p = s5_state(st_p)
    s5_re_s, s5_im_s = s5_state(st_s)
    return (y_prompt, y_sample,
            s5_re_p, s5_im_p, p_rows(k32, (h_b, 2 * DH_B)), p_rows(v32, (h_b, 2 * DH_B)),
            p_rows(fk32, (h_c, DH_C)), p_rows(fv32, (h_c, DH_C)), p_rows(logf, (h_c,)),
            p_rows(ckv32, (kv_lora,)), p_rows(kpe32, (ROPE_D,)),
            s5_re_s, s5_im_s, s_rows(k32, (h_b, 2 * DH_B)), s_rows(v32, (h_b, 2 * DH_B)),
            s_rows(fk32, (h_c, DH_C)), s_rows(fv32, (h_c, DH_C)), s_rows(logf, (h_c,)),
            s_rows(ckv32, (kv_lora,)), s_rows(kpe32, (ROPE_D,)))
```

```python
import functools
import math

import jax
import jax.numpy as jnp
from jax import lax
from jax.experimental import pallas as pl
from jax.experimental.pallas import tpu as pltpu

F32 = jnp.float32
BF16 = jnp.bfloat16

EPS = 1e-6
CHUNK = 64
ROW_ALIGN = 256
S5_GROUP = 16
S5_STEP = 16
P_A = 64
DH_B = 64
DH_C = 64
NOPE_D = 64
ROPE_D = 32
V_D = 64
HEAD_PAD = 128
ROPE_THETA = 10000.0
NEG = -1e30
LOG2E = math.log2(math.e)
VMEM_LIMIT = 56 * 1024 * 1024
IN_PARTS = 2
OUT_PARTS = 4
ATT_BLOCK = 256
ATT_STREAMS = 4
ATT_ROWS = 64
DIFF_VT_ROWS = 2 * DH_B + 16
DEC_KB = 4096


def _dot(a, b):
    return jnp.dot(a, b, preferred_element_type=F32)


def _dot_nt(a, b):
    return lax.dot_general(a, b, (((1,), (1,)), ((), ())), preferred_element_type=F32)


def _rms_rows(x, g):
    ms = jnp.mean(x * x, axis=-1, keepdims=True)
    return x * lax.rsqrt(ms + EPS) * g


def _group_sumsq(x, ones_bd):
    w = x.shape[-1]
    parts = [_dot((x[:, c:c + 256] * x[:, c:c + 256]).astype(BF16), ones_bd) for c in range(0, w, 256)]
    return parts[0] if len(parts) == 1 else jnp.concatenate(parts, axis=1)


def _block_diag_ones(group, n=256):
    r = jnp.arange(n) // group
    return (r[:, None] == r[None, :]).astype(BF16)


def _const_spec(shape):
    nd = len(shape)
    return pl.BlockSpec(shape, lambda *_: (0,) * nd, pipeline_mode=pl.Buffered(1))


def _row_tile(rows, cap=512):
    t = cap
    while rows % t:
        t //= 2
    return t


def _group_spec(a, tm):
    return pl.BlockSpec((a.shape[0], tm // S5_STEP, a.shape[2]), lambda i: (0, i, 0))


def _row_call(body, row_ins, consts, out_widths, out_dtypes, tm, group_ins=(), group_outs=(), scratch=(),
              col_outs=(), head_outs=()):
    rows = row_ins[0].shape[0]
    in_specs = [pl.BlockSpec((tm, a.shape[1]), lambda i: (i, 0)) for a in row_ins]
    in_specs += [_group_spec(a, tm) for a in group_ins]
    in_specs += [_const_spec(c.shape) for c in consts]
    out_specs = [pl.BlockSpec((tm, w), lambda i: (i, 0)) for w in out_widths]
    out_specs += [_group_spec(a, tm) for a in group_outs]
    out_specs += [pl.BlockSpec((w, tm), lambda i: (0, i)) for w, _ in col_outs]
    out_shape = [jax.ShapeDtypeStruct((rows, w), d) for w, d in zip(out_widths, out_dtypes)]
    out_shape += list(group_outs)
    out_shape += [jax.ShapeDtypeStruct((w, rows), d) for w, d in col_outs]
    out_specs += [pl.BlockSpec((tm * h, 128), lambda i: (i, 0)) for h, _ in head_outs]
    out_shape += [jax.ShapeDtypeStruct((rows * h, 128), d) for h, d in head_outs]
    return pl.pallas_call(
        body,
        grid=(rows // tm,),
        in_specs=in_specs,
        out_specs=out_specs,
        out_shape=out_shape,
        scratch_shapes=list(scratch),
        compiler_params=pltpu.CompilerParams(
            dimension_semantics=("parallel",), vmem_limit_bytes=VMEM_LIMIT),
    )(*row_ins, *group_ins, *consts)


def _ffn_body(x_ref, g_ref, win_ref, wout_ref, o_ref, *, d_ff, tf):
    x = x_ref[...]
    xn = _rms_rows(x, g_ref[...]).astype(BF16)
    acc = jnp.zeros(x.shape, F32)
    for c in range(0, d_ff, tf):
        gate = _dot(xn, win_ref[:, c:c + tf])
        up = _dot(xn, win_ref[:, d_ff + c:d_ff + c + tf])
        a = (gate * jax.nn.sigmoid(gate) * up).astype(BF16)
        acc = acc + _dot(a, wout_ref[c:c + tf, :])
    o_ref[...] = x + 0.5 * acc


def _ffn(x, g, w_in, w_out, tm):
    d_ff = w_out.shape[0]
    body = functools.partial(_ffn_body, d_ff=d_ff, tf=256)
    return _row_call(body, [x], [g.reshape(1, -1), w_in.astype(BF16), w_out.astype(BF16)],
                     [x.shape[1]], [F32], tm)[0]


def _ab_in_body(x_ref, g_ref, w_ref, gq_ref, gk_ref, ones_ref,
                q_ref, k16_ref, ut_ref, vt_ref, k32_ref, v32_ref, us_ref, *, widths):
    s5w, qkw = widths
    xn = _rms_rows(x_ref[...], g_ref[...]).astype(BF16)
    h = _dot(xn, w_ref[...])
    n_chunk = us_ref.shape[1] // S5_STEP
    per_col = 128 // S5_GROUP
    for v in range(s5w // 128):
        us_ref[v] = h[:, v * 128:(v + 1) * 128]
        steps = [us_ref[v, pl.ds(t, n_chunk, stride=S5_STEP), :] for t in range(S5_STEP)]
        for gl in range(per_col):
            ut_ref[v * per_col + gl] = jnp.concatenate(
                [x[:, gl * S5_GROUP:(gl + 1) * S5_GROUP] for x in steps], axis=1).astype(BF16)
    q = h[:, s5w:s5w + qkw]
    k = h[:, s5w + qkw:s5w + 2 * qkw]
    v = h[:, s5w + 2 * qkw:]
    ones_bd = ones_ref[...]
    qn = q * lax.rsqrt(_group_sumsq(q, ones_bd) * (1.0 / DH_B) + EPS) * gq_ref[...]
    kn = k * lax.rsqrt(_group_sumsq(k, ones_bd) * (1.0 / DH_B) + EPS) * gk_ref[...]
    q_ref[...] = qn.astype(BF16)
    k16_ref[...] = kn.astype(BF16)
    dv = 2 * DH_B
    n_head = qkw // dv
    tm = kn.shape[0]
    for hd in range(n_head):
        k32_ref[pl.ds(hd, tm, stride=n_head), :] = kn[:, hd * dv:(hd + 1) * dv]
        v32_ref[pl.ds(hd, tm, stride=n_head), :] = v[:, hd * dv:(hd + 1) * dv]
    vt = v.T
    ones = jnp.ones((DIFF_VT_ROWS - dv, vt.shape[1]), F32)
    vt_ref[...] = jnp.concatenate(
        [a for h in range(vt.shape[0] // dv) for a in (vt[h * dv:(h + 1) * dv], ones)], axis=0).astype(BF16)


def _ab_out_body(x_ref, o_ref, yt_ref, gluw_ref, glub_ref, wout_ref, out_ref, ys_ref, *, s5w):
    tm = ys_ref.shape[1]
    part = tm // OUT_PARTS
    n_chunk = part // S5_STEP
    per_col = 128 // S5_GROUP
    for h in range(OUT_PARTS):
        c0 = h * n_chunk
        for v in range(s5w // 128):
            for t in range(S5_STEP):
                ys_ref[v, pl.ds(h * part + t, n_chunk, stride=S5_STEP), :] = jnp.concatenate(
                    [yt_ref[v * per_col + gl, c0:c0 + n_chunk, t * S5_GROUP:(t + 1) * S5_GROUP]
                     for gl in range(per_col)], axis=1)
        rows = slice(h * part, (h + 1) * part)
        y = jnp.concatenate([ys_ref[v, rows] for v in range(s5w // 128)], axis=1)
        g = 0.5 * y * (1.0 + jnp.tanh(math.sqrt(2.0 / math.pi) * (y + 0.044715 * (y * y * y))))
        z = _dot(g.astype(BF16), gluw_ref[...]) + glub_ref[...]
        s5o = g * jax.nn.sigmoid(z)
        m = _dot(s5o.astype(BF16), wout_ref[:s5w, :]) + _dot(o_ref[rows], wout_ref[s5w:, :])
        out_ref[rows] = x_ref[rows] + m


def _heads_with_ones_t(vt):
    ones = jnp.ones((64, vt.shape[1]), vt.dtype)
    outs = []
    for h in range(vt.shape[0] // 64):
        outs += [vt[h * 64:(h + 1) * 64], ones]
    return jnp.concatenate(outs, axis=0)


def _cd_in_body(*refs, parts, **dims):
    n = refs[0].shape[0] // parts
    for h in range(parts):
        rows = pl.ds(h * n, n)
        _cd_in_part(*[r.at[rows] for r in refs[:3]], *refs[3:16],
                    *[r.at[rows] for r in refs[16:25]], *[r.at[:, rows] for r in refs[25:27]], **dims)


def _cd_in_part(x_ref, cos_ref, sin_ref, g_ref, w_ref, gfq_ref, gfk_ref, fb_ref, gqa_ref, wq2_ref,
                gkva_ref, wk_ref, wv_ref, gmq_ref, ones64_ref, ones128_ref,
                fq_ref, fk32_ref, fk16_ref, fv32_ref, logf_ref, qm_ref, ckv_ref, kpe_ref,
                km_ref, fvt_ref, vmt_ref, *, fox_w, q_lora, kv_lora, n_heads):
    xn = _rms_rows(x_ref[...], g_ref[...]).astype(BF16)
    h = _dot(xn, w_ref[...])
    ones64 = ones64_ref[...]
    ones128 = ones128_ref[...]
    fq = h[:, :fox_w]
    fk = h[:, fox_w:2 * fox_w]
    fv = h[:, 2 * fox_w:3 * fox_w]
    c0 = 3 * fox_w
    qa = h[:, c0:c0 + q_lora]
    kva = h[:, c0 + q_lora:c0 + q_lora + kv_lora]
    c1 = c0 + q_lora + kv_lora
    pe_a = h[:, c1:c1 + HEAD_PAD]
    pe_b = h[:, c1 + HEAD_PAD:c1 + 2 * HEAD_PAD]
    fg = h[:, c1 + 2 * HEAD_PAD:c1 + 3 * HEAD_PAD]

    fqn = fq * lax.rsqrt(_group_sumsq(fq, ones64) * (1.0 / DH_C) + EPS) * gfq_ref[...]
    fkn = fk * lax.rsqrt(_group_sumsq(fk, ones64) * (1.0 / DH_C) + EPS) * gfk_ref[...]
    fq_ref[...] = fqn.astype(BF16)
    fk32_ref[...] = fkn
    fk16_ref[...] = fkn.astype(BF16)
    fv32_ref[...] = fv
    fvt_ref[...] = _heads_with_ones_t(fv.T).astype(BF16)

    z = fg + fb_ref[...]
    logf = jnp.minimum(z, 0.0) - jnp.log1p(jnp.exp(-jnp.abs(z)))
    logf_ref[...] = logf[:, :logf_ref.shape[1]]

    cos = cos_ref[...]
    sin = sin_ref[...]
    qan = _rms_rows(qa, gqa_ref[...]).astype(BF16)
    q2 = _dot(qan, wq2_ref[...])
    hw = n_heads * HEAD_PAD
    cos_t = jnp.concatenate([cos] * n_heads, axis=1)
    sin_t = jnp.concatenate([sin] * n_heads, axis=1)
    qr = q2[:, :hw] * cos_t + q2[:, hw:] * sin_t
    d_qk = NOPE_D + ROPE_D
    qm = qr * lax.rsqrt(_group_sumsq(qr, ones128) * (1.0 / d_qk) + EPS) * gmq_ref[...]
    qm_ref[...] = qm.astype(BF16)

    ckv = _rms_rows(kva, gkva_ref[...])
    ckv_ref[...] = ckv
    pe = pe_a * cos + pe_b * sin
    kpe_ref[...] = pe[:, NOPE_D:NOPE_D + ROPE_D]
    ckv16 = ckv.astype(BF16)
    kraw = _dot(ckv16, wk_ref[...]) + jnp.concatenate([pe] * n_heads, axis=1)
    km = kraw * lax.rsqrt(_group_sumsq(kraw, ones128) * (1.0 / d_qk) + EPS)
    km_ref[...] = km.astype(BF16)
    vmt_ref[...] = _heads_with_ones_t(_dot_nt(wv_ref[...], ckv16)).astype(BF16)


def _cd_out_body(x_ref, oc_ref, od_ref, wout_ref, out_ref, *, fox_w):
    m = _dot(oc_ref[...], wout_ref[:fox_w, :]) + _dot(od_ref[...], wout_ref[fox_w:, :])
    out_ref[...] = x_ref[...] + m


def _s5_body(*refs, n_chunks, bsz, aliased):
    if aliased:
        u_ref, h0_ref, m_ref, bm_ref, cm_ref, coef_ref, _, y_ref, st_ref, s2_ref, hp_ref = refs
    else:
        u_ref, h0_ref, m_ref, bm_ref, cm_ref, coef_ref, y_ref, st_ref, s2_ref, hp_ref = refs
    u = u_ref[0]
    half = 2 * P_A
    s2 = _dot(u, bm_ref[0])
    s2_ref[0] = s2[:, :half]
    s2_ref[1] = s2[:, half:]
    c1 = coef_ref[0, 0:1, :]
    c2 = coef_ref[0, 1:2, :]
    c3 = coef_ref[0, 2:3, :]

    def step(j, carry):
        ha, hb = carry
        hp_ref[pl.ds(j, bsz, stride=n_chunks), :] = ha
        sa = s2_ref[0, pl.ds(j, bsz, stride=n_chunks), :]
        sb = s2_ref[1, pl.ds(j, bsz, stride=n_chunks), :]
        return ha * c1 + hb * c2 + sa, hb * c1 + ha * c3 + sb

    h0 = h0_ref[0]
    ha, _ = lax.fori_loop(0, n_chunks, step, (h0[:, :half], h0[:, half:]))
    st_ref[0] = ha
    y_ref[0] = _dot(u, m_ref[0]) + _dot(hp_ref[...].astype(BF16), cm_ref[0])


def _s5_scan(u_t, h0, mats, n_chunks, bsz, row0, y_prev=None):
    m_mat, bm, cm, coef = mats
    g, rows_all, w = u_t.shape
    rows = n_chunks * bsz
    assert row0 % rows == 0
    blk = row0 // rows
    aliased = y_prev is not None
    body = functools.partial(_s5_body, n_chunks=n_chunks, bsz=bsz, aliased=aliased)
    per_g = lambda a: pl.BlockSpec((1,) + a.shape[1:], lambda i: (i, 0, 0))
    in_specs = [pl.BlockSpec((1, rows, w), lambda i: (i, blk, 0)),
                per_g(h0), per_g(m_mat), per_g(bm), per_g(cm), per_g(coef)]
    args = [u_t, h0, m_mat, bm, cm, coef]
    if aliased:
        in_specs.append(pl.BlockSpec(memory_space=pl.ANY))
        args.append(y_prev)
    return pl.pallas_call(
        body,
        grid=(g,),
        in_specs=in_specs,
        out_specs=[pl.BlockSpec((1, rows, w), lambda i: (i, blk, 0)),
                   pl.BlockSpec((1, bsz, 2 * P_A), lambda i: (i, 0, 0))],
        out_shape=[jax.ShapeDtypeStruct((g, rows_all, w), F32),
                   jax.ShapeDtypeStruct((g, bsz, 2 * P_A), F32)],
        input_output_aliases={6: 0} if aliased else {},
        scratch_shapes=[pltpu.VMEM((2, rows, 2 * P_A), F32), pltpu.VMEM((rows, 2 * P_A), F32)],
        compiler_params=pltpu.CompilerParams(
            dimension_semantics=("parallel",), vmem_limit_bytes=VMEM_LIMIT),
    )(*args)


def _s5_matrices(a_re, a_im, log_step, b_re, b_im, c_re, c_im, d):
    g = a_re.shape[0]
    t = S5_STEP
    lam = lax.complex(a_re, a_im)
    dl = lam * jnp.exp(log_step)[:, None]
    lam_bar = jnp.exp(dl)
    b_bar = ((lam_bar - 1.0) / lam)[..., None] * lax.complex(b_re, b_im)
    c = lax.complex(c_re, c_im)
    pw = jnp.exp(dl[:, None, :] * jnp.arange(t + 1, dtype=F32)[None, :, None])
    bmc = pw[:, t - 1::-1][:, :, :, None] * b_bar[:, None]
    bmc = jnp.swapaxes(bmc, 2, 3).reshape(g, t * S5_GROUP, P_A)
    bm = jnp.concatenate([bmc.real, bmc.imag, bmc.imag, bmc.real], axis=-1)
    kk = jnp.einsum('gcp,gkp,gpd->gkcd', c, pw[:, :t], b_bar).real
    kk = kk.at[:, 0].add(d.reshape(g, S5_GROUP)[:, :, None] * jnp.eye(S5_GROUP, dtype=F32))
    lag = jnp.arange(t)[None, :] - jnp.arange(t)[:, None]
    toep = jnp.where((lag >= 0)[None, :, :, None, None], kk[:, jnp.clip(lag, 0, t - 1)], 0.0)
    m_mat = jnp.transpose(toep, (0, 1, 4, 2, 3)).reshape(g, t * S5_GROUP, t * S5_GROUP)
    cp = c[:, None] * pw[:, 1:, None, :]
    cpm = jnp.transpose(cp, (0, 3, 1, 2)).reshape(g, P_A, t * S5_GROUP)
    cm = jnp.concatenate([cpm.real, -cpm.imag], axis=1)
    a_t = pw[:, t]
    ar, ai = a_t.real, a_t.imag
    zeros = jnp.zeros_like(ar)
    coef = jnp.stack([jnp.concatenate([ar, ar], -1), jnp.concatenate([-ai, ai], -1),
                      jnp.concatenate([ai, -ai], -1), jnp.concatenate([zeros, zeros], -1)], axis=1)
    return m_mat.astype(BF16), bm.astype(BF16), cm.astype(BF16), coef.astype(F32)


def _online(logits, vt, e, m_ref, acc_ref, p_ref, block_max=None):
    tk = p_ref.shape[1]
    if block_max is None:
        part = logits(0, ATT_ROWS)
        for r0 in range(ATT_ROWS, tk, ATT_ROWS):
            part = jnp.maximum(part, logits(r0, ATT_ROWS))
    else:
        part = block_max
    m_prev = m_ref[e]
    m_new = jnp.maximum(m_prev, jnp.max(part, axis=0, keepdims=True))
    alpha = jnp.exp2(m_prev - m_new)
    m_ref[e] = m_new
    for r0 in range(0, tk, ATT_ROWS):
        p_ref[e, r0:r0 + ATT_ROWS] = jnp.exp2(logits(r0, ATT_ROWS) - m_new).astype(BF16)
    acc_ref[e] = alpha * acc_ref[e] + _dot(vt, p_ref[e])


def _attn_query_block(i, hg, refs, kind, tq, ns):
    if kind == "diff":
        (par_ref, q_ref, k_ref, vt_ref, kb_ref, g_ref, o_ref,
         m_ref, acc_ref, s_ref, p_ref, kbc_ref, mx_ref) = refs
    else:
        q_ref, k_ref, vt_ref, kb_ref, o_ref, m_ref, acc_ref, s_ref, p_ref, kbc_ref, mx_ref = refs
    vrows = acc_ref.shape[1]
    m_ref[...] = jnp.full(m_ref.shape, NEG, F32)
    acc_ref[...] = jnp.zeros(acc_ref.shape, F32)

    qstart = pl.multiple_of(i * tq, tq)
    q = q_ref[0, pl.ds(qstart, tq), :]
    lane = lax.broadcasted_iota(jnp.int32, (1, 128), 1)
    qs = []
    for e in range(ns):
        if kind == "mla":
            qs.append(q[:, e * HEAD_PAD:(e + 1) * HEAD_PAD])
        else:
            qp = q[:, (e // 2) * 128:(e // 2 + 1) * 128]
            qs.append(jnp.where((lane < 64) if e % 2 == 0 else (lane >= 64), qp, jnp.zeros_like(qp)))
    lane_q = lax.broadcasted_iota(jnp.int32, (1, tq), 1)
    ref = [-kb_ref[0, 0, e:e + 1, pl.ds(qstart + (tq - 128), 128)][:, 127:128] for e in range(ns)]
    if kind == "diff":
        slope = [LOG2E * par_ref[1 + hg * (ns // 2) + p] for p in range(ns // 2)]

    tk = tq

    def k_slot(k, e):
        if kind == "mla":
            return k[:, e * HEAD_PAD:(e + 1) * HEAD_PAD]
        return k[:, (e // 2) * 128:(e // 2 + 1) * 128]

    def vt_slot(vt, e):
        r0 = (e // 2 if kind == "diff" else e) * vrows
        return vt[r0:r0 + vrows]

    def scores(j, slot, first=False):
        k0 = pl.multiple_of(j * tk, tk)
        k = k_ref[0, pl.ds(k0, tk), :]
        for e in range(ns):
            s = _dot_nt(k_slot(k, e), qs[e])
            if first or kind != "mla":
                bias = kbc_ref[e, pl.ds(k0, tk), :] + ref[e]
                s = s + jnp.concatenate([bias] * (tq // 128), axis=1)
            s_ref[slot, e] = s
            part = s[0:8]
            for r0 in range(8, tk, 8):
                part = jnp.maximum(part, s[r0:r0 + 8])
            mx_ref[slot, e] = part

    def softmax_pv(j, slot, diag):
        k0 = pl.multiple_of(j * tk, tk)
        vt = vt_ref[:, pl.ds(k0, tk)]
        for e in range(ns):
            def logits(r0, n, e=e):
                s = s_ref[slot, e, r0:r0 + n, :]
                if diag:
                    row_i = r0 + lax.broadcasted_iota(jnp.int32, (n, 1), 0)
                    if kind == "diff":
                        s = s - (2.0 * slope[e // 2]) * jnp.maximum(row_i - lane_q, 0).astype(F32)
                    if kind == "fox":
                        s = jnp.where(row_i <= lane_q, s, NEG)
                    elif r0 > 0:
                        s = jnp.where(lane_q >= r0, s, NEG)
                return s

            _online(logits, vt_slot(vt, e), e, m_ref, acc_ref, p_ref,
                    block_max=None if diag else mx_ref[slot, e])

    scores(0, 0, first=True)

    def pair_body(jj, c):
        j = 2 * jj
        scores(j + 1, 1)
        softmax_pv(j, 0, False)
        scores(j + 2, 0)
        softmax_pv(j + 1, 1, False)
        return c

    lax.fori_loop(0, i // 2, pair_body, 0)

    @pl.when(i % 2 == 0)
    def _():
        softmax_pv(i, 0, True)

    @pl.when(i % 2 == 1)
    def _():
        scores(i, 1)
        softmax_pv(i - 1, 0, False)
        softmax_pv(i, 1, True)

    outs = []
    for p in range(ns // 2):
        e0, e1 = 2 * p, 2 * p + 1
        if kind == "diff":
            a0, a1 = acc_ref[e0], acc_ref[e1]
            dv = 2 * DH_B
            o = (a0[:dv] / a0[dv:dv + 1] - par_ref[0] * (a1[:dv] / a1[dv:dv + 1])).T
            ms = jnp.mean(o * o, axis=-1, keepdims=True)
            outs.append(o * lax.rsqrt(ms + EPS) * g_ref[:, p * 128:(p + 1) * 128])
        else:
            a0, a1 = acc_ref[e0], acc_ref[e1]
            outs.append(jnp.concatenate([a0[:64] / a0[64:], a1[:64] / a1[64:]], axis=0).T)
    o_ref[0, pl.ds(qstart, tq), :] = (
        outs[0] if len(outs) == 1 else jnp.concatenate(outs, axis=1)).astype(o_ref.dtype)


def _prompt_attn_body(*refs, kind, tq, ns, nq):
    kb_ref, kbc_ref = (refs[4], refs[11]) if kind == "diff" else (refs[3], refs[9])
    def fill_bias_columns():
        def fill(c, carry):
            c0 = pl.multiple_of(c * 128, 128)
            for e in range(ns):
                row = kb_ref[0, 0, e:e + 1, pl.ds(c0, 128)]
                kbc_ref[e, pl.ds(c0, 128), :] = jnp.broadcast_to(row, (128, 128)).T
            return carry

        lax.fori_loop(0, kbc_ref.shape[1] // 128, fill, 0)

    if kind == "fox":
        hg = pl.program_id(1)
        fill_bias_columns()
    else:
        hg = pl.program_id(0)
        pl.when(pl.program_id(1) == 0)(fill_bias_columns)

    def query_block(i, carry):
        _attn_query_block(i, hg, refs, kind, tq, ns)
        return carry

    lax.fori_loop(0, nq, query_block, 0)


def _prompt_attn(kind, q, k, vt, kb, bsz, lp, par=None, gain=None):
    rows = q.shape[0]
    tq = ATT_BLOCK
    ns = ATT_STREAMS
    wq = (ns // 2) * (2 * HEAD_PAD if kind == "mla" else 128)
    vrows = DIFF_VT_ROWS if kind == "diff" else 128
    wv = (ns // 2) * (vrows if kind == "diff" else 2 * vrows)
    wo = (ns // 2) * 128
    n_hg = vt.shape[0] // wv
    nq = lp // tq
    kb_b, kb_h = kb.shape[0] > 1, kb.shape[1] > 1
    seq_major = kind == "fox"

    def bh(f):
        return (lambda b, h: f(b, h)) if seq_major else (lambda h, b: f(b, h))

    in_specs = [pl.BlockSpec((1, lp, wq), bh(lambda b, h: (0, b, h))),
                pl.BlockSpec((1, lp, wq), bh(lambda b, h: (0, b, h))),
                pl.BlockSpec((wv, lp), bh(lambda b, h: (h, b))),
                pl.BlockSpec((1, 1, ns, lp), bh(lambda b, h: (b if kb_b else 0, h if kb_h else 0, 0, 0)))]
    args = [q[None], k[None], vt, kb]
    if kind == "diff":
        in_specs = ([pl.BlockSpec(memory_space=pltpu.SMEM)] + in_specs
                    + [pl.BlockSpec((1, wo), bh(lambda b, h: (0, h)))])
        args = [par] + args + [gain]
    assert ATT_ROWS == CHUNK
    body = functools.partial(_prompt_attn_body, kind=kind, tq=tq, ns=ns, nq=nq)
    return pl.pallas_call(
        body,
        grid=(bsz, n_hg) if seq_major else (n_hg, bsz),
        in_specs=in_specs,
        out_specs=pl.BlockSpec((1, lp, wo), bh(lambda b, h: (0, b, h))),
        out_shape=jax.ShapeDtypeStruct((1, rows, n_hg * wo), BF16),
        scratch_shapes=[pltpu.VMEM((ns, 1, tq), F32),
                        pltpu.VMEM((ns, vrows, tq), F32), pltpu.VMEM((2, ns, tq, tq), F32),
                        pltpu.VMEM((ns, tq, tq), BF16), pltpu.VMEM((ns, lp, 128), F32),
                        pltpu.VMEM((2, ns, 8, tq), F32)],
        compiler_params=pltpu.CompilerParams(
            dimension_semantics=("arbitrary", "arbitrary"), vmem_limit_bytes=VMEM_LIMIT),
    )(*args)[0]


def _cumsum_body(x_ref, tri_ref, o_ref, carry_ref):
    @pl.when(pl.program_id(0) == 0)
    def _():
        carry_ref[...] = jnp.zeros(carry_ref.shape, F32)

    y = jnp.dot(x_ref[...], tri_ref[...], preferred_element_type=F32,
                precision=lax.Precision.HIGHEST) + carry_ref[...]
    o_ref[...] = y
    carry_ref[...] = y[:, -1:]


def _cumsum_lanes(x, blk=256):
    rows, n = x.shape
    tri = (jnp.arange(blk)[:, None] <= jnp.arange(blk)[None, :]).astype(F32)
    return pl.pallas_call(
        _cumsum_body,
        grid=(n // blk,),
        in_specs=[pl.BlockSpec((rows, blk), lambda j: (0, j)), _const_spec((blk, blk))],
        out_specs=pl.BlockSpec((rows, blk), lambda j: (0, j)),
        out_shape=jax.ShapeDtypeStruct((rows, n), F32),
        scratch_shapes=[pltpu.VMEM((rows, 1), F32)],
        compiler_params=pltpu.CompilerParams(dimension_semantics=("arbitrary",)),
    )(x, tri)


def _expand_rows(x, rep):
    h, n = x.shape
    return jnp.broadcast_to(x[:, None, :], (h, rep, n)).reshape(h * rep, n)


def _dec_online(s, v16, m_ref, l_ref, acc_ref):
    m_prev = m_ref[...]
    m_new = jnp.maximum(m_prev, jnp.max(s, axis=-1, keepdims=True))
    alpha = jnp.exp2(m_prev - m_new)
    p = jnp.exp2(s - m_new)
    l_ref[...] = alpha * l_ref[...] + jnp.sum(p, axis=-1, keepdims=True)
    acc_ref[...] = alpha * acc_ref[...] + _dot(p.astype(BF16), v16)
    m_ref[...] = m_new


def _dec_init(m_ref, l_ref, acc_ref):
    m_ref[...] = jnp.full(m_ref.shape, NEG, F32)
    l_ref[...] = jnp.zeros(l_ref.shape, F32)
    acc_ref[...] = jnp.zeros(acc_ref.shape, F32)


def _diag_blocks(o, n_heads, ds, width):
    return jnp.concatenate([o[h * ds:(h + 1) * ds, h * width:(h + 1) * width] for h in range(n_heads)], axis=1)


def _diff_dec_body(par_ref, q_ref, kc_ref, vc_ref, kn_ref, vn_ref, g_ref, _, o_ref, m_ref, l_ref, acc_ref,
                   *, past, kb, ds, n_heads):
    jb = pl.program_id(1)
    hr = 2 * ds
    rows = n_heads * hr
    dv = 2 * DH_B
    r = lax.broadcasted_iota(jnp.int32, (rows, 1), 0)
    head = r // hr
    slope = LOG2E * jnp.exp2(-8.0 * (head + 1).astype(F32) / n_heads)
    qpos = past + (r % ds)
    q = q_ref[0]

    def key_block(k16, v16, key0):
        col = lax.broadcasted_iota(jnp.int32, (1, k16.shape[0]), 1)
        kpos = key0 + col // n_heads
        s = _dot_nt(q, k16) - slope * jnp.abs(qpos - kpos).astype(F32)
        s = jnp.where(col % n_heads == head, s, NEG)
        _dec_online(s, v16, m_ref, l_ref, acc_ref)

    @pl.when(jb == 0)
    def _():
        _dec_init(m_ref, l_ref, acc_ref)
        key_block(kn_ref[0], vn_ref[0], past)

    key_block(kc_ref[0, 0].astype(BF16), vc_ref[0, 0].astype(BF16), jb * kb)

    @pl.when(jb == pl.num_programs(1) - 1)
    def _():
        o = acc_ref[...] / l_ref[...]
        outs = []
        for h in range(n_heads):
            oh = o[h * hr:h * hr + ds] - par_ref[0] * o[h * hr + ds:(h + 1) * hr]
            ms = jnp.mean(oh * oh, axis=-1, keepdims=True)
            outs.append(oh * lax.rsqrt(ms + EPS) * g_ref[:, h * dv:(h + 1) * dv])
        o_ref[0] = jnp.concatenate(outs, axis=1).astype(o_ref.dtype)


def _fox_dec_body(q_ref, kc_ref, vc_ref, kn_ref, vn_ref, fc_ref, fn_ref, _, o_ref, m_ref, l_ref, acc_ref,
                  *, ds, n_heads):
    jb = pl.program_id(1)
    rows = n_heads * ds
    q = q_ref[0]
    fnew = fn_ref[0][:, :ds]
    fref = _expand_rows(fn_ref[0][:, 0:1], ds)

    @pl.when(jb == 0)
    def _():
        _dec_init(m_ref, l_ref, acc_ref)
        r = lax.broadcasted_iota(jnp.int32, (rows, 1), 0)
        kidx = lax.broadcasted_iota(jnp.int32, (1, ds), 1)
        s = _dot_nt(q, kn_ref[0]) + LOG2E * (fref - _expand_rows(fnew, ds))
        s = jnp.where(kidx <= (r % ds), s, NEG)
        _dec_online(s, vn_ref[0].astype(BF16), m_ref, l_ref, acc_ref)

    s = _dot_nt(q, kc_ref[0, 0].astype(BF16)) + LOG2E * (fref - _expand_rows(fc_ref[0], ds))
    _dec_online(s, vc_ref[0, 0].astype(BF16), m_ref, l_ref, acc_ref)

    @pl.when(jb == pl.num_programs(1) - 1)
    def _():
        o = acc_ref[...] / l_ref[...]
        o_ref[0] = _diag_blocks(o, n_heads, ds, DH_C).astype(o_ref.dtype)


def _mla_dec_body(qn_ref, qp_ref, cc_ref, pc_ref, cn_ref, pn_ref, wk_ref, wv_ref, ones_ref, _,
                  o_ref, m_ref, l_ref, acc_ref, *, ds, n_heads):
    jb = pl.program_id(1)
    qn = qn_ref[0]
    qp = qp_ref[0]
    ones_h = ones_ref[...]

    def key_block(ckv, kpe):
        c16 = ckv.astype(BF16)
        kn = _dot(c16, wk_ref[...])
        v = _dot(c16, wv_ref[...])
        n = kpe.shape[0]
        ss = _dot_nt(ones_h, (kn * kn).astype(BF16)) + _dot_nt(jnp.ones((n_heads, ROPE_D), BF16),
                                                               (kpe * kpe).astype(BF16))
        rinv = lax.rsqrt(ss * (1.0 / (NOPE_D + ROPE_D)) + EPS)
        s = _dot_nt(qn, kn.astype(BF16)) + _dot_nt(qp, kpe.astype(BF16))
        s = s * _expand_rows(rinv, ds)
        _dec_online(s, v.astype(BF16), m_ref, l_ref, acc_ref)

    @pl.when(jb == 0)
    def _():
        _dec_init(m_ref, l_ref, acc_ref)
        key_block(cn_ref[0], pn_ref[0])

    key_block(cc_ref[0, 0], pc_ref[0, 0])

    @pl.when(jb == pl.num_programs(1) - 1)
    def _():
        o = acc_ref[...] / l_ref[...]
        o_ref[0] = _diag_blocks(o, n_heads, ds, V_D).astype(o_ref.dtype)


def _per_seq(a):
    return (a, pl.BlockSpec((1,) + a.shape[1:], lambda b, j: (b, 0, 0)))


def _dec_const(a):
    return (a, _const_spec(a.shape))


def _dec_call(body, ins, prev, row0, ds, rows, acc_w, nb, n_kb, smem=None):
    in_specs = [spec for _, spec in ins] + [pl.BlockSpec(memory_space=pl.ANY)]
    args = [a for a, _ in ins] + [prev[None]]
    if smem is not None:
        in_specs = [pl.BlockSpec(memory_space=pltpu.SMEM)] + in_specs
        args = [smem] + args
    blk0 = row0 // ds
    return pl.pallas_call(
        body,
        grid=(nb, n_kb),
        in_specs=in_specs,
        out_specs=pl.BlockSpec((1, ds, prev.shape[1]), lambda b, j: (0, blk0 + b, 0)),
        out_shape=jax.ShapeDtypeStruct((1,) + prev.shape, prev.dtype),
        input_output_aliases={len(args) - 1: 0},
        scratch_shapes=[pltpu.VMEM((rows, 1), F32), pltpu.VMEM((rows, 1), F32), pltpu.VMEM((rows, acc_w), F32)],
        compiler_params=pltpu.CompilerParams(
            dimension_semantics=("parallel", "arbitrary"), vmem_limit_bytes=VMEM_LIMIT),
    )(*args)[0]


def kernel(x_prompt, x_sample, state_s5_re, state_s5_im, cache_diff_k, cache_diff_v, cache_fox_k, cache_fox_v, cache_fox_logf, cache_mla_ckv, cache_mla_kpe, meta_tokens, ffn_norm, ffn_w_in, ffn_w_out, mix_norm, ab_w_in, ab_w_out, s5_a_re, s5_a_im, s5_log_step, s5_b_re, s5_b_im, s5_c_re, s5_c_im, s5_d, s5_glu_w, s5_glu_b, diff_q_norm, diff_k_norm, diff_lam, diff_sub_norm, cd_w_in, cd_w_out, fox_q_norm, fox_k_norm, fox_f_bias, mla_q_a_norm, mla_q_b, mla_kv_a_norm, mla_kv_b, mla_q_norm, mla_k_norm):
    bsz, seq, dm = x_prompt.shape
    nb, ds, _ = x_sample.shape
    n_meta = meta_tokens.shape[0]
    past = cache_diff_k.shape[2]
    front = ROW_ALIGN - n_meta
    lp = front + n_meta + seq
    ltot = n_meta + seq
    assert n_meta + front == ROW_ALIGN and lp % ATT_BLOCK == 0 and front % CHUNK == CHUNK - n_meta
    assert ds == S5_STEP and past % CHUNK == 0 and ds <= CHUNK
    kb = min(DEC_KB, past)
    assert past % kb == 0
    n_kb = past // kb
    assert ffn_norm.shape[0] == 2 and ab_w_in.shape[0] == 1 and cd_w_in.shape[0] == 1

    h_b = cache_diff_k.shape[3]
    h_c = cache_fox_k.shape[3]
    h_d = mla_q_b.shape[2] // (NOPE_D + ROPE_D)
    s5w = s5_glu_w.shape[1]
    n_grp = s5w // S5_GROUP
    qkw = h_b * 2 * DH_B
    fox_w = h_c * DH_C
    q_lora = mla_q_a_norm.shape[1]
    kv_lora = mla_kv_a_norm.shape[1]
    d_qk = NOPE_D + ROPE_D

    n_p = bsz * lp
    head_rows_x = jnp.concatenate([jnp.zeros((front, dm), F32), meta_tokens.astype(F32)], axis=0)
    pieces = []
    for b in range(bsz):
        pieces += [head_rows_x, x_prompt[b]]
    x = jnp.concatenate(pieces + [x_sample.reshape(nb * ds, dm)], axis=0)
    rows = x.shape[0]
    tm = _row_tile(rows)

    ones64 = _block_diag_ones(64)
    ones128 = _block_diag_ones(128)

    x = _ffn(x, ffn_norm[0, 0], ffn_w_in[0, 0], ffn_w_out[0, 0], tm)

    gq = (jnp.tile(diff_q_norm[0], 2 * h_b) * (DH_B ** -0.5 * LOG2E)).reshape(1, qkw)
    gk = jnp.tile(diff_k_norm[0], 2 * h_b).reshape(1, qkw)
    s5_cols = S5_STEP * S5_GROUP
    q16, k16, u_t, vt_diff, k32h, v32h = _row_call(
        functools.partial(_ab_in_body, widths=(s5w, qkw)),
        [x], [mix_norm[0].reshape(1, dm), ab_w_in[0].astype(BF16), gq, gk, ones64],
        [qkw, qkw], [BF16, BF16], tm,
        group_outs=[jax.ShapeDtypeStruct((n_grp, rows // S5_STEP, s5_cols), BF16)],
        scratch=[pltpu.VMEM((s5w // 128, tm, 128), F32)],
        col_outs=[(h_b * DIFF_VT_ROWS, BF16)], head_outs=[(h_b, F32), (h_b, F32)])
    k32 = k32h.reshape(rows, h_b, 2 * DH_B)
    v32 = v32h.reshape(rows, h_b, 2 * DH_B)

    mats = _s5_matrices(s5_a_re[0], s5_a_im[0], s5_log_step[0], s5_b_re[0], s5_b_im[0],
                        s5_c_re[0], s5_c_im[0], s5_d[0])
    n_ch = lp // S5_STEP
    y_t, st_p = _s5_scan(u_t, jnp.zeros((n_grp, bsz, 4 * P_A), F32), mats, n_ch, bsz, 0)
    h_re = jnp.transpose(state_s5_re[0].astype(F32), (1, 0, 2))
    h_im = jnp.transpose(state_s5_im[0].astype(F32), (1, 0, 2))
    y_t, st_s = _s5_scan(u_t, jnp.concatenate([h_re, h_im, h_im, h_re], axis=-1), mats, 1, nb,
                         n_p // S5_STEP, y_prev=y_t)

    lv = diff_lam[0].astype(F32)
    lam_init = 0.8 - 0.6 * math.exp(-0.3 * 0)
    lam = jnp.exp(jnp.sum(lv[0] * lv[1])) - jnp.exp(jnp.sum(lv[2] * lv[3])) + lam_init
    slopes = jnp.exp2(-8.0 * jnp.arange(1, h_b + 1, dtype=F32) / h_b)
    par = jnp.concatenate([lam[None], slopes]).astype(F32)
    subg = (jnp.tile(diff_sub_norm[0], h_b) * (1.0 - lam_init)).reshape(1, qkw)
    kpad = jnp.arange(lp) < front
    kb_diff = jnp.where(kpad[None, :], NEG, LOG2E * slopes[:, None] * jnp.arange(lp, dtype=F32)[None, :])
    kb_diff = jnp.broadcast_to(kb_diff[None, :, None, :], (1, h_b, 2, lp)).reshape(
        1, 2 * h_b // ATT_STREAMS, ATT_STREAMS, lp)
    o_all = _prompt_attn("diff", q16, k16, vt_diff, kb_diff, bsz, lp, par=par, gain=subg)
    qs = q16[n_p:].reshape(nb, ds, h_b, 2, DH_B)
    eye_2 = jnp.eye(2, dtype=BF16)
    qbd = jnp.einsum('bqhmd,mM->bhmqMd', qs, eye_2).reshape(nb, h_b * 2 * ds, 2 * DH_B)
    cache_spec = lambda w: pl.BlockSpec((1, 1, kb, w), lambda b, j: (0, b, j, 0))
    head_rows = lambda a: a.reshape(a.shape[0], nb, past * h_b, 2 * DH_B)
    head_cache_spec = pl.BlockSpec((1, 1, kb * h_b, 2 * DH_B), lambda b, j: (0, b, j, 0))
    o_all = _dec_call(
        functools.partial(_diff_dec_body, past=past, kb=kb, ds=ds, n_heads=h_b),
        [_per_seq(qbd), (head_rows(cache_diff_k), head_cache_spec), (head_rows(cache_diff_v), head_cache_spec),
         _per_seq(k16[n_p:].reshape(nb, ds * h_b, 2 * DH_B)), _per_seq(v32h[n_p * h_b:].astype(BF16).reshape(nb, ds * h_b, 2 * DH_B)),
         _dec_const(subg)],
        o_all, n_p, ds, 2 * h_b * ds, 2 * DH_B, nb, n_kb, smem=par)

    x = _row_call(
        functools.partial(_ab_out_body, s5w=s5w),
        [x, o_all],
        [s5_glu_w[0].astype(BF16), s5_glu_b[0].reshape(1, s5w), ab_w_out[0].astype(BF16)],
        [dm], [F32], tm, group_ins=[y_t], scratch=[pltpu.VMEM((s5w // 128, tm, 128), F32)])[0]

    x = _ffn(x, ffn_norm[0, 1], ffn_w_in[0, 1], ffn_w_out[0, 1], tm)

    x = _ffn(x, ffn_norm[1, 0], ffn_w_in[1, 0], ffn_w_out[1, 0], tm)

    half = ROPE_D // 2
    inv = ROPE_THETA ** (-jnp.arange(half, dtype=F32) / half)
    pos = jnp.concatenate([jnp.tile(jnp.arange(lp, dtype=jnp.int32) - front, bsz),
                           jnp.tile(past + jnp.arange(ds, dtype=jnp.int32), nb)]).astype(F32)
    ang = pos[:, None] * inv[None, :]
    pad_r = HEAD_PAD - NOPE_D - ROPE_D
    cos_t = jnp.concatenate([jnp.ones((rows, NOPE_D), F32), jnp.cos(ang), jnp.cos(ang),
                             jnp.zeros((rows, pad_r), F32)], axis=1)
    sin_t = jnp.concatenate([jnp.zeros((rows, NOPE_D), F32), jnp.sin(ang), jnp.sin(ang),
                             jnp.zeros((rows, pad_r), F32)], axis=1)

    wcd = cd_w_in[0]
    c_fg = 3 * fox_w
    c_qa = c_fg + h_c
    c_kva = c_qa + q_lora
    c_pe = c_kva + kv_lora
    w_pe = wcd[:, c_pe:c_pe + ROPE_D]
    zc = lambda n: jnp.zeros((dm, n), F32)
    w_cd = jnp.concatenate([
        wcd[:, :3 * fox_w], wcd[:, c_qa:c_qa + q_lora], wcd[:, c_kva:c_kva + kv_lora],
        zc(NOPE_D), w_pe, zc(pad_r),
        zc(NOPE_D), -w_pe[:, half:], w_pe[:, :half], zc(pad_r),
        wcd[:, c_fg:c_fg + h_c], zc(HEAD_PAD - h_c)], axis=1).astype(BF16)
    qb = mla_q_b[0].reshape(q_lora, h_d, d_qk)
    zq = lambda n: jnp.zeros((q_lora, h_d, n), F32)
    qb_pad = jnp.concatenate([qb, zq(pad_r)], axis=-1).reshape(q_lora, h_d * HEAD_PAD)
    qb_rot = jnp.concatenate([zq(NOPE_D), -qb[..., NOPE_D + half:], qb[..., NOPE_D:NOPE_D + half], zq(pad_r)],
                             axis=-1).reshape(q_lora, h_d * HEAD_PAD)
    wq2 = jnp.concatenate([qb_pad, qb_rot], axis=1).astype(BF16)
    kvb = mla_kv_b[0].reshape(kv_lora, h_d, NOPE_D + V_D)
    wk_pad = jnp.concatenate([kvb[..., :NOPE_D], jnp.zeros((kv_lora, h_d, HEAD_PAD - NOPE_D), F32)],
                             axis=-1).reshape(kv_lora, h_d * HEAD_PAD).astype(BF16)
    wk_cmp = kvb[..., :NOPE_D].reshape(kv_lora, h_d * NOPE_D).astype(BF16)
    wv_cmp = kvb[..., NOPE_D:].reshape(kv_lora, h_d * V_D).astype(BF16)
    gfq = (jnp.tile(fox_q_norm[0], h_c) * (DH_C ** -0.5 * LOG2E)).reshape(1, fox_w)
    gfk = jnp.tile(fox_k_norm[0], h_c).reshape(1, fox_w)
    fbias = jnp.concatenate([fox_f_bias[0], jnp.zeros((HEAD_PAD - h_c,), F32)]).reshape(1, HEAD_PAD)
    gmq = jnp.tile(jnp.concatenate([mla_q_norm[0] * mla_k_norm[0] * (d_qk ** -0.5 * LOG2E), jnp.zeros((pad_r,), F32)]),
                   h_d).reshape(1, h_d * HEAD_PAD)

    (fq16, fk32, fk16, fv32, logf, qm16, ckv32, kpe32, km16, fvt, vmt) = _row_call(
        functools.partial(_cd_in_body, parts=IN_PARTS, fox_w=fox_w, q_lora=q_lora, kv_lora=kv_lora, n_heads=h_d),
        [x, cos_t, sin_t],
        [mix_norm[1].reshape(1, dm), w_cd, gfq, gfk, fbias, mla_q_a_norm[0].reshape(1, q_lora), wq2,
         mla_kv_a_norm[0].reshape(1, kv_lora), wk_pad, wv_cmp.T, gmq, ones64, ones128],
        [fox_w, fox_w, fox_w, fox_w, h_c, h_d * HEAD_PAD, kv_lora, ROPE_D, h_d * HEAD_PAD],
        [BF16, F32, BF16, F32, F32, BF16, F32, F32, BF16], tm,
        col_outs=[(2 * fox_w, BF16), (2 * h_d * V_D, BF16)])

    logf_p = jnp.transpose(logf[:n_p].reshape(bsz, lp, h_c), (0, 2, 1)).reshape(bsz * h_c, lp)
    f_p = _cumsum_lanes(logf_p).reshape(bsz, h_c, lp)
    logf_s = jnp.concatenate([
        jnp.transpose(cache_fox_logf[0].astype(F32), (0, 2, 1)),
        jnp.transpose(logf[n_p:].reshape(nb, ds, h_c), (0, 2, 1)),
        jnp.zeros((nb, h_c, 256 - ds), F32)], axis=2).reshape(nb * h_c, past + 256)
    f_s = _cumsum_lanes(logf_s).reshape(nb, h_c, past + 256)

    kb_fox = jnp.where(kpad[None, None, :], NEG, -LOG2E * f_p).reshape(bsz, h_c // ATT_STREAMS, ATT_STREAMS, lp)
    oc_all = _prompt_attn("fox", fq16, fk16, fvt, kb_fox, bsz, lp)
    kb_mla = jnp.broadcast_to(jnp.where(kpad, NEG, 0.0).astype(F32)[None, None, None, :], (1, 1, ATT_STREAMS, lp))
    od_all = _prompt_attn("mla", qm16, km16, vmt, kb_mla, bsz, lp)

    eye_c = jnp.eye(h_c, dtype=BF16)
    fqs = fq16[n_p:].reshape(nb, ds, h_c, DH_C)
    fq_bd = jnp.einsum('bqhd,hH->bhqHd', fqs, eye_c).reshape(nb, h_c * ds, fox_w)
    fkc = cache_fox_k.reshape(cache_fox_k.shape[0], nb, past, fox_w)
    fvc = cache_fox_v.reshape(cache_fox_v.shape[0], nb, past, fox_w)
    oc_all = _dec_call(
        functools.partial(_fox_dec_body, ds=ds, n_heads=h_c),
        [_per_seq(fq_bd), (fkc, cache_spec(fox_w)), (fvc, cache_spec(fox_w)),
         _per_seq(fk16[n_p:].reshape(nb, ds, fox_w)), _per_seq(fv32[n_p:].reshape(nb, ds, fox_w)),
         (f_s, pl.BlockSpec((1, h_c, kb), lambda b, j: (b, 0, j))),
         (f_s, pl.BlockSpec((1, h_c, 128), lambda b, j: (b, 0, past // 128)))],
        oc_all, n_p, ds, h_c * ds, fox_w, nb, n_kb)

    eye_d = jnp.eye(h_d, dtype=BF16)
    qms = qm16[n_p:].reshape(nb, ds, h_d, HEAD_PAD)
    qn_bd = jnp.einsum('bqhd,hH->bhqHd', qms[..., :NOPE_D], eye_d).reshape(nb, h_d * ds, h_d * NOPE_D)
    qp_s = jnp.transpose(qms[..., NOPE_D:NOPE_D + ROPE_D], (0, 2, 1, 3)).reshape(nb, h_d * ds, ROPE_D)
    ones_h = jnp.repeat(jnp.eye(h_d, dtype=BF16), NOPE_D, axis=1)
    od_all = _dec_call(
        functools.partial(_mla_dec_body, ds=ds, n_heads=h_d),
        [_per_seq(qn_bd), _per_seq(qp_s),
         (cache_mla_ckv, pl.BlockSpec((1, 1, kb, kv_lora), lambda b, j: (0, b, j, 0))),
         (cache_mla_kpe, pl.BlockSpec((1, 1, kb, ROPE_D), lambda b, j: (0, b, j, 0))),
         _per_seq(ckv32[n_p:].reshape(nb, ds, kv_lora)), _per_seq(kpe32[n_p:].reshape(nb, ds, ROPE_D)),
         _dec_const(wk_cmp), _dec_const(wv_cmp), _dec_const(ones_h)],
        od_all, n_p, ds, h_d * ds, h_d * V_D, nb, n_kb)

    x = _row_call(functools.partial(_cd_out_body, fox_w=fox_w), [x, oc_all, od_all],
                  [cd_w_out[0].astype(BF16)], [dm], [F32], tm)[0]

    x = _ffn(x, ffn_norm[1, 1], ffn_w_in[1, 1], ffn_w_out[1, 1], tm)

    def p_rows(a, shape):
        return a[:n_p].reshape((bsz, lp) + a.shape[1:])[:, front:front + ltot].reshape((1, bsz, ltot) + shape)

    def s_rows(a, shape):
        return a[n_p:].reshape((1, nb, ds) + shape)

    def s5_state(st):
        st = jnp.transpose(st, (1, 0, 2))
        return st[None, :, :, :P_A], st[None, :, :, P_A:]

    y_prompt = jnp.stack([x[b * lp + front + n_meta:(b + 1) * lp] for b in range(bsz)])
    y_sample = x[n_p:].reshape(nb, ds, dm)
    s5_re_p, s5_im_p = s5_state(st_p)
    s5_re_s, s5_im_s = s5_state(st_s)
    return (y_prompt, y_sample,
            s5_re_p, s5_im_p, p_rows(k32, (h_b, 2 * DH_B)), p_rows(v32, (h_b, 2 * DH_B)),
            p_rows(fk32, (h_c, DH_C)), p_rows(fv32, (h_c, DH_C)), p_rows(logf, (h_c,)),
            p_rows(ckv32, (kv_lora,)), p_rows(kpe32, (ROPE_D,)),
            s5_re_s, s5_im_s, s_rows(k32, (h_b, 2 * DH_B)), s_rows(v32, (h_b, 2 * DH_B)),
            s_rows(fk32, (h_c, DH_C)), s_rows(fv32, (h_c, DH_C)), s_rows(logf, (h_c,)),
            s_rows(ckv32, (kv_lora,)), s_rows(kpe32, (ROPE_D,)))
```

```python
import functools
import math

import jax
import jax.numpy as jnp
from jax import lax
from jax.experimental import pallas as pl
from jax.experimental.pallas import tpu as pltpu

F32 = jnp.float32
BF16 = jnp.bfloat16

EPS = 1e-6
CHUNK = 64
ROW_ALIGN = 256
S5_GROUP = 16
S5_STEP = 16
S5_GROUPS_PER_STEP = 4
P_A = 64
DH_B = 64
DH_C = 64
NOPE_D = 64
ROPE_D = 32
V_D = 64
HEAD_PAD = 128
ROPE_THETA = 10000.0
NEG = -1e30
LOG2E = math.log2(math.e)
VMEM_LIMIT = 56 * 1024 * 1024
IN_PARTS = 2
OUT_PARTS = 4
ATT_BLOCK = 256
ATT_STREAMS = 4
ATT_ROWS = 64
DIFF_VT_ROWS = 2 * DH_B + 16
DEC_KB = 4096


def _dot(a, b):
    return jnp.dot(a, b, preferred_element_type=F32)


def _dot_nt(a, b):
    return lax.dot_general(a, b, (((1,), (1,)), ((), ())), preferred_element_type=F32)


def _rms_rows(x, g):
    ms = jnp.mean(x * x, axis=-1, keepdims=True)
    return x * lax.rsqrt(ms + EPS) * g


def _group_sumsq(x, ones_bd):
    w = x.shape[-1]
    parts = [_dot((x[:, c:c + 256] * x[:, c:c + 256]).astype(BF16), ones_bd) for c in range(0, w, 256)]
    return parts[0] if len(parts) == 1 else jnp.concatenate(parts, axis=1)


def _block_diag_ones(group, n=256):
    r = jnp.arange(n) // group
    return (r[:, None] == r[None, :]).astype(BF16)


def _const_spec(shape):
    nd = len(shape)
    return pl.BlockSpec(shape, lambda *_: (0,) * nd, pipeline_mode=pl.Buffered(1))


def _row_tile(rows, cap=512):
    t = cap
    while rows % t:
        t //= 2
    return t


def _group_spec(a, tm):
    return pl.BlockSpec((a.shape[0], tm // S5_STEP, a.shape[2]), lambda i: (0, i, 0))


def _row_call(body, row_ins, consts, out_widths, out_dtypes, tm, group_ins=(), group_outs=(), scratch=(),
              col_outs=(), head_outs=()):
    rows = row_ins[0].shape[0]
    in_specs = [pl.BlockSpec((tm, a.shape[1]), lambda i: (i, 0)) for a in row_ins]
    in_specs += [_group_spec(a, tm) for a in group_ins]
    in_specs += [_const_spec(c.shape) for c in consts]
    out_specs = [pl.BlockSpec((tm, w), lambda i: (i, 0)) for w in out_widths]
    out_specs += [_group_spec(a, tm) for a in group_outs]
    out_specs += [pl.BlockSpec((w, tm), lambda i: (0, i)) for w, _ in col_outs]
    out_shape = [jax.ShapeDtypeStruct((rows, w), d) for w, d in zip(out_widths, out_dtypes)]
    out_shape += list(group_outs)
    out_shape += [jax.ShapeDtypeStruct((w, rows), d) for w, d in col_outs]
    out_specs += [pl.BlockSpec((tm * h, 128), lambda i: (i, 0)) for h, _ in head_outs]
    out_shape += [jax.ShapeDtypeStruct((rows * h, 128), d) for h, d in head_outs]
    return pl.pallas_call(
        body,
        grid=(rows // tm,),
        in_specs=in_specs,
        out_specs=out_specs,
        out_shape=out_shape,
        scratch_shapes=list(scratch),
        compiler_params=pltpu.CompilerParams(
            dimension_semantics=("parallel",), vmem_limit_bytes=VMEM_LIMIT),
    )(*row_ins, *group_ins, *consts)


def _ffn_body(x_ref, g_ref, win_ref, wout_ref, o_ref, *, d_ff, tf):
    x = x_ref[...]
    xn = _rms_rows(x, g_ref[...]).astype(BF16)
    acc = jnp.zeros(x.shape, F32)
    for c in range(0, d_ff, tf):
        gate = _dot(xn, win_ref[:, c:c + tf])
        up = _dot(xn, win_ref[:, d_ff + c:d_ff + c + tf])
        a = (gate * jax.nn.sigmoid(gate) * up).astype(BF16)
        acc = acc + _dot(a, wout_ref[c:c + tf, :])
    o_ref[...] = x + 0.5 * acc


def _ffn(x, g, w_in, w_out, tm):
    d_ff = w_out.shape[0]
    body = functools.partial(_ffn_body, d_ff=d_ff, tf=256)
    return _row_call(body, [x], [g.reshape(1, -1), w_in.astype(BF16), w_out.astype(BF16)],
                     [x.shape[1]], [F32], tm)[0]


def _ab_in_body(x_ref, g_ref, w_ref, gq_ref, gk_ref, ones_ref,
                q_ref, k16_ref, ut_ref, vt_ref, k32_ref, v32_ref, us_ref, *, widths):
    s5w, qkw = widths
    xn = _rms_rows(x_ref[...], g_ref[...]).astype(BF16)
    h = _dot(xn, w_ref[...])
    n_chunk = us_ref.shape[1] // S5_STEP
    per_col = 128 // S5_GROUP
    for v in range(s5w // 128):
        us_ref[v] = h[:, v * 128:(v + 1) * 128]
        steps = [us_ref[v, pl.ds(t, n_chunk, stride=S5_STEP), :] for t in range(S5_STEP)]
        for gl in range(per_col):
            ut_ref[v * per_col + gl] = jnp.concatenate(
                [x[:, gl * S5_GROUP:(gl + 1) * S5_GROUP] for x in steps], axis=1).astype(BF16)
    q = h[:, s5w:s5w + qkw]
    k = h[:, s5w + qkw:s5w + 2 * qkw]
    v = h[:, s5w + 2 * qkw:]
    ones_bd = ones_ref[...]
    qn = q * lax.rsqrt(_group_sumsq(q, ones_bd) * (1.0 / DH_B) + EPS) * gq_ref[...]
    kn = k * lax.rsqrt(_group_sumsq(k, ones_bd) * (1.0 / DH_B) + EPS) * gk_ref[...]
    q_ref[...] = qn.astype(BF16)
    k16_ref[...] = kn.astype(BF16)
    dv = 2 * DH_B
    n_head = qkw // dv
    tm = kn.shape[0]
    for hd in range(n_head):
        k32_ref[pl.ds(hd, tm, stride=n_head), :] = kn[:, hd * dv:(hd + 1) * dv]
        v32_ref[pl.ds(hd, tm, stride=n_head), :] = v[:, hd * dv:(hd + 1) * dv]
    vt = v.T
    ones = jnp.ones((DIFF_VT_ROWS - dv, vt.shape[1]), F32)
    vt_ref[...] = jnp.concatenate(
        [a for h in range(vt.shape[0] // dv) for a in (vt[h * dv:(h + 1) * dv], ones)], axis=0).astype(BF16)


def _ab_out_body(x_ref, o_ref, yt_ref, gluw_ref, glub_ref, wout_ref, out_ref, ys_ref, *, s5w):
    tm = ys_ref.shape[1]
    part = tm // OUT_PARTS
    n_chunk = part // S5_STEP
    per_col = 128 // S5_GROUP
    for h in range(OUT_PARTS):
        c0 = h * n_chunk
        for v in range(s5w // 128):
            for t in range(S5_STEP):
                ys_ref[v, pl.ds(h * part + t, n_chunk, stride=S5_STEP), :] = jnp.concatenate(
                    [yt_ref[v * per_col + gl, c0:c0 + n_chunk, t * S5_GROUP:(t + 1) * S5_GROUP]
                     for gl in range(per_col)], axis=1)
        rows = slice(h * part, (h + 1) * part)
        y = jnp.concatenate([ys_ref[v, rows] for v in range(s5w // 128)], axis=1)
        g = 0.5 * y * (1.0 + jnp.tanh(math.sqrt(2.0 / math.pi) * (y + 0.044715 * (y * y * y))))
        z = _dot(g.astype(BF16), gluw_ref[...]) + glub_ref[...]
        s5o = g * jax.nn.sigmoid(z)
        m = _dot(s5o.astype(BF16), wout_ref[:s5w, :]) + _dot(o_ref[rows], wout_ref[s5w:, :])
        out_ref[rows] = x_ref[rows] + m


def _heads_with_ones_t(vt):
    ones = jnp.ones((64, vt.shape[1]), vt.dtype)
    outs = []
    for h in range(vt.shape[0] // 64):
        outs += [vt[h * 64:(h + 1) * 64], ones]
    return jnp.concatenate(outs, axis=0)


def _cd_in_body(*refs, parts, **dims):
    n = refs[0].shape[0] // parts
    for h in range(parts):
        rows = pl.ds(h * n, n)
        _cd_in_part(*[r.at[rows] for r in refs[:3]], *refs[3:16],
                    *[r.at[rows] for r in refs[16:25]], *[r.at[:, rows] for r in refs[25:27]], **dims)


def _cd_in_part(x_ref, cos_ref, sin_ref, g_ref, w_ref, gfq_ref, gfk_ref, fb_ref, gqa_ref, wq2_ref,
                gkva_ref, wk_ref, wv_ref, gmq_ref, ones64_ref, ones128_ref,
                fq_ref, fk32_ref, fk16_ref, fv32_ref, logf_ref, qm_ref, ckv_ref, kpe_ref,
                km_ref, fvt_ref, vmt_ref, *, fox_w, q_lora, kv_lora, n_heads):
    xn = _rms_rows(x_ref[...], g_ref[...]).astype(BF16)
    h = _dot(xn, w_ref[...])
    ones64 = ones64_ref[...]
    ones128 = ones128_ref[...]
    fq = h[:, :fox_w]
    fk = h[:, fox_w:2 * fox_w]
    fv = h[:, 2 * fox_w:3 * fox_w]
    c0 = 3 * fox_w
    qa = h[:, c0:c0 + q_lora]
    kva = h[:, c0 + q_lora:c0 + q_lora + kv_lora]
    c1 = c0 + q_lora + kv_lora
    pe_a = h[:, c1:c1 + HEAD_PAD]
    pe_b = h[:, c1 + HEAD_PAD:c1 + 2 * HEAD_PAD]
    fg = h[:, c1 + 2 * HEAD_PAD:c1 + 3 * HEAD_PAD]

    fqn = fq * lax.rsqrt(_group_sumsq(fq, ones64) * (1.0 / DH_C) + EPS) * gfq_ref[...]
    fkn = fk * lax.rsqrt(_group_sumsq(fk, ones64) * (1.0 / DH_C) + EPS) * gfk_ref[...]
    fq_ref[...] = fqn.astype(BF16)
    fk32_ref[...] = fkn
    fk16_ref[...] = fkn.astype(BF16)
    fv32_ref[...] = fv
    fvt_ref[...] = _heads_with_ones_t(fv.T).astype(BF16)

    z = fg + fb_ref[...]
    logf = jnp.minimum(z, 0.0) - jnp.log1p(jnp.exp(-jnp.abs(z)))
    logf_ref[...] = logf[:, :logf_ref.shape[1]]

    cos = cos_ref[...]
    sin = sin_ref[...]
    qan = _rms_rows(qa, gqa_ref[...]).astype(BF16)
    q2 = _dot(qan, wq2_ref[...])
    hw = n_heads * HEAD_PAD
    cos_t = jnp.concatenate([cos] * n_heads, axis=1)
    sin_t = jnp.concatenate([sin] * n_heads, axis=1)
    qr = q2[:, :hw] * cos_t + q2[:, hw:] * sin_t
    d_qk = NOPE_D + ROPE_D
    qm = qr * lax.rsqrt(_group_sumsq(qr, ones128) * (1.0 / d_qk) + EPS) * gmq_ref[...]
    qm_ref[...] = qm.astype(BF16)

    ckv = _rms_rows(kva, gkva_ref[...])
    ckv_ref[...] = ckv
    pe = pe_a * cos + pe_b * sin
    kpe_ref[...] = pe[:, NOPE_D:NOPE_D + ROPE_D]
    ckv16 = ckv.astype(BF16)
    kraw = _dot(ckv16, wk_ref[...]) + jnp.concatenate([pe] * n_heads, axis=1)
    km = kraw * lax.rsqrt(_group_sumsq(kraw, ones128) * (1.0 / d_qk) + EPS)
    km_ref[...] = km.astype(BF16)
    vmt_ref[...] = _heads_with_ones_t(_dot_nt(wv_ref[...], ckv16)).astype(BF16)


def _cd_out_body(x_ref, oc_ref, od_ref, wout_ref, out_ref, *, fox_w):
    m = _dot(oc_ref[...], wout_ref[:fox_w, :]) + _dot(od_ref[...], wout_ref[fox_w:, :])
    out_ref[...] = x_ref[...] + m


def _s5_body(*refs, n_chunks, bsz, aliased):
    if aliased:
        u_ref, h0_ref, m_ref, bm_ref, cm_ref, coef_ref, _, y_ref, st_ref, s2_ref, hp_ref = refs
    else:
        u_ref, h0_ref, m_ref, bm_ref, cm_ref, coef_ref, y_ref, st_ref, s2_ref, hp_ref = refs
    ng = u_ref.shape[0]
    half = 2 * P_A
    coefs = []
    for g in range(ng):
        s2 = _dot(u_ref[g], bm_ref[g])
        s2_ref[g, 0] = s2[:, :half]
        s2_ref[g, 1] = s2[:, half:]
        coefs.append((coef_ref[g, 0:1, :], coef_ref[g, 1:2, :], coef_ref[g, 2:3, :]))

    def step(j, carry):
        new = []
        for g in range(ng):
            ha, hb = carry[2 * g], carry[2 * g + 1]
            c1, c2, c3 = coefs[g]
            hp_ref[g, pl.ds(j, bsz, stride=n_chunks), :] = ha
            sa = s2_ref[g, 0, pl.ds(j, bsz, stride=n_chunks), :]
            sb = s2_ref[g, 1, pl.ds(j, bsz, stride=n_chunks), :]
            new += [ha * c1 + hb * c2 + sa, hb * c1 + ha * c3 + sb]
        return tuple(new)

    init = []
    for g in range(ng):
        init += [h0_ref[g][:, :half], h0_ref[g][:, half:]]
    final = lax.fori_loop(0, n_chunks, step, tuple(init))
    for g in range(ng):
        st_ref[g] = final[2 * g]
        y_ref[g] = _dot(u_ref[g], m_ref[g]) + _dot(hp_ref[g].astype(BF16), cm_ref[g])


def _s5_scan(u_t, h0, mats, n_chunks, bsz, row0, y_prev=None):
    m_mat, bm, cm, coef = mats
    g, rows_all, w = u_t.shape
    rows = n_chunks * bsz
    assert row0 % rows == 0
    blk = row0 // rows
    aliased = y_prev is not None
    body = functools.partial(_s5_body, n_chunks=n_chunks, bsz=bsz, aliased=aliased)
    ng = S5_GROUPS_PER_STEP
    per_g = lambda a: pl.BlockSpec((ng,) + a.shape[1:], lambda i: (i, 0, 0))
    in_specs = [pl.BlockSpec((ng, rows, w), lambda i: (i, blk, 0)),
                per_g(h0), per_g(m_mat), per_g(bm), per_g(cm), per_g(coef)]
    args = [u_t, h0, m_mat, bm, cm, coef]
    if aliased:
        in_specs.append(pl.BlockSpec(memory_space=pl.ANY))
        args.append(y_prev)
    return pl.pallas_call(
        body,
        grid=(g // ng,),
        in_specs=in_specs,
        out_specs=[pl.BlockSpec((ng, rows, w), lambda i: (i, blk, 0)),
                   pl.BlockSpec((ng, bsz, 2 * P_A), lambda i: (i, 0, 0))],
        out_shape=[jax.ShapeDtypeStruct((g, rows_all, w), F32),
                   jax.ShapeDtypeStruct((g, bsz, 2 * P_A), F32)],
        input_output_aliases={6: 0} if aliased else {},
        scratch_shapes=[pltpu.VMEM((ng, 2, rows, 2 * P_A), F32), pltpu.VMEM((ng, rows, 2 * P_A), F32)],
        compiler_params=pltpu.CompilerParams(
            dimension_semantics=("parallel",), vmem_limit_bytes=VMEM_LIMIT),
    )(*args)


def _s5_matrices(a_re, a_im, log_step, b_re, b_im, c_re, c_im, d):
    g = a_re.shape[0]
    t = S5_STEP
    lam = lax.complex(a_re, a_im)
    dl = lam * jnp.exp(log_step)[:, None]
    lam_bar = jnp.exp(dl)
    b_bar = ((lam_bar - 1.0) / lam)[..., None] * lax.complex(b_re, b_im)
    c = lax.complex(c_re, c_im)
    pw = jnp.exp(dl[:, None, :] * jnp.arange(t + 1, dtype=F32)[None, :, None])
    bmc = pw[:, t - 1::-1][:, :, :, None] * b_bar[:, None]
    bmc = jnp.swapaxes(bmc, 2, 3).reshape(g, t * S5_GROUP, P_A)
    bm = jnp.concatenate([bmc.real, bmc.imag, bmc.imag, bmc.real], axis=-1)
    kk = jnp.einsum('gcp,gkp,gpd->gkcd', c, pw[:, :t], b_bar).real
    kk = kk.at[:, 0].add(d.reshape(g, S5_GROUP)[:, :, None] * jnp.eye(S5_GROUP, dtype=F32))
    lag = jnp.arange(t)[None, :] - jnp.arange(t)[:, None]
    toep = jnp.where((lag >= 0)[None, :, :, None, None], kk[:, jnp.clip(lag, 0, t - 1)], 0.0)
    m_mat = jnp.transpose(toep, (0, 1, 4, 2, 3)).reshape(g, t * S5_GROUP, t * S5_GROUP)
    cp = c[:, None] * pw[:, 1:, None, :]
    cpm = jnp.transpose(cp, (0, 3, 1, 2)).reshape(g, P_A, t * S5_GROUP)
    cm = jnp.concatenate([cpm.real, -cpm.imag], axis=1)
    a_t = pw[:, t]
    ar, ai = a_t.real, a_t.imag
    zeros = jnp.zeros_like(ar)
    coef = jnp.stack([jnp.concatenate([ar, ar], -1), jnp.concatenate([-ai, ai], -1),
                      jnp.concatenate([ai, -ai], -1), jnp.concatenate([zeros, zeros], -1)], axis=1)
    return m_mat.astype(BF16), bm.astype(BF16), cm.astype(BF16), coef.astype(F32)


def _online(logits, vt, e, m_ref, acc_ref, p_ref, block_max=None):
    tk = p_ref.shape[1]
    if block_max is None:
        part = logits(0, ATT_ROWS)
        for r0 in range(ATT_ROWS, tk, ATT_ROWS):
            part = jnp.maximum(part, logits(r0, ATT_ROWS))
    else:
        part = block_max
    m_prev = m_ref[e]
    m_new = jnp.maximum(m_prev, jnp.max(part, axis=0, keepdims=True))
    alpha = jnp.exp2(m_prev - m_new)
    m_ref[e] = m_new
    for r0 in range(0, tk, ATT_ROWS):
        p_ref[e, r0:r0 + ATT_ROWS] = jnp.exp2(logits(r0, ATT_ROWS) - m_new).astype(BF16)
    acc_ref[e] = alpha * acc_ref[e] + _dot(vt, p_ref[e])


def _attn_query_block(i, hg, refs, kind, tq, ns):
    if kind == "diff":
        (par_ref, q_ref, k_ref, vt_ref, kb_ref, g_ref, o_ref,
         m_ref, acc_ref, s_ref, p_ref, kbc_ref, mx_ref) = refs
    else:
        q_ref, k_ref, vt_ref, kb_ref, o_ref, m_ref, acc_ref, s_ref, p_ref, kbc_ref, mx_ref = refs
    vrows = acc_ref.shape[1]
    m_ref[...] = jnp.full(m_ref.shape, NEG, F32)
    acc_ref[...] = jnp.zeros(acc_ref.shape, F32)

    qstart = pl.multiple_of(i * tq, tq)
    q = q_ref[0, pl.ds(qstart, tq), :]
    lane = lax.broadcasted_iota(jnp.int32, (1, 128), 1)
    qs = []
    for e in range(ns):
        if kind == "mla":
            qs.append(q[:, e * HEAD_PAD:(e + 1) * HEAD_PAD])
        else:
            qp = q[:, (e // 2) * 128:(e // 2 + 1) * 128]
            qs.append(jnp.where((lane < 64) if e % 2 == 0 else (lane >= 64), qp, jnp.zeros_like(qp)))
    lane_q = lax.broadcasted_iota(jnp.int32, (1, tq), 1)
    ref = [-kb_ref[0, 0, e:e + 1, pl.ds(qstart + (tq - 128), 128)][:, 127:128] for e in range(ns)]
    if kind == "diff":
        slope = [LOG2E * par_ref[1 + hg * (ns // 2) + p] for p in range(ns // 2)]

    tk = tq

    def k_slot(k, e):
        if kind == "mla":
            return k[:, e * HEAD_PAD:(e + 1) * HEAD_PAD]
        return k[:, (e // 2) * 128:(e // 2 + 1) * 128]

    def vt_slot(vt, e):
        r0 = (e // 2 if kind == "diff" else e) * vrows
        return vt[r0:r0 + vrows]

    def scores(j, slot, first=False):
        k0 = pl.multiple_of(j * tk, tk)
        k = k_ref[0, pl.ds(k0, tk), :]
        for e in range(ns):
            s = _dot_nt(k_slot(k, e), qs[e])
            if first or kind != "mla":
                bias = kbc_ref[e, pl.ds(k0, tk), :] + ref[e]
                s = s + jnp.concatenate([bias] * (tq // 128), axis=1)
            s_ref[slot, e] = s
            part = s[0:8]
            for r0 in range(8, tk, 8):
                part = jnp.maximum(part, s[r0:r0 + 8])
            mx_ref[slot, e] = part

    def softmax_pv(j, slot, diag):
        k0 = pl.multiple_of(j * tk, tk)
        vt = vt_ref[:, pl.ds(k0, tk)]
        for e in range(ns):
            def logits(r0, n, e=e):
                s = s_ref[slot, e, r0:r0 + n, :]
                if diag:
                    row_i = r0 + lax.broadcasted_iota(jnp.int32, (n, 1), 0)
                    if kind == "diff":
                        s = s - (2.0 * slope[e // 2]) * jnp.maximum(row_i - lane_q, 0).astype(F32)
                    if kind == "fox":
                        s = jnp.where(row_i <= lane_q, s, NEG)
                    elif r0 > 0:
                        s = jnp.where(lane_q >= r0, s, NEG)
                return s

            _online(logits, vt_slot(vt, e), e, m_ref, acc_ref, p_ref,
                    block_max=None if diag else mx_ref[slot, e])

    scores(0, 0, first=True)

    def pair_body(jj, c):
        j = 2 * jj
        scores(j + 1, 1)
        softmax_pv(j, 0, False)
        scores(j + 2, 0)
        softmax_pv(j + 1, 1, False)
        return c

    lax.fori_loop(0, i // 2, pair_body, 0)

    @pl.when(i % 2 == 0)
    def _():
        softmax_pv(i, 0, True)

    @pl.when(i % 2 == 1)
    def _():
        scores(i, 1)
        softmax_pv(i - 1, 0, False)
        softmax_pv(i, 1, True)

    outs = []
    for p in range(ns // 2):
        e0, e1 = 2 * p, 2 * p + 1
        if kind == "diff":
            a0, a1 = acc_ref[e0], acc_ref[e1]
            dv = 2 * DH_B
            o = (a0[:dv] / a0[dv:dv + 1] - par_ref[0] * (a1[:dv] / a1[dv:dv + 1])).T
            ms = jnp.mean(o * o, axis=-1, keepdims=True)
            outs.append(o * lax.rsqrt(ms + EPS) * g_ref[:, p * 128:(p + 1) * 128])
        else:
            a0, a1 = acc_ref[e0], acc_ref[e1]
            outs.append(jnp.concatenate([a0[:64] / a0[64:], a1[:64] / a1[64:]], axis=0).T)
    o_ref[0, pl.ds(qstart, tq), :] = (
        outs[0] if len(outs) == 1 else jnp.concatenate(outs, axis=1)).astype(o_ref.dtype)


def _prompt_attn_body(*refs, kind, tq, ns, nq):
    kb_ref, kbc_ref = (refs[4], refs[11]) if kind == "diff" else (refs[3], refs[9])
    def fill_bias_columns():
        def fill(c, carry):
            c0 = pl.multiple_of(c * 128, 128)
            for e in range(ns):
                row = kb_ref[0, 0, e:e + 1, pl.ds(c0, 128)]
                kbc_ref[e, pl.ds(c0, 128), :] = jnp.broadcast_to(row, (128, 128)).T
            return carry

        lax.fori_loop(0, kbc_ref.shape[1] // 128, fill, 0)

    if kind == "fox":
        hg = pl.program_id(1)
        fill_bias_columns()
    else:
        hg = pl.program_id(0)
        pl.when(pl.program_id(1) == 0)(fill_bias_columns)

    def query_block(i, carry):
        _attn_query_block(i, hg, refs, kind, tq, ns)
        return carry

    lax.fori_loop(0, nq, query_block, 0)


def _prompt_attn(kind, q, k, vt, kb, bsz, lp, par=None, gain=None):
    rows = q.shape[0]
    tq = ATT_BLOCK
    ns = ATT_STREAMS
    wq = (ns // 2) * (2 * HEAD_PAD if kind == "mla" else 128)
    vrows = DIFF_VT_ROWS if kind == "diff" else 128
    wv = (ns // 2) * (vrows if kind == "diff" else 2 * vrows)
    wo = (ns // 2) * 128
    n_hg = vt.shape[0] // wv
    nq = lp // tq
    kb_b, kb_h = kb.shape[0] > 1, kb.shape[1] > 1
    seq_major = kind == "fox"

    def bh(f):
        return (lambda b, h: f(b, h)) if seq_major else (lambda h, b: f(b, h))

    in_specs = [pl.BlockSpec((1, lp, wq), bh(lambda b, h: (0, b, h))),
                pl.BlockSpec((1, lp, wq), bh(lambda b, h: (0, b, h))),
                pl.BlockSpec((wv, lp), bh(lambda b, h: (h, b))),
                pl.BlockSpec((1, 1, ns, lp), bh(lambda b, h: (b if kb_b else 0, h if kb_h else 0, 0, 0)))]
    args = [q[None], k[None], vt, kb]
    if kind == "diff":
        in_specs = ([pl.BlockSpec(memory_space=pltpu.SMEM)] + in_specs
                    + [pl.BlockSpec((1, wo), bh(lambda b, h: (0, h)))])
        args = [par] + args + [gain]
    assert ATT_ROWS == CHUNK
    body = functools.partial(_prompt_attn_body, kind=kind, tq=tq, ns=ns, nq=nq)
    return pl.pallas_call(
        body,
        grid=(bsz, n_hg) if seq_major else (n_hg, bsz),
        in_specs=in_specs,
        out_specs=pl.BlockSpec((1, lp, wo), bh(lambda b, h: (0, b, h))),
        out_shape=jax.ShapeDtypeStruct((1, rows, n_hg * wo), BF16),
        scratch_shapes=[pltpu.VMEM((ns, 1, tq), F32),
                        pltpu.VMEM((ns, vrows, tq), F32), pltpu.VMEM((2, ns, tq, tq), F32),
                        pltpu.VMEM((ns, tq, tq), BF16), pltpu.VMEM((ns, lp, 128), F32),
                        pltpu.VMEM((2, ns, 8, tq), F32)],
        compiler_params=pltpu.CompilerParams(
            dimension_semantics=("arbitrary", "arbitrary"), vmem_limit_bytes=VMEM_LIMIT),
    )(*args)[0]


def _cumsum_body(x_ref, tri_ref, o_ref, carry_ref):
    @pl.when(pl.program_id(0) == 0)
    def _():
        carry_ref[...] = jnp.zeros(carry_ref.shape, F32)

    y = jnp.dot(x_ref[...], tri_ref[...], preferred_element_type=F32,
                precision=lax.Precision.HIGHEST) + carry_ref[...]
    o_ref[...] = y
    carry_ref[...] = y[:, -1:]


def _cumsum_lanes(x, blk=256):
    rows, n = x.shape
    tri = (jnp.arange(blk)[:, None] <= jnp.arange(blk)[None, :]).astype(F32)
    return pl.pallas_call(
        _cumsum_body,
        grid=(n // blk,),
        in_specs=[pl.BlockSpec((rows, blk), lambda j: (0, j)), _const_spec((blk, blk))],
        out_specs=pl.BlockSpec((rows, blk), lambda j: (0, j)),
        out_shape=jax.ShapeDtypeStruct((rows, n), F32),
        scratch_shapes=[pltpu.VMEM((rows, 1), F32)],
        compiler_params=pltpu.CompilerParams(dimension_semantics=("arbitrary",)),
    )(x, tri)


def _expand_rows(x, rep):
    h, n = x.shape
    return jnp.broadcast_to(x[:, None, :], (h, rep, n)).reshape(h * rep, n)


def _dec_online(s, v16, m_ref, l_ref, acc_ref):
    m_prev = m_ref[...]
    m_new = jnp.maximum(m_prev, jnp.max(s, axis=-1, keepdims=True))
    alpha = jnp.exp2(m_prev - m_new)
    p = jnp.exp2(s - m_new)
    l_ref[...] = alpha * l_ref[...] + jnp.sum(p, axis=-1, keepdims=True)
    acc_ref[...] = alpha * acc_ref[...] + _dot(p.astype(BF16), v16)
    m_ref[...] = m_new


def _dec_init(m_ref, l_ref, acc_ref):
    m_ref[...] = jnp.full(m_ref.shape, NEG, F32)
    l_ref[...] = jnp.zeros(l_ref.shape, F32)
    acc_ref[...] = jnp.zeros(acc_ref.shape, F32)


def _diag_blocks(o, n_heads, ds, width):
    return jnp.concatenate([o[h * ds:(h + 1) * ds, h * width:(h + 1) * width] for h in range(n_heads)], axis=1)


def _diff_dec_body(par_ref, q_ref, kc_ref, vc_ref, kn_ref, vn_ref, g_ref, _, o_ref, m_ref, l_ref, acc_ref,
                   *, past, kb, ds, n_heads):
    jb = pl.program_id(1)
    hr = 2 * ds
    rows = n_heads * hr
    dv = 2 * DH_B
    r = lax.broadcasted_iota(jnp.int32, (rows, 1), 0)
    head = r // hr
    slope = LOG2E * jnp.exp2(-8.0 * (head + 1).astype(F32) / n_heads)
    qpos = past + (r % ds)
    q = q_ref[0]

    def key_block(k16, v16, key0):
        col = lax.broadcasted_iota(jnp.int32, (1, k16.shape[0]), 1)
        kpos = key0 + col // n_heads
        s = _dot_nt(q, k16) - slope * jnp.abs(qpos - kpos).astype(F32)
        s = jnp.where(col % n_heads == head, s, NEG)
        _dec_online(s, v16, m_ref, l_ref, acc_ref)

    @pl.when(jb == 0)
    def _():
        _dec_init(m_ref, l_ref, acc_ref)
        key_block(kn_ref[0], vn_ref[0], past)

    key_block(kc_ref[0, 0].astype(BF16), vc_ref[0, 0].astype(BF16), jb * kb)

    @pl.when(jb == pl.num_programs(1) - 1)
    def _():
        o = acc_ref[...] / l_ref[...]
        outs = []
        for h in range(n_heads):
            oh = o[h * hr:h * hr + ds] - par_ref[0] * o[h * hr + ds:(h + 1) * hr]
            ms = jnp.mean(oh * oh, axis=-1, keepdims=True)
            outs.append(oh * lax.rsqrt(ms + EPS) * g_ref[:, h * dv:(h + 1) * dv])
        o_ref[0] = jnp.concatenate(outs, axis=1).astype(o_ref.dtype)


def _fox_dec_body(q_ref, kc_ref, vc_ref, kn_ref, vn_ref, fc_ref, fn_ref, _, o_ref, m_ref, l_ref, acc_ref,
                  *, ds, n_heads):
    jb = pl.program_id(1)
    rows = n_heads * ds
    q = q_ref[0]
    fnew = fn_ref[0][:, :ds]
    fref = _expand_rows(fn_ref[0][:, 0:1], ds)

    @pl.when(jb == 0)
    def _():
        _dec_init(m_ref, l_ref, acc_ref)
        r = lax.broadcasted_iota(jnp.int32, (rows, 1), 0)
        kidx = lax.broadcasted_iota(jnp.int32, (1, ds), 1)
        s = _dot_nt(q, kn_ref[0]) + LOG2E * (fref - _expand_rows(fnew, ds))
        s = jnp.where(kidx <= (r % ds), s, NEG)
        _dec_online(s, vn_ref[0].astype(BF16), m_ref, l_ref, acc_ref)

    s = _dot_nt(q, kc_ref[0, 0].astype(BF16)) + LOG2E * (fref - _expand_rows(fc_ref[0], ds))
    _dec_online(s, vc_ref[0, 0].astype(BF16), m_ref, l_ref, acc_ref)

    @pl.when(jb == pl.num_programs(1) - 1)
    def _():
        o = acc_ref[...] / l_ref[...]
        o_ref[0] = _diag_blocks(o, n_heads, ds, DH_C).astype(o_ref.dtype)


def _mla_dec_body(qn_ref, qp_ref, cc_ref, pc_ref, cn_ref, pn_ref, wk_ref, wv_ref, ones_ref, _,
                  o_ref, m_ref, l_ref, acc_ref, *, ds, n_heads):
    jb = pl.program_id(1)
    qn = qn_ref[0]
    qp = qp_ref[0]
    ones_h = ones_ref[...]

    def key_block(ckv, kpe):
        c16 = ckv.astype(BF16)
        kn = _dot(c16, wk_ref[...])
        v = _dot(c16, wv_ref[...])
        n = kpe.shape[0]
        ss = _dot_nt(ones_h, (kn * kn).astype(BF16)) + _dot_nt(jnp.ones((n_heads, ROPE_D), BF16),
                                                               (kpe * kpe).astype(BF16))
        rinv = lax.rsqrt(ss * (1.0 / (NOPE_D + ROPE_D)) + EPS)
        s = _dot_nt(qn, kn.astype(BF16)) + _dot_nt(qp, kpe.astype(BF16))
        s = s * _expand_rows(rinv, ds)
        _dec_online(s, v.astype(BF16), m_ref, l_ref, acc_ref)

    @pl.when(jb == 0)
    def _():
        _dec_init(m_ref, l_ref, acc_ref)
        key_block(cn_ref[0], pn_ref[0])

    key_block(cc_ref[0, 0], pc_ref[0, 0])

    @pl.when(jb == pl.num_programs(1) - 1)
    def _():
        o = acc_ref[...] / l_ref[...]
        o_ref[0] = _diag_blocks(o, n_heads, ds, V_D).astype(o_ref.dtype)


def _per_seq(a):
    return (a, pl.BlockSpec((1,) + a.shape[1:], lambda b, j: (b, 0, 0)))


def _dec_const(a):
    return (a, _const_spec(a.shape))


def _dec_call(body, ins, prev, row0, ds, rows, acc_w, nb, n_kb, smem=None):
    in_specs = [spec for _, spec in ins] + [pl.BlockSpec(memory_space=pl.ANY)]
    args = [a for a, _ in ins] + [prev[None]]
    if smem is not None:
        in_specs = [pl.BlockSpec(memory_space=pltpu.SMEM)] + in_specs
        args = [smem] + args
    blk0 = row0 // ds
    return pl.pallas_call(
        body,
        grid=(nb, n_kb),
        in_specs=in_specs,
        out_specs=pl.BlockSpec((1, ds, prev.shape[1]), lambda b, j: (0, blk0 + b, 0)),
        out_shape=jax.ShapeDtypeStruct((1,) + prev.shape, prev.dtype),
        input_output_aliases={len(args) - 1: 0},
        scratch_shapes=[pltpu.VMEM((rows, 1), F32), pltpu.VMEM((rows, 1), F32), pltpu.VMEM((rows, acc_w), F32)],
        compiler_params=pltpu.CompilerParams(
            dimension_semantics=("parallel", "arbitrary"), vmem_limit_bytes=VMEM_LIMIT),
    )(*args)[0]


def kernel(x_prompt, x_sample, state_s5_re, state_s5_im, cache_diff_k, cache_diff_v, cache_fox_k, cache_fox_v, cache_fox_logf, cache_mla_ckv, cache_mla_kpe, meta_tokens, ffn_norm, ffn_w_in, ffn_w_out, mix_norm, ab_w_in, ab_w_out, s5_a_re, s5_a_im, s5_log_step, s5_b_re, s5_b_im, s5_c_re, s5_c_im, s5_d, s5_glu_w, s5_glu_b, diff_q_norm, diff_k_norm, diff_lam, diff_sub_norm, cd_w_in, cd_w_out, fox_q_norm, fox_k_norm, fox_f_bias, mla_q_a_norm, mla_q_b, mla_kv_a_norm, mla_kv_b, mla_q_norm, mla_k_norm):
    bsz, seq, dm = x_prompt.shape
    nb, ds, _ = x_sample.shape
    n_meta = meta_tokens.shape[0]
    past = cache_diff_k.shape[2]
    front = ROW_ALIGN - n_meta
    lp = front + n_meta + seq
    ltot = n_meta + seq
    assert n_meta + front == ROW_ALIGN and lp % ATT_BLOCK == 0 and front % CHUNK == CHUNK - n_meta
    assert ds == S5_STEP and past % CHUNK == 0 and ds <= CHUNK
    kb = min(DEC_KB, past)
    assert past % kb == 0
    n_kb = past // kb
    assert ffn_norm.shape[0] == 2 and ab_w_in.shape[0] == 1 and cd_w_in.shape[0] == 1

    h_b = cache_diff_k.shape[3]
    h_c = cache_fox_k.shape[3]
    h_d = mla_q_b.shape[2] // (NOPE_D + ROPE_D)
    s5w = s5_glu_w.shape[1]
    n_grp = s5w // S5_GROUP
    qkw = h_b * 2 * DH_B
    fox_w = h_c * DH_C
    q_lora = mla_q_a_norm.shape[1]
    kv_lora = mla_kv_a_norm.shape[1]
    d_qk = NOPE_D + ROPE_D

    n_p = bsz * lp
    head_rows_x = jnp.concatenate([jnp.zeros((front, dm), F32), meta_tokens.astype(F32)], axis=0)
    pieces = []
    for b in range(bsz):
        pieces += [head_rows_x, x_prompt[b]]
    x = jnp.concatenate(pieces + [x_sample.reshape(nb * ds, dm)], axis=0)
    rows = x.shape[0]
    tm = _row_tile(rows)

    ones64 = _block_diag_ones(64)
    ones128 = _block_diag_ones(128)

    x = _ffn(x, ffn_norm[0, 0], ffn_w_in[0, 0], ffn_w_out[0, 0], tm)

    gq = (jnp.tile(diff_q_norm[0], 2 * h_b) * (DH_B ** -0.5 * LOG2E)).reshape(1, qkw)
    gk = jnp.tile(diff_k_norm[0], 2 * h_b).reshape(1, qkw)
    s5_cols = S5_STEP * S5_GROUP
    q16, k16, u_t, vt_diff, k32h, v32h = _row_call(
        functools.partial(_ab_in_body, widths=(s5w, qkw)),
        [x], [mix_norm[0].reshape(1, dm), ab_w_in[0].astype(BF16), gq, gk, ones64],
        [qkw, qkw], [BF16, BF16], tm,
        group_outs=[jax.ShapeDtypeStruct((n_grp, rows // S5_STEP, s5_cols), BF16)],
        scratch=[pltpu.VMEM((s5w // 128, tm, 128), F32)],
        col_outs=[(h_b * DIFF_VT_ROWS, BF16)], head_outs=[(h_b, F32), (h_b, F32)])
    k32 = k32h.reshape(rows, h_b, 2 * DH_B)
    v32 = v32h.reshape(rows, h_b, 2 * DH_B)

    mats = _s5_matrices(s5_a_re[0], s5_a_im[0], s5_log_step[0], s5_b_re[0], s5_b_im[0],
                        s5_c_re[0], s5_c_im[0], s5_d[0])
    n_ch = lp // S5_STEP
    y_t, st_p = _s5_scan(u_t, jnp.zeros((n_grp, bsz, 4 * P_A), F32), mats, n_ch, bsz, 0)
    h_re = jnp.transpose(state_s5_re[0].astype(F32), (1, 0, 2))
    h_im = jnp.transpose(state_s5_im[0].astype(F32), (1, 0, 2))
    y_t, st_s = _s5_scan(u_t, jnp.concatenate([h_re, h_im, h_im, h_re], axis=-1), mats, 1, nb,
                         n_p // S5_STEP, y_prev=y_t)

    lv = diff_lam[0].astype(F32)
    lam_init = 0.8 - 0.6 * math.exp(-0.3 * 0)
    lam = jnp.exp(jnp.sum(lv[0] * lv[1])) - jnp.exp(jnp.sum(lv[2] * lv[3])) + lam_init
    slopes = jnp.exp2(-8.0 * jnp.arange(1, h_b + 1, dtype=F32) / h_b)
    par = jnp.concatenate([lam[None], slopes]).astype(F32)
    subg = (jnp.tile(diff_sub_norm[0], h_b) * (1.0 - lam_init)).reshape(1, qkw)
    kpad = jnp.arange(lp) < front
    kb_diff = jnp.where(kpad[None, :], NEG, LOG2E * slopes[:, None] * jnp.arange(lp, dtype=F32)[None, :])
    kb_diff = jnp.broadcast_to(kb_diff[None, :, None, :], (1, h_b, 2, lp)).reshape(
        1, 2 * h_b // ATT_STREAMS, ATT_STREAMS, lp)
    o_all = _prompt_attn("diff", q16, k16, vt_diff, kb_diff, bsz, lp, par=par, gain=subg)
    qs = q16[n_p:].reshape(nb, ds, h_b, 2, DH_B)
    eye_2 = jnp.eye(2, dtype=BF16)
    qbd = jnp.einsum('bqhmd,mM->bhmqMd', qs, eye_2).reshape(nb, h_b * 2 * ds, 2 * DH_B)
    cache_spec = lambda w: pl.BlockSpec((1, 1, kb, w), lambda b, j: (0, b, j, 0))
    head_rows = lambda a: a.reshape(a.shape[0], nb, past * h_b, 2 * DH_B)
    head_cache_spec = pl.BlockSpec((1, 1, kb * h_b, 2 * DH_B), lambda b, j: (0, b, j, 0))
    o_all = _dec_call(
        functools.partial(_diff_dec_body, past=past, kb=kb, ds=ds, n_heads=h_b),
        [_per_seq(qbd), (head_rows(cache_diff_k), head_cache_spec), (head_rows(cache_diff_v), head_cache_spec),
         _per_seq(k16[n_p:].reshape(nb, ds * h_b, 2 * DH_B)), _per_seq(v32h[n_p * h_b:].astype(BF16).reshape(nb, ds * h_b, 2 * DH_B)),
         _dec_const(subg)],
        o_all, n_p, ds, 2 * h_b * ds, 2 * DH_B, nb, n_kb, smem=par)

    x = _row_call(
        functools.partial(_ab_out_body, s5w=s5w),
        [x, o_all],
        [s5_glu_w[0].astype(BF16), s5_glu_b[0].reshape(1, s5w), ab_w_out[0].astype(BF16)],
        [dm], [F32], tm, group_ins=[y_t], scratch=[pltpu.VMEM((s5w // 128, tm, 128), F32)])[0]

    x = _ffn(x, ffn_norm[0, 1], ffn_w_in[0, 1], ffn_w_out[0, 1], tm)

    x = _ffn(x, ffn_norm[1, 0], ffn_w_in[1, 0], ffn_w_out[1, 0], tm)

    half = ROPE_D // 2
    inv = ROPE_THETA ** (-jnp.arange(half, dtype=F32) / half)
    pos = jnp.concatenate([jnp.tile(jnp.arange(lp, dtype=jnp.int32) - front, bsz),
                           jnp.tile(past + jnp.arange(ds, dtype=jnp.int32), nb)]).astype(F32)
    ang = pos[:, None] * inv[None, :]
    pad_r = HEAD_PAD - NOPE_D - ROPE_D
    cos_t = jnp.concatenate([jnp.ones((rows, NOPE_D), F32), jnp.cos(ang), jnp.cos(ang),
                             jnp.zeros((rows, pad_r), F32)], axis=1)
    sin_t = jnp.concatenate([jnp.zeros((rows, NOPE_D), F32), jnp.sin(ang), jnp.sin(ang),
                             jnp.zeros((rows, pad_r), F32)], axis=1)

    wcd = cd_w_in[0]
    c_fg = 3 * fox_w
    c_qa = c_fg + h_c
    c_kva = c_qa + q_lora
    c_pe = c_kva + kv_lora
    w_pe = wcd[:, c_pe:c_pe + ROPE_D]
    zc = lambda n: jnp.zeros((dm, n), F32)
    w_cd = jnp.concatenate([
        wcd[:, :3 * fox_w], wcd[:, c_qa:c_qa + q_lora], wcd[:, c_kva:c_kva + kv_lora],
        zc(NOPE_D), w_pe, zc(pad_r),
        zc(NOPE_D), -w_pe[:, half:], w_pe[:, :half], zc(pad_r),
        wcd[:, c_fg:c_fg + h_c], zc(HEAD_PAD - h_c)], axis=1).astype(BF16)
    qb = mla_q_b[0].reshape(q_lora, h_d, d_qk)
    zq = lambda n: jnp.zeros((q_lora, h_d, n), F32)
    qb_pad = jnp.concatenate([qb, zq(pad_r)], axis=-1).reshape(q_lora, h_d * HEAD_PAD)
    qb_rot = jnp.concatenate([zq(NOPE_D), -qb[..., NOPE_D + half:], qb[..., NOPE_D:NOPE_D + half], zq(pad_r)],
                             axis=-1).reshape(q_lora, h_d * HEAD_PAD)
    wq2 = jnp.concatenate([qb_pad, qb_rot], axis=1).astype(BF16)
    kvb = mla_kv_b[0].reshape(kv_lora, h_d, NOPE_D + V_D)
    wk_pad = jnp.concatenate([kvb[..., :NOPE_D], jnp.zeros((kv_lora, h_d, HEAD_PAD - NOPE_D), F32)],
                             axis=-1).reshape(kv_lora, h_d * HEAD_PAD).astype(BF16)
    wk_cmp = kvb[..., :NOPE_D].reshape(kv_lora, h_d * NOPE_D).astype(BF16)
    wv_cmp = kvb[..., NOPE_D:].reshape(kv_lora, h_d * V_D).astype(BF16)
    gfq = (jnp.tile(fox_q_norm[0], h_c) * (DH_C ** -0.5 * LOG2E)).reshape(1, fox_w)
    gfk = jnp.tile(fox_k_norm[0], h_c).reshape(1, fox_w)
    fbias = jnp.concatenate([fox_f_bias[0], jnp.zeros((HEAD_PAD - h_c,), F32)]).reshape(1, HEAD_PAD)
    gmq = jnp.tile(jnp.concatenate([mla_q_norm[0] * mla_k_norm[0] * (d_qk ** -0.5 * LOG2E), jnp.zeros((pad_r,), F32)]),
                   h_d).reshape(1, h_d * HEAD_PAD)

    (fq16, fk32, fk16, fv32, logf, qm16, ckv32, kpe32, km16, fvt, vmt) = _row_call(
        functools.partial(_cd_in_body, parts=IN_PARTS, fox_w=fox_w, q_lora=q_lora, kv_lora=kv_lora, n_heads=h_d),
        [x, cos_t, sin_t],
        [mix_norm[1].reshape(1, dm), w_cd, gfq, gfk, fbias, mla_q_a_norm[0].reshape(1, q_lora), wq2,
         mla_kv_a_norm[0].reshape(1, kv_lora), wk_pad, wv_cmp.T, gmq, ones64, ones128],
        [fox_w, fox_w, fox_w, fox_w, h_c, h_d * HEAD_PAD, kv_lora, ROPE_D, h_d * HEAD_PAD],
        [BF16, F32, BF16, F32, F32, BF16, F32, F32, BF16], tm,
        col_outs=[(2 * fox_w, BF16), (2 * h_d * V_D, BF16)])

    logf_p = jnp.transpose(logf[:n_p].reshape(bsz, lp, h_c), (0, 2, 1)).reshape(bsz * h_c, lp)
    f_p = _cumsum_lanes(logf_p).reshape(bsz, h_c, lp)
    logf_s = jnp.concatenate([
        jnp.transpose(cache_fox_logf[0].astype(F32), (0, 2, 1)),
        jnp.transpose(logf[n_p:].reshape(nb, ds, h_c), (0, 2, 1)),
        jnp.zeros((nb, h_c, 256 - ds), F32)], axis=2).reshape(nb * h_c, past + 256)
    f_s = _cumsum_lanes(logf_s).reshape(nb, h_c, past + 256)

    kb_fox = jnp.where(kpad[None, None, :], NEG, -LOG2E * f_p).reshape(bsz, h_c // ATT_STREAMS, ATT_STREAMS, lp)
    oc_all = _prompt_attn("fox", fq16, fk16, fvt, kb_fox, bsz, lp)
    kb_mla = jnp.broadcast_to(jnp.where(kpad, NEG, 0.0).astype(F32)[None, None, None, :], (1, 1, ATT_STREAMS, lp))
    od_all = _prompt_attn("mla", qm16, km16, vmt, kb_mla, bsz, lp)

    eye_c = jnp.eye(h_c, dtype=BF16)
    fqs = fq16[n_p:].reshape(nb, ds, h_c, DH_C)
    fq_bd = jnp.einsum('bqhd,hH->bhqHd', fqs, eye_c).reshape(nb, h_c * ds, fox_w)
    fkc = cache_fox_k.reshape(cache_fox_k.shape[0], nb, past, fox_w)
    fvc = cache_fox_v.reshape(cache_fox_v.shape[0], nb, past, fox_w)
    oc_all = _dec_call(
        functools.partial(_fox_dec_body, ds=ds, n_heads=h_c),
        [_per_seq(fq_bd), (fkc, cache_spec(fox_w)), (fvc, cache_spec(fox_w)),
         _per_seq(fk16[n_p:].reshape(nb, ds, fox_w)), _per_seq(fv32[n_p:].reshape(nb, ds, fox_w)),
         (f_s, pl.BlockSpec((1, h_c, kb), lambda b, j: (b, 0, j))),
         (f_s, pl.BlockSpec((1, h_c, 128), lambda b, j: (b, 0, past // 128)))],
        oc_all, n_p, ds, h_c * ds, fox_w, nb, n_kb)

    eye_d = jnp.eye(h_d, dtype=BF16)
    qms = qm16[n_p:].reshape(nb, ds, h_d, HEAD_PAD)
    qn_bd = jnp.einsum('bqhd,hH->bhqHd', qms[..., :NOPE_D], eye_d).reshape(nb, h_d * ds, h_d * NOPE_D)
    qp_s = jnp.transpose(qms[..., NOPE_D:NOPE_D + ROPE_D], (0, 2, 1, 3)).reshape(nb, h_d * ds, ROPE_D)
    ones_h = jnp.repeat(jnp.eye(h_d, dtype=BF16), NOPE_D, axis=1)
    od_all = _dec_call(
        functools.partial(_mla_dec_body, ds=ds, n_heads=h_d),
        [_per_seq(qn_bd), _per_seq(qp_s),
         (cache_mla_ckv, pl.BlockSpec((1, 1, kb, kv_lora), lambda b, j: (0, b, j, 0))),
         (cache_mla_kpe, pl.BlockSpec((1, 1, kb, ROPE_D), lambda b, j: (0, b, j, 0))),
         _per_seq(ckv32[n_p:].reshape(nb, ds, kv_lora)), _per_seq(kpe32[n_p:].reshape(nb, ds, ROPE_D)),
         _dec_const(wk_cmp), _dec_const(wv_cmp), _dec_const(ones_h)],
        od_all, n_p, ds, h_d * ds, h_d * V_D, nb, n_kb)

    x = _row_call(functools.partial(_cd_out_body, fox_w=fox_w), [x, oc_all, od_all],
                  [cd_w_out[0].astype(BF16)], [dm], [F32], tm)[0]

    x = _ffn(x, ffn_norm[1, 1], ffn_w_in[1, 1], ffn_w_out[1, 1], tm)

    def p_rows(a, shape):
        return a[:n_p].reshape((bsz, lp) + a.shape[1:])[:, front:front + ltot].reshape((1, bsz, ltot) + shape)

    def s_rows(a, shape):
        return a[n_p:].reshape((1, nb, ds) + shape)

    def s5_state(st):
        st = jnp.transpose(st, (1, 0, 2))
        return st[None, :, :, :P_A], st[None, :, :, P_A:]

    y_prompt = jnp.stack([x[b * lp + front + n_meta:(b + 1) * lp] for b in range(bsz)])
    y_sample = x[n_p:].reshape(nb, ds, dm)
    s5_re_p, s5_im_p = s5_state(st_p)
    s5_re_s, s5_im_s = s5_state(st_s)
    return (y_prompt, y_sample,
            s5_re_p, s5_im_p, p_rows(k32, (h_b, 2 * DH_B)), p_rows(v32, (h_b, 2 * DH_B)),
            p_rows(fk32, (h_c, DH_C)), p_rows(fv32, (h_c, DH_C)), p_rows(logf, (h_c,)),
            p_rows(ckv32, (kv_lora,)), p_rows(kpe32, (ROPE_D,)),
            s5_re_s, s5_im_s, s_rows(k32, (h_b, 2 * DH_B)), s_rows(v32, (h_b, 2 * DH_B)),
            s_rows(fk32, (h_c, DH_C)), s_rows(fv32, (h_c, DH_C)), s_rows(logf, (h_c,)),
            s_rows(ckv32, (kv_lora,)), s_rows(kpe32, (ROPE_D,)))
```

```python
import functools
import math

import jax
import jax.numpy as jnp
from jax import lax
from jax.experimental import pallas as pl
from jax.experimental.pallas import tpu as pltpu

F32 = jnp.float32
BF16 = jnp.bfloat16

EPS = 1e-6
CHUNK = 64
ROW_ALIGN = 256
S5_GROUP = 16
S5_STEP = 16
S5_GROUPS_PER_STEP = 4
P_A = 64
DH_B = 64
DH_C = 64
NOPE_D = 64
ROPE_D = 32
V_D = 64
HEAD_PAD = 128
ROPE_THETA = 10000.0
NEG = -1e30
LOG2E = math.log2(math.e)
VMEM_LIMIT = 56 * 1024 * 1024
IN_PARTS = 2
OUT_PARTS = 4
ATT_BLOCK = 256
ATT_STREAMS = 4
ATT_ROWS = 64
DIFF_VT_ROWS = 2 * DH_B + 16
DEC_KB = 4096


def _dot(a, b):
    return jnp.dot(a, b, preferred_element_type=F32)


def _dot_nt(a, b):
    return lax.dot_general(a, b, (((1,), (1,)), ((), ())), preferred_element_type=F32)


def _rms_rows(x, g):
    ms = jnp.mean(x * x, axis=-1, keepdims=True)
    return x * lax.rsqrt(ms + EPS) * g


def _group_sumsq(x, ones_bd):
    w = x.shape[-1]
    parts = [_dot((x[:, c:c + 256] * x[:, c:c + 256]).astype(BF16), ones_bd) for c in range(0, w, 256)]
    return parts[0] if len(parts) == 1 else jnp.concatenate(parts, axis=1)


def _block_diag_ones(group, n=256):
    r = jnp.arange(n) // group
    return (r[:, None] == r[None, :]).astype(BF16)


def _const_spec(shape):
    nd = len(shape)
    return pl.BlockSpec(shape, lambda *_: (0,) * nd, pipeline_mode=pl.Buffered(1))


def _row_tile(rows, cap=512):
    t = cap
    while rows % t:
        t //= 2
    return t


def _group_spec(a, tm):
    return pl.BlockSpec((a.shape[0], tm // S5_STEP, a.shape[2]), lambda i: (0, i, 0))


def _row_call(body, row_ins, consts, out_widths, out_dtypes, tm, group_ins=(), group_outs=(), scratch=(),
              col_outs=(), head_outs=()):
    rows = row_ins[0].shape[0]
    in_specs = [pl.BlockSpec((tm, a.shape[1]), lambda i: (i, 0)) for a in row_ins]
    in_specs += [_group_spec(a, tm) for a in group_ins]
    in_specs += [_const_spec(c.shape) for c in consts]
    out_specs = [pl.BlockSpec((tm, w), lambda i: (i, 0)) for w in out_widths]
    out_specs += [_group_spec(a, tm) for a in group_outs]
    out_specs += [pl.BlockSpec((w, tm), lambda i: (0, i)) for w, _ in col_outs]
    out_shape = [jax.ShapeDtypeStruct((rows, w), d) for w, d in zip(out_widths, out_dtypes)]
    out_shape += list(group_outs)
    out_shape += [jax.ShapeDtypeStruct((w, rows), d) for w, d in col_outs]
    out_specs += [pl.BlockSpec((tm * h, 128), lambda i: (i, 0)) for h, _ in head_outs]
    out_shape += [jax.ShapeDtypeStruct((rows * h, 128), d) for h, d in head_outs]
    return pl.pallas_call(
        body,
        grid=(rows // tm,),
        in_specs=in_specs,
        out_specs=out_specs,
        out_shape=out_shape,
        scratch_shapes=list(scratch),
        compiler_params=pltpu.CompilerParams(
            dimension_semantics=("parallel",), vmem_limit_bytes=VMEM_LIMIT),
    )(*row_ins, *group_ins, *consts)


def _ffn_body(x_ref, g_ref, win_ref, wout_ref, o_ref, *, d_ff, tf):
    x = x_ref[...]
    xn = _rms_rows(x, g_ref[...]).astype(BF16)
    acc = jnp.zeros(x.shape, F32)
    for c in range(0, d_ff, tf):
        gate = _dot(xn, win_ref[:, c:c + tf])
        up = _dot(xn, win_ref[:, d_ff + c:d_ff + c + tf])
        a = (gate * jax.nn.sigmoid(gate) * up).astype(BF16)
        acc = acc + _dot(a, wout_ref[c:c + tf, :])
    o_ref[...] = x + 0.5 * acc


def _ffn(x, g, w_in, w_out, tm):
    d_ff = w_out.shape[0]
    body = functools.partial(_ffn_body, d_ff=d_ff, tf=256)
    return _row_call(body, [x], [g.reshape(1, -1), w_in.astype(BF16), w_out.astype(BF16)],
                     [x.shape[1]], [F32], tm)[0]


def _ab_in_body(x_ref, g_ref, w_ref, gq_ref, gk_ref, ones_ref,
                q_ref, k16_ref, ut_ref, vt_ref, k32_ref, v32_ref, us_ref, *, widths):
    s5w, qkw = widths
    xn = _rms_rows(x_ref[...], g_ref[...]).astype(BF16)
    h = _dot(xn, w_ref[...])
    n_chunk = us_ref.shape[1] // S5_STEP
    per_col = 128 // S5_GROUP
    for v in range(s5w // 128):
        us_ref[v] = h[:, v * 128:(v + 1) * 128]
        steps = [us_ref[v, pl.ds(t, n_chunk, stride=S5_STEP), :] for t in range(S5_STEP)]
        for gl in range(per_col):
            ut_ref[v * per_col + gl] = jnp.concatenate(
                [x[:, gl * S5_GROUP:(gl + 1) * S5_GROUP] for x in steps], axis=1).astype(BF16)
    q = h[:, s5w:s5w + qkw]
    k = h[:, s5w + qkw:s5w + 2 * qkw]
    v = h[:, s5w + 2 * qkw:]
    ones_bd = ones_ref[...]
    qn = q * lax.rsqrt(_group_sumsq(q, ones_bd) * (1.0 / DH_B) + EPS) * gq_ref[...]
    kn = k * lax.rsqrt(_group_sumsq(k, ones_bd) * (1.0 / DH_B) + EPS) * gk_ref[...]
    q_ref[...] = qn.astype(BF16)
    k16_ref[...] = kn.astype(BF16)
    dv = 2 * DH_B
    n_head = qkw // dv
    tm = kn.shape[0]
    for hd in range(n_head):
        k32_ref[pl.ds(hd, tm, stride=n_head), :] = kn[:, hd * dv:(hd + 1) * dv]
        v32_ref[pl.ds(hd, tm, stride=n_head), :] = v[:, hd * dv:(hd + 1) * dv]
    vt = v.T
    ones = jnp.ones((DIFF_VT_ROWS - dv, vt.shape[1]), F32)
    vt_ref[...] = jnp.concatenate(
        [a for h in range(vt.shape[0] // dv) for a in (vt[h * dv:(h + 1) * dv], ones)], axis=0).astype(BF16)


def _ab_out_body(x_ref, o_ref, yt_ref, gluw_ref, glub_ref, wout_ref, out_ref, ys_ref, *, s5w):
    tm = ys_ref.shape[1]
    part = tm // OUT_PARTS
    n_chunk = part // S5_STEP
    per_col = 128 // S5_GROUP
    for h in range(OUT_PARTS):
        c0 = h * n_chunk
        for v in range(s5w // 128):
            for t in range(S5_STEP):
                ys_ref[v, pl.ds(h * part + t, n_chunk, stride=S5_STEP), :] = jnp.concatenate(
                    [yt_ref[v * per_col + gl, c0:c0 + n_chunk, t * S5_GROUP:(t + 1) * S5_GROUP]
                     for gl in range(per_col)], axis=1)
        rows = slice(h * part, (h + 1) * part)
        y = jnp.concatenate([ys_ref[v, rows] for v in range(s5w // 128)], axis=1)
        g = 0.5 * y * (1.0 + jnp.tanh(math.sqrt(2.0 / math.pi) * (y + 0.044715 * (y * y * y))))
        z = _dot(g.astype(BF16), gluw_ref[...]) + glub_ref[...]
        s5o = g * jax.nn.sigmoid(z)
        m = _dot(s5o.astype(BF16), wout_ref[:s5w, :]) + _dot(o_ref[rows], wout_ref[s5w:, :])
        out_ref[rows] = x_ref[rows] + m


def _heads_with_ones_t(vt):
    ones = jnp.ones((64, vt.shape[1]), vt.dtype)
    outs = []
    for h in range(vt.shape[0] // 64):
        outs += [vt[h * 64:(h + 1) * 64], ones]
    return jnp.concatenate(outs, axis=0)


def _cd_in_body(*refs, parts, **dims):
    n = refs[0].shape[0] // parts
    for h in range(parts):
        rows = pl.ds(h * n, n)
        _cd_in_part(*[r.at[rows] for r in refs[:3]], *refs[3:16],
                    *[r.at[rows] for r in refs[16:23]], *[r.at[:, rows] for r in refs[23:27]], **dims)


def _cd_in_part(x_ref, cos_ref, sin_ref, g_ref, w_ref, gfq_ref, gfk_ref, fb_ref, gqa_ref, wq2_ref,
                gkva_ref, wk_ref, wv_ref, gmq_ref, ones64_ref, ones128_ref,
                fq_ref, fk32_ref, fk16_ref, fv32_ref, qm_ref, ckv_ref,
                km_ref, fvt_ref, vmt_ref, logft_ref, kpet_ref, *, fox_w, q_lora, kv_lora, n_heads):
    xn = _rms_rows(x_ref[...], g_ref[...]).astype(BF16)
    h = _dot(xn, w_ref[...])
    ones64 = ones64_ref[...]
    ones128 = ones128_ref[...]
    fq = h[:, :fox_w]
    fk = h[:, fox_w:2 * fox_w]
    fv = h[:, 2 * fox_w:3 * fox_w]
    c0 = 3 * fox_w
    qa = h[:, c0:c0 + q_lora]
    kva = h[:, c0 + q_lora:c0 + q_lora + kv_lora]
    c1 = c0 + q_lora + kv_lora
    pe_a = h[:, c1:c1 + HEAD_PAD]
    pe_b = h[:, c1 + HEAD_PAD:c1 + 2 * HEAD_PAD]
    fg = h[:, c1 + 2 * HEAD_PAD:c1 + 3 * HEAD_PAD]

    fqn = fq * lax.rsqrt(_group_sumsq(fq, ones64) * (1.0 / DH_C) + EPS) * gfq_ref[...]
    fkn = fk * lax.rsqrt(_group_sumsq(fk, ones64) * (1.0 / DH_C) + EPS) * gfk_ref[...]
    fq_ref[...] = fqn.astype(BF16)
    fk32_ref[...] = fkn
    fk16_ref[...] = fkn.astype(BF16)
    fv32_ref[...] = fv
    fvt_ref[...] = _heads_with_ones_t(fv.T).astype(BF16)

    z = fg + fb_ref[...]
    logf = jnp.minimum(z, 0.0) - jnp.log1p(jnp.exp(-jnp.abs(z)))
    logft_ref[...] = logf.T[:logft_ref.shape[0]]

    cos = cos_ref[...]
    sin = sin_ref[...]
    qan = _rms_rows(qa, gqa_ref[...]).astype(BF16)
    q2 = _dot(qan, wq2_ref[...])
    hw = n_heads * HEAD_PAD
    cos_t = jnp.concatenate([cos] * n_heads, axis=1)
    sin_t = jnp.concatenate([sin] * n_heads, axis=1)
    qr = q2[:, :hw] * cos_t + q2[:, hw:] * sin_t
    d_qk = NOPE_D + ROPE_D
    qm = qr * lax.rsqrt(_group_sumsq(qr, ones128) * (1.0 / d_qk) + EPS) * gmq_ref[...]
    qm_ref[...] = qm.astype(BF16)

    ckv = _rms_rows(kva, gkva_ref[...])
    ckv_ref[...] = ckv
    pe = pe_a * cos + pe_b * sin
    kpet_ref[...] = pe.T[NOPE_D:NOPE_D + ROPE_D]
    ckv16 = ckv.astype(BF16)
    kraw = _dot(ckv16, wk_ref[...]) + jnp.concatenate([pe] * n_heads, axis=1)
    km = kraw * lax.rsqrt(_group_sumsq(kraw, ones128) * (1.0 / d_qk) + EPS)
    km_ref[...] = km.astype(BF16)
    vmt_ref[...] = _heads_with_ones_t(_dot_nt(wv_ref[...], ckv16)).astype(BF16)


def _cd_out_body(x_ref, oc_ref, od_ref, wout_ref, out_ref, *, fox_w):
    m = _dot(oc_ref[...], wout_ref[:fox_w, :]) + _dot(od_ref[...], wout_ref[fox_w:, :])
    out_ref[...] = x_ref[...] + m


def _s5_body(*refs, n_chunks, bsz, aliased):
    if aliased:
        u_ref, h0_ref, m_ref, bm_ref, cm_ref, coef_ref, _, y_ref, st_ref, s2_ref, hp_ref = refs
    else:
        u_ref, h0_ref, m_ref, bm_ref, cm_ref, coef_ref, y_ref, st_ref, s2_ref, hp_ref = refs
    ng = u_ref.shape[0]
    half = 2 * P_A
    coefs = []
    for g in range(ng):
        s2 = _dot(u_ref[g], bm_ref[g])
        s2_ref[g, 0] = s2[:, :half]
        s2_ref[g, 1] = s2[:, half:]
        coefs.append((coef_ref[g, 0:1, :], coef_ref[g, 1:2, :], coef_ref[g, 2:3, :]))

    def step(j, carry):
        new = []
        for g in range(ng):
            ha, hb = carry[2 * g], carry[2 * g + 1]
            c1, c2, c3 = coefs[g]
            hp_ref[g, pl.ds(j, bsz, stride=n_chunks), :] = ha
            sa = s2_ref[g, 0, pl.ds(j, bsz, stride=n_chunks), :]
            sb = s2_ref[g, 1, pl.ds(j, bsz, stride=n_chunks), :]
            new += [ha * c1 + hb * c2 + sa, hb * c1 + ha * c3 + sb]
        return tuple(new)

    init = []
    for g in range(ng):
        init += [h0_ref[g][:, :half], h0_ref[g][:, half:]]
    final = lax.fori_loop(0, n_chunks, step, tuple(init))
    for g in range(ng):
        st_ref[g] = final[2 * g]
        y_ref[g] = _dot(u_ref[g], m_ref[g]) + _dot(hp_ref[g].astype(BF16), cm_ref[g])


def _s5_scan(u_t, h0, mats, n_chunks, bsz, row0, y_prev=None):
    m_mat, bm, cm, coef = mats
    g, rows_all, w = u_t.shape
    rows = n_chunks * bsz
    assert row0 % rows == 0
    blk = row0 // rows
    aliased = y_prev is not None
    body = functools.partial(_s5_body, n_chunks=n_chunks, bsz=bsz, aliased=aliased)
    ng = S5_GROUPS_PER_STEP
    per_g = lambda a: pl.BlockSpec((ng,) + a.shape[1:], lambda i: (i, 0, 0))
    in_specs = [pl.BlockSpec((ng, rows, w), lambda i: (i, blk, 0)),
                per_g(h0), per_g(m_mat), per_g(bm), per_g(cm), per_g(coef)]
    args = [u_t, h0, m_mat, bm, cm, coef]
    if aliased:
        in_specs.append(pl.BlockSpec(memory_space=pl.ANY))
        args.append(y_prev)
    return pl.pallas_call(
        body,
        grid=(g // ng,),
        in_specs=in_specs,
        out_specs=[pl.BlockSpec((ng, rows, w), lambda i: (i, blk, 0)),
                   pl.BlockSpec((ng, bsz, 2 * P_A), lambda i: (i, 0, 0))],
        out_shape=[jax.ShapeDtypeStruct((g, rows_all, w), F32),
                   jax.ShapeDtypeStruct((g, bsz, 2 * P_A), F32)],
        input_output_aliases={6: 0} if aliased else {},
        scratch_shapes=[pltpu.VMEM((ng, 2, rows, 2 * P_A), F32), pltpu.VMEM((ng, rows, 2 * P_A), F32)],
        compiler_params=pltpu.CompilerParams(
            dimension_semantics=("parallel",), vmem_limit_bytes=VMEM_LIMIT),
    )(*args)


def _s5_matrices(a_re, a_im, log_step, b_re, b_im, c_re, c_im, d):
    g = a_re.shape[0]
    t = S5_STEP
    lam = lax.complex(a_re, a_im)
    dl = lam * jnp.exp(log_step)[:, None]
    lam_bar = jnp.exp(dl)
    b_bar = ((lam_bar - 1.0) / lam)[..., None] * lax.complex(b_re, b_im)
    c = lax.complex(c_re, c_im)
    pw = jnp.exp(dl[:, None, :] * jnp.arange(t + 1, dtype=F32)[None, :, None])
    bmc = pw[:, t - 1::-1][:, :, :, None] * b_bar[:, None]
    bmc = jnp.swapaxes(bmc, 2, 3).reshape(g, t * S5_GROUP, P_A)
    bm = jnp.concatenate([bmc.real, bmc.imag, bmc.imag, bmc.real], axis=-1)
    kk = jnp.einsum('gcp,gkp,gpd->gkcd', c, pw[:, :t], b_bar).real
    kk = kk.at[:, 0].add(d.reshape(g, S5_GROUP)[:, :, None] * jnp.eye(S5_GROUP, dtype=F32))
    lag = jnp.arange(t)[None, :] - jnp.arange(t)[:, None]
    toep = jnp.where((lag >= 0)[None, :, :, None, None], kk[:, jnp.clip(lag, 0, t - 1)], 0.0)
    m_mat = jnp.transpose(toep, (0, 1, 4, 2, 3)).reshape(g, t * S5_GROUP, t * S5_GROUP)
    cp = c[:, None] * pw[:, 1:, None, :]
    cpm = jnp.transpose(cp, (0, 3, 1, 2)).reshape(g, P_A, t * S5_GROUP)
    cm = jnp.concatenate([cpm.real, -cpm.imag], axis=1)
    a_t = pw[:, t]
    ar, ai = a_t.real, a_t.imag
    zeros = jnp.zeros_like(ar)
    coef = jnp.stack([jnp.concatenate([ar, ar], -1), jnp.concatenate([-ai, ai], -1),
                      jnp.concatenate([ai, -ai], -1), jnp.concatenate([zeros, zeros], -1)], axis=1)
    return m_mat.astype(BF16), bm.astype(BF16), cm.astype(BF16), coef.astype(F32)


def _online(logits, vt, e, m_ref, acc_ref, p_ref, block_max=None):
    tk = p_ref.shape[1]
    if block_max is None:
        part = logits(0, ATT_ROWS)
        for r0 in range(ATT_ROWS, tk, ATT_ROWS):
            part = jnp.maximum(part, logits(r0, ATT_ROWS))
    else:
        part = block_max
    m_prev = m_ref[e]
    m_new = jnp.maximum(m_prev, jnp.max(part, axis=0, keepdims=True))
    alpha = jnp.exp2(m_prev - m_new)
    m_ref[e] = m_new
    for r0 in range(0, tk, ATT_ROWS):
        p_ref[e, r0:r0 + ATT_ROWS] = jnp.exp2(logits(r0, ATT_ROWS) - m_new).astype(BF16)
    acc_ref[e] = alpha * acc_ref[e] + _dot(vt, p_ref[e])


def _attn_query_block(i, hg, refs, kind, tq, ns):
    if kind == "diff":
        (par_ref, q_ref, k_ref, vt_ref, kb_ref, g_ref, o_ref,
         m_ref, acc_ref, s_ref, p_ref, kbc_ref, mx_ref) = refs
    else:
        q_ref, k_ref, vt_ref, kb_ref, o_ref, m_ref, acc_ref, s_ref, p_ref, kbc_ref, mx_ref = refs
    vrows = acc_ref.shape[1]
    m_ref[...] = jnp.full(m_ref.shape, NEG, F32)
    acc_ref[...] = jnp.zeros(acc_ref.shape, F32)

    qstart = pl.multiple_of(i * tq, tq)
    q = q_ref[0, pl.ds(qstart, tq), :]
    lane = lax.broadcasted_iota(jnp.int32, (1, 128), 1)
    qs = []
    for e in range(ns):
        if kind == "mla":
            qs.append(q[:, e * HEAD_PAD:(e + 1) * HEAD_PAD])
        else:
            qp = q[:, (e // 2) * 128:(e // 2 + 1) * 128]
            qs.append(jnp.where((lane < 64) if e % 2 == 0 else (lane >= 64), qp, jnp.zeros_like(qp)))
    lane_q = lax.broadcasted_iota(jnp.int32, (1, tq), 1)
    ref = [-kb_ref[0, 0, e:e + 1, pl.ds(qstart + (tq - 128), 128)][:, 127:128] for e in range(ns)]
    if kind == "diff":
        slope = [LOG2E * par_ref[1 + hg * (ns // 2) + p] for p in range(ns // 2)]

    tk = tq

    def k_slot(k, e):
        if kind == "mla":
            return k[:, e * HEAD_PAD:(e + 1) * HEAD_PAD]
        return k[:, (e // 2) * 128:(e // 2 + 1) * 128]

    def vt_slot(vt, e):
        r0 = (e // 2 if kind == "diff" else e) * vrows
        return vt[r0:r0 + vrows]

    def scores(j, slot, first=False):
        k0 = pl.multiple_of(j * tk, tk)
        k = k_ref[0, pl.ds(k0, tk), :]
        for e in range(ns):
            s = _dot_nt(k_slot(k, e), qs[e])
            if first or kind != "mla":
                bias = kbc_ref[e, pl.ds(k0, tk), :] + ref[e]
                s = s + jnp.concatenate([bias] * (tq // 128), axis=1)
            s_ref[slot, e] = s
            part = s[0:8]
            for r0 in range(8, tk, 8):
                part = jnp.maximum(part, s[r0:r0 + 8])
            mx_ref[slot, e] = part

    def softmax_pv(j, slot, diag):
        k0 = pl.multiple_of(j * tk, tk)
        vt = vt_ref[:, pl.ds(k0, tk)]
        for e in range(ns):
            def logits(r0, n, e=e):
                s = s_ref[slot, e, r0:r0 + n, :]
                if diag:
                    row_i = r0 + lax.broadcasted_iota(jnp.int32, (n, 1), 0)
                    if kind == "diff":
                        s = s - (2.0 * slope[e // 2]) * jnp.maximum(row_i - lane_q, 0).astype(F32)
                    if kind == "fox":
                        s = jnp.where(row_i <= lane_q, s, NEG)
                    elif r0 > 0:
                        s = jnp.where(lane_q >= r0, s, NEG)
                return s

            _online(logits, vt_slot(vt, e), e, m_ref, acc_ref, p_ref,
                    block_max=None if diag else mx_ref[slot, e])

    scores(0, 0, first=True)

    def pair_body(jj, c):
        j = 2 * jj
        scores(j + 1, 1)
        softmax_pv(j, 0, False)
        scores(j + 2, 0)
        softmax_pv(j + 1, 1, False)
        return c

    lax.fori_loop(0, i // 2, pair_body, 0)

    @pl.when(i % 2 == 0)
    def _():
        softmax_pv(i, 0, True)

    @pl.when(i % 2 == 1)
    def _():
        scores(i, 1)
        softmax_pv(i - 1, 0, False)
        softmax_pv(i, 1, True)

    outs = []
    for p in range(ns // 2):
        e0, e1 = 2 * p, 2 * p + 1
        if kind == "diff":
            a0, a1 = acc_ref[e0], acc_ref[e1]
            dv = 2 * DH_B
            o = (a0[:dv] / a0[dv:dv + 1] - par_ref[0] * (a1[:dv] / a1[dv:dv + 1])).T
            ms = jnp.mean(o * o, axis=-1, keepdims=True)
            outs.append(o * lax.rsqrt(ms + EPS) * g_ref[:, p * 128:(p + 1) * 128])
        else:
            a0, a1 = acc_ref[e0], acc_ref[e1]
            outs.append(jnp.concatenate([a0[:64] / a0[64:], a1[:64] / a1[64:]], axis=0).T)
    o_ref[0, pl.ds(qstart, tq), :] = (
        outs[0] if len(outs) == 1 else jnp.concatenate(outs, axis=1)).astype(o_ref.dtype)


def _prompt_attn_body(*refs, kind, tq, ns, nq):
    kb_ref, kbc_ref = (refs[4], refs[11]) if kind == "diff" else (refs[3], refs[9])
    def fill_bias_columns():
        def fill(c, carry):
            c0 = pl.multiple_of(c * 128, 128)
            for e in range(ns):
                row = kb_ref[0, 0, e:e + 1, pl.ds(c0, 128)]
                kbc_ref[e, pl.ds(c0, 128), :] = jnp.broadcast_to(row, (128, 128)).T
            return carry

        lax.fori_loop(0, kbc_ref.shape[1] // 128, fill, 0)

    if kind == "fox":
        hg = pl.program_id(1)
        fill_bias_columns()
    else:
        hg = pl.program_id(0)
        pl.when(pl.program_id(1) == 0)(fill_bias_columns)

    def query_block(i, carry):
        _attn_query_block(i, hg, refs, kind, tq, ns)
        return carry

    lax.fori_loop(0, nq, query_block, 0)


def _prompt_attn(kind, q, k, vt, kb, bsz, lp, par=None, gain=None):
    rows = q.shape[0]
    tq = ATT_BLOCK
    ns = ATT_STREAMS
    wq = (ns // 2) * (2 * HEAD_PAD if kind == "mla" else 128)
    vrows = DIFF_VT_ROWS if kind == "diff" else 128
    wv = (ns // 2) * (vrows if kind == "diff" else 2 * vrows)
    wo = (ns // 2) * 128
    n_hg = vt.shape[0] // wv
    nq = lp // tq
    kb_b, kb_h = kb.shape[0] > 1, kb.shape[1] > 1
    seq_major = kind == "fox"

    def bh(f):
        return (lambda b, h: f(b, h)) if seq_major else (lambda h, b: f(b, h))

    in_specs = [pl.BlockSpec((1, lp, wq), bh(lambda b, h: (0, b, h))),
                pl.BlockSpec((1, lp, wq), bh(lambda b, h: (0, b, h))),
                pl.BlockSpec((wv, lp), bh(lambda b, h: (h, b))),
                pl.BlockSpec((1, 1, ns, lp), bh(lambda b, h: (b if kb_b else 0, h if kb_h else 0, 0, 0)))]
    args = [q[None], k[None], vt, kb]
    if kind == "diff":
        in_specs = ([pl.BlockSpec(memory_space=pltpu.SMEM)] + in_specs
                    + [pl.BlockSpec((1, wo), bh(lambda b, h: (0, h)))])
        args = [par] + args + [gain]
    assert ATT_ROWS == CHUNK
    body = functools.partial(_prompt_attn_body, kind=kind, tq=tq, ns=ns, nq=nq)
    return pl.pallas_call(
        body,
        grid=(bsz, n_hg) if seq_major else (n_hg, bsz),
        in_specs=in_specs,
        out_specs=pl.BlockSpec((1, lp, wo), bh(lambda b, h: (0, b, h))),
        out_shape=jax.ShapeDtypeStruct((1, rows, n_hg * wo), BF16),
        scratch_shapes=[pltpu.VMEM((ns, 1, tq), F32),
                        pltpu.VMEM((ns, vrows, tq), F32), pltpu.VMEM((2, ns, tq, tq), F32),
                        pltpu.VMEM((ns, tq, tq), BF16), pltpu.VMEM((ns, lp, 128), F32),
                        pltpu.VMEM((2, ns, 8, tq), F32)],
        compiler_params=pltpu.CompilerParams(
            dimension_semantics=("arbitrary", "arbitrary"), vmem_limit_bytes=VMEM_LIMIT),
    )(*args)[0]


def _cumsum_body(x_ref, tri_ref, o_ref, carry_ref):
    @pl.when(pl.program_id(0) == 0)
    def _():
        carry_ref[...] = jnp.zeros(carry_ref.shape, F32)

    y = jnp.dot(x_ref[...], tri_ref[...], preferred_element_type=F32,
                precision=lax.Precision.HIGHEST) + carry_ref[...]
    o_ref[...] = y
    carry_ref[...] = y[:, -1:]


def _cumsum_lanes(x, blk=256):
    rows, n = x.shape
    tri = (jnp.arange(blk)[:, None] <= jnp.arange(blk)[None, :]).astype(F32)
    return pl.pallas_call(
        _cumsum_body,
        grid=(n // blk,),
        in_specs=[pl.BlockSpec((rows, blk), lambda j: (0, j)), _const_spec((blk, blk))],
        out_specs=pl.BlockSpec((rows, blk), lambda j: (0, j)),
        out_shape=jax.ShapeDtypeStruct((rows, n), F32),
        scratch_shapes=[pltpu.VMEM((rows, 1), F32)],
        compiler_params=pltpu.CompilerParams(dimension_semantics=("arbitrary",)),
    )(x, tri)


def _expand_rows(x, rep):
    h, n = x.shape
    return jnp.broadcast_to(x[:, None, :], (h, rep, n)).reshape(h * rep, n)


def _dec_online(s, v16, m_ref, l_ref, acc_ref):
    m_prev = m_ref[...]
    m_new = jnp.maximum(m_prev, jnp.max(s, axis=-1, keepdims=True))
    alpha = jnp.exp2(m_prev - m_new)
    p = jnp.exp2(s - m_new)
    l_ref[...] = alpha * l_ref[...] + jnp.sum(p, axis=-1, keepdims=True)
    acc_ref[...] = alpha * acc_ref[...] + _dot(p.astype(BF16), v16)
    m_ref[...] = m_new


def _dec_init(m_ref, l_ref, acc_ref):
    m_ref[...] = jnp.full(m_ref.shape, NEG, F32)
    l_ref[...] = jnp.zeros(l_ref.shape, F32)
    acc_ref[...] = jnp.zeros(acc_ref.shape, F32)


def _diag_blocks(o, n_heads, ds, width):
    return jnp.concatenate([o[h * ds:(h + 1) * ds, h * width:(h + 1) * width] for h in range(n_heads)], axis=1)


def _diff_dec_body(par_ref, q_ref, kc_ref, vc_ref, kn_ref, vn_ref, g_ref, _, o_ref, m_ref, l_ref, acc_ref,
                   *, past, kb, ds, n_heads):
    jb = pl.program_id(1)
    hr = 2 * ds
    rows = n_heads * hr
    dv = 2 * DH_B
    r = lax.broadcasted_iota(jnp.int32, (rows, 1), 0)
    head = r // hr
    slope = LOG2E * jnp.exp2(-8.0 * (head + 1).astype(F32) / n_heads)
    qpos = past + (r % ds)
    q = q_ref[0]

    def key_block(k16, v16, key0):
        col = lax.broadcasted_iota(jnp.int32, (1, k16.shape[0]), 1)
        kpos = key0 + col // n_heads
        s = _dot_nt(q, k16) - slope * jnp.abs(qpos - kpos).astype(F32)
        s = jnp.where(col % n_heads == head, s, NEG)
        _dec_online(s, v16, m_ref, l_ref, acc_ref)

    @pl.when(jb == 0)
    def _():
        _dec_init(m_ref, l_ref, acc_ref)
        key_block(kn_ref[0], vn_ref[0], past)

    key_block(kc_ref[0, 0].astype(BF16), vc_ref[0, 0].astype(BF16), jb * kb)

    @pl.when(jb == pl.num_programs(1) - 1)
    def _():
        o = acc_ref[...] / l_ref[...]
        outs = []
        for h in range(n_heads):
            oh = o[h * hr:h * hr + ds] - par_ref[0] * o[h * hr + ds:(h + 1) * hr]
            ms = jnp.mean(oh * oh, axis=-1, keepdims=True)
            outs.append(oh * lax.rsqrt(ms + EPS) * g_ref[:, h * dv:(h + 1) * dv])
        o_ref[0] = jnp.concatenate(outs, axis=1).astype(o_ref.dtype)


def _fox_dec_body(q_ref, kc_ref, vc_ref, kn_ref, vn_ref, fc_ref, fn_ref, _, o_ref, m_ref, l_ref, acc_ref,
                  *, ds, n_heads):
    jb = pl.program_id(1)
    rows = n_heads * ds
    q = q_ref[0]
    fnew = fn_ref[0][:, :ds]
    fref = _expand_rows(fn_ref[0][:, 0:1], ds)

    @pl.when(jb == 0)
    def _():
        _dec_init(m_ref, l_ref, acc_ref)
        r = lax.broadcasted_iota(jnp.int32, (rows, 1), 0)
        kidx = lax.broadcasted_iota(jnp.int32, (1, ds), 1)
        s = _dot_nt(q, kn_ref[0]) + LOG2E * (fref - _expand_rows(fnew, ds))
        s = jnp.where(kidx <= (r % ds), s, NEG)
        _dec_online(s, vn_ref[0].astype(BF16), m_ref, l_ref, acc_ref)

    s = _dot_nt(q, kc_ref[0, 0].astype(BF16)) + LOG2E * (fref - _expand_rows(fc_ref[0], ds))
    _dec_online(s, vc_ref[0, 0].astype(BF16), m_ref, l_ref, acc_ref)

    @pl.when(jb == pl.num_programs(1) - 1)
    def _():
        o = acc_ref[...] / l_ref[...]
        o_ref[0] = _diag_blocks(o, n_heads, ds, DH_C).astype(o_ref.dtype)


def _mla_dec_body(qn_ref, qp_ref, cc_ref, pc_ref, cn_ref, pn_ref, wk_ref, wv_ref, ones_ref, _,
                  o_ref, m_ref, l_ref, acc_ref, *, ds, n_heads):
    jb = pl.program_id(1)
    qn = qn_ref[0]
    qp = qp_ref[0]
    ones_h = ones_ref[...]

    def key_block(ckv, kpe):
        c16 = ckv.astype(BF16)
        kn = _dot(c16, wk_ref[...])
        v = _dot(c16, wv_ref[...])
        n = kpe.shape[0]
        ss = _dot_nt(ones_h, (kn * kn).astype(BF16)) + _dot_nt(jnp.ones((n_heads, ROPE_D), BF16),
                                                               (kpe * kpe).astype(BF16))
        rinv = lax.rsqrt(ss * (1.0 / (NOPE_D + ROPE_D)) + EPS)
        s = _dot_nt(qn, kn.astype(BF16)) + _dot_nt(qp, kpe.astype(BF16))
        s = s * _expand_rows(rinv, ds)
        _dec_online(s, v.astype(BF16), m_ref, l_ref, acc_ref)

    @pl.when(jb == 0)
    def _():
        _dec_init(m_ref, l_ref, acc_ref)
        key_block(cn_ref[0], pn_ref[0])

    key_block(cc_ref[0, 0], pc_ref[0, 0])

    @pl.when(jb == pl.num_programs(1) - 1)
    def _():
        o = acc_ref[...] / l_ref[...]
        o_ref[0] = _diag_blocks(o, n_heads, ds, V_D).astype(o_ref.dtype)


def _per_seq(a):
    return (a, pl.BlockSpec((1,) + a.shape[1:], lambda b, j: (b, 0, 0)))


def _dec_const(a):
    return (a, _const_spec(a.shape))


def _dec_call(body, ins, prev, row0, ds, rows, acc_w, nb, n_kb, smem=None):
    in_specs = [spec for _, spec in ins] + [pl.BlockSpec(memory_space=pl.ANY)]
    args = [a for a, _ in ins] + [prev[None]]
    if smem is not None:
        in_specs = [pl.BlockSpec(memory_space=pltpu.SMEM)] + in_specs
        args = [smem] + args
    blk0 = row0 // ds
    return pl.pallas_call(
        body,
        grid=(nb, n_kb),
        in_specs=in_specs,
        out_specs=pl.BlockSpec((1, ds, prev.shape[1]), lambda b, j: (0, blk0 + b, 0)),
        out_shape=jax.ShapeDtypeStruct((1,) + prev.shape, prev.dtype),
        input_output_aliases={len(args) - 1: 0},
        scratch_shapes=[pltpu.VMEM((rows, 1), F32), pltpu.VMEM((rows, 1), F32), pltpu.VMEM((rows, acc_w), F32)],
        compiler_params=pltpu.CompilerParams(
            dimension_semantics=("parallel", "arbitrary"), vmem_limit_bytes=VMEM_LIMIT),
    )(*args)[0]


def kernel(x_prompt, x_sample, state_s5_re, state_s5_im, cache_diff_k, cache_diff_v, cache_fox_k, cache_fox_v, cache_fox_logf, cache_mla_ckv, cache_mla_kpe, meta_tokens, ffn_norm, ffn_w_in, ffn_w_out, mix_norm, ab_w_in, ab_w_out, s5_a_re, s5_a_im, s5_log_step, s5_b_re, s5_b_im, s5_c_re, s5_c_im, s5_d, s5_glu_w, s5_glu_b, diff_q_norm, diff_k_norm, diff_lam, diff_sub_norm, cd_w_in, cd_w_out, fox_q_norm, fox_k_norm, fox_f_bias, mla_q_a_norm, mla_q_b, mla_kv_a_norm, mla_kv_b, mla_q_norm, mla_k_norm):
    bsz, seq, dm = x_prompt.shape
    nb, ds, _ = x_sample.shape
    n_meta = meta_tokens.shape[0]
    past = cache_diff_k.shape[2]
    front = ROW_ALIGN - n_meta
    lp = front + n_meta + seq
    ltot = n_meta + seq
    assert n_meta + front == ROW_ALIGN and lp % ATT_BLOCK == 0 and front % CHUNK == CHUNK - n_meta
    assert ds == S5_STEP and past % CHUNK == 0 and ds <= CHUNK
    kb = min(DEC_KB, past)
    assert past % kb == 0
    n_kb = past // kb
    assert ffn_norm.shape[0] == 2 and ab_w_in.shape[0] == 1 and cd_w_in.shape[0] == 1

    h_b = cache_diff_k.shape[3]
    h_c = cache_fox_k.shape[3]
    h_d = mla_q_b.shape[2] // (NOPE_D + ROPE_D)
    s5w = s5_glu_w.shape[1]
    n_grp = s5w // S5_GROUP
    qkw = h_b * 2 * DH_B
    fox_w = h_c * DH_C
    q_lora = mla_q_a_norm.shape[1]
    kv_lora = mla_kv_a_norm.shape[1]
    d_qk = NOPE_D + ROPE_D

    n_p = bsz * lp
    head_rows_x = jnp.concatenate([jnp.zeros((front, dm), F32), meta_tokens.astype(F32)], axis=0)
    pieces = []
    for b in range(bsz):
        pieces += [head_rows_x, x_prompt[b]]
    x = jnp.concatenate(pieces + [x_sample.reshape(nb * ds, dm)], axis=0)
    rows = x.shape[0]
    tm = _row_tile(rows)

    ones64 = _block_diag_ones(64)
    ones128 = _block_diag_ones(128)

    x = _ffn(x, ffn_norm[0, 0], ffn_w_in[0, 0], ffn_w_out[0, 0], tm)

    gq = (jnp.tile(diff_q_norm[0], 2 * h_b) * (DH_B ** -0.5 * LOG2E)).reshape(1, qkw)
    gk = jnp.tile(diff_k_norm[0], 2 * h_b).reshape(1, qkw)
    s5_cols = S5_STEP * S5_GROUP
    q16, k16, u_t, vt_diff, k32h, v32h = _row_call(
        functools.partial(_ab_in_body, widths=(s5w, qkw)),
        [x], [mix_norm[0].reshape(1, dm), ab_w_in[0].astype(BF16), gq, gk, ones64],
        [qkw, qkw], [BF16, BF16], tm,
        group_outs=[jax.ShapeDtypeStruct((n_grp, rows // S5_STEP, s5_cols), BF16)],
        scratch=[pltpu.VMEM((s5w // 128, tm, 128), F32)],
        col_outs=[(h_b * DIFF_VT_ROWS, BF16)], head_outs=[(h_b, F32), (h_b, F32)])
    k32 = k32h.reshape(rows, h_b, 2 * DH_B)
    v32 = v32h.reshape(rows, h_b, 2 * DH_B)

    mats = _s5_matrices(s5_a_re[0], s5_a_im[0], s5_log_step[0], s5_b_re[0], s5_b_im[0],
                        s5_c_re[0], s5_c_im[0], s5_d[0])
    n_ch = lp // S5_STEP
    y_t, st_p = _s5_scan(u_t, jnp.zeros((n_grp, bsz, 4 * P_A), F32), mats, n_ch, bsz, 0)
    h_re = jnp.transpose(state_s5_re[0].astype(F32), (1, 0, 2))
    h_im = jnp.transpose(state_s5_im[0].astype(F32), (1, 0, 2))
    y_t, st_s = _s5_scan(u_t, jnp.concatenate([h_re, h_im, h_im, h_re], axis=-1), mats, 1, nb,
                         n_p // S5_STEP, y_prev=y_t)

    lv = diff_lam[0].astype(F32)
    lam_init = 0.8 - 0.6 * math.exp(-0.3 * 0)
    lam = jnp.exp(jnp.sum(lv[0] * lv[1])) - jnp.exp(jnp.sum(lv[2] * lv[3])) + lam_init
    slopes = jnp.exp2(-8.0 * jnp.arange(1, h_b + 1, dtype=F32) / h_b)
    par = jnp.concatenate([lam[None], slopes]).astype(F32)
    subg = (jnp.tile(diff_sub_norm[0], h_b) * (1.0 - lam_init)).reshape(1, qkw)
    kpad = jnp.arange(lp) < front
    kb_diff = jnp.where(kpad[None, :], NEG, LOG2E * slopes[:, None] * jnp.arange(lp, dtype=F32)[None, :])
    kb_diff = jnp.broadcast_to(kb_diff[None, :, None, :], (1, h_b, 2, lp)).reshape(
        1, 2 * h_b // ATT_STREAMS, ATT_STREAMS, lp)
    o_all = _prompt_attn("diff", q16, k16, vt_diff, kb_diff, bsz, lp, par=par, gain=subg)
    qs = q16[n_p:].reshape(nb, ds, h_b, 2, DH_B)
    eye_2 = jnp.eye(2, dtype=BF16)
    qbd = jnp.einsum('bqhmd,mM->bhmqMd', qs, eye_2).reshape(nb, h_b * 2 * ds, 2 * DH_B)
    cache_spec = lambda w: pl.BlockSpec((1, 1, kb, w), lambda b, j: (0, b, j, 0))
    head_rows = lambda a: a.reshape(a.shape[0], nb, past * h_b, 2 * DH_B)
    head_cache_spec = pl.BlockSpec((1, 1, kb * h_b, 2 * DH_B), lambda b, j: (0, b, j, 0))
    o_all = _dec_call(
        functools.partial(_diff_dec_body, past=past, kb=kb, ds=ds, n_heads=h_b),
        [_per_seq(qbd), (head_rows(cache_diff_k), head_cache_spec), (head_rows(cache_diff_v), head_cache_spec),
         _per_seq(k16[n_p:].reshape(nb, ds * h_b, 2 * DH_B)), _per_seq(v32h[n_p * h_b:].astype(BF16).reshape(nb, ds * h_b, 2 * DH_B)),
         _dec_const(subg)],
        o_all, n_p, ds, 2 * h_b * ds, 2 * DH_B, nb, n_kb, smem=par)

    x = _row_call(
        functools.partial(_ab_out_body, s5w=s5w),
        [x, o_all],
        [s5_glu_w[0].astype(BF16), s5_glu_b[0].reshape(1, s5w), ab_w_out[0].astype(BF16)],
        [dm], [F32], tm, group_ins=[y_t], scratch=[pltpu.VMEM((s5w // 128, tm, 128), F32)])[0]

    x = _ffn(x, ffn_norm[0, 1], ffn_w_in[0, 1], ffn_w_out[0, 1], tm)

    x = _ffn(x, ffn_norm[1, 0], ffn_w_in[1, 0], ffn_w_out[1, 0], tm)

    half = ROPE_D // 2
    inv = ROPE_THETA ** (-jnp.arange(half, dtype=F32) / half)
    pos = jnp.concatenate([jnp.tile(jnp.arange(lp, dtype=jnp.int32) - front, bsz),
                           jnp.tile(past + jnp.arange(ds, dtype=jnp.int32), nb)]).astype(F32)
    ang = pos[:, None] * inv[None, :]
    pad_r = HEAD_PAD - NOPE_D - ROPE_D
    cos_t = jnp.concatenate([jnp.ones((rows, NOPE_D), F32), jnp.cos(ang), jnp.cos(ang),
                             jnp.zeros((rows, pad_r), F32)], axis=1)
    sin_t = jnp.concatenate([jnp.zeros((rows, NOPE_D), F32), jnp.sin(ang), jnp.sin(ang),
                             jnp.zeros((rows, pad_r), F32)], axis=1)

    wcd = cd_w_in[0]
    c_fg = 3 * fox_w
    c_qa = c_fg + h_c
    c_kva = c_qa + q_lora
    c_pe = c_kva + kv_lora
    w_pe = wcd[:, c_pe:c_pe + ROPE_D]
    zc = lambda n: jnp.zeros((dm, n), F32)
    w_cd = jnp.concatenate([
        wcd[:, :3 * fox_w], wcd[:, c_qa:c_qa + q_lora], wcd[:, c_kva:c_kva + kv_lora],
        zc(NOPE_D), w_pe, zc(pad_r),
        zc(NOPE_D), -w_pe[:, half:], w_pe[:, :half], zc(pad_r),
        wcd[:, c_fg:c_fg + h_c], zc(HEAD_PAD - h_c)], axis=1).astype(BF16)
    qb = mla_q_b[0].reshape(q_lora, h_d, d_qk)
    zq = lambda n: jnp.zeros((q_lora, h_d, n), F32)
    qb_pad = jnp.concatenate([qb, zq(pad_r)], axis=-1).reshape(q_lora, h_d * HEAD_PAD)
    qb_rot = jnp.concatenate([zq(NOPE_D), -qb[..., NOPE_D + half:], qb[..., NOPE_D:NOPE_D + half], zq(pad_r)],
                             axis=-1).reshape(q_lora, h_d * HEAD_PAD)
    wq2 = jnp.concatenate([qb_pad, qb_rot], axis=1).astype(BF16)
    kvb = mla_kv_b[0].reshape(kv_lora, h_d, NOPE_D + V_D)
    wk_pad = jnp.concatenate([kvb[..., :NOPE_D], jnp.zeros((kv_lora, h_d, HEAD_PAD - NOPE_D), F32)],
                             axis=-1).reshape(kv_lora, h_d * HEAD_PAD).astype(BF16)
    wk_cmp = kvb[..., :NOPE_D].reshape(kv_lora, h_d * NOPE_D).astype(BF16)
    wv_cmp = kvb[..., NOPE_D:].reshape(kv_lora, h_d * V_D).astype(BF16)
    gfq = (jnp.tile(fox_q_norm[0], h_c) * (DH_C ** -0.5 * LOG2E)).reshape(1, fox_w)
    gfk = jnp.tile(fox_k_norm[0], h_c).reshape(1, fox_w)
    fbias = jnp.concatenate([fox_f_bias[0], jnp.zeros((HEAD_PAD - h_c,), F32)]).reshape(1, HEAD_PAD)
    gmq = jnp.tile(jnp.concatenate([mla_q_norm[0] * mla_k_norm[0] * (d_qk ** -0.5 * LOG2E), jnp.zeros((pad_r,), F32)]),
                   h_d).reshape(1, h_d * HEAD_PAD)

    (fq16, fk32, fk16, fv32, qm16, ckv32, km16, fvt, vmt, logf_t, kpe_t) = _row_call(
        functools.partial(_cd_in_body, parts=IN_PARTS, fox_w=fox_w, q_lora=q_lora, kv_lora=kv_lora, n_heads=h_d),
        [x, cos_t, sin_t],
        [mix_norm[1].reshape(1, dm), w_cd, gfq, gfk, fbias, mla_q_a_norm[0].reshape(1, q_lora), wq2,
         mla_kv_a_norm[0].reshape(1, kv_lora), wk_pad, wv_cmp.T, gmq, ones64, ones128],
        [fox_w, fox_w, fox_w, fox_w, h_d * HEAD_PAD, kv_lora, h_d * HEAD_PAD],
        [BF16, F32, BF16, F32, BF16, F32, BF16], tm,
        col_outs=[(2 * fox_w, BF16), (2 * h_d * V_D, BF16), (h_c, F32), (ROPE_D, F32)])
    logf_ps = jnp.transpose(logf_t[:, :n_p].reshape(h_c, bsz, lp), (1, 0, 2))
    logf_ss = jnp.transpose(logf_t[:, n_p:].reshape(h_c, nb, ds), (1, 0, 2))
    kpe_s = jnp.transpose(kpe_t[:, n_p:]).reshape(nb, ds, ROPE_D)

    f_p = _cumsum_lanes(logf_ps.reshape(bsz * h_c, lp)).reshape(bsz, h_c, lp)
    logf_s = jnp.concatenate([
        jnp.transpose(cache_fox_logf[0].astype(F32), (0, 2, 1)),
        logf_ss,
        jnp.zeros((nb, h_c, 256 - ds), F32)], axis=2).reshape(nb * h_c, past + 256)
    f_s = _cumsum_lanes(logf_s).reshape(nb, h_c, past + 256)

    kb_fox = jnp.where(kpad[None, None, :], NEG, -LOG2E * f_p).reshape(bsz, h_c // ATT_STREAMS, ATT_STREAMS, lp)
    oc_all = _prompt_attn("fox", fq16, fk16, fvt, kb_fox, bsz, lp)
    kb_mla = jnp.broadcast_to(jnp.where(kpad, NEG, 0.0).astype(F32)[None, None, None, :], (1, 1, ATT_STREAMS, lp))
    od_all = _prompt_attn("mla", qm16, km16, vmt, kb_mla, bsz, lp)

    eye_c = jnp.eye(h_c, dtype=BF16)
    fqs = fq16[n_p:].reshape(nb, ds, h_c, DH_C)
    fq_bd = jnp.einsum('bqhd,hH->bhqHd', fqs, eye_c).reshape(nb, h_c * ds, fox_w)
    fkc = cache_fox_k.reshape(cache_fox_k.shape[0], nb, past, fox_w)
    fvc = cache_fox_v.reshape(cache_fox_v.shape[0], nb, past, fox_w)
    oc_all = _dec_call(
        functools.partial(_fox_dec_body, ds=ds, n_heads=h_c),
        [_per_seq(fq_bd), (fkc, cache_spec(fox_w)), (fvc, cache_spec(fox_w)),
         _per_seq(fk16[n_p:].reshape(nb, ds, fox_w)), _per_seq(fv32[n_p:].reshape(nb, ds, fox_w)),
         (f_s, pl.BlockSpec((1, h_c, kb), lambda b, j: (b, 0, j))),
         (f_s, pl.BlockSpec((1, h_c, 128), lambda b, j: (b, 0, past // 128)))],
        oc_all, n_p, ds, h_c * ds, fox_w, nb, n_kb)

    eye_d = jnp.eye(h_d, dtype=BF16)
    qms = qm16[n_p:].reshape(nb, ds, h_d, HEAD_PAD)
    qn_bd = jnp.einsum('bqhd,hH->bhqHd', qms[..., :NOPE_D], eye_d).reshape(nb, h_d * ds, h_d * NOPE_D)
    qp_s = jnp.transpose(qms[..., NOPE_D:NOPE_D + ROPE_D], (0, 2, 1, 3)).reshape(nb, h_d * ds, ROPE_D)
    ones_h = jnp.repeat(jnp.eye(h_d, dtype=BF16), NOPE_D, axis=1)
    od_all = _dec_call(
        functools.partial(_mla_dec_body, ds=ds, n_heads=h_d),
        [_per_seq(qn_bd), _per_seq(qp_s),
         (cache_mla_ckv, pl.BlockSpec((1, 1, kb, kv_lora), lambda b, j: (0, b, j, 0))),
         (cache_mla_kpe, pl.BlockSpec((1, 1, kb, ROPE_D), lambda b, j: (0, b, j, 0))),
         _per_seq(ckv32[n_p:].reshape(nb, ds, kv_lora)), _per_seq(kpe_s),
         _dec_const(wk_cmp), _dec_const(wv_cmp), _dec_const(ones_h)],
        od_all, n_p, ds, h_d * ds, h_d * V_D, nb, n_kb)

    x = _row_call(functools.partial(_cd_out_body, fox_w=fox_w), [x, oc_all, od_all],
                  [cd_w_out[0].astype(BF16)], [dm], [F32], tm)[0]

    x = _ffn(x, ffn_norm[1, 1], ffn_w_in[1, 1], ffn_w_out[1, 1], tm)

    def p_rows(a, shape):
        return a[:n_p].reshape((bsz, lp) + a.shape[1:])[:, front:front + ltot].reshape((1, bsz, ltot) + shape)

    def s_rows(a, shape):
        return a[n_p:].reshape((1, nb, ds) + shape)

    def p_cols(at):
        w = at.shape[0]
        return jnp.transpose(at[:, :n_p].reshape(w, bsz, lp)[:, :, front:front + ltot], (1, 2, 0))[None]

    def s_cols(at):
        return jnp.transpose(at[:, n_p:]).reshape(1, nb, ds, at.shape[0])

    def s5_state(st):
        st = jnp.transpose(st, (1, 0, 2))
        return st[None, :, :, :P_A], st[None, :, :, P_A:]

    y_prompt = jnp.stack([x[b * lp + front + n_meta:(b + 1) * lp] for b in range(bsz)])
    y_sample = x[n_p:].reshape(nb, ds, dm)
    s5_re_p, s5_im_p = s5_state(st_p)
    s5_re_s, s5_im_s = s5_state(st_s)
    return (y_prompt, y_sample,
            s5_re_p, s5_im_p, p_rows(k32, (h_b, 2 * DH_B)), p_rows(v32, (h_b, 2 * DH_B)),
            p_rows(fk32, (h_c, DH_C)), p_rows(fv32, (h_c, DH_C)), p_cols(logf_t),
            p_rows(ckv32, (kv_lora,)), p_cols(kpe_t),
            s5_re_s, s5_im_s, s_rows(k32, (h_b, 2 * DH_B)), s_rows(v32, (h_b, 2 * DH_B)),
            s_rows(fk32, (h_c, DH_C)), s_rows(fv32, (h_c, DH_C)), s_cols(logf_t),
            s_rows(ckv32, (kv_lora,)), s_cols(kpe_t))
```

```python
import functools
import math

import jax
import jax.numpy as jnp
from jax import lax
from jax.experimental import pallas as pl
from jax.experimental.pallas import tpu as pltpu

F32 = jnp.float32
BF16 = jnp.bfloat16

EPS = 1e-6
CHUNK = 64
ROW_ALIGN = 256
S5_GROUP = 16
S5_STEP = 16
S5_GROUPS_PER_STEP = 4
P_A = 64
DH_B = 64
DH_C = 64
NOPE_D = 64
ROPE_D = 32
V_D = 64
HEAD_PAD = 128
ROPE_THETA = 10000.0
NEG = -1e30
LOG2E = math.log2(math.e)
VMEM_LIMIT = 56 * 1024 * 1024
IN_PARTS = 2
OUT_PARTS = 4
ATT_BLOCK = 256
ATT_STREAMS = 4
ATT_ROWS = 64
DIFF_VT_ROWS = 2 * DH_B + 16
DEC_KB = 4096


def _dot(a, b):
    return jnp.dot(a, b, preferred_element_type=F32)


def _dot_nt(a, b):
    return lax.dot_general(a, b, (((1,), (1,)), ((), ())), preferred_element_type=F32)


def _rms_rows(x, g):
    ms = jnp.mean(x * x, axis=-1, keepdims=True)
    return x * lax.rsqrt(ms + EPS) * g


def _group_sumsq(x, ones_bd):
    w = x.shape[-1]
    parts = [_dot((x[:, c:c + 256] * x[:, c:c + 256]).astype(BF16), ones_bd) for c in range(0, w, 256)]
    return parts[0] if len(parts) == 1 else jnp.concatenate(parts, axis=1)


def _block_diag_ones(group, n=256):
    r = jnp.arange(n) // group
    return (r[:, None] == r[None, :]).astype(BF16)


def _const_spec(shape):
    nd = len(shape)
    return pl.BlockSpec(shape, lambda *_: (0,) * nd, pipeline_mode=pl.Buffered(1))


def _row_tile(rows, cap=512):
    t = cap
    while rows % t:
        t //= 2
    return t


def _group_spec(a, tm):
    return pl.BlockSpec((a.shape[0], tm // S5_STEP, a.shape[2]), lambda i: (0, i, 0))


def _row_call(body, row_ins, consts, out_widths, out_dtypes, tm, group_ins=(), group_outs=(), scratch=(),
              col_outs=(), head_outs=()):
    rows = row_ins[0].shape[0]
    in_specs = [pl.BlockSpec((tm, a.shape[1]), lambda i: (i, 0)) for a in row_ins]
    in_specs += [_group_spec(a, tm) for a in group_ins]
    in_specs += [_const_spec(c.shape) for c in consts]
    out_specs = [pl.BlockSpec((tm, w), lambda i: (i, 0)) for w in out_widths]
    out_specs += [_group_spec(a, tm) for a in group_outs]
    out_specs += [pl.BlockSpec((w, tm), lambda i: (0, i)) for w, _ in col_outs]
    out_shape = [jax.ShapeDtypeStruct((rows, w), d) for w, d in zip(out_widths, out_dtypes)]
    out_shape += list(group_outs)
    out_shape += [jax.ShapeDtypeStruct((w, rows), d) for w, d in col_outs]
    out_specs += [pl.BlockSpec((tm * h, 128), lambda i: (i, 0)) for h, _ in head_outs]
    out_shape += [jax.ShapeDtypeStruct((rows * h, 128), d) for h, d in head_outs]
    return pl.pallas_call(
        body,
        grid=(rows // tm,),
        in_specs=in_specs,
        out_specs=out_specs,
        out_shape=out_shape,
        scratch_shapes=list(scratch),
        compiler_params=pltpu.CompilerParams(
            dimension_semantics=("parallel",), vmem_limit_bytes=VMEM_LIMIT),
    )(*row_ins, *group_ins, *consts)


def _ffn_body(*refs, d_ff, tf, mix_split):
    if mix_split:
        x_ref, a_ref, b_ref, g_ref, win_ref, wout_ref, wmix_ref, o_ref = refs
        x = (x_ref[...] + _dot(a_ref[...], wmix_ref[:mix_split, :]) + _dot(b_ref[...], wmix_ref[mix_split:, :]))
    else:
        x_ref, g_ref, win_ref, wout_ref, o_ref = refs
        x = x_ref[...]
    xn = _rms_rows(x, g_ref[...]).astype(BF16)
    acc = jnp.zeros(x.shape, F32)
    for c in range(0, d_ff, tf):
        gate = _dot(xn, win_ref[:, c:c + tf])
        up = _dot(xn, win_ref[:, d_ff + c:d_ff + c + tf])
        a = (gate * jax.nn.sigmoid(gate) * up).astype(BF16)
        acc = acc + _dot(a, wout_ref[c:c + tf, :])
    o_ref[...] = x + 0.5 * acc


def _ffn(x, g, w_in, w_out, tm, mix=None):
    d_ff = w_out.shape[0]
    consts = [g.reshape(1, -1), w_in.astype(BF16), w_out.astype(BF16)]
    rows_in = [x]
    mix_split = 0
    if mix is not None:
        a, b, w_mix = mix
        rows_in += [a, b]
        consts.append(w_mix.astype(BF16))
        mix_split = a.shape[1]
    body = functools.partial(_ffn_body, d_ff=d_ff, tf=256, mix_split=mix_split)
    return _row_call(body, rows_in, consts, [x.shape[1]], [F32], tm)[0]


def _ab_in_body(x_ref, g_ref, w_ref, gq_ref, gk_ref, ones_ref,
                q_ref, k16_ref, ut_ref, vt_ref, k32_ref, v32_ref, us_ref, *, widths):
    s5w, qkw = widths
    xn = _rms_rows(x_ref[...], g_ref[...]).astype(BF16)
    h = _dot(xn, w_ref[...])
    n_chunk = us_ref.shape[1] // S5_STEP
    per_col = 128 // S5_GROUP
    for v in range(s5w // 128):
        us_ref[v] = h[:, v * 128:(v + 1) * 128]
        steps = [us_ref[v, pl.ds(t, n_chunk, stride=S5_STEP), :] for t in range(S5_STEP)]
        for gl in range(per_col):
            ut_ref[v * per_col + gl] = jnp.concatenate(
                [x[:, gl * S5_GROUP:(gl + 1) * S5_GROUP] for x in steps], axis=1).astype(BF16)
    q = h[:, s5w:s5w + qkw]
    k = h[:, s5w + qkw:s5w + 2 * qkw]
    v = h[:, s5w + 2 * qkw:]
    ones_bd = ones_ref[...]
    qn = q * lax.rsqrt(_group_sumsq(q, ones_bd) * (1.0 / DH_B) + EPS) * gq_ref[...]
    kn = k * lax.rsqrt(_group_sumsq(k, ones_bd) * (1.0 / DH_B) + EPS) * gk_ref[...]
    q_ref[...] = qn.astype(BF16)
    k16_ref[...] = kn.astype(BF16)
    dv = 2 * DH_B
    n_head = qkw // dv
    tm = kn.shape[0]
    for hd in range(n_head):
        k32_ref[pl.ds(hd, tm, stride=n_head), :] = kn[:, hd * dv:(hd + 1) * dv]
        v32_ref[pl.ds(hd, tm, stride=n_head), :] = v[:, hd * dv:(hd + 1) * dv]
    vt = v.T
    ones = jnp.ones((DIFF_VT_ROWS - dv, vt.shape[1]), F32)
    vt_ref[...] = jnp.concatenate(
        [a for h in range(vt.shape[0] // dv) for a in (vt[h * dv:(h + 1) * dv], ones)], axis=0).astype(BF16)


def _ab_out_body(x_ref, o_ref, yt_ref, gluw_ref, glub_ref, wout_ref, out_ref, ys_ref, *, s5w):
    tm = ys_ref.shape[1]
    part = tm // OUT_PARTS
    n_chunk = part // S5_STEP
    per_col = 128 // S5_GROUP
    for h in range(OUT_PARTS):
        c0 = h * n_chunk
        for v in range(s5w // 128):
            for t in range(S5_STEP):
                ys_ref[v, pl.ds(h * part + t, n_chunk, stride=S5_STEP), :] = jnp.concatenate(
                    [yt_ref[v * per_col + gl, c0:c0 + n_chunk, t * S5_GROUP:(t + 1) * S5_GROUP]
                     for gl in range(per_col)], axis=1)
        rows = slice(h * part, (h + 1) * part)
        y = jnp.concatenate([ys_ref[v, rows] for v in range(s5w // 128)], axis=1)
        g = 0.5 * y * (1.0 + jnp.tanh(math.sqrt(2.0 / math.pi) * (y + 0.044715 * (y * y * y))))
        z = _dot(g.astype(BF16), gluw_ref[...]) + glub_ref[...]
        s5o = g * jax.nn.sigmoid(z)
        m = _dot(s5o.astype(BF16), wout_ref[:s5w, :]) + _dot(o_ref[rows], wout_ref[s5w:, :])
        out_ref[rows] = x_ref[rows] + m


def _heads_with_ones_t(vt):
    ones = jnp.ones((64, vt.shape[1]), vt.dtype)
    outs = []
    for h in range(vt.shape[0] // 64):
        outs += [vt[h * 64:(h + 1) * 64], ones]
    return jnp.concatenate(outs, axis=0)


def _cd_in_body(*refs, parts, **dims):
    n = refs[0].shape[0] // parts
    for h in range(parts):
        rows = pl.ds(h * n, n)
        _cd_in_part(*[r.at[rows] for r in refs[:3]], *refs[3:16],
                    *[r.at[rows] for r in refs[16:23]], *[r.at[:, rows] for r in refs[23:27]], **dims)


def _cd_in_part(x_ref, cos_ref, sin_ref, g_ref, w_ref, gfq_ref, gfk_ref, fb_ref, gqa_ref, wq2_ref,
                gkva_ref, wk_ref, wv_ref, gmq_ref, ones64_ref, ones128_ref,
                fq_ref, fk32_ref, fk16_ref, fv32_ref, qm_ref, ckv_ref,
                km_ref, fvt_ref, vmt_ref, logft_ref, kpet_ref, *, fox_w, q_lora, kv_lora, n_heads):
    xn = _rms_rows(x_ref[...], g_ref[...]).astype(BF16)
    h = _dot(xn, w_ref[...])
    ones64 = ones64_ref[...]
    ones128 = ones128_ref[...]
    fq = h[:, :fox_w]
    fk = h[:, fox_w:2 * fox_w]
    fv = h[:, 2 * fox_w:3 * fox_w]
    c0 = 3 * fox_w
    qa = h[:, c0:c0 + q_lora]
    kva = h[:, c0 + q_lora:c0 + q_lora + kv_lora]
    c1 = c0 + q_lora + kv_lora
    pe_a = h[:, c1:c1 + HEAD_PAD]
    pe_b = h[:, c1 + HEAD_PAD:c1 + 2 * HEAD_PAD]
    fg = h[:, c1 + 2 * HEAD_PAD:c1 + 3 * HEAD_PAD]

    fqn = fq * lax.rsqrt(_group_sumsq(fq, ones64) * (1.0 / DH_C) + EPS) * gfq_ref[...]
    fkn = fk * lax.rsqrt(_group_sumsq(fk, ones64) * (1.0 / DH_C) + EPS) * gfk_ref[...]
    fq_ref[...] = fqn.astype(BF16)
    fk32_ref[...] = fkn
    fk16_ref[...] = fkn.astype(BF16)
    fv32_ref[...] = fv
    fvt_ref[...] = _heads_with_ones_t(fv.T).astype(BF16)

    z = fg + fb_ref[...]
    logf = jnp.minimum(z, 0.0) - jnp.log1p(jnp.exp(-jnp.abs(z)))
    logft_ref[...] = logf.T[:logft_ref.shape[0]]

    cos = cos_ref[...]
    sin = sin_ref[...]
    qan = _rms_rows(qa, gqa_ref[...]).astype(BF16)
    q2 = _dot(qan, wq2_ref[...])
    hw = n_heads * HEAD_PAD
    cos_t = jnp.concatenate([cos] * n_heads, axis=1)
    sin_t = jnp.concatenate([sin] * n_heads, axis=1)
    qr = q2[:, :hw] * cos_t + q2[:, hw:] * sin_t
    d_qk = NOPE_D + ROPE_D
    qm = qr * lax.rsqrt(_group_sumsq(qr, ones128) * (1.0 / d_qk) + EPS) * gmq_ref[...]
    qm_ref[...] = qm.astype(BF16)

    ckv = _rms_rows(kva, gkva_ref[...])
    ckv_ref[...] = ckv
    pe = pe_a * cos + pe_b * sin
    kpet_ref[...] = pe.T[NOPE_D:NOPE_D + ROPE_D]
    ckv16 = ckv.astype(BF16)
    kraw = _dot(ckv16, wk_ref[...]) + jnp.concatenate([pe] * n_heads, axis=1)
    km = kraw * lax.rsqrt(_group_sumsq(kraw, ones128) * (1.0 / d_qk) + EPS)
    km_ref[...] = km.astype(BF16)
    vmt_ref[...] = _heads_with_ones_t(_dot_nt(wv_ref[...], ckv16)).astype(BF16)


def _s5_body(*refs, n_chunks, bsz, aliased):
    if aliased:
        u_ref, h0_ref, m_ref, bm_ref, cm_ref, coef_ref, _, y_ref, st_ref, s2_ref, hp_ref = refs
    else:
        u_ref, h0_ref, m_ref, bm_ref, cm_ref, coef_ref, y_ref, st_ref, s2_ref, hp_ref = refs
    ng = u_ref.shape[0]
    half = 2 * P_A
    coefs = []
    for g in range(ng):
        s2 = _dot(u_ref[g], bm_ref[g])
        s2_ref[g, 0] = s2[:, :half]
        s2_ref[g, 1] = s2[:, half:]
        coefs.append((coef_ref[g, 0:1, :], coef_ref[g, 1:2, :], coef_ref[g, 2:3, :]))

    def step(j, carry):
        new = []
        for g in range(ng):
            ha, hb = carry[2 * g], carry[2 * g + 1]
            c1, c2, c3 = coefs[g]
            hp_ref[g, pl.ds(j, bsz, stride=n_chunks), :] = ha
            sa = s2_ref[g, 0, pl.ds(j, bsz, stride=n_chunks), :]
            sb = s2_ref[g, 1, pl.ds(j, bsz, stride=n_chunks), :]
            new += [ha * c1 + hb * c2 + sa, hb * c1 + ha * c3 + sb]
        return tuple(new)

    init = []
    for g in range(ng):
        init += [h0_ref[g][:, :half], h0_ref[g][:, half:]]
    final = lax.fori_loop(0, n_chunks, step, tuple(init))
    for g in range(ng):
        st_ref[g] = final[2 * g]
        y_ref[g] = _dot(u_ref[g], m_ref[g]) + _dot(hp_ref[g].astype(BF16), cm_ref[g])


def _s5_scan(u_t, h0, mats, n_chunks, bsz, row0, y_prev=None):
    m_mat, bm, cm, coef = mats
    g, rows_all, w = u_t.shape
    rows = n_chunks * bsz
    assert row0 % rows == 0
    blk = row0 // rows
    aliased = y_prev is not None
    body = functools.partial(_s5_body, n_chunks=n_chunks, bsz=bsz, aliased=aliased)
    ng = S5_GROUPS_PER_STEP
    per_g = lambda a: pl.BlockSpec((ng,) + a.shape[1:], lambda i: (i, 0, 0))
    in_specs = [pl.BlockSpec((ng, rows, w), lambda i: (i, blk, 0)),
                per_g(h0), per_g(m_mat), per_g(bm), per_g(cm), per_g(coef)]
    args = [u_t, h0, m_mat, bm, cm, coef]
    if aliased:
        in_specs.append(pl.BlockSpec(memory_space=pl.ANY))
        args.append(y_prev)
    return pl.pallas_call(
        body,
        grid=(g // ng,),
        in_specs=in_specs,
        out_specs=[pl.BlockSpec((ng, rows, w), lambda i: (i, blk, 0)),
                   pl.BlockSpec((ng, bsz, 2 * P_A), lambda i: (i, 0, 0))],
        out_shape=[jax.ShapeDtypeStruct((g, rows_all, w), F32),
                   jax.ShapeDtypeStruct((g, bsz, 2 * P_A), F32)],
        input_output_aliases={6: 0} if aliased else {},
        scratch_shapes=[pltpu.VMEM((ng, 2, rows, 2 * P_A), F32), pltpu.VMEM((ng, rows, 2 * P_A), F32)],
        compiler_params=pltpu.CompilerParams(
            dimension_semantics=("parallel",), vmem_limit_bytes=VMEM_LIMIT),
    )(*args)


def _s5_matrices(a_re, a_im, log_step, b_re, b_im, c_re, c_im, d):
    g = a_re.shape[0]
    t = S5_STEP
    lam = lax.complex(a_re, a_im)
    dl = lam * jnp.exp(log_step)[:, None]
    lam_bar = jnp.exp(dl)
    b_bar = ((lam_bar - 1.0) / lam)[..., None] * lax.complex(b_re, b_im)
    c = lax.complex(c_re, c_im)
    pw = jnp.exp(dl[:, None, :] * jnp.arange(t + 1, dtype=F32)[None, :, None])
    bmc = pw[:, t - 1::-1][:, :, :, None] * b_bar[:, None]
    bmc = jnp.swapaxes(bmc, 2, 3).reshape(g, t * S5_GROUP, P_A)
    bm = jnp.concatenate([bmc.real, bmc.imag, bmc.imag, bmc.real], axis=-1)
    kk = jnp.einsum('gcp,gkp,gpd->gkcd', c, pw[:, :t], b_bar).real
    kk = kk.at[:, 0].add(d.reshape(g, S5_GROUP)[:, :, None] * jnp.eye(S5_GROUP, dtype=F32))
    lag = jnp.arange(t)[None, :] - jnp.arange(t)[:, None]
    toep = jnp.where((lag >= 0)[None, :, :, None, None], kk[:, jnp.clip(lag, 0, t - 1)], 0.0)
    m_mat = jnp.transpose(toep, (0, 1, 4, 2, 3)).reshape(g, t * S5_GROUP, t * S5_GROUP)
    cp = c[:, None] * pw[:, 1:, None, :]
    cpm = jnp.transpose(cp, (0, 3, 1, 2)).reshape(g, P_A, t * S5_GROUP)
    cm = jnp.concatenate([cpm.real, -cpm.imag], axis=1)
    a_t = pw[:, t]
    ar, ai = a_t.real, a_t.imag
    zeros = jnp.zeros_like(ar)
    coef = jnp.stack([jnp.concatenate([ar, ar], -1), jnp.concatenate([-ai, ai], -1),
                      jnp.concatenate([ai, -ai], -1), jnp.concatenate([zeros, zeros], -1)], axis=1)
    return m_mat.astype(BF16), bm.astype(BF16), cm.astype(BF16), coef.astype(F32)


def _online(logits, vt, e, m_ref, acc_ref, p_ref, block_max=None):
    tk = p_ref.shape[1]
    if block_max is None:
        part = logits(0, ATT_ROWS)
        for r0 in range(ATT_ROWS, tk, ATT_ROWS):
            part = jnp.maximum(part, logits(r0, ATT_ROWS))
    else:
        part = block_max
    m_prev = m_ref[e]
    m_new = jnp.maximum(m_prev, jnp.max(part, axis=0, keepdims=True))
    alpha = jnp.exp2(m_prev - m_new)
    m_ref[e] = m_new
    for r0 in range(0, tk, ATT_ROWS):
        p_ref[e, r0:r0 + ATT_ROWS] = jnp.exp2(logits(r0, ATT_ROWS) - m_new).astype(BF16)
    acc_ref[e] = alpha * acc_ref[e] + _dot(vt, p_ref[e])


def _attn_query_block(i, hg, refs, kind, tq, ns):
    if kind == "diff":
        (par_ref, q_ref, k_ref, vt_ref, kb_ref, g_ref, o_ref,
         m_ref, acc_ref, s_ref, p_ref, kbc_ref, mx_ref) = refs
    else:
        q_ref, k_ref, vt_ref, kb_ref, o_ref, m_ref, acc_ref, s_ref, p_ref, kbc_ref, mx_ref = refs
    vrows = acc_ref.shape[1]
    m_ref[...] = jnp.full(m_ref.shape, NEG, F32)
    acc_ref[...] = jnp.zeros(acc_ref.shape, F32)

    qstart = pl.multiple_of(i * tq, tq)
    q = q_ref[0, pl.ds(qstart, tq), :]
    lane = lax.broadcasted_iota(jnp.int32, (1, 128), 1)
    qs = []
    for e in range(ns):
        if kind == "mla":
            qs.append(q[:, e * HEAD_PAD:(e + 1) * HEAD_PAD])
        else:
            qp = q[:, (e // 2) * 128:(e // 2 + 1) * 128]
            qs.append(jnp.where((lane < 64) if e % 2 == 0 else (lane >= 64), qp, jnp.zeros_like(qp)))
    lane_q = lax.broadcasted_iota(jnp.int32, (1, tq), 1)
    ref = [-kb_ref[0, 0, e:e + 1, pl.ds(qstart + (tq - 128), 128)][:, 127:128] for e in range(ns)]
    if kind == "diff":
        slope = [LOG2E * par_ref[1 + hg * (ns // 2) + p] for p in range(ns // 2)]

    tk = tq

    def k_slot(k, e):
        if kind == "mla":
            return k[:, e * HEAD_PAD:(e + 1) * HEAD_PAD]
        return k[:, (e // 2) * 128:(e // 2 + 1) * 128]

    def vt_slot(vt, e):
        r0 = (e // 2 if kind == "diff" else e) * vrows
        return vt[r0:r0 + vrows]

    def scores(j, slot, first=False):
        k0 = pl.multiple_of(j * tk, tk)
        k = k_ref[0, pl.ds(k0, tk), :]
        for e in range(ns):
            s = _dot_nt(k_slot(k, e), qs[e])
            if first or kind != "mla":
                bias = kbc_ref[e, pl.ds(k0, tk), :] + ref[e]
                s = s + jnp.concatenate([bias] * (tq // 128), axis=1)
            s_ref[slot, e] = s
            part = s[0:8]
            for r0 in range(8, tk, 8):
                part = jnp.maximum(part, s[r0:r0 + 8])
            mx_ref[slot, e] = part

    def softmax_pv(j, slot, diag):
        k0 = pl.multiple_of(j * tk, tk)
        vt = vt_ref[:, pl.ds(k0, tk)]
        for e in range(ns):
            def logits(r0, n, e=e):
                s = s_ref[slot, e, r0:r0 + n, :]
                if diag:
                    row_i = r0 + lax.broadcasted_iota(jnp.int32, (n, 1), 0)
                    if kind == "diff":
                        s = s - (2.0 * slope[e // 2]) * jnp.maximum(row_i - lane_q, 0).astype(F32)
                    if kind == "fox":
                        s = jnp.where(row_i <= lane_q, s, NEG)
                    elif r0 > 0:
                        s = jnp.where(lane_q >= r0, s, NEG)
                return s

            _online(logits, vt_slot(vt, e), e, m_ref, acc_ref, p_ref,
                    block_max=None if diag else mx_ref[slot, e])

    scores(0, 0, first=True)

    def pair_body(jj, c):
        j = 2 * jj
        scores(j + 1, 1)
        softmax_pv(j, 0, False)
        scores(j + 2, 0)
        softmax_pv(j + 1, 1, False)
        return c

    lax.fori_loop(0, i // 2, pair_body, 0)

    @pl.when(i % 2 == 0)
    def _():
        softmax_pv(i, 0, True)

    @pl.when(i % 2 == 1)
    def _():
        scores(i, 1)
        softmax_pv(i - 1, 0, False)
        softmax_pv(i, 1, True)

    outs = []
    for p in range(ns // 2):
        e0, e1 = 2 * p, 2 * p + 1
        if kind == "diff":
            a0, a1 = acc_ref[e0], acc_ref[e1]
            dv = 2 * DH_B
            o = (a0[:dv] / a0[dv:dv + 1] - par_ref[0] * (a1[:dv] / a1[dv:dv + 1])).T
            ms = jnp.mean(o * o, axis=-1, keepdims=True)
            outs.append(o * lax.rsqrt(ms + EPS) * g_ref[:, p * 128:(p + 1) * 128])
        else:
            a0, a1 = acc_ref[e0], acc_ref[e1]
            outs.append(jnp.concatenate([a0[:64] / a0[64:], a1[:64] / a1[64:]], axis=0).T)
    o_ref[0, pl.ds(qstart, tq), :] = (
        outs[0] if len(outs) == 1 else jnp.concatenate(outs, axis=1)).astype(o_ref.dtype)


def _prompt_attn_body(*refs, kind, tq, ns, nq):
    kb_ref, kbc_ref = (refs[4], refs[11]) if kind == "diff" else (refs[3], refs[9])
    def fill_bias_columns():
        def fill(c, carry):
            c0 = pl.multiple_of(c * 128, 128)
            for e in range(ns):
                row = kb_ref[0, 0, e:e + 1, pl.ds(c0, 128)]
                kbc_ref[e, pl.ds(c0, 128), :] = jnp.broadcast_to(row, (128, 128)).T
            return carry

        lax.fori_loop(0, kbc_ref.shape[1] // 128, fill, 0)

    if kind == "fox":
        hg = pl.program_id(1)
        fill_bias_columns()
    else:
        hg = pl.program_id(0)
        pl.when(pl.program_id(1) == 0)(fill_bias_columns)

    def query_block(i, carry):
        _attn_query_block(i, hg, refs, kind, tq, ns)
        return carry

    lax.fori_loop(0, nq, query_block, 0)


def _prompt_attn(kind, q, k, vt, kb, bsz, lp, par=None, gain=None):
    rows = q.shape[0]
    tq = ATT_BLOCK
    ns = ATT_STREAMS
    wq = (ns // 2) * (2 * HEAD_PAD if kind == "mla" else 128)
    vrows = DIFF_VT_ROWS if kind == "diff" else 128
    wv = (ns // 2) * (vrows if kind == "diff" else 2 * vrows)
    wo = (ns // 2) * 128
    n_hg = vt.shape[0] // wv
    nq = lp // tq
    kb_b, kb_h = kb.shape[0] > 1, kb.shape[1] > 1
    seq_major = kind == "fox"

    def bh(f):
        return (lambda b, h: f(b, h)) if seq_major else (lambda h, b: f(b, h))

    in_specs = [pl.BlockSpec((1, lp, wq), bh(lambda b, h: (0, b, h))),
                pl.BlockSpec((1, lp, wq), bh(lambda b, h: (0, b, h))),
                pl.BlockSpec((wv, lp), bh(lambda b, h: (h, b))),
                pl.BlockSpec((1, 1, ns, lp), bh(lambda b, h: (b if kb_b else 0, h if kb_h else 0, 0, 0)))]
    args = [q[None], k[None], vt, kb]
    if kind == "diff":
        in_specs = ([pl.BlockSpec(memory_space=pltpu.SMEM)] + in_specs
                    + [pl.BlockSpec((1, wo), bh(lambda b, h: (0, h)))])
        args = [par] + args + [gain]
    assert ATT_ROWS == CHUNK
    body = functools.partial(_prompt_attn_body, kind=kind, tq=tq, ns=ns, nq=nq)
    return pl.pallas_call(
        body,
        grid=(bsz, n_hg) if seq_major else (n_hg, bsz),
        in_specs=in_specs,
        out_specs=pl.BlockSpec((1, lp, wo), bh(lambda b, h: (0, b, h))),
        out_shape=jax.ShapeDtypeStruct((1, rows, n_hg * wo), BF16),
        scratch_shapes=[pltpu.VMEM((ns, 1, tq), F32),
                        pltpu.VMEM((ns, vrows, tq), F32), pltpu.VMEM((2, ns, tq, tq), F32),
                        pltpu.VMEM((ns, tq, tq), BF16), pltpu.VMEM((ns, lp, 128), F32),
                        pltpu.VMEM((2, ns, 8, tq), F32)],
        compiler_params=pltpu.CompilerParams(
            dimension_semantics=("arbitrary", "arbitrary"), vmem_limit_bytes=VMEM_LIMIT),
    )(*args)[0]


def _cumsum_body(x_ref, tri_ref, o_ref, carry_ref):
    @pl.when(pl.program_id(0) == 0)
    def _():
        carry_ref[...] = jnp.zeros(carry_ref.shape, F32)

    y = jnp.dot(x_ref[...], tri_ref[...], preferred_element_type=F32,
                precision=lax.Precision.HIGHEST) + carry_ref[...]
    o_ref[...] = y
    carry_ref[...] = y[:, -1:]


def _cumsum_lanes(x, blk=256):
    rows, n = x.shape
    tri = (jnp.arange(blk)[:, None] <= jnp.arange(blk)[None, :]).astype(F32)
    return pl.pallas_call(
        _cumsum_body,
        grid=(n // blk,),
        in_specs=[pl.BlockSpec((rows, blk), lambda j: (0, j)), _const_spec((blk, blk))],
        out_specs=pl.BlockSpec((rows, blk), lambda j: (0, j)),
        out_shape=jax.ShapeDtypeStruct((rows, n), F32),
        scratch_shapes=[pltpu.VMEM((rows, 1), F32)],
        compiler_params=pltpu.CompilerParams(dimension_semantics=("arbitrary",)),
    )(x, tri)


def _expand_rows(x, rep):
    h, n = x.shape
    return jnp.broadcast_to(x[:, None, :], (h, rep, n)).reshape(h * rep, n)


def _dec_online(s, v16, m_ref, l_ref, acc_ref):
    m_prev = m_ref[...]
    m_new = jnp.maximum(m_prev, jnp.max(s, axis=-1, keepdims=True))
    alpha = jnp.exp2(m_prev - m_new)
    p = jnp.exp2(s - m_new)
    l_ref[...] = alpha * l_ref[...] + jnp.sum(p, axis=-1, keepdims=True)
    acc_ref[...] = alpha * acc_ref[...] + _dot(p.astype(BF16), v16)
    m_ref[...] = m_new


def _dec_init(m_ref, l_ref, acc_ref):
    m_ref[...] = jnp.full(m_ref.shape, NEG, F32)
    l_ref[...] = jnp.zeros(l_ref.shape, F32)
    acc_ref[...] = jnp.zeros(acc_ref.shape, F32)


def _diag_blocks(o, n_heads, ds, width):
    return jnp.concatenate([o[h * ds:(h + 1) * ds, h * width:(h + 1) * width] for h in range(n_heads)], axis=1)


def _diff_dec_body(par_ref, q_ref, kc_ref, vc_ref, kn_ref, vn_ref, g_ref, _, o_ref, m_ref, l_ref, acc_ref,
                   *, past, kb, ds, n_heads):
    jb = pl.program_id(1)
    hr = 2 * ds
    rows = n_heads * hr
    dv = 2 * DH_B
    r = lax.broadcasted_iota(jnp.int32, (rows, 1), 0)
    head = r // hr
    slope = LOG2E * jnp.exp2(-8.0 * (head + 1).astype(F32) / n_heads)
    qpos = past + (r % ds)
    q = q_ref[0]

    def key_block(k16, v16, key0):
        col = lax.broadcasted_iota(jnp.int32, (1, k16.shape[0]), 1)
        kpos = key0 + col // n_heads
        s = _dot_nt(q, k16) - slope * jnp.abs(qpos - kpos).astype(F32)
        s = jnp.where(col % n_heads == head, s, NEG)
        _dec_online(s, v16, m_ref, l_ref, acc_ref)

    @pl.when(jb == 0)
    def _():
        _dec_init(m_ref, l_ref, acc_ref)
        key_block(kn_ref[0], vn_ref[0], past)

    key_block(kc_ref[0, 0].astype(BF16), vc_ref[0, 0].astype(BF16), jb * kb)

    @pl.when(jb == pl.num_programs(1) - 1)
    def _():
        o = acc_ref[...] / l_ref[...]
        outs = []
        for h in range(n_heads):
            oh = o[h * hr:h * hr + ds] - par_ref[0] * o[h * hr + ds:(h + 1) * hr]
            ms = jnp.mean(oh * oh, axis=-1, keepdims=True)
            outs.append(oh * lax.rsqrt(ms + EPS) * g_ref[:, h * dv:(h + 1) * dv])
        o_ref[0] = jnp.concatenate(outs, axis=1).astype(o_ref.dtype)


def _fox_dec_body(q_ref, kc_ref, vc_ref, kn_ref, vn_ref, fc_ref, fn_ref, _, o_ref, m_ref, l_ref, acc_ref,
                  *, ds, n_heads):
    jb = pl.program_id(1)
    rows = n_heads * ds
    q = q_ref[0]
    fnew = fn_ref[0][:, :ds]
    fref = _expand_rows(fn_ref[0][:, 0:1], ds)

    @pl.when(jb == 0)
    def _():
        _dec_init(m_ref, l_ref, acc_ref)
        r = lax.broadcasted_iota(jnp.int32, (rows, 1), 0)
        kidx = lax.broadcasted_iota(jnp.int32, (1, ds), 1)
        s = _dot_nt(q, kn_ref[0]) + LOG2E * (fref - _expand_rows(fnew, ds))
        s = jnp.where(kidx <= (r % ds), s, NEG)
        _dec_online(s, vn_ref[0].astype(BF16), m_ref, l_ref, acc_ref)

    s = _dot_nt(q, kc_ref[0, 0].astype(BF16)) + LOG2E * (fref - _expand_rows(fc_ref[0], ds))
    _dec_online(s, vc_ref[0, 0].astype(BF16), m_ref, l_ref, acc_ref)

    @pl.when(jb == pl.num_programs(1) - 1)
    def _():
        o = acc_ref[...] / l_ref[...]
        o_ref[0] = _diag_blocks(o, n_heads, ds, DH_C).astype(o_ref.dtype)


def _mla_dec_body(qn_ref, qp_ref, cc_ref, pc_ref, cn_ref, pn_ref, wk_ref, wv_ref, ones_ref, _,
                  o_ref, m_ref, l_ref, acc_ref, *, ds, n_heads):
    jb = pl.program_id(1)
    qn = qn_ref[0]
    qp = qp_ref[0]
    ones_h = ones_ref[...]

    def key_block(ckv, kpe):
        c16 = ckv.astype(BF16)
        kn = _dot(c16, wk_ref[...])
        v = _dot(c16, wv_ref[...])
        n = kpe.shape[0]
        ss = _dot_nt(ones_h, (kn * kn).astype(BF16)) + _dot_nt(jnp.ones((n_heads, ROPE_D), BF16),
                                                               (kpe * kpe).astype(BF16))
        rinv = lax.rsqrt(ss * (1.0 / (NOPE_D + ROPE_D)) + EPS)
        s = _dot_nt(qn, kn.astype(BF16)) + _dot_nt(qp, kpe.astype(BF16))
        s = s * _expand_rows(rinv, ds)
        _dec_online(s, v.astype(BF16), m_ref, l_ref, acc_ref)

    @pl.when(jb == 0)
    def _():
        _dec_init(m_ref, l_ref, acc_ref)
        key_block(cn_ref[0], pn_ref[0])

    key_block(cc_ref[0, 0], pc_ref[0, 0])

    @pl.when(jb == pl.num_programs(1) - 1)
    def _():
        o = acc_ref[...] / l_ref[...]
        o_ref[0] = _diag_blocks(o, n_heads, ds, V_D).astype(o_ref.dtype)


def _per_seq(a):
    return (a, pl.BlockSpec((1,) + a.shape[1:], lambda b, j: (b, 0, 0)))


def _dec_const(a):
    return (a, _const_spec(a.shape))


def _dec_call(body, ins, prev, row0, ds, rows, acc_w, nb, n_kb, smem=None):
    in_specs = [spec for _, spec in ins] + [pl.BlockSpec(memory_space=pl.ANY)]
    args = [a for a, _ in ins] + [prev[None]]
    if smem is not None:
        in_specs = [pl.BlockSpec(memory_space=pltpu.SMEM)] + in_specs
        args = [smem] + args
    blk0 = row0 // ds
    return pl.pallas_call(
        body,
        grid=(nb, n_kb),
        in_specs=in_specs,
        out_specs=pl.BlockSpec((1, ds, prev.shape[1]), lambda b, j: (0, blk0 + b, 0)),
        out_shape=jax.ShapeDtypeStruct((1,) + prev.shape, prev.dtype),
        input_output_aliases={len(args) - 1: 0},
        scratch_shapes=[pltpu.VMEM((rows, 1), F32), pltpu.VMEM((rows, 1), F32), pltpu.VMEM((rows, acc_w), F32)],
        compiler_params=pltpu.CompilerParams(
            dimension_semantics=("parallel", "arbitrary"), vmem_limit_bytes=VMEM_LIMIT),
    )(*args)[0]


def kernel(x_prompt, x_sample, state_s5_re, state_s5_im, cache_diff_k, cache_diff_v, cache_fox_k, cache_fox_v, cache_fox_logf, cache_mla_ckv, cache_mla_kpe, meta_tokens, ffn_norm, ffn_w_in, ffn_w_out, mix_norm, ab_w_in, ab_w_out, s5_a_re, s5_a_im, s5_log_step, s5_b_re, s5_b_im, s5_c_re, s5_c_im, s5_d, s5_glu_w, s5_glu_b, diff_q_norm, diff_k_norm, diff_lam, diff_sub_norm, cd_w_in, cd_w_out, fox_q_norm, fox_k_norm, fox_f_bias, mla_q_a_norm, mla_q_b, mla_kv_a_norm, mla_kv_b, mla_q_norm, mla_k_norm):
    bsz, seq, dm = x_prompt.shape
    nb, ds, _ = x_sample.shape
    n_meta = meta_tokens.shape[0]
    past = cache_diff_k.shape[2]
    front = ROW_ALIGN - n_meta
    lp = front + n_meta + seq
    ltot = n_meta + seq
    assert n_meta + front == ROW_ALIGN and lp % ATT_BLOCK == 0 and front % CHUNK == CHUNK - n_meta
    assert ds == S5_STEP and past % CHUNK == 0 and ds <= CHUNK
    kb = min(DEC_KB, past)
    assert past % kb == 0
    n_kb = past // kb
    assert ffn_norm.shape[0] == 2 and ab_w_in.shape[0] == 1 and cd_w_in.shape[0] == 1

    h_b = cache_diff_k.shape[3]
    h_c = cache_fox_k.shape[3]
    h_d = mla_q_b.shape[2] // (NOPE_D + ROPE_D)
    s5w = s5_glu_w.shape[1]
    n_grp = s5w // S5_GROUP
    qkw = h_b * 2 * DH_B
    fox_w = h_c * DH_C
    q_lora = mla_q_a_norm.shape[1]
    kv_lora = mla_kv_a_norm.shape[1]
    d_qk = NOPE_D + ROPE_D

    n_p = bsz * lp
    head_rows_x = jnp.concatenate([jnp.zeros((front, dm), F32), meta_tokens.astype(F32)], axis=0)
    pieces = []
    for b in range(bsz):
        pieces += [head_rows_x, x_prompt[b]]
    x = jnp.concatenate(pieces + [x_sample.reshape(nb * ds, dm)], axis=0)
    rows = x.shape[0]
    tm = _row_tile(rows)

    ones64 = _block_diag_ones(64)
    ones128 = _block_diag_ones(128)

    x = _ffn(x, ffn_norm[0, 0], ffn_w_in[0, 0], ffn_w_out[0, 0], tm)

    gq = (jnp.tile(diff_q_norm[0], 2 * h_b) * (DH_B ** -0.5 * LOG2E)).reshape(1, qkw)
    gk = jnp.tile(diff_k_norm[0], 2 * h_b).reshape(1, qkw)
    s5_cols = S5_STEP * S5_GROUP
    q16, k16, u_t, vt_diff, k32h, v32h = _row_call(
        functools.partial(_ab_in_body, widths=(s5w, qkw)),
        [x], [mix_norm[0].reshape(1, dm), ab_w_in[0].astype(BF16), gq, gk, ones64],
        [qkw, qkw], [BF16, BF16], tm,
        group_outs=[jax.ShapeDtypeStruct((n_grp, rows // S5_STEP, s5_cols), BF16)],
        scratch=[pltpu.VMEM((s5w // 128, tm, 128), F32)],
        col_outs=[(h_b * DIFF_VT_ROWS, BF16)], head_outs=[(h_b, F32), (h_b, F32)])
    k32 = k32h.reshape(rows, h_b, 2 * DH_B)
    v32 = v32h.reshape(rows, h_b, 2 * DH_B)

    mats = _s5_matrices(s5_a_re[0], s5_a_im[0], s5_log_step[0], s5_b_re[0], s5_b_im[0],
                        s5_c_re[0], s5_c_im[0], s5_d[0])
    n_ch = lp // S5_STEP
    y_t, st_p = _s5_scan(u_t, jnp.zeros((n_grp, bsz, 4 * P_A), F32), mats, n_ch, bsz, 0)
    h_re = jnp.transpose(state_s5_re[0].astype(F32), (1, 0, 2))
    h_im = jnp.transpose(state_s5_im[0].astype(F32), (1, 0, 2))
    y_t, st_s = _s5_scan(u_t, jnp.concatenate([h_re, h_im, h_im, h_re], axis=-1), mats, 1, nb,
                         n_p // S5_STEP, y_prev=y_t)

    lv = diff_lam[0].astype(F32)
    lam_init = 0.8 - 0.6 * math.exp(-0.3 * 0)
    lam = jnp.exp(jnp.sum(lv[0] * lv[1])) - jnp.exp(jnp.sum(lv[2] * lv[3])) + lam_init
    slopes = jnp.exp2(-8.0 * jnp.arange(1, h_b + 1, dtype=F32) / h_b)
    par = jnp.concatenate([lam[None], slopes]).astype(F32)
    subg = (jnp.tile(diff_sub_norm[0], h_b) * (1.0 - lam_init)).reshape(1, qkw)
    kpad = jnp.arange(lp) < front
    kb_diff = jnp.where(kpad[None, :], NEG, LOG2E * slopes[:, None] * jnp.arange(lp, dtype=F32)[None, :])
    kb_diff = jnp.broadcast_to(kb_diff[None, :, None, :], (1, h_b, 2, lp)).reshape(
        1, 2 * h_b // ATT_STREAMS, ATT_STREAMS, lp)
    o_all = _prompt_attn("diff", q16, k16, vt_diff, kb_diff, bsz, lp, par=par, gain=subg)
    qs = q16[n_p:].reshape(nb, ds, h_b, 2, DH_B)
    eye_2 = jnp.eye(2, dtype=BF16)
    qbd = jnp.einsum('bqhmd,mM->bhmqMd', qs, eye_2).reshape(nb, h_b * 2 * ds, 2 * DH_B)
    cache_spec = lambda w: pl.BlockSpec((1, 1, kb, w), lambda b, j: (0, b, j, 0))
    head_rows = lambda a: a.reshape(a.shape[0], nb, past * h_b, 2 * DH_B)
    head_cache_spec = pl.BlockSpec((1, 1, kb * h_b, 2 * DH_B), lambda b, j: (0, b, j, 0))
    o_all = _dec_call(
        functools.partial(_diff_dec_body, past=past, kb=kb, ds=ds, n_heads=h_b),
        [_per_seq(qbd), (head_rows(cache_diff_k), head_cache_spec), (head_rows(cache_diff_v), head_cache_spec),
         _per_seq(k16[n_p:].reshape(nb, ds * h_b, 2 * DH_B)), _per_seq(v32h[n_p * h_b:].astype(BF16).reshape(nb, ds * h_b, 2 * DH_B)),
         _dec_const(subg)],
        o_all, n_p, ds, 2 * h_b * ds, 2 * DH_B, nb, n_kb, smem=par)

    x = _row_call(
        functools.partial(_ab_out_body, s5w=s5w),
        [x, o_all],
        [s5_glu_w[0].astype(BF16), s5_glu_b[0].reshape(1, s5w), ab_w_out[0].astype(BF16)],
        [dm], [F32], tm, group_ins=[y_t], scratch=[pltpu.VMEM((s5w // 128, tm, 128), F32)])[0]

    x = _ffn(x, ffn_norm[0, 1], ffn_w_in[0, 1], ffn_w_out[0, 1], tm)

    x = _ffn(x, ffn_norm[1, 0], ffn_w_in[1, 0], ffn_w_out[1, 0], tm)

    half = ROPE_D // 2
    inv = ROPE_THETA ** (-jnp.arange(half, dtype=F32) / half)
    pos = jnp.concatenate([jnp.tile(jnp.arange(lp, dtype=jnp.int32) - front, bsz),
                           jnp.tile(past + jnp.arange(ds, dtype=jnp.int32), nb)]).astype(F32)
    ang = pos[:, None] * inv[None, :]
    pad_r = HEAD_PAD - NOPE_D - ROPE_D
    cos_t = jnp.concatenate([jnp.ones((rows, NOPE_D), F32), jnp.cos(ang), jnp.cos(ang),
                             jnp.zeros((rows, pad_r), F32)], axis=1)
    sin_t = jnp.concatenate([jnp.zeros((rows, NOPE_D), F32), jnp.sin(ang), jnp.sin(ang),
                             jnp.zeros((rows, pad_r), F32)], axis=1)

    wcd = cd_w_in[0]
    c_fg = 3 * fox_w
    c_qa = c_fg + h_c
    c_kva = c_qa + q_lora
    c_pe = c_kva + kv_lora
    w_pe = wcd[:, c_pe:c_pe + ROPE_D]
    zc = lambda n: jnp.zeros((dm, n), F32)
    w_cd = jnp.concatenate([
        wcd[:, :3 * fox_w], wcd[:, c_qa:c_qa + q_lora], wcd[:, c_kva:c_kva + kv_lora],
        zc(NOPE_D), w_pe, zc(pad_r),
        zc(NOPE_D), -w_pe[:, half:], w_pe[:, :half], zc(pad_r),
        wcd[:, c_fg:c_fg + h_c], zc(HEAD_PAD - h_c)], axis=1).astype(BF16)
    qb = mla_q_b[0].reshape(q_lora, h_d, d_qk)
    zq = lambda n: jnp.zeros((q_lora, h_d, n), F32)
    qb_pad = jnp.concatenate([qb, zq(pad_r)], axis=-1).reshape(q_lora, h_d * HEAD_PAD)
    qb_rot = jnp.concatenate([zq(NOPE_D), -qb[..., NOPE_D + half:], qb[..., NOPE_D:NOPE_D + half], zq(pad_r)],
                             axis=-1).reshape(q_lora, h_d * HEAD_PAD)
    wq2 = jnp.concatenate([qb_pad, qb_rot], axis=1).astype(BF16)
    kvb = mla_kv_b[0].reshape(kv_lora, h_d, NOPE_D + V_D)
    wk_pad = jnp.concatenate([kvb[..., :NOPE_D], jnp.zeros((kv_lora, h_d, HEAD_PAD - NOPE_D), F32)],
                             axis=-1).reshape(kv_lora, h_d * HEAD_PAD).astype(BF16)
    wk_cmp = kvb[..., :NOPE_D].reshape(kv_lora, h_d * NOPE_D).astype(BF16)
    wv_cmp = kvb[..., NOPE_D:].reshape(kv_lora, h_d * V_D).astype(BF16)
    gfq = (jnp.tile(fox_q_norm[0], h_c) * (DH_C ** -0.5 * LOG2E)).reshape(1, fox_w)
    gfk = jnp.tile(fox_k_norm[0], h_c).reshape(1, fox_w)
    fbias = jnp.concatenate([fox_f_bias[0], jnp.zeros((HEAD_PAD - h_c,), F32)]).reshape(1, HEAD_PAD)
    gmq = jnp.tile(jnp.concatenate([mla_q_norm[0] * mla_k_norm[0] * (d_qk ** -0.5 * LOG2E), jnp.zeros((pad_r,), F32)]),
                   h_d).reshape(1, h_d * HEAD_PAD)

    (fq16, fk32, fk16, fv32, qm16, ckv32, km16, fvt, vmt, logf_t, kpe_t) = _row_call(
        functools.partial(_cd_in_body, parts=IN_PARTS, fox_w=fox_w, q_lora=q_lora, kv_lora=kv_lora, n_heads=h_d),
        [x, cos_t, sin_t],
        [mix_norm[1].reshape(1, dm), w_cd, gfq, gfk, fbias, mla_q_a_norm[0].reshape(1, q_lora), wq2,
         mla_kv_a_norm[0].reshape(1, kv_lora), wk_pad, wv_cmp.T, gmq, ones64, ones128],
        [fox_w, fox_w, fox_w, fox_w, h_d * HEAD_PAD, kv_lora, h_d * HEAD_PAD],
        [BF16, F32, BF16, F32, BF16, F32, BF16], tm,
        col_outs=[(2 * fox_w, BF16), (2 * h_d * V_D, BF16), (h_c, F32), (ROPE_D, F32)])
    logf_ps = jnp.transpose(logf_t[:, :n_p].reshape(h_c, bsz, lp), (1, 0, 2))
    logf_ss = jnp.transpose(logf_t[:, n_p:].reshape(h_c, nb, ds), (1, 0, 2))
    kpe_s = jnp.transpose(kpe_t[:, n_p:]).reshape(nb, ds, ROPE_D)

    f_p = _cumsum_lanes(logf_ps.reshape(bsz * h_c, lp)).reshape(bsz, h_c, lp)
    logf_s = jnp.concatenate([
        jnp.transpose(cache_fox_logf[0].astype(F32), (0, 2, 1)),
        logf_ss,
        jnp.zeros((nb, h_c, 256 - ds), F32)], axis=2).reshape(nb * h_c, past + 256)
    f_s = _cumsum_lanes(logf_s).reshape(nb, h_c, past + 256)

    kb_fox = jnp.where(kpad[None, None, :], NEG, -LOG2E * f_p).reshape(bsz, h_c // ATT_STREAMS, ATT_STREAMS, lp)
    oc_all = _prompt_attn("fox", fq16, fk16, fvt, kb_fox, bsz, lp)
    kb_mla = jnp.broadcast_to(jnp.where(kpad, NEG, 0.0).astype(F32)[None, None, None, :], (1, 1, ATT_STREAMS, lp))
    od_all = _prompt_attn("mla", qm16, km16, vmt, kb_mla, bsz, lp)

    eye_c = jnp.eye(h_c, dtype=BF16)
    fqs = fq16[n_p:].reshape(nb, ds, h_c, DH_C)
    fq_bd = jnp.einsum('bqhd,hH->bhqHd', fqs, eye_c).reshape(nb, h_c * ds, fox_w)
    fkc = cache_fox_k.reshape(cache_fox_k.shape[0], nb, past, fox_w)
    fvc = cache_fox_v.reshape(cache_fox_v.shape[0], nb, past, fox_w)
    oc_all = _dec_call(
        functools.partial(_fox_dec_body, ds=ds, n_heads=h_c),
        [_per_seq(fq_bd), (fkc, cache_spec(fox_w)), (fvc, cache_spec(fox_w)),
         _per_seq(fk16[n_p:].reshape(nb, ds, fox_w)), _per_seq(fv32[n_p:].reshape(nb, ds, fox_w)),
         (f_s, pl.BlockSpec((1, h_c, kb), lambda b, j: (b, 0, j))),
         (f_s, pl.BlockSpec((1, h_c, 128), lambda b, j: (b, 0, past // 128)))],
        oc_all, n_p, ds, h_c * ds, fox_w, nb, n_kb)

    eye_d = jnp.eye(h_d, dtype=BF16)
    qms = qm16[n_p:].reshape(nb, ds, h_d, HEAD_PAD)
    qn_bd = jnp.einsum('bqhd,hH->bhqHd', qms[..., :NOPE_D], eye_d).reshape(nb, h_d * ds, h_d * NOPE_D)
    qp_s = jnp.transpose(qms[..., NOPE_D:NOPE_D + ROPE_D], (0, 2, 1, 3)).reshape(nb, h_d * ds, ROPE_D)
    ones_h = jnp.repeat(jnp.eye(h_d, dtype=BF16), NOPE_D, axis=1)
    od_all = _dec_call(
        functools.partial(_mla_dec_body, ds=ds, n_heads=h_d),
        [_per_seq(qn_bd), _per_seq(qp_s),
         (cache_mla_ckv, pl.BlockSpec((1, 1, kb, kv_lora), lambda b, j: (0, b, j, 0))),
         (cache_mla_kpe, pl.BlockSpec((1, 1, kb, ROPE_D), lambda b, j: (0, b, j, 0))),
         _per_seq(ckv32[n_p:].reshape(nb, ds, kv_lora)), _per_seq(kpe_s),
         _dec_const(wk_cmp), _dec_const(wv_cmp), _dec_const(ones_h)],
        od_all, n_p, ds, h_d * ds, h_d * V_D, nb, n_kb)

    x = _ffn(x, ffn_norm[1, 1], ffn_w_in[1, 1], ffn_w_out[1, 1], tm, mix=(oc_all, od_all, cd_w_out[0]))

    def p_rows(a, shape):
        return a[:n_p].reshape((bsz, lp) + a.shape[1:])[:, front:front + ltot].reshape((1, bsz, ltot) + shape)

    def s_rows(a, shape):
        return a[n_p:].reshape((1, nb, ds) + shape)

    def p_cols(at):
        w = at.shape[0]
        return jnp.transpose(at[:, :n_p].reshape(w, bsz, lp)[:, :, front:front + ltot], (1, 2, 0))[None]

    def s_cols(at):
        return jnp.transpose(at[:, n_p:]).reshape(1, nb, ds, at.shape[0])

    def s5_state(st):
        st = jnp.transpose(st, (1, 0, 2))
        return st[None, :, :, :P_A], st[None, :, :, P_A:]

    y_prompt = jnp.stack([x[b * lp + front + n_meta:(b + 1) * lp] for b in range(bsz)])
    y_sample = x[n_p:].reshape(nb, ds, dm)
    s5_re_p, s5_im_p = s5_state(st_p)
    s5_re_s, s5_im_s = s5_state(st_s)
    return (y_prompt, y_sample,
            s5_re_p, s5_im_p, p_rows(k32, (h_b, 2 * DH_B)), p_rows(v32, (h_b, 2 * DH_B)),
            p_rows(fk32, (h_c, DH_C)), p_rows(fv32, (h_c, DH_C)), p_cols(logf_t),
            p_rows(ckv32, (kv_lora,)), p_cols(kpe_t),
            s5_re_s, s5_im_s, s_rows(k32, (h_b, 2 * DH_B)), s_rows(v32, (h_b, 2 * DH_B)),
            s_rows(fk32, (h_c, DH_C)), s_rows(fv32, (h_c, DH_C)), s_cols(logf_t),
            s_rows(ckv32, (kv_lora,)), s_cols(kpe_t))
```

```python
import functools
import math

import jax
import jax.numpy as jnp
from jax import lax
from jax.experimental import pallas as pl
from jax.experimental.pallas import tpu as pltpu

F32 = jnp.float32
BF16 = jnp.bfloat16

EPS = 1e-6
CHUNK = 64
ROW_ALIGN = 256
S5_GROUP = 16
S5_STEP = 16
S5_GROUPS_PER_STEP = 4
P_A = 64
DH_B = 64
DH_C = 64
NOPE_D = 64
ROPE_D = 32
V_D = 64
HEAD_PAD = 128
ROPE_THETA = 10000.0
NEG = -1e30
LOG2E = math.log2(math.e)
VMEM_LIMIT = 56 * 1024 * 1024
IN_PARTS = 2
OUT_PARTS = 4
ATT_BLOCK = 256
ATT_STREAMS = 4
ATT_ROWS = 64
DIFF_VT_ROWS = 2 * DH_B + 16
DEC_KB = 4096


def _dot(a, b):
    return jnp.dot(a, b, preferred_element_type=F32)


def _dot_nt(a, b):
    return lax.dot_general(a, b, (((1,), (1,)), ((), ())), preferred_element_type=F32)


def _rms_rows(x, g):
    ms = jnp.mean(x * x, axis=-1, keepdims=True)
    return x * lax.rsqrt(ms + EPS) * g


def _group_sumsq(x, ones_bd):
    w = x.shape[-1]
    parts = [_dot((x[:, c:c + 256] * x[:, c:c + 256]).astype(BF16), ones_bd) for c in range(0, w, 256)]
    return parts[0] if len(parts) == 1 else jnp.concatenate(parts, axis=1)


def _block_diag_ones(group, n=256):
    r = jnp.arange(n) // group
    return (r[:, None] == r[None, :]).astype(BF16)


def _const_spec(shape):
    nd = len(shape)
    return pl.BlockSpec(shape, lambda *_: (0,) * nd, pipeline_mode=pl.Buffered(1))


def _row_tile(rows, cap=512):
    t = cap
    while rows % t:
        t //= 2
    return t


def _group_spec(a, tm):
    return pl.BlockSpec((a.shape[0], tm // S5_STEP, a.shape[2]), lambda i: (0, i, 0))


def _row_call(body, row_ins, consts, out_widths, out_dtypes, tm, group_ins=(), group_outs=(), scratch=(),
              col_outs=(), head_outs=()):
    rows = row_ins[0].shape[0]
    in_specs = [pl.BlockSpec((tm, a.shape[1]), lambda i: (i, 0)) for a in row_ins]
    in_specs += [_group_spec(a, tm) for a in group_ins]
    in_specs += [_const_spec(c.shape) for c in consts]
    out_specs = [pl.BlockSpec((tm, w), lambda i: (i, 0)) for w in out_widths]
    out_specs += [_group_spec(a, tm) for a in group_outs]
    out_specs += [pl.BlockSpec((w, tm), lambda i: (0, i)) for w, _ in col_outs]
    out_shape = [jax.ShapeDtypeStruct((rows, w), d) for w, d in zip(out_widths, out_dtypes)]
    out_shape += list(group_outs)
    out_shape += [jax.ShapeDtypeStruct((w, rows), d) for w, d in col_outs]
    out_specs += [pl.BlockSpec((tm * h, 128), lambda i: (i, 0)) for h, _ in head_outs]
    out_shape += [jax.ShapeDtypeStruct((rows * h, 128), d) for h, d in head_outs]
    return pl.pallas_call(
        body,
        grid=(rows // tm,),
        in_specs=in_specs,
        out_specs=out_specs,
        out_shape=out_shape,
        scratch_shapes=list(scratch),
        compiler_params=pltpu.CompilerParams(
            dimension_semantics=("parallel",), vmem_limit_bytes=VMEM_LIMIT),
    )(*row_ins, *group_ins, *consts)


def _ffn_body(*refs, d_ff, tf, mix_split):
    if mix_split:
        x_ref, a_ref, b_ref, g_ref, win_ref, wout_ref, wmix_ref, o_ref = refs
        x = (x_ref[...] + _dot(a_ref[...], wmix_ref[:mix_split, :]) + _dot(b_ref[...], wmix_ref[mix_split:, :]))
    else:
        x_ref, g_ref, win_ref, wout_ref, o_ref = refs
        x = x_ref[...]
    xn = _rms_rows(x, g_ref[...]).astype(BF16)
    acc = jnp.zeros(x.shape, F32)
    for c in range(0, d_ff, tf):
        gate = _dot(xn, win_ref[:, c:c + tf])
        up = _dot(xn, win_ref[:, d_ff + c:d_ff + c + tf])
        a = (gate * jax.nn.sigmoid(gate) * up).astype(BF16)
        acc = acc + _dot(a, wout_ref[c:c + tf, :])
    o_ref[...] = x + 0.5 * acc


def _ffn(x, g, w_in, w_out, tm, mix=None):
    d_ff = w_out.shape[0]
    consts = [g.reshape(1, -1), w_in.astype(BF16), w_out.astype(BF16)]
    rows_in = [x]
    mix_split = 0
    if mix is not None:
        a, b, w_mix = mix
        rows_in += [a, b]
        consts.append(w_mix.astype(BF16))
        mix_split = a.shape[1]
    body = functools.partial(_ffn_body, d_ff=d_ff, tf=256, mix_split=mix_split)
    return _row_call(body, rows_in, consts, [x.shape[1]], [F32], tm)[0]


def _ab_in_body(x_ref, g_ref, w_ref, gq_ref, gk_ref, ones_ref,
                q_ref, k16_ref, ut_ref, vt_ref, k32_ref, v32_ref, us_ref, *, widths):
    s5w, qkw = widths
    xn = _rms_rows(x_ref[...], g_ref[...]).astype(BF16)
    h = _dot(xn, w_ref[...])
    n_chunk = us_ref.shape[1] // S5_STEP
    per_col = 128 // S5_GROUP
    for v in range(s5w // 128):
        us_ref[v] = h[:, v * 128:(v + 1) * 128]
        steps = [us_ref[v, pl.ds(t, n_chunk, stride=S5_STEP), :] for t in range(S5_STEP)]
        for gl in range(per_col):
            ut_ref[v * per_col + gl] = jnp.concatenate(
                [x[:, gl * S5_GROUP:(gl + 1) * S5_GROUP] for x in steps], axis=1).astype(BF16)
    q = h[:, s5w:s5w + qkw]
    k = h[:, s5w + qkw:s5w + 2 * qkw]
    v = h[:, s5w + 2 * qkw:]
    ones_bd = ones_ref[...]
    qn = q * lax.rsqrt(_group_sumsq(q, ones_bd) * (1.0 / DH_B) + EPS) * gq_ref[...]
    kn = k * lax.rsqrt(_group_sumsq(k, ones_bd) * (1.0 / DH_B) + EPS) * gk_ref[...]
    q_ref[...] = qn.astype(BF16)
    k16_ref[...] = kn.astype(BF16)
    dv = 2 * DH_B
    n_head = qkw // dv
    tm = kn.shape[0]
    for hd in range(n_head):
        k32_ref[pl.ds(hd, tm, stride=n_head), :] = kn[:, hd * dv:(hd + 1) * dv]
        v32_ref[pl.ds(hd, tm, stride=n_head), :] = v[:, hd * dv:(hd + 1) * dv]
    vt = v.T
    ones = jnp.ones((DIFF_VT_ROWS - dv, vt.shape[1]), F32)
    vt_ref[...] = jnp.concatenate(
        [a for h in range(vt.shape[0] // dv) for a in (vt[h * dv:(h + 1) * dv], ones)], axis=0).astype(BF16)


def _ab_out_body(x_ref, o_ref, yt_ref, gluw_ref, glub_ref, wout_ref, out_ref, ys_ref, *, s5w):
    tm = ys_ref.shape[1]
    part = tm // OUT_PARTS
    n_chunk = part // S5_STEP
    per_col = 128 // S5_GROUP
    for h in range(OUT_PARTS):
        c0 = h * n_chunk
        for v in range(s5w // 128):
            for t in range(S5_STEP):
                ys_ref[v, pl.ds(h * part + t, n_chunk, stride=S5_STEP), :] = jnp.concatenate(
                    [yt_ref[v * per_col + gl, c0:c0 + n_chunk, t * S5_GROUP:(t + 1) * S5_GROUP]
                     for gl in range(per_col)], axis=1)
        rows = slice(h * part, (h + 1) * part)
        y = jnp.concatenate([ys_ref[v, rows] for v in range(s5w // 128)], axis=1)
        g = 0.5 * y * (1.0 + jnp.tanh(math.sqrt(2.0 / math.pi) * (y + 0.044715 * (y * y * y))))
        z = _dot(g.astype(BF16), gluw_ref[...]) + glub_ref[...]
        s5o = g * jax.nn.sigmoid(z)
        m = _dot(s5o.astype(BF16), wout_ref[:s5w, :]) + _dot(o_ref[rows], wout_ref[s5w:, :])
        out_ref[rows] = x_ref[rows] + m


def _heads_with_ones_t(vt):
    ones = jnp.ones((64, vt.shape[1]), vt.dtype)
    outs = []
    for h in range(vt.shape[0] // 64):
        outs += [vt[h * 64:(h + 1) * 64], ones]
    return jnp.concatenate(outs, axis=0)


def _cd_in_body(*refs, parts, **dims):
    n = refs[0].shape[0] // parts
    for h in range(parts):
        rows = pl.ds(h * n, n)
        _cd_in_part(*[r.at[rows] for r in refs[:3]], *refs[3:16],
                    *[r.at[rows] for r in refs[16:23]], *[r.at[:, rows] for r in refs[23:27]], **dims)


def _cd_in_part(x_ref, cos_ref, sin_ref, g_ref, w_ref, gfq_ref, gfk_ref, fb_ref, gqa_ref, wq2_ref,
                gkva_ref, wk_ref, wv_ref, gmq_ref, ones64_ref, ones128_ref,
                fq_ref, fk32_ref, fk16_ref, fv32_ref, qm_ref, ckv_ref,
                km_ref, fvt_ref, vmt_ref, logft_ref, kpet_ref, *, fox_w, q_lora, kv_lora, n_heads):
    xn = _rms_rows(x_ref[...], g_ref[...]).astype(BF16)
    h = _dot(xn, w_ref[...])
    ones64 = ones64_ref[...]
    ones128 = ones128_ref[...]
    fq = h[:, :fox_w]
    fk = h[:, fox_w:2 * fox_w]
    fv = h[:, 2 * fox_w:3 * fox_w]
    c0 = 3 * fox_w
    qa = h[:, c0:c0 + q_lora]
    kva = h[:, c0 + q_lora:c0 + q_lora + kv_lora]
    c1 = c0 + q_lora + kv_lora
    pe_a = h[:, c1:c1 + HEAD_PAD]
    pe_b = h[:, c1 + HEAD_PAD:c1 + 2 * HEAD_PAD]
    fg = h[:, c1 + 2 * HEAD_PAD:c1 + 3 * HEAD_PAD]

    fqn = fq * lax.rsqrt(_group_sumsq(fq, ones64) * (1.0 / DH_C) + EPS) * gfq_ref[...]
    fkn = fk * lax.rsqrt(_group_sumsq(fk, ones64) * (1.0 / DH_C) + EPS) * gfk_ref[...]
    fq_ref[...] = fqn.astype(BF16)
    fk32_ref[...] = fkn
    fk16_ref[...] = fkn.astype(BF16)
    fv32_ref[...] = fv
    fvt_ref[...] = _heads_with_ones_t(fv.T).astype(BF16)

    z = fg + fb_ref[...]
    logf = jnp.minimum(z, 0.0) - jnp.log1p(jnp.exp(-jnp.abs(z)))
    logft_ref[...] = logf.T[:logft_ref.shape[0]]

    cos = cos_ref[...]
    sin = sin_ref[...]
    qan = _rms_rows(qa, gqa_ref[...]).astype(BF16)
    q2 = _dot(qan, wq2_ref[...])
    hw = n_heads * HEAD_PAD
    cos_t = jnp.concatenate([cos] * n_heads, axis=1)
    sin_t = jnp.concatenate([sin] * n_heads, axis=1)
    qr = q2[:, :hw] * cos_t + q2[:, hw:] * sin_t
    d_qk = NOPE_D + ROPE_D
    qm = qr * lax.rsqrt(_group_sumsq(qr, ones128) * (1.0 / d_qk) + EPS) * gmq_ref[...]
    qm_ref[...] = qm.astype(BF16)

    ckv = _rms_rows(kva, gkva_ref[...])
    ckv_ref[...] = ckv
    pe = pe_a * cos + pe_b * sin
    kpet_ref[...] = pe.T[NOPE_D:NOPE_D + ROPE_D]
    ckv16 = ckv.astype(BF16)
    kraw = _dot(ckv16, wk_ref[...]) + jnp.concatenate([pe] * n_heads, axis=1)
    km = kraw * lax.rsqrt(_group_sumsq(kraw, ones128) * (1.0 / d_qk) + EPS)
    km_ref[...] = km.astype(BF16)
    vmt_ref[...] = _heads_with_ones_t(_dot_nt(wv_ref[...], ckv16)).astype(BF16)


def _s5_body(*refs, n_chunks, bsz, aliased):
    if aliased:
        u_ref, h0_ref, m_ref, bm_ref, cm_ref, coef_ref, _, y_ref, st_ref, s2_ref, hp_ref = refs
    else:
        u_ref, h0_ref, m_ref, bm_ref, cm_ref, coef_ref, y_ref, st_ref, s2_ref, hp_ref = refs
    ng = u_ref.shape[0]
    half = 2 * P_A
    coefs = []
    for g in range(ng):
        s2 = _dot(u_ref[g], bm_ref[g])
        s2_ref[g, 0] = s2[:, :half]
        s2_ref[g, 1] = s2[:, half:]
        coefs.append((coef_ref[g, 0:1, :], coef_ref[g, 1:2, :], coef_ref[g, 2:3, :]))

    def step(j, carry):
        new = []
        for g in range(ng):
            ha, hb = carry[2 * g], carry[2 * g + 1]
            c1, c2, c3 = coefs[g]
            hp_ref[g, pl.ds(j, bsz, stride=n_chunks), :] = ha
            sa = s2_ref[g, 0, pl.ds(j, bsz, stride=n_chunks), :]
            sb = s2_ref[g, 1, pl.ds(j, bsz, stride=n_chunks), :]
            new += [ha * c1 + hb * c2 + sa, hb * c1 + ha * c3 + sb]
        return tuple(new)

    init = []
    for g in range(ng):
        init += [h0_ref[g][:, :half], h0_ref[g][:, half:]]
    final = lax.fori_loop(0, n_chunks, step, tuple(init))
    for g in range(ng):
        st_ref[g] = final[2 * g]
        y_ref[g] = _dot(u_ref[g], m_ref[g]) + _dot(hp_ref[g].astype(BF16), cm_ref[g])


def _s5_scan(u_t, h0, mats, n_chunks, bsz, row0, y_prev=None):
    m_mat, bm, cm, coef = mats
    g, rows_all, w = u_t.shape
    rows = n_chunks * bsz
    assert row0 % rows == 0
    blk = row0 // rows
    aliased = y_prev is not None
    body = functools.partial(_s5_body, n_chunks=n_chunks, bsz=bsz, aliased=aliased)
    ng = S5_GROUPS_PER_STEP
    per_g = lambda a: pl.BlockSpec((ng,) + a.shape[1:], lambda i: (i, 0, 0))
    in_specs = [pl.BlockSpec((ng, rows, w), lambda i: (i, blk, 0)),
                per_g(h0), per_g(m_mat), per_g(bm), per_g(cm), per_g(coef)]
    args = [u_t, h0, m_mat, bm, cm, coef]
    if aliased:
        in_specs.append(pl.BlockSpec(memory_space=pl.ANY))
        args.append(y_prev)
    return pl.pallas_call(
        body,
        grid=(g // ng,),
        in_specs=in_specs,
        out_specs=[pl.BlockSpec((ng, rows, w), lambda i: (i, blk, 0)),
                   pl.BlockSpec((ng, bsz, 2 * P_A), lambda i: (i, 0, 0))],
        out_shape=[jax.ShapeDtypeStruct((g, rows_all, w), F32),
                   jax.ShapeDtypeStruct((g, bsz, 2 * P_A), F32)],
        input_output_aliases={6: 0} if aliased else {},
        scratch_shapes=[pltpu.VMEM((ng, 2, rows, 2 * P_A), F32), pltpu.VMEM((ng, rows, 2 * P_A), F32)],
        compiler_params=pltpu.CompilerParams(
            dimension_semantics=("parallel",), vmem_limit_bytes=VMEM_LIMIT),
    )(*args)


def _s5_matrices(a_re, a_im, log_step, b_re, b_im, c_re, c_im, d):
    g = a_re.shape[0]
    t = S5_STEP
    lam = lax.complex(a_re, a_im)
    dl = lam * jnp.exp(log_step)[:, None]
    lam_bar = jnp.exp(dl)
    b_bar = ((lam_bar - 1.0) / lam)[..., None] * lax.complex(b_re, b_im)
    c = lax.complex(c_re, c_im)
    pw = jnp.exp(dl[:, None, :] * jnp.arange(t + 1, dtype=F32)[None, :, None])
    bmc = pw[:, t - 1::-1][:, :, :, None] * b_bar[:, None]
    bmc = jnp.swapaxes(bmc, 2, 3).reshape(g, t * S5_GROUP, P_A)
    bm = jnp.concatenate([bmc.real, bmc.imag, bmc.imag, bmc.real], axis=-1)
    kk = jnp.einsum('gcp,gkp,gpd->gkcd', c, pw[:, :t], b_bar).real
    kk = kk.at[:, 0].add(d.reshape(g, S5_GROUP)[:, :, None] * jnp.eye(S5_GROUP, dtype=F32))
    lag = jnp.arange(t)[None, :] - jnp.arange(t)[:, None]
    toep = jnp.where((lag >= 0)[None, :, :, None, None], kk[:, jnp.clip(lag, 0, t - 1)], 0.0)
    m_mat = jnp.transpose(toep, (0, 1, 4, 2, 3)).reshape(g, t * S5_GROUP, t * S5_GROUP)
    cp = c[:, None] * pw[:, 1:, None, :]
    cpm = jnp.transpose(cp, (0, 3, 1, 2)).reshape(g, P_A, t * S5_GROUP)
    cm = jnp.concatenate([cpm.real, -cpm.imag], axis=1)
    a_t = pw[:, t]
    ar, ai = a_t.real, a_t.imag
    zeros = jnp.zeros_like(ar)
    coef = jnp.stack([jnp.concatenate([ar, ar], -1), jnp.concatenate([-ai, ai], -1),
                      jnp.concatenate([ai, -ai], -1), jnp.concatenate([zeros, zeros], -1)], axis=1)
    return m_mat.astype(BF16), bm.astype(BF16), cm.astype(BF16), coef.astype(F32)


def _online(logits, vt, e, m_ref, acc_ref, p_ref, block_max=None):
    tk = p_ref.shape[1]
    if block_max is None:
        part = logits(0, ATT_ROWS)
        for r0 in range(ATT_ROWS, tk, ATT_ROWS):
            part = jnp.maximum(part, logits(r0, ATT_ROWS))
    else:
        part = block_max
    m_prev = m_ref[e]
    m_new = jnp.maximum(m_prev, jnp.max(part, axis=0, keepdims=True))
    alpha = jnp.exp2(m_prev - m_new)
    m_ref[e] = m_new
    for r0 in range(0, tk, ATT_ROWS):
        p_ref[e, r0:r0 + ATT_ROWS] = jnp.exp2(logits(r0, ATT_ROWS) - m_new).astype(BF16)
    acc_ref[e] = alpha * acc_ref[e] + _dot(vt, p_ref[e])


def _attn_query_block(i, hg, refs, kind, tq, ns):
    if kind == "diff":
        (par_ref, q_ref, k_ref, vt_ref, kb_ref, g_ref, o_ref,
         m_ref, acc_ref, s_ref, p_ref, kbc_ref, mx_ref) = refs
    else:
        q_ref, k_ref, vt_ref, kb_ref, o_ref, m_ref, acc_ref, s_ref, p_ref, kbc_ref, mx_ref = refs
    vrows = acc_ref.shape[1]
    m_ref[...] = jnp.full(m_ref.shape, NEG, F32)
    acc_ref[...] = jnp.zeros(acc_ref.shape, F32)

    qstart = pl.multiple_of(i * tq, tq)
    q = q_ref[0, pl.ds(qstart, tq), :]
    lane = lax.broadcasted_iota(jnp.int32, (1, 128), 1)
    qs = []
    for e in range(ns):
        if kind == "mla":
            qs.append(q[:, e * HEAD_PAD:(e + 1) * HEAD_PAD])
        else:
            qp = q[:, (e // 2) * 128:(e // 2 + 1) * 128]
            qs.append(jnp.where((lane < 64) if e % 2 == 0 else (lane >= 64), qp, jnp.zeros_like(qp)))
    lane_q = lax.broadcasted_iota(jnp.int32, (1, tq), 1)
    ref = [-kb_ref[0, 0, e:e + 1, pl.ds(qstart + (tq - 128), 128)][:, 127:128] for e in range(ns)]
    if kind == "diff":
        slope = [LOG2E * par_ref[1 + hg * (ns // 2) + p] for p in range(ns // 2)]

    tk = tq

    def k_slot(k, e):
        if kind == "mla":
            return k[:, e * HEAD_PAD:(e + 1) * HEAD_PAD]
        return k[:, (e // 2) * 128:(e // 2 + 1) * 128]

    def vt_slot(vt, e):
        r0 = (e // 2 if kind == "diff" else e) * vrows
        return vt[r0:r0 + vrows]

    def scores(j, slot, first=False):
        k0 = pl.multiple_of(j * tk, tk)
        k = k_ref[0, pl.ds(k0, tk), :]
        for e in range(ns):
            s = _dot_nt(k_slot(k, e), qs[e])
            if first or kind != "mla":
                bias = kbc_ref[e, pl.ds(k0, tk), :] + ref[e]
                s = s + jnp.concatenate([bias] * (tq // 128), axis=1)
            s_ref[slot, e] = s
            part = s[0:8]
            for r0 in range(8, tk, 8):
                part = jnp.maximum(part, s[r0:r0 + 8])
            mx_ref[slot, e] = part

    def softmax_pv(j, slot, diag):
        k0 = pl.multiple_of(j * tk, tk)
        vt = vt_ref[:, pl.ds(k0, tk)]
        for e in range(ns):
            def logits(r0, n, e=e):
                s = s_ref[slot, e, r0:r0 + n, :]
                if diag:
                    row_i = r0 + lax.broadcasted_iota(jnp.int32, (n, 1), 0)
                    if kind == "diff":
                        s = s - (2.0 * slope[e // 2]) * jnp.maximum(row_i - lane_q, 0).astype(F32)
                    if kind == "fox":
                        s = jnp.where(row_i <= lane_q, s, NEG)
                    elif r0 > 0:
                        s = jnp.where(lane_q >= r0, s, NEG)
                return s

            _online(logits, vt_slot(vt, e), e, m_ref, acc_ref, p_ref,
                    block_max=None if diag else mx_ref[slot, e])

    scores(0, 0, first=True)

    def pair_body(jj, c):
        j = 2 * jj
        scores(j + 1, 1)
        softmax_pv(j, 0, False)
        scores(j + 2, 0)
        softmax_pv(j + 1, 1, False)
        return c

    lax.fori_loop(0, i // 2, pair_body, 0)

    @pl.when(i % 2 == 0)
    def _():
        softmax_pv(i, 0, True)

    @pl.when(i % 2 == 1)
    def _():
        scores(i, 1)
        softmax_pv(i - 1, 0, False)
        softmax_pv(i, 1, True)

    outs = []
    for p in range(ns // 2):
        e0, e1 = 2 * p, 2 * p + 1
        if kind == "diff":
            a0, a1 = acc_ref[e0], acc_ref[e1]
            dv = 2 * DH_B
            o = (a0[:dv] / a0[dv:dv + 1] - par_ref[0] * (a1[:dv] / a1[dv:dv + 1])).T
            ms = jnp.mean(o * o, axis=-1, keepdims=True)
            outs.append(o * lax.rsqrt(ms + EPS) * g_ref[:, p * 128:(p + 1) * 128])
        else:
            a0, a1 = acc_ref[e0], acc_ref[e1]
            outs.append(jnp.concatenate([a0[:64] / a0[64:], a1[:64] / a1[64:]], axis=0).T)
    o_ref[0, pl.ds(qstart, tq), :] = (
        outs[0] if len(outs) == 1 else jnp.concatenate(outs, axis=1)).astype(o_ref.dtype)


def _prompt_attn_body(*refs, kind, tq, ns, nq):
    kb_ref, kbc_ref = (refs[4], refs[11]) if kind == "diff" else (refs[3], refs[9])
    def fill_bias_columns():
        def fill(c, carry):
            c0 = pl.multiple_of(c * 128, 128)
            for e in range(ns):
                row = kb_ref[0, 0, e:e + 1, pl.ds(c0, 128)]
                kbc_ref[e, pl.ds(c0, 128), :] = jnp.broadcast_to(row, (128, 128)).T
            return carry

        lax.fori_loop(0, kbc_ref.shape[1] // 128, fill, 0)

    if kind == "fox":
        hg = pl.program_id(1)
        fill_bias_columns()
    else:
        hg = pl.program_id(0)
        pl.when(pl.program_id(1) == 0)(fill_bias_columns)

    def query_block(i, carry):
        _attn_query_block(i, hg, refs, kind, tq, ns)
        return carry

    lax.fori_loop(0, nq, query_block, 0)


def _prompt_attn(kind, q, k, vt, kb, bsz, lp, par=None, gain=None):
    rows = q.shape[0]
    tq = ATT_BLOCK
    ns = ATT_STREAMS
    wq = (ns // 2) * (2 * HEAD_PAD if kind == "mla" else 128)
    vrows = DIFF_VT_ROWS if kind == "diff" else 128
    wv = (ns // 2) * (vrows if kind == "diff" else 2 * vrows)
    wo = (ns // 2) * 128
    n_hg = vt.shape[0] // wv
    nq = lp // tq
    kb_b, kb_h = kb.shape[0] > 1, kb.shape[1] > 1
    seq_major = kind == "fox"

    def bh(f):
        return (lambda b, h: f(b, h)) if seq_major else (lambda h, b: f(b, h))

    in_specs = [pl.BlockSpec((1, lp, wq), bh(lambda b, h: (0, b, h))),
                pl.BlockSpec((1, lp, wq), bh(lambda b, h: (0, b, h))),
                pl.BlockSpec((wv, lp), bh(lambda b, h: (h, b))),
                pl.BlockSpec((1, 1, ns, lp), bh(lambda b, h: (b if kb_b else 0, h if kb_h else 0, 0, 0)))]
    args = [q[None], k[None], vt, kb]
    if kind == "diff":
        in_specs = ([pl.BlockSpec(memory_space=pltpu.SMEM)] + in_specs
                    + [pl.BlockSpec((1, wo), bh(lambda b, h: (0, h)))])
        args = [par] + args + [gain]
    assert ATT_ROWS == CHUNK
    body = functools.partial(_prompt_attn_body, kind=kind, tq=tq, ns=ns, nq=nq)
    return pl.pallas_call(
        body,
        grid=(bsz, n_hg) if seq_major else (n_hg, bsz),
        in_specs=in_specs,
        out_specs=pl.BlockSpec((1, lp, wo), bh(lambda b, h: (0, b, h))),
        out_shape=jax.ShapeDtypeStruct((1, rows, n_hg * wo), BF16),
        scratch_shapes=[pltpu.VMEM((ns, 1, tq), F32),
                        pltpu.VMEM((ns, vrows, tq), F32), pltpu.VMEM((2, ns, tq, tq), F32),
                        pltpu.VMEM((ns, tq, tq), BF16), pltpu.VMEM((ns, lp, 128), F32),
                        pltpu.VMEM((2, ns, 8, tq), F32)],
        compiler_params=pltpu.CompilerParams(
            dimension_semantics=("arbitrary", "arbitrary"), vmem_limit_bytes=VMEM_LIMIT),
    )(*args)[0]


def _cumsum_body(x_ref, tri_ref, o_ref, carry_ref):
    @pl.when(pl.program_id(0) == 0)
    def _():
        carry_ref[...] = jnp.zeros(carry_ref.shape, F32)

    y = jnp.dot(x_ref[...], tri_ref[...], preferred_element_type=F32,
                precision=lax.Precision.HIGHEST) + carry_ref[...]
    o_ref[...] = y
    carry_ref[...] = y[:, -1:]


def _cumsum_lanes(x, blk=256):
    rows, n = x.shape
    tri = (jnp.arange(blk)[:, None] <= jnp.arange(blk)[None, :]).astype(F32)
    return pl.pallas_call(
        _cumsum_body,
        grid=(n // blk,),
        in_specs=[pl.BlockSpec((rows, blk), lambda j: (0, j)), _const_spec((blk, blk))],
        out_specs=pl.BlockSpec((rows, blk), lambda j: (0, j)),
        out_shape=jax.ShapeDtypeStruct((rows, n), F32),
        scratch_shapes=[pltpu.VMEM((rows, 1), F32)],
        compiler_params=pltpu.CompilerParams(dimension_semantics=("arbitrary",)),
    )(x, tri)


def _expand_rows(x, rep):
    h, n = x.shape
    return jnp.broadcast_to(x[:, None, :], (h, rep, n)).reshape(h * rep, n)


def _dec_online(s, v16, m_ref, l_ref, acc_ref):
    m_prev = m_ref[...]
    m_new = jnp.maximum(m_prev, jnp.max(s, axis=-1, keepdims=True))
    alpha = jnp.exp2(m_prev - m_new)
    p = jnp.exp2(s - m_new)
    l_ref[...] = alpha * l_ref[...] + jnp.sum(p, axis=-1, keepdims=True)
    acc_ref[...] = alpha * acc_ref[...] + _dot(p.astype(BF16), v16)
    m_ref[...] = m_new


def _dec_init(m_ref, l_ref, acc_ref):
    m_ref[...] = jnp.full(m_ref.shape, NEG, F32)
    l_ref[...] = jnp.zeros(l_ref.shape, F32)
    acc_ref[...] = jnp.zeros(acc_ref.shape, F32)


def _diag_blocks(o, n_heads, ds, width):
    return jnp.concatenate([o[h * ds:(h + 1) * ds, h * width:(h + 1) * width] for h in range(n_heads)], axis=1)


def _diff_dec_body(par_ref, q_ref, kc_ref, vc_ref, kn_ref, vn_ref, g_ref, _, o_ref, m_ref, l_ref, acc_ref,
                   *, past, kb, ds, n_heads):
    jb = pl.program_id(1)
    hr = 2 * ds
    rows = n_heads * hr
    dv = 2 * DH_B
    r = lax.broadcasted_iota(jnp.int32, (rows, 1), 0)
    head = r // hr
    slope = LOG2E * jnp.exp2(-8.0 * (head + 1).astype(F32) / n_heads)
    qpos = past + (r % ds)
    q = q_ref[0]

    def key_block(k16, v16, key0):
        col = lax.broadcasted_iota(jnp.int32, (1, k16.shape[0]), 1)
        kpos = key0 + col // n_heads
        s = _dot_nt(q, k16) - slope * jnp.abs(qpos - kpos).astype(F32)
        s = jnp.where(col % n_heads == head, s, NEG)
        _dec_online(s, v16, m_ref, l_ref, acc_ref)

    @pl.when(jb == 0)
    def _():
        _dec_init(m_ref, l_ref, acc_ref)
        key_block(kn_ref[0], vn_ref[0], past)

    key_block(kc_ref[0, 0].astype(BF16), vc_ref[0, 0].astype(BF16), jb * kb)

    @pl.when(jb == pl.num_programs(1) - 1)
    def _():
        o = acc_ref[...] / l_ref[...]
        outs = []
        for h in range(n_heads):
            oh = o[h * hr:h * hr + ds] - par_ref[0] * o[h * hr + ds:(h + 1) * hr]
            ms = jnp.mean(oh * oh, axis=-1, keepdims=True)
            outs.append(oh * lax.rsqrt(ms + EPS) * g_ref[:, h * dv:(h + 1) * dv])
        o_ref[0] = jnp.concatenate(outs, axis=1).astype(o_ref.dtype)


def _fox_dec_body(q_ref, kc_ref, vc_ref, kn_ref, vn_ref, fc_ref, fn_ref, _, o_ref, m_ref, l_ref, acc_ref,
                  *, ds, n_heads):
    jb = pl.program_id(1)
    rows = n_heads * ds
    q = q_ref[0]
    fnew = fn_ref[0][:, :ds]
    fref = _expand_rows(fn_ref[0][:, 0:1], ds)

    @pl.when(jb == 0)
    def _():
        _dec_init(m_ref, l_ref, acc_ref)
        r = lax.broadcasted_iota(jnp.int32, (rows, 1), 0)
        kidx = lax.broadcasted_iota(jnp.int32, (1, ds), 1)
        s = _dot_nt(q, kn_ref[0]) + LOG2E * (fref - _expand_rows(fnew, ds))
        s = jnp.where(kidx <= (r % ds), s, NEG)
        _dec_online(s, vn_ref[0].astype(BF16), m_ref, l_ref, acc_ref)

    s = _dot_nt(q, kc_ref[0, 0].astype(BF16)) + LOG2E * (fref - _expand_rows(fc_ref[0], ds))
    _dec_online(s, vc_ref[0, 0].astype(BF16), m_ref, l_ref, acc_ref)

    @pl.when(jb == pl.num_programs(1) - 1)
    def _():
        o = acc_ref[...] / l_ref[...]
        o_ref[0] = _diag_blocks(o, n_heads, ds, DH_C).astype(o_ref.dtype)


def _mla_dec_body(qn_ref, qp_ref, cc_ref, pc_ref, cn_ref, pn_ref, wk_ref, wv_ref, ones_ref, _,
                  o_ref, m_ref, l_ref, acc_ref, *, ds, n_heads):
    jb = pl.program_id(1)
    qn = qn_ref[0]
    qp = qp_ref[0]
    ones_h = ones_ref[...]

    def key_block(ckv, kpe):
        c16 = ckv.astype(BF16)
        kn = _dot(c16, wk_ref[...])
        v = _dot(c16, wv_ref[...])
        n = kpe.shape[0]
        ss = _dot_nt(ones_h, (kn * kn).astype(BF16)) + _dot_nt(jnp.ones((n_heads, ROPE_D), BF16),
                                                               (kpe * kpe).astype(BF16))
        rinv = lax.rsqrt(ss * (1.0 / (NOPE_D + ROPE_D)) + EPS)
        s = _dot_nt(qn, kn.astype(BF16)) + _dot_nt(qp, kpe.astype(BF16))
        s = s * _expand_rows(rinv, ds)
        _dec_online(s, v.astype(BF16), m_ref, l_ref, acc_ref)

    @pl.when(jb == 0)
    def _():
        _dec_init(m_ref, l_ref, acc_ref)
        key_block(cn_ref[0], pn_ref[0])

    key_block(cc_ref[0, 0], pc_ref[0, 0])

    @pl.when(jb == pl.num_programs(1) - 1)
    def _():
        o = acc_ref[...] / l_ref[...]
        o_ref[0] = _diag_blocks(o, n_heads, ds, V_D).astype(o_ref.dtype)


def _per_seq(a):
    return (a, pl.BlockSpec((1,) + a.shape[1:], lambda b, j: (b, 0, 0)))


def _dec_const(a):
    return (a, _const_spec(a.shape))


def _dec_call(body, ins, prev, row0, ds, rows, acc_w, nb, n_kb, smem=None):
    in_specs = [spec for _, spec in ins] + [pl.BlockSpec(memory_space=pl.ANY)]
    args = [a for a, _ in ins] + [prev[None]]
    if smem is not None:
        in_specs = [pl.BlockSpec(memory_space=pltpu.SMEM)] + in_specs
        args = [smem] + args
    blk0 = row0 // ds
    return pl.pallas_call(
        body,
        grid=(nb, n_kb),
        in_specs=in_specs,
        out_specs=pl.BlockSpec((1, ds, prev.shape[1]), lambda b, j: (0, blk0 + b, 0)),
        out_shape=jax.ShapeDtypeStruct((1,) + prev.shape, prev.dtype),
        input_output_aliases={len(args) - 1: 0},
        scratch_shapes=[pltpu.VMEM((rows, 1), F32), pltpu.VMEM((rows, 1), F32), pltpu.VMEM((rows, acc_w), F32)],
        compiler_params=pltpu.CompilerParams(
            dimension_semantics=("parallel", "arbitrary"), vmem_limit_bytes=VMEM_LIMIT),
    )(*args)[0]


def kernel(x_prompt, x_sample, state_s5_re, state_s5_im, cache_diff_k, cache_diff_v, cache_fox_k, cache_fox_v, cache_fox_logf, cache_mla_ckv, cache_mla_kpe, meta_tokens, ffn_norm, ffn_w_in, ffn_w_out, mix_norm, ab_w_in, ab_w_out, s5_a_re, s5_a_im, s5_log_step, s5_b_re, s5_b_im, s5_c_re, s5_c_im, s5_d, s5_glu_w, s5_glu_b, diff_q_norm, diff_k_norm, diff_lam, diff_sub_norm, cd_w_in, cd_w_out, fox_q_norm, fox_k_norm, fox_f_bias, mla_q_a_norm, mla_q_b, mla_kv_a_norm, mla_kv_b, mla_q_norm, mla_k_norm):
    bsz, seq, dm = x_prompt.shape
    nb, ds, _ = x_sample.shape
    n_meta = meta_tokens.shape[0]
    past = cache_diff_k.shape[2]
    front = ROW_ALIGN - n_meta
    lp = front + n_meta + seq
    ltot = n_meta + seq
    assert n_meta + front == ROW_ALIGN and lp % ATT_BLOCK == 0 and front % CHUNK == CHUNK - n_meta
    assert ds == S5_STEP and past % CHUNK == 0 and ds <= CHUNK
    kb = min(DEC_KB, past)
    assert past % kb == 0
    n_kb = past // kb
    assert ffn_norm.shape[0] == 2 and ab_w_in.shape[0] == 1 and cd_w_in.shape[0] == 1

    h_b = cache_diff_k.shape[3]
    h_c = cache_fox_k.shape[3]
    h_d = mla_q_b.shape[2] // (NOPE_D + ROPE_D)
    s5w = s5_glu_w.shape[1]
    n_grp = s5w // S5_GROUP
    qkw = h_b * 2 * DH_B
    fox_w = h_c * DH_C
    q_lora = mla_q_a_norm.shape[1]
    kv_lora = mla_kv_a_norm.shape[1]
    d_qk = NOPE_D + ROPE_D

    n_p = bsz * lp
    head_rows_x = jnp.concatenate([jnp.zeros((front, dm), F32), meta_tokens.astype(F32)], axis=0)
    pieces = []
    for b in range(bsz):
        pieces += [head_rows_x, x_prompt[b]]
    x = jnp.concatenate(pieces + [x_sample.reshape(nb * ds, dm)], axis=0)
    rows = x.shape[0]
    tm = _row_tile(rows)

    ones64 = _block_diag_ones(64)
    ones128 = _block_diag_ones(128)

    x = _ffn(x, ffn_norm[0, 0], ffn_w_in[0, 0], ffn_w_out[0, 0], tm)

    gq = (jnp.tile(diff_q_norm[0], 2 * h_b) * (DH_B ** -0.5 * LOG2E)).reshape(1, qkw)
    gk = jnp.tile(diff_k_norm[0], 2 * h_b).reshape(1, qkw)
    s5_cols = S5_STEP * S5_GROUP
    q16, k16, u_t, vt_diff, k32h, v32h = _row_call(
        functools.partial(_ab_in_body, widths=(s5w, qkw)),
        [x], [mix_norm[0].reshape(1, dm), ab_w_in[0].astype(BF16), gq, gk, ones64],
        [qkw, qkw], [BF16, BF16], tm,
        group_outs=[jax.ShapeDtypeStruct((n_grp, rows // S5_STEP, s5_cols), BF16)],
        scratch=[pltpu.VMEM((s5w // 128, tm, 128), F32)],
        col_outs=[(h_b * DIFF_VT_ROWS, BF16)], head_outs=[(h_b, F32), (h_b, F32)])
    k32 = k32h.reshape(rows, h_b, 2 * DH_B)
    v32 = v32h.reshape(rows, h_b, 2 * DH_B)

    mats = _s5_matrices(s5_a_re[0], s5_a_im[0], s5_log_step[0], s5_b_re[0], s5_b_im[0],
                        s5_c_re[0], s5_c_im[0], s5_d[0])
    n_ch = lp // S5_STEP
    y_t, st_p = _s5_scan(u_t, jnp.zeros((n_grp, bsz, 4 * P_A), F32), mats, n_ch, bsz, 0)
    h_re = jnp.transpose(state_s5_re[0].astype(F32), (1, 0, 2))
    h_im = jnp.transpose(state_s5_im[0].astype(F32), (1, 0, 2))
    y_t, st_s = _s5_scan(u_t, jnp.concatenate([h_re, h_im, h_im, h_re], axis=-1), mats, 1, nb,
                         n_p // S5_STEP, y_prev=y_t)

    lv = diff_lam[0].astype(F32)
    lam_init = 0.8 - 0.6 * math.exp(-0.3 * 0)
    lam = jnp.exp(jnp.sum(lv[0] * lv[1])) - jnp.exp(jnp.sum(lv[2] * lv[3])) + lam_init
    slopes = jnp.exp2(-8.0 * jnp.arange(1, h_b + 1, dtype=F32) / h_b)
    par = jnp.concatenate([lam[None], slopes]).astype(F32)
    subg = (jnp.tile(diff_sub_norm[0], h_b) * (1.0 - lam_init)).reshape(1, qkw)
    kpad = jnp.arange(lp) < front
    kb_diff = jnp.where(kpad[None, :], NEG, LOG2E * slopes[:, None] * jnp.arange(lp, dtype=F32)[None, :])
    kb_diff = jnp.broadcast_to(kb_diff[None, :, None, :], (1, h_b, 2, lp)).reshape(
        1, 2 * h_b // ATT_STREAMS, ATT_STREAMS, lp)
    o_all = _prompt_attn("diff", q16, k16, vt_diff, kb_diff, bsz, lp, par=par, gain=subg)
    qs = q16[n_p:].reshape(nb, ds, h_b, 2, DH_B)
    eye_2 = jnp.eye(2, dtype=BF16)
    qbd = jnp.einsum('bqhmd,mM->bhmqMd', qs, eye_2).reshape(nb, h_b * 2 * ds, 2 * DH_B)
    cache_spec = lambda w: pl.BlockSpec((1, 1, kb, w), lambda b, j: (0, b, j, 0))
    head_rows = lambda a: a.reshape(a.shape[0], nb, past * h_b, 2 * DH_B)
    head_cache_spec = pl.BlockSpec((1, 1, kb * h_b, 2 * DH_B), lambda b, j: (0, b, j, 0))
    o_all = _dec_call(
        functools.partial(_diff_dec_body, past=past, kb=kb, ds=ds, n_heads=h_b),
        [_per_seq(qbd), (head_rows(cache_diff_k), head_cache_spec), (head_rows(cache_diff_v), head_cache_spec),
         _per_seq(k16[n_p:].reshape(nb, ds * h_b, 2 * DH_B)), _per_seq(v32h[n_p * h_b:].astype(BF16).reshape(nb, ds * h_b, 2 * DH_B)),
         _dec_const(subg)],
        o_all, n_p, ds, 2 * h_b * ds, 2 * DH_B, nb, n_kb, smem=par)

    x = _row_call(
        functools.partial(_ab_out_body, s5w=s5w),
        [x, o_all],
        [s5_glu_w[0].astype(BF16), s5_glu_b[0].reshape(1, s5w), ab_w_out[0].astype(BF16)],
        [dm], [F32], tm, group_ins=[y_t], scratch=[pltpu.VMEM((s5w // 128, tm, 128), F32)])[0]

    x = _ffn(x, ffn_norm[0, 1], ffn_w_in[0, 1], ffn_w_out[0, 1], tm)

    x = _ffn(x, ffn_norm[1, 0], ffn_w_in[1, 0], ffn_w_out[1, 0], tm)

    half = ROPE_D // 2
    inv = ROPE_THETA ** (-jnp.arange(half, dtype=F32) / half)
    pad_r = HEAD_PAD - NOPE_D - ROPE_D

    def rope_tables(pos):
        ang = pos.astype(F32)[:, None] * inv[None, :]
        n = pos.shape[0]
        cos = jnp.concatenate([jnp.ones((n, NOPE_D), F32), jnp.cos(ang), jnp.cos(ang), jnp.zeros((n, pad_r), F32)], 1)
        sin = jnp.concatenate([jnp.zeros((n, NOPE_D), F32), jnp.sin(ang), jnp.sin(ang), jnp.zeros((n, pad_r), F32)], 1)
        return lax.optimization_barrier((cos, sin))

    cos_p, sin_p = rope_tables(jnp.arange(lp, dtype=jnp.int32) - front)
    cos_s, sin_s = rope_tables(past + jnp.arange(ds, dtype=jnp.int32))
    cos_t = jnp.concatenate([jnp.tile(cos_p, (bsz, 1)), jnp.tile(cos_s, (nb, 1))], axis=0)
    sin_t = jnp.concatenate([jnp.tile(sin_p, (bsz, 1)), jnp.tile(sin_s, (nb, 1))], axis=0)

    wcd = cd_w_in[0]
    c_fg = 3 * fox_w
    c_qa = c_fg + h_c
    c_kva = c_qa + q_lora
    c_pe = c_kva + kv_lora
    w_pe = wcd[:, c_pe:c_pe + ROPE_D]
    zc = lambda n: jnp.zeros((dm, n), F32)
    w_cd = jnp.concatenate([
        wcd[:, :3 * fox_w], wcd[:, c_qa:c_qa + q_lora], wcd[:, c_kva:c_kva + kv_lora],
        zc(NOPE_D), w_pe, zc(pad_r),
        zc(NOPE_D), -w_pe[:, half:], w_pe[:, :half], zc(pad_r),
        wcd[:, c_fg:c_fg + h_c], zc(HEAD_PAD - h_c)], axis=1).astype(BF16)
    qb = mla_q_b[0].reshape(q_lora, h_d, d_qk)
    zq = lambda n: jnp.zeros((q_lora, h_d, n), F32)
    qb_pad = jnp.concatenate([qb, zq(pad_r)], axis=-1).reshape(q_lora, h_d * HEAD_PAD)
    qb_rot = jnp.concatenate([zq(NOPE_D), -qb[..., NOPE_D + half:], qb[..., NOPE_D:NOPE_D + half], zq(pad_r)],
                             axis=-1).reshape(q_lora, h_d * HEAD_PAD)
    wq2 = jnp.concatenate([qb_pad, qb_rot], axis=1).astype(BF16)
    kvb = mla_kv_b[0].reshape(kv_lora, h_d, NOPE_D + V_D)
    wk_pad = jnp.concatenate([kvb[..., :NOPE_D], jnp.zeros((kv_lora, h_d, HEAD_PAD - NOPE_D), F32)],
                             axis=-1).reshape(kv_lora, h_d * HEAD_PAD).astype(BF16)
    wk_cmp = kvb[..., :NOPE_D].reshape(kv_lora, h_d * NOPE_D).astype(BF16)
    wv_cmp = kvb[..., NOPE_D:].reshape(kv_lora, h_d * V_D).astype(BF16)
    gfq = (jnp.tile(fox_q_norm[0], h_c) * (DH_C ** -0.5 * LOG2E)).reshape(1, fox_w)
    gfk = jnp.tile(fox_k_norm[0], h_c).reshape(1, fox_w)
    fbias = jnp.concatenate([fox_f_bias[0], jnp.zeros((HEAD_PAD - h_c,), F32)]).reshape(1, HEAD_PAD)
    gmq = jnp.tile(jnp.concatenate([mla_q_norm[0] * mla_k_norm[0] * (d_qk ** -0.5 * LOG2E), jnp.zeros((pad_r,), F32)]),
                   h_d).reshape(1, h_d * HEAD_PAD)

    (fq16, fk32, fk16, fv32, qm16, ckv32, km16, fvt, vmt, logf_t, kpe_t) = _row_call(
        functools.partial(_cd_in_body, parts=IN_PARTS, fox_w=fox_w, q_lora=q_lora, kv_lora=kv_lora, n_heads=h_d),
        [x, cos_t, sin_t],
        [mix_norm[1].reshape(1, dm), w_cd, gfq, gfk, fbias, mla_q_a_norm[0].reshape(1, q_lora), wq2,
         mla_kv_a_norm[0].reshape(1, kv_lora), wk_pad, wv_cmp.T, gmq, ones64, ones128],
        [fox_w, fox_w, fox_w, fox_w, h_d * HEAD_PAD, kv_lora, h_d * HEAD_PAD],
        [BF16, F32, BF16, F32, BF16, F32, BF16], tm,
        col_outs=[(2 * fox_w, BF16), (2 * h_d * V_D, BF16), (h_c, F32), (ROPE_D, F32)])
    logf_ps = jnp.transpose(logf_t[:, :n_p].reshape(h_c, bsz, lp), (1, 0, 2))
    logf_ss = jnp.transpose(logf_t[:, n_p:].reshape(h_c, nb, ds), (1, 0, 2))
    kpe_s = jnp.transpose(kpe_t[:, n_p:]).reshape(nb, ds, ROPE_D)

    f_p = _cumsum_lanes(logf_ps.reshape(bsz * h_c, lp)).reshape(bsz, h_c, lp)
    logf_s = jnp.concatenate([
        jnp.transpose(cache_fox_logf[0].astype(F32), (0, 2, 1)),
        logf_ss,
        jnp.zeros((nb, h_c, 256 - ds), F32)], axis=2).reshape(nb * h_c, past + 256)
    f_s = _cumsum_lanes(logf_s).reshape(nb, h_c, past + 256)

    kb_fox = jnp.where(kpad[None, None, :], NEG, -LOG2E * f_p).reshape(bsz, h_c // ATT_STREAMS, ATT_STREAMS, lp)
    oc_all = _prompt_attn("fox", fq16, fk16, fvt, kb_fox, bsz, lp)
    kb_mla = jnp.broadcast_to(jnp.where(kpad, NEG, 0.0).astype(F32)[None, None, None, :], (1, 1, ATT_STREAMS, lp))
    od_all = _prompt_attn("mla", qm16, km16, vmt, kb_mla, bsz, lp)

    eye_c = jnp.eye(h_c, dtype=BF16)
    fqs = fq16[n_p:].reshape(nb, ds, h_c, DH_C)
    fq_bd = jnp.einsum('bqhd,hH->bhqHd', fqs, eye_c).reshape(nb, h_c * ds, fox_w)
    fkc = cache_fox_k.reshape(cache_fox_k.shape[0], nb, past, fox_w)
    fvc = cache_fox_v.reshape(cache_fox_v.shape[0], nb, past, fox_w)
    oc_all = _dec_call(
        functools.partial(_fox_dec_body, ds=ds, n_heads=h_c),
        [_per_seq(fq_bd), (fkc, cache_spec(fox_w)), (fvc, cache_spec(fox_w)),
         _per_seq(fk16[n_p:].reshape(nb, ds, fox_w)), _per_seq(fv32[n_p:].reshape(nb, ds, fox_w)),
         (f_s, pl.BlockSpec((1, h_c, kb), lambda b, j: (b, 0, j))),
         (f_s, pl.BlockSpec((1, h_c, 128), lambda b, j: (b, 0, past // 128)))],
        oc_all, n_p, ds, h_c * ds, fox_w, nb, n_kb)

    eye_d = jnp.eye(h_d, dtype=BF16)
    qms = qm16[n_p:].reshape(nb, ds, h_d, HEAD_PAD)
    qn_bd = jnp.einsum('bqhd,hH->bhqHd', qms[..., :NOPE_D], eye_d).reshape(nb, h_d * ds, h_d * NOPE_D)
    qp_s = jnp.transpose(qms[..., NOPE_D:NOPE_D + ROPE_D], (0, 2, 1, 3)).reshape(nb, h_d * ds, ROPE_D)
    ones_h = jnp.repeat(jnp.eye(h_d, dtype=BF16), NOPE_D, axis=1)
    od_all = _dec_call(
        functools.partial(_mla_dec_body, ds=ds, n_heads=h_d),
        [_per_seq(qn_bd), _per_seq(qp_s),
         (cache_mla_ckv, pl.BlockSpec((1, 1, kb, kv_lora), lambda b, j: (0, b, j, 0))),
         (cache_mla_kpe, pl.BlockSpec((1, 1, kb, ROPE_D), lambda b, j: (0, b, j, 0))),
         _per_seq(ckv32[n_p:].reshape(nb, ds, kv_lora)), _per_seq(kpe_s),
         _dec_const(wk_cmp), _dec_const(wv_cmp), _dec_const(ones_h)],
        od_all, n_p, ds, h_d * ds, h_d * V_D, nb, n_kb)

    x = _ffn(x, ffn_norm[1, 1], ffn_w_in[1, 1], ffn_w_out[1, 1], tm, mix=(oc_all, od_all, cd_w_out[0]))

    def p_rows(a, shape):
        return a[:n_p].reshape((bsz, lp) + a.shape[1:])[:, front:front + ltot].reshape((1, bsz, ltot) + shape)

    def s_rows(a, shape):
        return a[n_p:].reshape((1, nb, ds) + shape)

    def p_cols(at):
        w = at.shape[0]
        return jnp.transpose(at[:, :n_p].reshape(w, bsz, lp)[:, :, front:front + ltot], (1, 2, 0))[None]

    def s_cols(at):
        return jnp.transpose(at[:, n_p:]).reshape(1, nb, ds, at.shape[0])

    def s5_state(st):
        st = jnp.transpose(st, (1, 0, 2))
        return st[None, :, :, :P_A], st[None, :, :, P_A:]

    y_prompt = jnp.stack([x[b * lp + front + n_meta:(b + 1) * lp] for b in range(bsz)])
    y_sample = x[n_p:].reshape(nb, ds, dm)
    s5_re_p, s5_im_p = s5_state(st_p)
    s5_re_s, s5_im_s = s5_state(st_s)
    return (y_prompt, y_sample,
            s5_re_p, s5_im_p, p_rows(k32, (h_b, 2 * DH_B)), p_rows(v32, (h_b, 2 * DH_B)),
            p_rows(fk32, (h_c, DH_C)), p_rows(fv32, (h_c, DH_C)), p_cols(logf_t),
            p_rows(ckv32, (kv_lora,)), p_cols(kpe_t),
            s5_re_s, s5_im_s, s_rows(k32, (h_b, 2 * DH_B)), s_rows(v32, (h_b, 2 * DH_B)),
            s_rows(fk32, (h_c, DH_C)), s_rows(fv32, (h_c, DH_C)), s_cols(logf_t),
            s_rows(ckv32, (kv_lora,)), s_cols(kpe_t))
```

```python
import functools
import math

import jax
import jax.numpy as jnp
from jax import lax
from jax.experimental import pallas as pl
from jax.experimental.pallas import tpu as pltpu

F32 = jnp.float32
BF16 = jnp.bfloat16

EPS = 1e-6
CHUNK = 64
ROW_ALIGN = 256
S5_GROUP = 16
S5_STEP = 16
S5_GROUPS_PER_STEP = 4
P_A = 64
DH_B = 64
DH_C = 64
NOPE_D = 64
ROPE_D = 32
V_D = 64
HEAD_PAD = 128
ROPE_THETA = 10000.0
NEG = -1e30
LOG2E = math.log2(math.e)
VMEM_LIMIT = 56 * 1024 * 1024
IN_PARTS = 2
OUT_PARTS = 4
ATT_BLOCK = 256
ATT_STREAMS = 4
ATT_ROWS = 64
DIFF_VT_ROWS = 2 * DH_B + 16
DEC_KB = 4096


def _dot(a, b):
    return jnp.dot(a, b, preferred_element_type=F32)


def _dot_nt(a, b):
    return lax.dot_general(a, b, (((1,), (1,)), ((), ())), preferred_element_type=F32)


def _rms_rows(x, g):
    ms = jnp.mean(x * x, axis=-1, keepdims=True)
    return x * lax.rsqrt(ms + EPS) * g


def _group_sumsq(x, ones_bd):
    w = x.shape[-1]
    parts = [_dot((x[:, c:c + 256] * x[:, c:c + 256]).astype(BF16), ones_bd) for c in range(0, w, 256)]
    return parts[0] if len(parts) == 1 else jnp.concatenate(parts, axis=1)


def _block_diag_ones(group, n=256):
    r = jnp.arange(n) // group
    return (r[:, None] == r[None, :]).astype(BF16)


def _const_spec(shape):
    nd = len(shape)
    return pl.BlockSpec(shape, lambda *_: (0,) * nd, pipeline_mode=pl.Buffered(1))


def _row_tile(rows, cap=512):
    t = cap
    while rows % t:
        t //= 2
    return t


def _group_spec(a, tm):
    return pl.BlockSpec((a.shape[0], tm // S5_STEP, a.shape[2]), lambda i: (0, i, 0))


def _row_call(body, row_ins, consts, out_widths, out_dtypes, tm, group_ins=(), group_outs=(), scratch=(),
              col_outs=(), head_outs=()):
    rows = row_ins[0].shape[0]
    in_specs = [pl.BlockSpec((tm, a.shape[1]), lambda i: (i, 0)) for a in row_ins]
    in_specs += [_group_spec(a, tm) for a in group_ins]
    in_specs += [_const_spec(c.shape) for c in consts]
    out_specs = [pl.BlockSpec((tm, w), lambda i: (i, 0)) for w in out_widths]
    out_specs += [_group_spec(a, tm) for a in group_outs]
    out_specs += [pl.BlockSpec((w, tm), lambda i: (0, i)) for w, _ in col_outs]
    out_shape = [jax.ShapeDtypeStruct((rows, w), d) for w, d in zip(out_widths, out_dtypes)]
    out_shape += list(group_outs)
    out_shape += [jax.ShapeDtypeStruct((w, rows), d) for w, d in col_outs]
    out_specs += [pl.BlockSpec((tm * h, 128), lambda i: (i, 0)) for h, _ in head_outs]
    out_shape += [jax.ShapeDtypeStruct((rows * h, 128), d) for h, d in head_outs]
    return pl.pallas_call(
        body,
        grid=(rows // tm,),
        in_specs=in_specs,
        out_specs=out_specs,
        out_shape=out_shape,
        scratch_shapes=list(scratch),
        compiler_params=pltpu.CompilerParams(
            dimension_semantics=("parallel",), vmem_limit_bytes=VMEM_LIMIT),
    )(*row_ins, *group_ins, *consts)


def _ffn_body(*refs, d_ff, tf, mix_split):
    if mix_split:
        x_ref, a_ref, b_ref, g_ref, win_ref, wout_ref, wmix_ref, o_ref = refs
        x = (x_ref[...] + _dot(a_ref[...], wmix_ref[:mix_split, :]) + _dot(b_ref[...], wmix_ref[mix_split:, :]))
    else:
        x_ref, g_ref, win_ref, wout_ref, o_ref = refs
        x = x_ref[...]
    o_ref[...] = _half_ffn(x, g_ref, win_ref, wout_ref, d_ff, tf)


def _half_ffn(x, g_ref, win_ref, wout_ref, d_ff, tf):
    xn = _rms_rows(x, g_ref[...]).astype(BF16)
    acc = jnp.zeros(x.shape, F32)
    for c in range(0, d_ff, tf):
        gate = _dot(xn, win_ref[:, c:c + tf])
        up = _dot(xn, win_ref[:, d_ff + c:d_ff + c + tf])
        a = (gate * jax.nn.sigmoid(gate) * up).astype(BF16)
        acc = acc + _dot(a, wout_ref[c:c + tf, :])
    return x + 0.5 * acc


def _ffn_first_body(xp0_ref, xp1_ref, head_ref, xs0_ref, xs1_ref, g_ref, win_ref, wout_ref, o_ref,
                    *, d_ff, tf, tiles_per_seq, n_prompt):
    parts = []
    for a, (xp_ref, xs_ref) in enumerate(((xp0_ref, xs0_ref), (xp1_ref, xs1_ref))):
        s = 2 * pl.program_id(0) + a
        is_sample = s >= n_prompt
        is_head = jnp.logical_and(s % tiles_per_seq == 0, jnp.logical_not(is_sample))
        parts.append(jnp.where(is_sample, xs_ref[...], jnp.where(is_head, head_ref[...], xp_ref[...])))
    o_ref[...] = _half_ffn(jnp.concatenate(parts, axis=0), g_ref, win_ref, wout_ref, d_ff, tf)


def _ffn_first(x_prompt, head_rows, x_sample, g, w_in, w_out):
    bsz, seq, dm = x_prompt.shape
    tm = head_rows.shape[0]
    n_sample = x_sample.shape[0] * x_sample.shape[1]
    assert seq % tm == 0 and n_sample % tm == 0
    per_seq = seq // tm
    tps = per_seq + 1
    n_prompt = bsz * tps
    rows = n_prompt * tm + n_sample
    d_ff = w_out.shape[0]
    body = functools.partial(_ffn_first_body, d_ff=d_ff, tf=256, tiles_per_seq=tps, n_prompt=n_prompt)
    consts = [g.reshape(1, -1), w_in.astype(BF16), w_out.astype(BF16)]
    assert (rows // tm) % 2 == 0

    def frames(a):
        return pl.BlockSpec((tm, dm), lambda i: (jnp.minimum((2 * i + a) // tps, bsz - 1) * per_seq
                                                 + jnp.maximum((2 * i + a) % tps - 1, 0), 0))

    def samples(a):
        return pl.BlockSpec((tm, dm), lambda i: (jnp.clip(2 * i + a - n_prompt, 0, n_sample // tm - 1), 0))

    xp2 = x_prompt.reshape(bsz * seq, dm)
    xs2 = x_sample.reshape(n_sample, dm)
    return pl.pallas_call(
        body,
        grid=(rows // (2 * tm),),
        in_specs=[frames(0), frames(1), _const_spec(head_rows.shape), samples(0), samples(1)]
                 + [_const_spec(c.shape) for c in consts],
        out_specs=pl.BlockSpec((2 * tm, dm), lambda i: (i, 0)),
        out_shape=jax.ShapeDtypeStruct((rows, dm), F32),
        compiler_params=pltpu.CompilerParams(
            dimension_semantics=("parallel",), vmem_limit_bytes=VMEM_LIMIT),
    )(xp2, xp2, head_rows, xs2, xs2, *consts)


def _ffn(x, g, w_in, w_out, tm, mix=None):
    d_ff = w_out.shape[0]
    consts = [g.reshape(1, -1), w_in.astype(BF16), w_out.astype(BF16)]
    rows_in = [x]
    mix_split = 0
    if mix is not None:
        a, b, w_mix = mix
        rows_in += [a, b]
        consts.append(w_mix.astype(BF16))
        mix_split = a.shape[1]
    body = functools.partial(_ffn_body, d_ff=d_ff, tf=256, mix_split=mix_split)
    return _row_call(body, rows_in, consts, [x.shape[1]], [F32], tm)[0]


def _ab_in_body(x_ref, g_ref, w_ref, gq_ref, gk_ref, ones_ref,
                q_ref, k16_ref, ut_ref, vt_ref, k32_ref, v32_ref, us_ref, *, widths):
    s5w, qkw = widths
    xn = _rms_rows(x_ref[...], g_ref[...]).astype(BF16)
    h = _dot(xn, w_ref[...])
    n_chunk = us_ref.shape[1] // S5_STEP
    per_col = 128 // S5_GROUP
    for v in range(s5w // 128):
        us_ref[v] = h[:, v * 128:(v + 1) * 128]
        steps = [us_ref[v, pl.ds(t, n_chunk, stride=S5_STEP), :] for t in range(S5_STEP)]
        for gl in range(per_col):
            ut_ref[v * per_col + gl] = jnp.concatenate(
                [x[:, gl * S5_GROUP:(gl + 1) * S5_GROUP] for x in steps], axis=1).astype(BF16)
    q = h[:, s5w:s5w + qkw]
    k = h[:, s5w + qkw:s5w + 2 * qkw]
    v = h[:, s5w + 2 * qkw:]
    ones_bd = ones_ref[...]
    qn = q * lax.rsqrt(_group_sumsq(q, ones_bd) * (1.0 / DH_B) + EPS) * gq_ref[...]
    kn = k * lax.rsqrt(_group_sumsq(k, ones_bd) * (1.0 / DH_B) + EPS) * gk_ref[...]
    q_ref[...] = qn.astype(BF16)
    k16_ref[...] = kn.astype(BF16)
    dv = 2 * DH_B
    n_head = qkw // dv
    tm = kn.shape[0]
    for hd in range(n_head):
        k32_ref[pl.ds(hd, tm, stride=n_head), :] = kn[:, hd * dv:(hd + 1) * dv]
        v32_ref[pl.ds(hd, tm, stride=n_head), :] = v[:, hd * dv:(hd + 1) * dv]
    vt = v.T
    ones = jnp.ones((DIFF_VT_ROWS - dv, vt.shape[1]), F32)
    vt_ref[...] = jnp.concatenate(
        [a for h in range(vt.shape[0] // dv) for a in (vt[h * dv:(h + 1) * dv], ones)], axis=0).astype(BF16)


def _ab_out_body(x_ref, o_ref, yt_ref, gluw_ref, glub_ref, wout_ref, out_ref, ys_ref, *, s5w):
    tm = ys_ref.shape[1]
    part = tm // OUT_PARTS
    n_chunk = part // S5_STEP
    per_col = 128 // S5_GROUP
    for h in range(OUT_PARTS):
        c0 = h * n_chunk
        for v in range(s5w // 128):
            for t in range(S5_STEP):
                ys_ref[v, pl.ds(h * part + t, n_chunk, stride=S5_STEP), :] = jnp.concatenate(
                    [yt_ref[v * per_col + gl, c0:c0 + n_chunk, t * S5_GROUP:(t + 1) * S5_GROUP]
                     for gl in range(per_col)], axis=1)
        rows = slice(h * part, (h + 1) * part)
        y = jnp.concatenate([ys_ref[v, rows] for v in range(s5w // 128)], axis=1)
        g = 0.5 * y * (1.0 + jnp.tanh(math.sqrt(2.0 / math.pi) * (y + 0.044715 * (y * y * y))))
        z = _dot(g.astype(BF16), gluw_ref[...]) + glub_ref[...]
        s5o = g * jax.nn.sigmoid(z)
        m = _dot(s5o.astype(BF16), wout_ref[:s5w, :]) + _dot(o_ref[rows], wout_ref[s5w:, :])
        out_ref[rows] = x_ref[rows] + m


def _heads_with_ones_t(vt):
    ones = jnp.ones((64, vt.shape[1]), vt.dtype)
    outs = []
    for h in range(vt.shape[0] // 64):
        outs += [vt[h * 64:(h + 1) * 64], ones]
    return jnp.concatenate(outs, axis=0)


def _cd_in_body(*refs, parts, **dims):
    n = refs[0].shape[0] // parts
    for h in range(parts):
        rows = pl.ds(h * n, n)
        _cd_in_part(*[r.at[rows] for r in refs[:3]], *refs[3:16],
                    *[r.at[rows] for r in refs[16:23]], *[r.at[:, rows] for r in refs[23:27]], **dims)


def _cd_in_part(x_ref, cos_ref, sin_ref, g_ref, w_ref, gfq_ref, gfk_ref, fb_ref, gqa_ref, wq2_ref,
                gkva_ref, wk_ref, wv_ref, gmq_ref, ones64_ref, ones128_ref,
                fq_ref, fk32_ref, fk16_ref, fv32_ref, qm_ref, ckv_ref,
                km_ref, fvt_ref, vmt_ref, logft_ref, kpet_ref, *, fox_w, q_lora, kv_lora, n_heads):
    xn = _rms_rows(x_ref[...], g_ref[...]).astype(BF16)
    h = _dot(xn, w_ref[...])
    ones64 = ones64_ref[...]
    ones128 = ones128_ref[...]
    fq = h[:, :fox_w]
    fk = h[:, fox_w:2 * fox_w]
    fv = h[:, 2 * fox_w:3 * fox_w]
    c0 = 3 * fox_w
    qa = h[:, c0:c0 + q_lora]
    kva = h[:, c0 + q_lora:c0 + q_lora + kv_lora]
    c1 = c0 + q_lora + kv_lora
    pe_a = h[:, c1:c1 + HEAD_PAD]
    pe_b = h[:, c1 + HEAD_PAD:c1 + 2 * HEAD_PAD]
    fg = h[:, c1 + 2 * HEAD_PAD:c1 + 3 * HEAD_PAD]

    fqn = fq * lax.rsqrt(_group_sumsq(fq, ones64) * (1.0 / DH_C) + EPS) * gfq_ref[...]
    fkn = fk * lax.rsqrt(_group_sumsq(fk, ones64) * (1.0 / DH_C) + EPS) * gfk_ref[...]
    fq_ref[...] = fqn.astype(BF16)
    fk32_ref[...] = fkn
    fk16_ref[...] = fkn.astype(BF16)
    fv32_ref[...] = fv
    fvt_ref[...] = _heads_with_ones_t(fv.T).astype(BF16)

    z = fg + fb_ref[...]
    logf = jnp.minimum(z, 0.0) - jnp.log1p(jnp.exp(-jnp.abs(z)))
    logft_ref[...] = logf.T[:logft_ref.shape[0]]

    cos = cos_ref[...]
    sin = sin_ref[...]
    qan = _rms_rows(qa, gqa_ref[...]).astype(BF16)
    q2 = _dot(qan, wq2_ref[...])
    hw = n_heads * HEAD_PAD
    cos_t = jnp.concatenate([cos] * n_heads, axis=1)
    sin_t = jnp.concatenate([sin] * n_heads, axis=1)
    qr = q2[:, :hw] * cos_t + q2[:, hw:] * sin_t
    d_qk = NOPE_D + ROPE_D
    qm = qr * lax.rsqrt(_group_sumsq(qr, ones128) * (1.0 / d_qk) + EPS) * gmq_ref[...]
    qm_ref[...] = qm.astype(BF16)

    ckv = _rms_rows(kva, gkva_ref[...])
    ckv_ref[...] = ckv
    pe = pe_a * cos + pe_b * sin
    kpet_ref[...] = pe.T[NOPE_D:NOPE_D + ROPE_D]
    ckv16 = ckv.astype(BF16)
    kraw = _dot(ckv16, wk_ref[...]) + jnp.concatenate([pe] * n_heads, axis=1)
    km = kraw * lax.rsqrt(_group_sumsq(kraw, ones128) * (1.0 / d_qk) + EPS)
    km_ref[...] = km.astype(BF16)
    vmt_ref[...] = _heads_with_ones_t(_dot_nt(wv_ref[...], ckv16)).astype(BF16)


def _s5_body(*refs, n_chunks, bsz, aliased):
    if aliased:
        u_ref, h0_ref, m_ref, bm_ref, cm_ref, coef_ref, _, y_ref, st_ref, s2_ref, hp_ref = refs
    else:
        u_ref, h0_ref, m_ref, bm_ref, cm_ref, coef_ref, y_ref, st_ref, s2_ref, hp_ref = refs
    ng = u_ref.shape[0]
    half = 2 * P_A
    coefs = []
    for g in range(ng):
        s2 = _dot(u_ref[g], bm_ref[g])
        s2_ref[g, 0] = s2[:, :half]
        s2_ref[g, 1] = s2[:, half:]
        coefs.append((coef_ref[g, 0:1, :], coef_ref[g, 1:2, :], coef_ref[g, 2:3, :]))

    def step(j, carry):
        new = []
        for g in range(ng):
            ha, hb = carry[2 * g], carry[2 * g + 1]
            c1, c2, c3 = coefs[g]
            hp_ref[g, pl.ds(j, bsz, stride=n_chunks), :] = ha
            sa = s2_ref[g, 0, pl.ds(j, bsz, stride=n_chunks), :]
            sb = s2_ref[g, 1, pl.ds(j, bsz, stride=n_chunks), :]
            new += [ha * c1 + hb * c2 + sa, hb * c1 + ha * c3 + sb]
        return tuple(new)

    init = []
    for g in range(ng):
        init += [h0_ref[g][:, :half], h0_ref[g][:, half:]]
    final = lax.fori_loop(0, n_chunks, step, tuple(init))
    for g in range(ng):
        st_ref[g] = final[2 * g]
        y_ref[g] = _dot(u_ref[g], m_ref[g]) + _dot(hp_ref[g].astype(BF16), cm_ref[g])


def _s5_scan(u_t, h0, mats, n_chunks, bsz, row0, y_prev=None):
    m_mat, bm, cm, coef = mats
    g, rows_all, w = u_t.shape
    rows = n_chunks * bsz
    assert row0 % rows == 0
    blk = row0 // rows
    aliased = y_prev is not None
    body = functools.partial(_s5_body, n_chunks=n_chunks, bsz=bsz, aliased=aliased)
    ng = S5_GROUPS_PER_STEP
    per_g = lambda a: pl.BlockSpec((ng,) + a.shape[1:], lambda i: (i, 0, 0))
    in_specs = [pl.BlockSpec((ng, rows, w), lambda i: (i, blk, 0)),
                per_g(h0), per_g(m_mat), per_g(bm), per_g(cm), per_g(coef)]
    args = [u_t, h0, m_mat, bm, cm, coef]
    if aliased:
        in_specs.append(pl.BlockSpec(memory_space=pl.ANY))
        args.append(y_prev)
    return pl.pallas_call(
        body,
        grid=(g // ng,),
        in_specs=in_specs,
        out_specs=[pl.BlockSpec((ng, rows, w), lambda i: (i, blk, 0)),
                   pl.BlockSpec((ng, bsz, 2 * P_A), lambda i: (i, 0, 0))],
        out_shape=[jax.ShapeDtypeStruct((g, rows_all, w), F32),
                   jax.ShapeDtypeStruct((g, bsz, 2 * P_A), F32)],
        input_output_aliases={6: 0} if aliased else {},
        scratch_shapes=[pltpu.VMEM((ng, 2, rows, 2 * P_A), F32), pltpu.VMEM((ng, rows, 2 * P_A), F32)],
        compiler_params=pltpu.CompilerParams(
            dimension_semantics=("parallel",), vmem_limit_bytes=VMEM_LIMIT),
    )(*args)


def _s5_matrices(a_re, a_im, log_step, b_re, b_im, c_re, c_im, d):
    g = a_re.shape[0]
    t = S5_STEP
    lam = lax.complex(a_re, a_im)
    dl = lam * jnp.exp(log_step)[:, None]
    lam_bar = jnp.exp(dl)
    b_bar = ((lam_bar - 1.0) / lam)[..., None] * lax.complex(b_re, b_im)
    c = lax.complex(c_re, c_im)
    pw = jnp.exp(dl[:, None, :] * jnp.arange(t + 1, dtype=F32)[None, :, None])
    bmc = pw[:, t - 1::-1][:, :, :, None] * b_bar[:, None]
    bmc = jnp.swapaxes(bmc, 2, 3).reshape(g, t * S5_GROUP, P_A)
    bm = jnp.concatenate([bmc.real, bmc.imag, bmc.imag, bmc.real], axis=-1)
    kk = jnp.einsum('gcp,gkp,gpd->gkcd', c, pw[:, :t], b_bar).real
    kk = kk.at[:, 0].add(d.reshape(g, S5_GROUP)[:, :, None] * jnp.eye(S5_GROUP, dtype=F32))
    lag = jnp.arange(t)[None, :] - jnp.arange(t)[:, None]
    toep = jnp.where((lag >= 0)[None, :, :, None, None], kk[:, jnp.clip(lag, 0, t - 1)], 0.0)
    m_mat = jnp.transpose(toep, (0, 1, 4, 2, 3)).reshape(g, t * S5_GROUP, t * S5_GROUP)
    cp = c[:, None] * pw[:, 1:, None, :]
    cpm = jnp.transpose(cp, (0, 3, 1, 2)).reshape(g, P_A, t * S5_GROUP)
    cm = jnp.concatenate([cpm.real, -cpm.imag], axis=1)
    a_t = pw[:, t]
    ar, ai = a_t.real, a_t.imag
    zeros = jnp.zeros_like(ar)
    coef = jnp.stack([jnp.concatenate([ar, ar], -1), jnp.concatenate([-ai, ai], -1),
                      jnp.concatenate([ai, -ai], -1), jnp.concatenate([zeros, zeros], -1)], axis=1)
    return m_mat.astype(BF16), bm.astype(BF16), cm.astype(BF16), coef.astype(F32)


def _online(logits, vt, e, m_ref, acc_ref, p_ref, block_max=None):
    tk = p_ref.shape[1]
    if block_max is None:
        part = logits(0, ATT_ROWS)
        for r0 in range(ATT_ROWS, tk, ATT_ROWS):
            part = jnp.maximum(part, logits(r0, ATT_ROWS))
    else:
        part = block_max
    m_prev = m_ref[e]
    m_new = jnp.maximum(m_prev, jnp.max(part, axis=0, keepdims=True))
    alpha = jnp.exp2(m_prev - m_new)
    m_ref[e] = m_new
    for r0 in range(0, tk, ATT_ROWS):
        p_ref[e, r0:r0 + ATT_ROWS] = jnp.exp2(logits(r0, ATT_ROWS) - m_new).astype(BF16)
    acc_ref[e] = alpha * acc_ref[e] + _dot(vt, p_ref[e])


def _attn_query_block(i, hg, refs, kind, tq, ns):
    if kind == "diff":
        (par_ref, q_ref, k_ref, vt_ref, kb_ref, g_ref, o_ref,
         m_ref, acc_ref, s_ref, p_ref, kbc_ref, mx_ref) = refs
    else:
        q_ref, k_ref, vt_ref, kb_ref, o_ref, m_ref, acc_ref, s_ref, p_ref, kbc_ref, mx_ref = refs
    vrows = acc_ref.shape[1]
    m_ref[...] = jnp.full(m_ref.shape, NEG, F32)
    acc_ref[...] = jnp.zeros(acc_ref.shape, F32)

    qstart = pl.multiple_of(i * tq, tq)
    q = q_ref[0, pl.ds(qstart, tq), :]
    lane = lax.broadcasted_iota(jnp.int32, (1, 128), 1)
    qs = []
    for e in range(ns):
        if kind == "mla":
            qs.append(q[:, e * HEAD_PAD:(e + 1) * HEAD_PAD])
        else:
            qp = q[:, (e // 2) * 128:(e // 2 + 1) * 128]
            qs.append(jnp.where((lane < 64) if e % 2 == 0 else (lane >= 64), qp, jnp.zeros_like(qp)))
    lane_q = lax.broadcasted_iota(jnp.int32, (1, tq), 1)
    ref = [-kb_ref[0, 0, e:e + 1, pl.ds(qstart + (tq - 128), 128)][:, 127:128] for e in range(ns)]
    if kind == "diff":
        slope = [LOG2E * par_ref[1 + hg * (ns // 2) + p] for p in range(ns // 2)]

    tk = tq

    def k_slot(k, e):
        if kind == "mla":
            return k[:, e * HEAD_PAD:(e + 1) * HEAD_PAD]
        return k[:, (e // 2) * 128:(e // 2 + 1) * 128]

    def vt_slot(vt, e):
        r0 = (e // 2 if kind == "diff" else e) * vrows
        return vt[r0:r0 + vrows]

    def scores(j, slot, first=False):
        k0 = pl.multiple_of(j * tk, tk)
        k = k_ref[0, pl.ds(k0, tk), :]
        for e in range(ns):
            s = _dot_nt(k_slot(k, e), qs[e])
            if first or kind != "mla":
                bias = kbc_ref[e, pl.ds(k0, tk), :] + ref[e]
                s = s + jnp.concatenate([bias] * (tq // 128), axis=1)
            s_ref[slot, e] = s
            part = s[0:8]
            for r0 in range(8, tk, 8):
                part = jnp.maximum(part, s[r0:r0 + 8])
            mx_ref[slot, e] = part

    def softmax_pv(j, slot, diag):
        k0 = pl.multiple_of(j * tk, tk)
        vt = vt_ref[:, pl.ds(k0, tk)]
        for e in range(ns):
            def logits(r0, n, e=e):
                s = s_ref[slot, e, r0:r0 + n, :]
                if diag:
                    row_i = r0 + lax.broadcasted_iota(jnp.int32, (n, 1), 0)
                    if kind == "diff":
                        s = s - (2.0 * slope[e // 2]) * jnp.maximum(row_i - lane_q, 0).astype(F32)
                    if kind == "fox":
                        s = jnp.where(row_i <= lane_q, s, NEG)
                    elif r0 > 0:
                        s = jnp.where(lane_q >= r0, s, NEG)
                return s

            _online(logits, vt_slot(vt, e), e, m_ref, acc_ref, p_ref,
                    block_max=None if diag else mx_ref[slot, e])

    scores(0, 0, first=True)

    def pair_body(jj, c):
        j = 2 * jj
        scores(j + 1, 1)
        softmax_pv(j, 0, False)
        scores(j + 2, 0)
        softmax_pv(j + 1, 1, False)
        return c

    lax.fori_loop(0, i // 2, pair_body, 0)

    @pl.when(i % 2 == 0)
    def _():
        softmax_pv(i, 0, True)

    @pl.when(i % 2 == 1)
    def _():
        scores(i, 1)
        softmax_pv(i - 1, 0, False)
        softmax_pv(i, 1, True)

    outs = []
    for p in range(ns // 2):
        e0, e1 = 2 * p, 2 * p + 1
        if kind == "diff":
            a0, a1 = acc_ref[e0], acc_ref[e1]
            dv = 2 * DH_B
            o = (a0[:dv] / a0[dv:dv + 1] - par_ref[0] * (a1[:dv] / a1[dv:dv + 1])).T
            ms = jnp.mean(o * o, axis=-1, keepdims=True)
            outs.append(o * lax.rsqrt(ms + EPS) * g_ref[:, p * 128:(p + 1) * 128])
        else:
            a0, a1 = acc_ref[e0], acc_ref[e1]
            outs.append(jnp.concatenate([a0[:64] / a0[64:], a1[:64] / a1[64:]], axis=0).T)
    o_ref[0, pl.ds(qstart, tq), :] = (
        outs[0] if len(outs) == 1 else jnp.concatenate(outs, axis=1)).astype(o_ref.dtype)


def _prompt_attn_body(*refs, kind, tq, ns, nq):
    kb_ref, kbc_ref = (refs[4], refs[11]) if kind == "diff" else (refs[3], refs[9])
    def fill_bias_columns():
        def fill(c, carry):
            c0 = pl.multiple_of(c * 128, 128)
            for e in range(ns):
                row = kb_ref[0, 0, e:e + 1, pl.ds(c0, 128)]
                kbc_ref[e, pl.ds(c0, 128), :] = jnp.broadcast_to(row, (128, 128)).T
            return carry

        lax.fori_loop(0, kbc_ref.shape[1] // 128, fill, 0)

    if kind == "fox":
        hg = pl.program_id(1)
        fill_bias_columns()
    else:
        hg = pl.program_id(0)
        pl.when(pl.program_id(1) == 0)(fill_bias_columns)

    def query_block(i, carry):
        _attn_query_block(i, hg, refs, kind, tq, ns)
        return carry

    lax.fori_loop(0, nq, query_block, 0)


def _prompt_attn(kind, q, k, vt, kb, bsz, lp, par=None, gain=None):
    rows = q.shape[0]
    tq = ATT_BLOCK
    ns = ATT_STREAMS
    wq = (ns // 2) * (2 * HEAD_PAD if kind == "mla" else 128)
    vrows = DIFF_VT_ROWS if kind == "diff" else 128
    wv = (ns // 2) * (vrows if kind == "diff" else 2 * vrows)
    wo = (ns // 2) * 128
    n_hg = vt.shape[0] // wv
    nq = lp // tq
    kb_b, kb_h = kb.shape[0] > 1, kb.shape[1] > 1
    seq_major = kind == "fox"

    def bh(f):
        return (lambda b, h: f(b, h)) if seq_major else (lambda h, b: f(b, h))

    in_specs = [pl.BlockSpec((1, lp, wq), bh(lambda b, h: (0, b, h))),
                pl.BlockSpec((1, lp, wq), bh(lambda b, h: (0, b, h))),
                pl.BlockSpec((wv, lp), bh(lambda b, h: (h, b))),
                pl.BlockSpec((1, 1, ns, lp), bh(lambda b, h: (b if kb_b else 0, h if kb_h else 0, 0, 0)))]
    args = [q[None], k[None], vt, kb]
    if kind == "diff":
        in_specs = ([pl.BlockSpec(memory_space=pltpu.SMEM)] + in_specs
                    + [pl.BlockSpec((1, wo), bh(lambda b, h: (0, h)))])
        args = [par] + args + [gain]
    assert ATT_ROWS == CHUNK
    body = functools.partial(_prompt_attn_body, kind=kind, tq=tq, ns=ns, nq=nq)
    return pl.pallas_call(
        body,
        grid=(bsz, n_hg) if seq_major else (n_hg, bsz),
        in_specs=in_specs,
        out_specs=pl.BlockSpec((1, lp, wo), bh(lambda b, h: (0, b, h))),
        out_shape=jax.ShapeDtypeStruct((1, rows, n_hg * wo), BF16),
        scratch_shapes=[pltpu.VMEM((ns, 1, tq), F32),
                        pltpu.VMEM((ns, vrows, tq), F32), pltpu.VMEM((2, ns, tq, tq), F32),
                        pltpu.VMEM((ns, tq, tq), BF16), pltpu.VMEM((ns, lp, 128), F32),
                        pltpu.VMEM((2, ns, 8, tq), F32)],
        compiler_params=pltpu.CompilerParams(
            dimension_semantics=("arbitrary", "arbitrary"), vmem_limit_bytes=VMEM_LIMIT),
    )(*args)[0]


def _cumsum_body(x_ref, tri_ref, o_ref, carry_ref):
    @pl.when(pl.program_id(0) == 0)
    def _():
        carry_ref[...] = jnp.zeros(carry_ref.shape, F32)

    y = jnp.dot(x_ref[...], tri_ref[...], preferred_element_type=F32,
                precision=lax.Precision.HIGHEST) + carry_ref[...]
    o_ref[...] = y
    carry_ref[...] = y[:, -1:]


def _cumsum_lanes(x, blk=256):
    rows, n = x.shape
    tri = (jnp.arange(blk)[:, None] <= jnp.arange(blk)[None, :]).astype(F32)
    return pl.pallas_call(
        _cumsum_body,
        grid=(n // blk,),
        in_specs=[pl.BlockSpec((rows, blk), lambda j: (0, j)), _const_spec((blk, blk))],
        out_specs=pl.BlockSpec((rows, blk), lambda j: (0, j)),
        out_shape=jax.ShapeDtypeStruct((rows, n), F32),
        scratch_shapes=[pltpu.VMEM((rows, 1), F32)],
        compiler_params=pltpu.CompilerParams(dimension_semantics=("arbitrary",)),
    )(x, tri)


def _expand_rows(x, rep):
    h, n = x.shape
    return jnp.broadcast_to(x[:, None, :], (h, rep, n)).reshape(h * rep, n)


def _dec_online(s, v16, m_ref, l_ref, acc_ref):
    m_prev = m_ref[...]
    m_new = jnp.maximum(m_prev, jnp.max(s, axis=-1, keepdims=True))
    alpha = jnp.exp2(m_prev - m_new)
    p = jnp.exp2(s - m_new)
    l_ref[...] = alpha * l_ref[...] + jnp.sum(p, axis=-1, keepdims=True)
    acc_ref[...] = alpha * acc_ref[...] + _dot(p.astype(BF16), v16)
    m_ref[...] = m_new


def _dec_init(m_ref, l_ref, acc_ref):
    m_ref[...] = jnp.full(m_ref.shape, NEG, F32)
    l_ref[...] = jnp.zeros(l_ref.shape, F32)
    acc_ref[...] = jnp.zeros(acc_ref.shape, F32)


def _diag_blocks(o, n_heads, ds, width):
    return jnp.concatenate([o[h * ds:(h + 1) * ds, h * width:(h + 1) * width] for h in range(n_heads)], axis=1)


def _diff_dec_body(par_ref, q_ref, kc_ref, vc_ref, kn_ref, vn_ref, g_ref, _, o_ref, m_ref, l_ref, acc_ref,
                   *, past, kb, ds, n_heads):
    jb = pl.program_id(1)
    hr = 2 * ds
    rows = n_heads * hr
    dv = 2 * DH_B
    r = lax.broadcasted_iota(jnp.int32, (rows, 1), 0)
    head = r // hr
    slope = LOG2E * jnp.exp2(-8.0 * (head + 1).astype(F32) / n_heads)
    qpos = past + (r % ds)
    q = q_ref[0]

    def key_block(k16, v16, key0):
        col = lax.broadcasted_iota(jnp.int32, (1, k16.shape[0]), 1)
        kpos = key0 + col // n_heads
        s = _dot_nt(q, k16) - slope * jnp.abs(qpos - kpos).astype(F32)
        s = jnp.where(col % n_heads == head, s, NEG)
        _dec_online(s, v16, m_ref, l_ref, acc_ref)

    @pl.when(jb == 0)
    def _():
        _dec_init(m_ref, l_ref, acc_ref)
        key_block(kn_ref[0], vn_ref[0], past)

    key_block(kc_ref[0, 0].astype(BF16), vc_ref[0, 0].astype(BF16), jb * kb)

    @pl.when(jb == pl.num_programs(1) - 1)
    def _():
        o = acc_ref[...] / l_ref[...]
        outs = []
        for h in range(n_heads):
            oh = o[h * hr:h * hr + ds] - par_ref[0] * o[h * hr + ds:(h + 1) * hr]
            ms = jnp.mean(oh * oh, axis=-1, keepdims=True)
            outs.append(oh * lax.rsqrt(ms + EPS) * g_ref[:, h * dv:(h + 1) * dv])
        o_ref[0] = jnp.concatenate(outs, axis=1).astype(o_ref.dtype)


def _fox_dec_body(q_ref, kc_ref, vc_ref, kn_ref, vn_ref, fc_ref, fn_ref, _, o_ref, m_ref, l_ref, acc_ref,
                  *, ds, n_heads):
    jb = pl.program_id(1)
    rows = n_heads * ds
    q = q_ref[0]
    fnew = fn_ref[0][:, :ds]
    fref = _expand_rows(fn_ref[0][:, 0:1], ds)

    @pl.when(jb == 0)
    def _():
        _dec_init(m_ref, l_ref, acc_ref)
        r = lax.broadcasted_iota(jnp.int32, (rows, 1), 0)
        kidx = lax.broadcasted_iota(jnp.int32, (1, ds), 1)
        s = _dot_nt(q, kn_ref[0]) + LOG2E * (fref - _expand_rows(fnew, ds))
        s = jnp.where(kidx <= (r % ds), s, NEG)
        _dec_online(s, vn_ref[0].astype(BF16), m_ref, l_ref, acc_ref)

    s = _dot_nt(q, kc_ref[0, 0].astype(BF16)) + LOG2E * (fref - _expand_rows(fc_ref[0], ds))
    _dec_online(s, vc_ref[0, 0].astype(BF16), m_ref, l_ref, acc_ref)

    @pl.when(jb == pl.num_programs(1) - 1)
    def _():
        o = acc_ref[...] / l_ref[...]
        o_ref[0] = _diag_blocks(o, n_heads, ds, DH_C).astype(o_ref.dtype)


def _mla_dec_body(qn_ref, qp_ref, cc_ref, pc_ref, cn_ref, pn_ref, wk_ref, wv_ref, ones_ref, _,
                  o_ref, m_ref, l_ref, acc_ref, *, ds, n_heads):
    jb = pl.program_id(1)
    qn = qn_ref[0]
    qp = qp_ref[0]
    ones_h = ones_ref[...]

    def key_block(ckv, kpe):
        c16 = ckv.astype(BF16)
        kn = _dot(c16, wk_ref[...])
        v = _dot(c16, wv_ref[...])
        n = kpe.shape[0]
        ss = _dot_nt(ones_h, (kn * kn).astype(BF16)) + _dot_nt(jnp.ones((n_heads, ROPE_D), BF16),
                                                               (kpe * kpe).astype(BF16))
        rinv = lax.rsqrt(ss * (1.0 / (NOPE_D + ROPE_D)) + EPS)
        s = _dot_nt(qn, kn.astype(BF16)) + _dot_nt(qp, kpe.astype(BF16))
        s = s * _expand_rows(rinv, ds)
        _dec_online(s, v.astype(BF16), m_ref, l_ref, acc_ref)

    @pl.when(jb == 0)
    def _():
        _dec_init(m_ref, l_ref, acc_ref)
        key_block(cn_ref[0], pn_ref[0])

    key_block(cc_ref[0, 0], pc_ref[0, 0])

    @pl.when(jb == pl.num_programs(1) - 1)
    def _():
        o = acc_ref[...] / l_ref[...]
        o_ref[0] = _diag_blocks(o, n_heads, ds, V_D).astype(o_ref.dtype)


def _per_seq(a):
    return (a, pl.BlockSpec((1,) + a.shape[1:], lambda b, j: (b, 0, 0)))


def _dec_const(a):
    return (a, _const_spec(a.shape))


def _dec_call(body, ins, prev, row0, ds, rows, acc_w, nb, n_kb, smem=None):
    in_specs = [spec for _, spec in ins] + [pl.BlockSpec(memory_space=pl.ANY)]
    args = [a for a, _ in ins] + [prev[None]]
    if smem is not None:
        in_specs = [pl.BlockSpec(memory_space=pltpu.SMEM)] + in_specs
        args = [smem] + args
    blk0 = row0 // ds
    return pl.pallas_call(
        body,
        grid=(nb, n_kb),
        in_specs=in_specs,
        out_specs=pl.BlockSpec((1, ds, prev.shape[1]), lambda b, j: (0, blk0 + b, 0)),
        out_shape=jax.ShapeDtypeStruct((1,) + prev.shape, prev.dtype),
        input_output_aliases={len(args) - 1: 0},
        scratch_shapes=[pltpu.VMEM((rows, 1), F32), pltpu.VMEM((rows, 1), F32), pltpu.VMEM((rows, acc_w), F32)],
        compiler_params=pltpu.CompilerParams(
            dimension_semantics=("parallel", "arbitrary"), vmem_limit_bytes=VMEM_LIMIT),
    )(*args)[0]


def kernel(x_prompt, x_sample, state_s5_re, state_s5_im, cache_diff_k, cache_diff_v, cache_fox_k, cache_fox_v, cache_fox_logf, cache_mla_ckv, cache_mla_kpe, meta_tokens, ffn_norm, ffn_w_in, ffn_w_out, mix_norm, ab_w_in, ab_w_out, s5_a_re, s5_a_im, s5_log_step, s5_b_re, s5_b_im, s5_c_re, s5_c_im, s5_d, s5_glu_w, s5_glu_b, diff_q_norm, diff_k_norm, diff_lam, diff_sub_norm, cd_w_in, cd_w_out, fox_q_norm, fox_k_norm, fox_f_bias, mla_q_a_norm, mla_q_b, mla_kv_a_norm, mla_kv_b, mla_q_norm, mla_k_norm):
    bsz, seq, dm = x_prompt.shape
    nb, ds, _ = x_sample.shape
    n_meta = meta_tokens.shape[0]
    past = cache_diff_k.shape[2]
    front = ROW_ALIGN - n_meta
    lp = front + n_meta + seq
    ltot = n_meta + seq
    assert n_meta + front == ROW_ALIGN and lp % ATT_BLOCK == 0 and front % CHUNK == CHUNK - n_meta
    assert ds == S5_STEP and past % CHUNK == 0 and ds <= CHUNK
    kb = min(DEC_KB, past)
    assert past % kb == 0
    n_kb = past // kb
    assert ffn_norm.shape[0] == 2 and ab_w_in.shape[0] == 1 and cd_w_in.shape[0] == 1

    h_b = cache_diff_k.shape[3]
    h_c = cache_fox_k.shape[3]
    h_d = mla_q_b.shape[2] // (NOPE_D + ROPE_D)
    s5w = s5_glu_w.shape[1]
    n_grp = s5w // S5_GROUP
    qkw = h_b * 2 * DH_B
    fox_w = h_c * DH_C
    q_lora = mla_q_a_norm.shape[1]
    kv_lora = mla_kv_a_norm.shape[1]
    d_qk = NOPE_D + ROPE_D

    n_p = bsz * lp
    head_rows_x = jnp.concatenate([jnp.zeros((front, dm), F32), meta_tokens.astype(F32)], axis=0)
    rows = n_p + nb * ds
    tm = _row_tile(rows)

    ones64 = _block_diag_ones(64)
    ones128 = _block_diag_ones(128)

    x = _ffn_first(x_prompt.astype(F32), head_rows_x, x_sample.astype(F32),
                   ffn_norm[0, 0], ffn_w_in[0, 0], ffn_w_out[0, 0])

    gq = (jnp.tile(diff_q_norm[0], 2 * h_b) * (DH_B ** -0.5 * LOG2E)).reshape(1, qkw)
    gk = jnp.tile(diff_k_norm[0], 2 * h_b).reshape(1, qkw)
    s5_cols = S5_STEP * S5_GROUP
    q16, k16, u_t, vt_diff, k32h, v32h = _row_call(
        functools.partial(_ab_in_body, widths=(s5w, qkw)),
        [x], [mix_norm[0].reshape(1, dm), ab_w_in[0].astype(BF16), gq, gk, ones64],
        [qkw, qkw], [BF16, BF16], tm,
        group_outs=[jax.ShapeDtypeStruct((n_grp, rows // S5_STEP, s5_cols), BF16)],
        scratch=[pltpu.VMEM((s5w // 128, tm, 128), F32)],
        col_outs=[(h_b * DIFF_VT_ROWS, BF16)], head_outs=[(h_b, F32), (h_b, F32)])
    k32 = k32h.reshape(rows, h_b, 2 * DH_B)
    v32 = v32h.reshape(rows, h_b, 2 * DH_B)

    mats = _s5_matrices(s5_a_re[0], s5_a_im[0], s5_log_step[0], s5_b_re[0], s5_b_im[0],
                        s5_c_re[0], s5_c_im[0], s5_d[0])
    n_ch = lp // S5_STEP
    y_t, st_p = _s5_scan(u_t, jnp.zeros((n_grp, bsz, 4 * P_A), F32), mats, n_ch, bsz, 0)
    h_re = jnp.transpose(state_s5_re[0].astype(F32), (1, 0, 2))
    h_im = jnp.transpose(state_s5_im[0].astype(F32), (1, 0, 2))
    y_t, st_s = _s5_scan(u_t, jnp.concatenate([h_re, h_im, h_im, h_re], axis=-1), mats, 1, nb,
                         n_p // S5_STEP, y_prev=y_t)

    lv = diff_lam[0].astype(F32)
    lam_init = 0.8 - 0.6 * math.exp(-0.3 * 0)
    lam = jnp.exp(jnp.sum(lv[0] * lv[1])) - jnp.exp(jnp.sum(lv[2] * lv[3])) + lam_init
    slopes = jnp.exp2(-8.0 * jnp.arange(1, h_b + 1, dtype=F32) / h_b)
    par = jnp.concatenate([lam[None], slopes]).astype(F32)
    subg = (jnp.tile(diff_sub_norm[0], h_b) * (1.0 - lam_init)).reshape(1, qkw)
    kpad = jnp.arange(lp) < front
    kb_diff = jnp.where(kpad[None, :], NEG, LOG2E * slopes[:, None] * jnp.arange(lp, dtype=F32)[None, :])
    kb_diff = jnp.broadcast_to(kb_diff[None, :, None, :], (1, h_b, 2, lp)).reshape(
        1, 2 * h_b // ATT_STREAMS, ATT_STREAMS, lp)
    o_all = _prompt_attn("diff", q16, k16, vt_diff, kb_diff, bsz, lp, par=par, gain=subg)
    qs = q16[n_p:].reshape(nb, ds, h_b, 2, DH_B)
    eye_2 = jnp.eye(2, dtype=BF16)
    qbd = jnp.einsum('bqhmd,mM->bhmqMd', qs, eye_2).reshape(nb, h_b * 2 * ds, 2 * DH_B)
    cache_spec = lambda w: pl.BlockSpec((1, 1, kb, w), lambda b, j: (0, b, j, 0))
    head_rows = lambda a: a.reshape(a.shape[0], nb, past * h_b, 2 * DH_B)
    head_cache_spec = pl.BlockSpec((1, 1, kb * h_b, 2 * DH_B), lambda b, j: (0, b, j, 0))
    o_all = _dec_call(
        functools.partial(_diff_dec_body, past=past, kb=kb, ds=ds, n_heads=h_b),
        [_per_seq(qbd), (head_rows(cache_diff_k), head_cache_spec), (head_rows(cache_diff_v), head_cache_spec),
         _per_seq(k16[n_p:].reshape(nb, ds * h_b, 2 * DH_B)), _per_seq(v32h[n_p * h_b:].astype(BF16).reshape(nb, ds * h_b, 2 * DH_B)),
         _dec_const(subg)],
        o_all, n_p, ds, 2 * h_b * ds, 2 * DH_B, nb, n_kb, smem=par)

    x = _row_call(
        functools.partial(_ab_out_body, s5w=s5w),
        [x, o_all],
        [s5_glu_w[0].astype(BF16), s5_glu_b[0].reshape(1, s5w), ab_w_out[0].astype(BF16)],
        [dm], [F32], tm, group_ins=[y_t], scratch=[pltpu.VMEM((s5w // 128, tm, 128), F32)])[0]

    x = _ffn(x, ffn_norm[0, 1], ffn_w_in[0, 1], ffn_w_out[0, 1], tm)

    x = _ffn(x, ffn_norm[1, 0], ffn_w_in[1, 0], ffn_w_out[1, 0], tm)

    half = ROPE_D // 2
    inv = ROPE_THETA ** (-jnp.arange(half, dtype=F32) / half)
    pad_r = HEAD_PAD - NOPE_D - ROPE_D

    def rope_tables(pos):
        ang = pos.astype(F32)[:, None] * inv[None, :]
        n = pos.shape[0]
        cos = jnp.concatenate([jnp.ones((n, NOPE_D), F32), jnp.cos(ang), jnp.cos(ang), jnp.zeros((n, pad_r), F32)], 1)
        sin = jnp.concatenate([jnp.zeros((n, NOPE_D), F32), jnp.sin(ang), jnp.sin(ang), jnp.zeros((n, pad_r), F32)], 1)
        return lax.optimization_barrier((cos, sin))

    cos_p, sin_p = rope_tables(jnp.arange(lp, dtype=jnp.int32) - front)
    cos_s, sin_s = rope_tables(past + jnp.arange(ds, dtype=jnp.int32))
    cos_t = jnp.concatenate([jnp.tile(cos_p, (bsz, 1)), jnp.tile(cos_s, (nb, 1))], axis=0)
    sin_t = jnp.concatenate([jnp.tile(sin_p, (bsz, 1)), jnp.tile(sin_s, (nb, 1))], axis=0)

    wcd = cd_w_in[0]
    c_fg = 3 * fox_w
    c_qa = c_fg + h_c
    c_kva = c_qa + q_lora
    c_pe = c_kva + kv_lora
    w_pe = wcd[:, c_pe:c_pe + ROPE_D]
    zc = lambda n: jnp.zeros((dm, n), F32)
    w_cd = jnp.concatenate([
        wcd[:, :3 * fox_w], wcd[:, c_qa:c_qa + q_lora], wcd[:, c_kva:c_kva + kv_lora],
        zc(NOPE_D), w_pe, zc(pad_r),
        zc(NOPE_D), -w_pe[:, half:], w_pe[:, :half], zc(pad_r),
        wcd[:, c_fg:c_fg + h_c], zc(HEAD_PAD - h_c)], axis=1).astype(BF16)
    qb = mla_q_b[0].reshape(q_lora, h_d, d_qk)
    zq = lambda n: jnp.zeros((q_lora, h_d, n), F32)
    qb_pad = jnp.concatenate([qb, zq(pad_r)], axis=-1).reshape(q_lora, h_d * HEAD_PAD)
    qb_rot = jnp.concatenate([zq(NOPE_D), -qb[..., NOPE_D + half:], qb[..., NOPE_D:NOPE_D + half], zq(pad_r)],
                             axis=-1).reshape(q_lora, h_d * HEAD_PAD)
    wq2 = jnp.concatenate([qb_pad, qb_rot], axis=1).astype(BF16)
    kvb = mla_kv_b[0].reshape(kv_lora, h_d, NOPE_D + V_D)
    wk_pad = jnp.concatenate([kvb[..., :NOPE_D], jnp.zeros((kv_lora, h_d, HEAD_PAD - NOPE_D), F32)],
                             axis=-1).reshape(kv_lora, h_d * HEAD_PAD).astype(BF16)
    wk_cmp = kvb[..., :NOPE_D].reshape(kv_lora, h_d * NOPE_D).astype(BF16)
    wv_cmp = kvb[..., NOPE_D:].reshape(kv_lora, h_d * V_D).astype(BF16)
    gfq = (jnp.tile(fox_q_norm[0], h_c) * (DH_C ** -0.5 * LOG2E)).reshape(1, fox_w)
    gfk = jnp.tile(fox_k_norm[0], h_c).reshape(1, fox_w)
    fbias = jnp.concatenate([fox_f_bias[0], jnp.zeros((HEAD_PAD - h_c,), F32)]).reshape(1, HEAD_PAD)
    gmq = jnp.tile(jnp.concatenate([mla_q_norm[0] * mla_k_norm[0] * (d_qk ** -0.5 * LOG2E), jnp.zeros((pad_r,), F32)]),
                   h_d).reshape(1, h_d * HEAD_PAD)

    (fq16, fk32, fk16, fv32, qm16, ckv32, km16, fvt, vmt, logf_t, kpe_t) = _row_call(
        functools.partial(_cd_in_body, parts=IN_PARTS, fox_w=fox_w, q_lora=q_lora, kv_lora=kv_lora, n_heads=h_d),
        [x, cos_t, sin_t],
        [mix_norm[1].reshape(1, dm), w_cd, gfq, gfk, fbias, mla_q_a_norm[0].reshape(1, q_lora), wq2,
         mla_kv_a_norm[0].reshape(1, kv_lora), wk_pad, wv_cmp.T, gmq, ones64, ones128],
        [fox_w, fox_w, fox_w, fox_w, h_d * HEAD_PAD, kv_lora, h_d * HEAD_PAD],
        [BF16, F32, BF16, F32, BF16, F32, BF16], tm,
        col_outs=[(2 * fox_w, BF16), (2 * h_d * V_D, BF16), (h_c, F32), (ROPE_D, F32)])
    logf_ps = jnp.transpose(logf_t[:, :n_p].reshape(h_c, bsz, lp), (1, 0, 2))
    logf_ss = jnp.transpose(logf_t[:, n_p:].reshape(h_c, nb, ds), (1, 0, 2))
    kpe_s = jnp.transpose(kpe_t[:, n_p:]).reshape(nb, ds, ROPE_D)

    f_p = _cumsum_lanes(logf_ps.reshape(bsz * h_c, lp)).reshape(bsz, h_c, lp)
    logf_s = jnp.concatenate([
        jnp.transpose(cache_fox_logf[0].astype(F32), (0, 2, 1)),
        logf_ss,
        jnp.zeros((nb, h_c, 256 - ds), F32)], axis=2).reshape(nb * h_c, past + 256)
    f_s = _cumsum_lanes(logf_s).reshape(nb, h_c, past + 256)

    kb_fox = jnp.where(kpad[None, None, :], NEG, -LOG2E * f_p).reshape(bsz, h_c // ATT_STREAMS, ATT_STREAMS, lp)
    oc_all = _prompt_attn("fox", fq16, fk16, fvt, kb_fox, bsz, lp)
    kb_mla = jnp.broadcast_to(jnp.where(kpad, NEG, 0.0).astype(F32)[None, None, None, :], (1, 1, ATT_STREAMS, lp))
    od_all = _prompt_attn("mla", qm16, km16, vmt, kb_mla, bsz, lp)

    eye_c = jnp.eye(h_c, dtype=BF16)
    fqs = fq16[n_p:].reshape(nb, ds, h_c, DH_C)
    fq_bd = jnp.einsum('bqhd,hH->bhqHd', fqs, eye_c).reshape(nb, h_c * ds, fox_w)
    fkc = cache_fox_k.reshape(cache_fox_k.shape[0], nb, past, fox_w)
    fvc = cache_fox_v.reshape(cache_fox_v.shape[0], nb, past, fox_w)
    oc_all = _dec_call(
        functools.partial(_fox_dec_body, ds=ds, n_heads=h_c),
        [_per_seq(fq_bd), (fkc, cache_spec(fox_w)), (fvc, cache_spec(fox_w)),
         _per_seq(fk16[n_p:].reshape(nb, ds, fox_w)), _per_seq(fv32[n_p:].reshape(nb, ds, fox_w)),
         (f_s, pl.BlockSpec((1, h_c, kb), lambda b, j: (b, 0, j))),
         (f_s, pl.BlockSpec((1, h_c, 128), lambda b, j: (b, 0, past // 128)))],
        oc_all, n_p, ds, h_c * ds, fox_w, nb, n_kb)

    eye_d = jnp.eye(h_d, dtype=BF16)
    qms = qm16[n_p:].reshape(nb, ds, h_d, HEAD_PAD)
    qn_bd = jnp.einsum('bqhd,hH->bhqHd', qms[..., :NOPE_D], eye_d).reshape(nb, h_d * ds, h_d * NOPE_D)
    qp_s = jnp.transpose(qms[..., NOPE_D:NOPE_D + ROPE_D], (0, 2, 1, 3)).reshape(nb, h_d * ds, ROPE_D)
    ones_h = jnp.repeat(jnp.eye(h_d, dtype=BF16), NOPE_D, axis=1)
    od_all = _dec_call(
        functools.partial(_mla_dec_body, ds=ds, n_heads=h_d),
        [_per_seq(qn_bd), _per_seq(qp_s),
         (cache_mla_ckv, pl.BlockSpec((1, 1, kb, kv_lora), lambda b, j: (0, b, j, 0))),
         (cache_mla_kpe, pl.BlockSpec((1, 1, kb, ROPE_D), lambda b, j: (0, b, j, 0))),
         _per_seq(ckv32[n_p:].reshape(nb, ds, kv_lora)), _per_seq(kpe_s),
         _dec_const(wk_cmp), _dec_const(wv_cmp), _dec_const(ones_h)],
        od_all, n_p, ds, h_d * ds, h_d * V_D, nb, n_kb)

    x = _ffn(x, ffn_norm[1, 1], ffn_w_in[1, 1], ffn_w_out[1, 1], tm, mix=(oc_all, od_all, cd_w_out[0]))

    def p_rows(a, shape):
        return a[:n_p].reshape((bsz, lp) + a.shape[1:])[:, front:front + ltot].reshape((1, bsz, ltot) + shape)

    def s_rows(a, shape):
        return a[n_p:].reshape((1, nb, ds) + shape)

    def p_cols(at):
        w = at.shape[0]
        return jnp.transpose(at[:, :n_p].reshape(w, bsz, lp)[:, :, front:front + ltot], (1, 2, 0))[None]

    def s_cols(at):
        return jnp.transpose(at[:, n_p:]).reshape(1, nb, ds, at.shape[0])

    def s5_state(st):
        st = jnp.transpose(st, (1, 0, 2))
        return st[None, :, :, :P_A], st[None, :, :, P_A:]

    y_prompt = jnp.stack([x[b * lp + front + n_meta:(b + 1) * lp] for b in range(bsz)])
    y_sample = x[n_p:].reshape(nb, ds, dm)
    s5_re_p, s5_im_p = s5_state(st_p)
    s5_re_s, s5_im_s = s5_state(st_s)
    return (y_prompt, y_sample,
            s5_re_p, s5_im_p, p_rows(k32, (h_b, 2 * DH_B)), p_rows(v32, (h_b, 2 * DH_B)),
            p_rows(fk32, (h_c, DH_C)), p_rows(fv32, (h_c, DH_C)), p_cols(logf_t),
            p_rows(ckv32, (kv_lora,)), p_cols(kpe_t),
            s5_re_s, s5_im_s, s_rows(k32, (h_b, 2 * DH_B)), s_rows(v32, (h_b, 2 * DH_B)),
            s_rows(fk32, (h_c, DH_C)), s_rows(fv32, (h_c, DH_C)), s_cols(logf_t),
            s_rows(ckv32, (kv_lora,)), s_cols(kpe_t))
```
